```python
import math
import jax, jax.numpy as jnp
from jax import lax
import numpy as np

D_MODEL = 1024
BATCH = 2
SEQ = 16384
DEPTH = 2
DEC_BATCH = 32
DEC_SEQ = 2048
PAST_LEN = 128

PLE_DIM = 256
GRID_W = 64
HEAD_DIM = 64
QBLK = 128
EPS = 1e-6
ROPE_THETA = 10000.0
A_HEADS = 8
A_KV_HEADS = 2
A_GROUPS = A_HEADS // A_KV_HEADS
A_WIDTH = A_HEADS * HEAD_DIM
A_KV_WIDTH = A_KV_HEADS * HEAD_DIM
B_HEADS = 4
B_QK_WIDTH = B_HEADS * 2 * HEAD_DIM
B_V_DIM = 2 * HEAD_DIM
B_WIDTH = B_HEADS * B_V_DIM
AB_SPLITS = (A_WIDTH, A_KV_WIDTH, A_KV_WIDTH, A_WIDTH, B_QK_WIDTH, B_QK_WIDTH, B_WIDTH, B_WIDTH)
AB_OFFSETS = tuple(int(v) for v in np.cumsum(AB_SPLITS)[:-1])
AB_IN = int(sum(AB_SPLITS))
AB_OUT = A_WIDTH + B_WIDTH
ALIBI_SLOPES = tuple(2.0 ** (-8.0 * (h + 1) / B_HEADS) for h in range(B_HEADS))
C_WIDTH = D_MODEL
POOL_WINDOWS = (2, 4, 8, 16)
C_GROUPS = len(POOL_WINDOWS)
C_GRP = C_WIDTH // C_GROUPS
N_EVEN = (DEPTH + 1) // 2
N_ODD = DEPTH // 2

kernel_name = "hybrid_gqa_diffattn_pool_encoder"

F32 = jnp.float32


def rms_norm(x, g):
    xf = x.astype(F32)
    y = xf * lax.rsqrt(jnp.mean(xf * xf, axis=-1, keepdims=True) + EPS) * g.astype(F32)
    return y.astype(x.dtype)


def axial_rope_tables(S):
    rows = S // GRID_W
    row = jnp.repeat(jnp.arange(rows), GRID_W).astype(F32)
    col = jnp.tile(jnp.arange(GRID_W), rows).astype(F32)
    half = HEAD_DIM // 2
    inv = ROPE_THETA ** (-jnp.arange(0, half, 2, dtype=F32) / half)
    ar = row[:, None] * inv
    ac = col[:, None] * inv
    ang = jnp.concatenate([ar, ar, ac, ac], axis=-1)
    return jnp.cos(ang), jnp.sin(ang)


def apply_rope(x, cos, sin):
    x1, x2, x3, x4 = jnp.split(x, 4, axis=-1)
    rot = jnp.concatenate([-x2, x1, -x4, x3], axis=-1)
    return (x.astype(F32) * cos + rot.astype(F32) * sin).astype(x.dtype)


def mixer_ab(xn, cos, sin, w_in, qn_a, kn_a, qn_b, kn_b, lq1, lk1, lq2, lk2, subln_b, w_out, lam_init):
    B, S, _ = xn.shape
    nb = S // QBLK
    proj = xn @ w_in
    qa, ka, va, ga, qb, kb, vb, gb = jnp.split(proj, AB_OFFSETS, axis=-1)
    qa = apply_rope(rms_norm(qa.reshape(B, S, A_KV_HEADS, A_GROUPS, HEAD_DIM), qn_a),
                    cos[:, None, None, :], sin[:, None, None, :])
    ka = apply_rope(rms_norm(ka.reshape(B, S, A_KV_HEADS, HEAD_DIM), kn_a),
                    cos[:, None, :], sin[:, None, :])
    va = va.reshape(B, S, A_KV_HEADS, HEAD_DIM)
    qb = rms_norm(qb.reshape(B, S, B_HEADS, 2, HEAD_DIM), qn_b)
    kb = rms_norm(kb.reshape(B, S, B_HEADS, 2, HEAD_DIM), kn_b)
    vb = vb.reshape(B, S, B_HEADS, B_V_DIM)
    lam = (jnp.exp(jnp.sum(lq1.astype(F32) * lk1.astype(F32)))
           - jnp.exp(jnp.sum(lq2.astype(F32) * lk2.astype(F32))) + lam_init)
    slopes = jnp.asarray(ALIBI_SLOPES, dtype=F32)
    kpos = jnp.arange(S)
    scale = HEAD_DIM ** -0.5

    def block(args):
        qa_blk, qb_blk, bi = args
        s_a = jnp.einsum('bqkgd,bskd->bkgqs', qa_blk, ka).astype(F32) * scale
        p_a = jax.nn.softmax(s_a, axis=-1).astype(va.dtype)
        o_a = jnp.einsum('bkgqs,bskd->bqkgd', p_a, va)
        qpos = bi * QBLK + jnp.arange(QBLK)
        dist = jnp.abs(qpos[:, None] - kpos[None, :]).astype(F32)
        bias = -slopes[:, None, None] * dist
        s_b = jnp.einsum('bqhmd,bshmd->bmhqs', qb_blk, kb).astype(F32) * scale + bias
        p_b = jax.nn.softmax(s_b, axis=-1)
        a_b = (p_b[:, 0] - lam * p_b[:, 1]).astype(vb.dtype)
        o_b = jnp.einsum('bhqs,bshe->bqhe', a_b, vb)
        return o_a, o_b

    def to_blocks(t):
        return jnp.moveaxis(t.reshape((B, nb, QBLK) + t.shape[2:]), 1, 0)

    o_a, o_b = lax.map(block, (to_blocks(qa), to_blocks(qb), jnp.arange(nb)))
    o_a = jnp.moveaxis(o_a, 0, 1).reshape(B, S, A_WIDTH)
    o_b = jnp.moveaxis(o_b, 0, 1).reshape(B, S, B_HEADS, B_V_DIM)
    o_b = (rms_norm(o_b, subln_b) * (1.0 - lam_init)).reshape(B, S, B_WIDTH)
    y = jnp.concatenate([o_a * jax.nn.silu(ga), o_b * jax.nn.silu(gb)], axis=-1)
    return y @ w_out


def mixer_c(xn, w_in, w_grp, c_scale, w_out):
    B, S, _ = xn.shape
    u, g = jnp.split(xn @ w_in, 2, axis=-1)
    ug = u.reshape(B, S, C_GROUPS, C_GRP).astype(F32)
    cs = jnp.concatenate([jnp.zeros_like(ug[:, :1]), jnp.cumsum(ug, axis=1)], axis=1)
    t = jnp.arange(S)
    pooled = []
    for gi, w in enumerate(POOL_WINDOWS):
        lo = jnp.maximum(t - w // 2, 0)
        hi = jnp.minimum(t + w // 2, S)
        cnt = (hi - lo).astype(F32)[None, :, None]
        pooled.append((cs[:, hi, gi] - cs[:, lo, gi]) / cnt - ug[:, :, gi])
    pooled = jnp.stack(pooled, axis=2)
    mixed = jnp.einsum('bsgc,gcd->bsgd', pooled, w_grp.astype(F32)).reshape(B, S, C_WIDTH)
    y = (mixed * c_scale.astype(F32)).astype(xn.dtype) * jax.nn.silu(g)
    return y @ w_out


def trunk(x, p, norm_mix, w_in_ab, qn_a, kn_a, qn_b, kn_b, lam_q1, lam_k1, lam_q2, lam_k2,
          subln_b, w_out_ab, w_in_c, w_grp_c, scale_c, w_out_c, norm_ple, w_ple_gate, w_ple_proj):
    S = x.shape[1]
    cos, sin = axial_rope_tables(S)
    h = x
    for i in range(DEPTH):
        j = i // 2
        xn = rms_norm(h, norm_mix[i])
        if i % 2 == 0:
            lam_init = 0.8 - 0.6 * math.exp(-0.3 * i)
            y = mixer_ab(xn, cos, sin, w_in_ab[j], qn_a[j], kn_a[j], qn_b[j], kn_b[j],
                         lam_q1[j], lam_k1[j], lam_q2[j], lam_k2[j], subln_b[j], w_out_ab[j], lam_init)
        else:
            y = mixer_c(xn, w_in_c[j], w_grp_c[j], scale_c[j], w_out_c[j])
        h = h + y
        gate = jax.nn.sigmoid(rms_norm(h, norm_ple[i]) @ w_ple_gate[i])
        h = h + gate * (p[i] @ w_ple_proj[i])
    return h


def setup_inputs(seed: int = 0) -> dict:
    key = jax.random.key(seed)
    ks = jax.random.split(key, 24)
    nrm = lambda k, shape, s: jax.random.normal(k, shape, F32) * s
    gain = lambda k, shape: 1.0 + 0.05 * jax.random.normal(k, shape, F32)
    return {
        "x_prompt": nrm(ks[0], (BATCH, SEQ, D_MODEL), 1.0),
        "x_sample": nrm(ks[1], (DEC_BATCH, DEC_SEQ, D_MODEL), 1.0),
        "p_prompt": nrm(ks[2], (DEPTH, BATCH, SEQ, PLE_DIM), 1.0),
        "p_sample": nrm(ks[3], (DEPTH, DEC_BATCH, DEC_SEQ, PLE_DIM), 1.0),
        "norm_mix": gain(ks[4], (DEPTH, D_MODEL)),
        "w_in_ab": nrm(ks[5], (N_EVEN, D_MODEL, AB_IN), D_MODEL ** -0.5),
        "qn_a": gain(ks[6], (N_EVEN, HEAD_DIM)),
        "kn_a": gain(ks[7], (N_EVEN, HEAD_DIM)),
        "qn_b": gain(ks[8], (N_EVEN, HEAD_DIM)),
        "kn_b": gain(ks[9], (N_EVEN, HEAD_DIM)),
        "lam_q1": nrm(ks[10], (N_EVEN, HEAD_DIM), 0.1),
        "lam_k1": nrm(ks[11], (N_EVEN, HEAD_DIM), 0.1),
        "lam_q2": nrm(ks[12], (N_EVEN, HEAD_DIM), 0.1),
        "lam_k2": nrm(ks[13], (N_EVEN, HEAD_DIM), 0.1),
        "subln_b": gain(ks[14], (N_EVEN, B_V_DIM)),
        "w_out_ab": nrm(ks[15], (N_EVEN, AB_OUT, D_MODEL), AB_OUT ** -0.5),
        "w_in_c": nrm(ks[16], (N_ODD, D_MODEL, 2 * C_WIDTH), D_MODEL ** -0.5),
        "w_grp_c": nrm(ks[17], (N_ODD, C_GROUPS, C_GRP, C_GRP), C_GRP ** -0.5),
        "scale_c": gain(ks[18], (N_ODD, C_WIDTH)),
        "w_out_c": nrm(ks[19], (N_ODD, C_WIDTH, D_MODEL), C_WIDTH ** -0.5),
        "norm_ple": gain(ks[20], (DEPTH, D_MODEL)),
        "w_ple_gate": nrm(ks[21], (DEPTH, D_MODEL, D_MODEL), D_MODEL ** -0.5),
        "w_ple_proj": nrm(ks[22], (DEPTH, PLE_DIM, D_MODEL), PLE_DIM ** -0.5),
    }


def reference(x_prompt, x_sample, p_prompt, p_sample, norm_mix, w_in_ab, qn_a, kn_a, qn_b, kn_b,
              lam_q1, lam_k1, lam_q2, lam_k2, subln_b, w_out_ab, w_in_c, w_grp_c, scale_c, w_out_c,
              norm_ple, w_ple_gate, w_ple_proj):
    y_prompt = trunk(x_prompt, p_prompt, norm_mix, w_in_ab, qn_a, kn_a, qn_b, kn_b, lam_q1, lam_k1,
                     lam_q2, lam_k2, subln_b, w_out_ab, w_in_c, w_grp_c, scale_c, w_out_c,
                     norm_ple, w_ple_gate, w_ple_proj)
    y_sample = trunk(x_sample, p_sample, norm_mix, w_in_ab, qn_a, kn_a, qn_b, kn_b, lam_q1, lam_k1,
                     lam_q2, lam_k2, subln_b, w_out_ab, w_in_c, w_grp_c, scale_c, w_out_c,
                     norm_ple, w_ple_gate, w_ple_proj)
    return (y_prompt, y_sample)
```

```python
import functools
import math

import jax
import jax.numpy as jnp
import numpy as np
from jax import lax
from jax.experimental import pallas as pl
from jax.experimental.pallas import tpu as pltpu

F32 = jnp.float32
BF16 = jnp.bfloat16

D_MODEL = 1024
PLE_DIM = 256
GRID_W = 64
HEAD_DIM = 64
EPS = 1e-6
ROPE_THETA = 10000.0
A_HEADS = 8
A_KV_HEADS = 2
B_HEADS = 4
B_V_DIM = 2 * HEAD_DIM
ALIBI_SLOPES = tuple(2.0 ** (-8.0 * (h + 1) / B_HEADS) for h in range(B_HEADS))
POOL_WINDOWS = (2, 4, 8, 16)
C_GRP = 256
POOL_HALO = 8
LOG2E = math.log2(math.e)
SM_SCALE = HEAD_DIM ** -0.5

LANES = 128
MXU_DIM = 256
VMEM_LIMIT_BYTES = 56 * 1024 * 1024

_QA, _KA, _VA, _GA, _QB, _KB, _VB, _GB, _AB_END = 0, 512, 640, 768, 1280, 1792, 2304, 2816, 3328

NEG_BIG = -1e30


def _rms_rows(x, g):
    ms = jnp.mean(x * x, axis=-1, keepdims=True)
    return x * lax.rsqrt(ms + EPS) * g


def _nt_dot(a, b):
    return lax.dot_general(a, b, (((1,), (1,)), ((), ())), preferred_element_type=F32)


def _proj_ab_kernel(x_ref, cos_ref, sa_ref, sb_ref, nm_ref, w_ref, gqa_ref, gkv_ref, gqb_ref,
                    gkb_ref, bd_ref,
                    qa_ref, ka_ref, va_ref, qb_ref, kb_ref, vb_ref, g_ref):
    x = x_ref[0]
    xn = _rms_rows(x, nm_ref[...]).astype(BF16)
    tm = x.shape[0]

    def seg(a, b):
        return jnp.dot(xn, w_ref[:, a:b], preferred_element_type=F32)

    def head_norm(z, gain):
        ss = jnp.dot((z * z).astype(BF16), bd_ref[...], preferred_element_type=F32)
        return z * lax.rsqrt(ss * (1.0 / HEAD_DIM) + EPS) * gain

    cos = cos_ref[...]
    sin_a = sa_ref[...]
    sin_b = sb_ref[...]

    def rope(z):
        return (z * cos + pltpu.roll(z, LANES - 16, 1) * sin_a + pltpu.roll(z, 16, 1) * sin_b)

    lane = lax.broadcasted_iota(jnp.int32, (tm, LANES), 1)
    lo = lane < HEAD_DIM

    for j in range(2):
        z = head_norm(seg(_QA + 256 * j, _QA + 256 * (j + 1)), gqa_ref[...])
        for p in range(2):
            c0 = 256 * j + 128 * p
            qa_ref[0, :, c0:c0 + 128] = rope(z[:, 128 * p:128 * (p + 1)]).astype(BF16)

    z = seg(_KA, _GA)
    zn = head_norm(z, gkv_ref[...])
    k = rope(zn[:, :128])
    k_sw = pltpu.roll(k, HEAD_DIM, 1)
    zero = jnp.zeros_like(k)
    ka_ref[0, 0, 0] = jnp.where(lo, k, zero).astype(BF16)
    ka_ref[0, 0, 1] = jnp.where(lo, zero, k_sw).astype(BF16)
    ka_ref[0, 1, 0] = jnp.where(lo, k_sw, zero).astype(BF16)
    ka_ref[0, 1, 1] = jnp.where(lo, zero, k).astype(BF16)
    v = z[:, 128:256]
    v_sw = pltpu.roll(v, HEAD_DIM, 1)
    one_a = (lane == HEAD_DIM).astype(F32)
    va_ref[0, 0] = jnp.where(lo, v, one_a).astype(BF16)
    va_ref[0, 1] = jnp.where(lo, v_sw, one_a).astype(BF16)

    for j in range(2):
        g = seg(_GA + 256 * j, _GA + 256 * (j + 1))
        g_ref[0, :, 256 * j:256 * (j + 1)] = g * jax.nn.sigmoid(g)
        g = seg(_GB + 256 * j, _GB + 256 * (j + 1))
        g_ref[0, :, 512 + 256 * j:512 + 256 * (j + 1)] = g * jax.nn.sigmoid(g)

    for j in range(2):
        z = head_norm(seg(_QB + 256 * j, _QB + 256 * (j + 1)), gqb_ref[...])
        qb_ref[0, :, 256 * j:256 * (j + 1)] = z.astype(BF16)
        z = head_norm(seg(_KB + 256 * j, _KB + 256 * (j + 1)), gkb_ref[...])
        for p in range(2):
            kh = z[:, 128 * p:128 * (p + 1)]
            zero = jnp.zeros_like(kh)
            kb_ref[0, 2 * j + p, 0] = jnp.where(lo, kh, zero).astype(BF16)
            kb_ref[0, 2 * j + p, 1] = jnp.where(lo, zero, kh).astype(BF16)

    one_b = (lane == 0).astype(BF16)
    for j in range(2):
        z = seg(_VB + 256 * j, _VB + 256 * (j + 1))
        for p in range(2):
            vb_ref[0, 2 * j + p, :, 0:128] = z[:, 128 * p:128 * (p + 1)].astype(BF16)
            vb_ref[0, 2 * j + p, :, 128:256] = one_b


def _rope_tables(S):
    rows = S // GRID_W
    row = jnp.repeat(jnp.arange(rows), GRID_W).astype(F32)
    col = jnp.tile(jnp.arange(GRID_W), rows).astype(F32)
    half = HEAD_DIM // 2
    inv = ROPE_THETA ** (-jnp.arange(0, half, 2, dtype=F32) / half)
    ar = row[:, None] * inv
    ac = col[:, None] * inv
    ang = jnp.concatenate([ar, ar, ac, ac], axis=-1)
    cos, sin = jnp.cos(ang), jnp.sin(ang)
    first = (np.arange(HEAD_DIM) % 32) < 16
    sin_a = jnp.where(first, -sin, 0.0)
    sin_b = jnp.where(first, 0.0, sin)
    two = lambda t: jnp.concatenate([t, t], axis=-1)
    return two(cos), two(sin_a), two(sin_b)


def _proj_ab(x, w_bf, nm, gqa, gkv, gqb, gkb, bd, tm):
    B, S, _ = x.shape
    cos, sin_a, sin_b = _rope_tables(S)
    tok = lambda b, i: (b, i, 0)
    tab = pl.BlockSpec((tm, LANES), lambda b, i: (i, 0))
    const2 = lambda shp: pl.BlockSpec(shp, lambda b, i: (0, 0))
    out_shape = (
        jax.ShapeDtypeStruct((B, S, 512), BF16),
        jax.ShapeDtypeStruct((B, 2, 2, S, LANES), BF16),
        jax.ShapeDtypeStruct((B, 2, S, LANES), BF16),
        jax.ShapeDtypeStruct((B, S, 512), BF16),
        jax.ShapeDtypeStruct((B, 4, 2, S, LANES), BF16),
        jax.ShapeDtypeStruct((B, 4, S, 2 * LANES), BF16),
        jax.ShapeDtypeStruct((B, S, 1024), F32),
    )
    out_specs = (
        pl.BlockSpec((1, tm, 512), tok),
        pl.BlockSpec((1, 2, 2, tm, LANES), lambda b, i: (b, 0, 0, i, 0)),
        pl.BlockSpec((1, 2, tm, LANES), lambda b, i: (b, 0, i, 0)),
        pl.BlockSpec((1, tm, 512), tok),
        pl.BlockSpec((1, 4, 2, tm, LANES), lambda b, i: (b, 0, 0, i, 0)),
        pl.BlockSpec((1, 4, tm, 2 * LANES), lambda b, i: (b, 0, i, 0)),
        pl.BlockSpec((1, tm, 1024), tok),
    )
    return pl.pallas_call(
        _proj_ab_kernel,
        grid=(B, S // tm),
        in_specs=[
            pl.BlockSpec((1, tm, D_MODEL), tok), tab, tab, tab,
            const2((1, D_MODEL)), const2((D_MODEL, _AB_END)),
            const2((1, 256)), const2((1, 256)), const2((1, 256)), const2((1, 256)),
            const2((MXU_DIM, MXU_DIM)),
        ],
        out_specs=out_specs,
        out_shape=out_shape,
        compiler_params=pltpu.CompilerParams(
            dimension_semantics=("parallel", "parallel"), vmem_limit_bytes=VMEM_LIMIT_BYTES),
        name="proj_ab",
    )(x, cos, sin_a, sin_b, nm, w_bf, gqa, gkv, gqb, gkb, bd)


def _online_update(s, v, m_ref, acc_ref, r0, nrows):
    rows = pl.ds(r0, nrows)
    m_old = m_ref[rows, :]
    m_new = jnp.maximum(m_old, jnp.max(s, axis=-1, keepdims=True))
    alpha = jnp.exp2(m_old - m_new)
    p = jnp.exp2(s - m_new[:, :1]).astype(BF16)
    pv = jnp.dot(p, v, preferred_element_type=F32)
    nv = v.shape[1]
    if nv != LANES:
        alpha = jnp.concatenate([alpha] * (nv // LANES), axis=1)
    acc_ref[rows, :] = acc_ref[rows, :] * alpha + pv
    m_ref[rows, :] = m_new


def _flash_a_kernel(q_ref, k_ref, v_ref, o_ref, q_scr, m_ref, acc_ref, *, tq, tk, n_kv):
    q_scr[0:tq, :] = q_ref[0, :, 0:128]
    q_scr[tq:2 * tq, :] = q_ref[0, :, 128:256]
    m_ref[...] = jnp.full(m_ref.shape, NEG_BIG, F32)
    acc_ref[...] = jnp.zeros(acc_ref.shape, F32)

    def body(c, carry):
        ks = pl.ds(pl.multiple_of(c * tk, tk), tk)
        q = q_scr[...]
        v = v_ref[0, 0, ks, :]
        for var in range(2):
            s = _nt_dot(q, k_ref[0, 0, var, ks, :])
            _online_update(s, v, m_ref, acc_ref, var * 2 * tq, 2 * tq)
        return carry

    lax.fori_loop(0, n_kv, body, 0)

    lane = lax.broadcasted_iota(jnp.int32, (tq, LANES), 1)
    lo = lane < HEAD_DIM

    def head_out(blk):
        a = acc_ref[blk * tq:(blk + 1) * tq, :]
        return a * (1.0 / a[:, HEAD_DIM:HEAD_DIM + 1])

    o_ref[0, :, 0:128] = jnp.where(lo, head_out(0), pltpu.roll(head_out(2), HEAD_DIM, 1))
    o_ref[0, :, 128:256] = jnp.where(lo, head_out(1), pltpu.roll(head_out(3), HEAD_DIM, 1))


def _kv_spec(block, index_map, single_buffer):
    if single_buffer:
        return pl.BlockSpec(block, index_map, pipeline_mode=pl.Buffered(1))
    return pl.BlockSpec(block, index_map)


def _flash_a(qa, ka, va, tq, tk):
    B, S, _ = qa.shape
    single = S * LANES * 2 * 3 * 2 > 8 * 1024 * 1024
    kern = functools.partial(_flash_a_kernel, tq=tq, tk=tk, n_kv=S // tk)
    return pl.pallas_call(
        kern,
        grid=(B, A_KV_HEADS, S // tq),
        in_specs=[
            pl.BlockSpec((1, tq, 256), lambda b, g, i: (b, i, g)),
            _kv_spec((1, 1, 2, S, LANES), lambda b, g, i: (b, g, 0, 0, 0), single),
            _kv_spec((1, 1, S, LANES), lambda b, g, i: (b, g, 0, 0), single),
        ],
        out_specs=pl.BlockSpec((1, tq, 256), lambda b, g, i: (b, i, g)),
        out_shape=jax.ShapeDtypeStruct((B, S, 512), F32),
        scratch_shapes=[
            pltpu.VMEM((2 * tq, LANES), BF16),
            pltpu.VMEM((4 * tq, LANES), F32),
            pltpu.VMEM((4 * tq, LANES), F32),
        ],
        compiler_params=pltpu.CompilerParams(
            dimension_semantics=("parallel", "parallel", "arbitrary"),
            vmem_limit_bytes=VMEM_LIMIT_BYTES),
        name="flash_a",
    )(qa, ka, va)


def _flash_b_kernel(q_ref, k_ref, v_ref, slope_ref, lq1_ref, lk1_ref, lq2_ref, lk2_ref, sub_ref,
                    o_ref, rel_ref, m_ref, acc_ref, *, tq, tk, n_kv, lam_init):
    m_ref[...] = jnp.full(m_ref.shape, NEG_BIG, F32)
    acc_ref[...] = jnp.zeros(acc_ref.shape, F32)
    rel_ref[...] = (lax.broadcasted_iota(jnp.int32, (tq, tk), 0)
                    - lax.broadcasted_iota(jnp.int32, (tq, tk), 1)).astype(F32)
    neg_slope = -slope_ref[0, 0:1, 0:1]
    q0 = pl.program_id(2) * tq

    def body(c, carry):
        k0 = pl.multiple_of(c * tk, tk)
        ks = pl.ds(k0, tk)
        q = q_ref[0]
        v = v_ref[0, 0, ks, :]
        bias = jnp.abs(rel_ref[...] + (q0 - k0).astype(F32)) * neg_slope
        for mp in range(2):
            s = _nt_dot(q, k_ref[0, 0, mp, ks, :]) + bias
            _online_update(s, v, m_ref, acc_ref, mp * tq, tq)
        return carry

    lax.fori_loop(0, n_kv, body, 0)

    lam = (jnp.exp(jnp.sum(lq1_ref[...] * lk1_ref[...], keepdims=True))
           - jnp.exp(jnp.sum(lq2_ref[...] * lk2_ref[...], keepdims=True)) + lam_init)
    a1 = acc_ref[0:tq, :]
    a2 = acc_ref[tq:2 * tq, :]
    o1 = a1[:, 0:B_V_DIM] * (1.0 / a1[:, B_V_DIM:B_V_DIM + 1])
    o2 = a2[:, 0:B_V_DIM] * (1.0 / a2[:, B_V_DIM:B_V_DIM + 1])
    o = o1 - lam * o2
    o_ref[0] = _rms_rows(o, sub_ref[...]) * (1.0 - lam_init)


def _flash_b(qb, kb, vb, slopes, lq1, lk1, lq2, lk2, subln, lam_init, tq, tk):
    B, S, _ = qb.shape
    single = S * LANES * 2 * 4 * 2 > 8 * 1024 * 1024
    kern = functools.partial(_flash_b_kernel, tq=tq, tk=tk, n_kv=S // tk, lam_init=lam_init)
    vec = lambda n: pl.BlockSpec((1, n), lambda b, h, i: (0, 0))
    return pl.pallas_call(
        kern,
        grid=(B, B_HEADS, S // tq),
        in_specs=[
            pl.BlockSpec((1, tq, LANES), lambda b, h, i: (b, i, h)),
            _kv_spec((1, 1, 2, S, LANES), lambda b, h, i: (b, h, 0, 0, 0), single),
            _kv_spec((1, 1, S, 2 * LANES), lambda b, h, i: (b, h, 0, 0), single),
            pl.BlockSpec((1, 8, LANES), lambda b, h, i: (h, 0, 0)),
            vec(HEAD_DIM), vec(HEAD_DIM), vec(HEAD_DIM), vec(HEAD_DIM), vec(B_V_DIM),
        ],
        out_specs=pl.BlockSpec((1, tq, LANES), lambda b, h, i: (b, i, h)),
        out_shape=jax.ShapeDtypeStruct((B, S, 512), F32),
        scratch_shapes=[
            pltpu.VMEM((tq, tk), F32),
            pltpu.VMEM((2 * tq, LANES), F32),
            pltpu.VMEM((2 * tq, 2 * LANES), F32),
        ],
        compiler_params=pltpu.CompilerParams(
            dimension_semantics=("parallel", "parallel", "arbitrary"),
            vmem_limit_bytes=VMEM_LIMIT_BYTES),
        name="flash_b",
    )(qb, kb, vb, slopes, lq1, lk1, lq2, lk2, subln)


def _ple_tail(h1, p, npl, wg_ref, wp_ref):
    rn = _rms_rows(h1, npl).astype(BF16)
    gate = jax.nn.sigmoid(jnp.dot(rn, wg_ref[...], preferred_element_type=F32))
    return h1 + gate * jnp.dot(p.astype(BF16), wp_ref[...], preferred_element_type=F32)


def _out_ple_kernel(h_ref, oa_ref, ob_ref, g_ref, p_ref, wo_ref, npl_ref, wg_ref, wp_ref, out_ref):
    g = g_ref[0]
    y = jnp.concatenate([oa_ref[0] * g[:, 0:512], ob_ref[0] * g[:, 512:1024]], axis=1).astype(BF16)
    h1 = h_ref[0] + jnp.dot(y, wo_ref[...], preferred_element_type=F32)
    out_ref[0] = _ple_tail(h1, p_ref[0, 0], npl_ref[...], wg_ref, wp_ref)


def _out_ple(h, oa, ob, gates, p, layer, wo, npl, wg, wp, tm):
    B, S, _ = h.shape
    tok = lambda b, i: (b, i, 0)
    const2 = lambda shp: pl.BlockSpec(shp, lambda b, i: (0, 0))
    return pl.pallas_call(
        _out_ple_kernel,
        grid=(B, S // tm),
        in_specs=[
            pl.BlockSpec((1, tm, D_MODEL), tok),
            pl.BlockSpec((1, tm, 512), tok),
            pl.BlockSpec((1, tm, 512), tok),
            pl.BlockSpec((1, tm, 1024), tok),
            pl.BlockSpec((1, 1, tm, PLE_DIM), lambda b, i: (layer, b, i, 0)),
            const2((1024, D_MODEL)), const2((1, D_MODEL)), const2((D_MODEL, D_MODEL)),
            const2((PLE_DIM, D_MODEL)),
        ],
        out_specs=pl.BlockSpec((1, tm, D_MODEL), tok),
        out_shape=jax.ShapeDtypeStruct((B, S, D_MODEL), F32),
        compiler_params=pltpu.CompilerParams(
            dimension_semantics=("parallel", "parallel"), vmem_limit_bytes=VMEM_LIMIT_BYTES),
        name="out_ple",
    )(h, oa, ob, gates, p, wo, npl, wg, wp)


def _layer_c_kernel(h_ref, hp_ref, hn_ref, p_ref, nm_ref, wi_ref, wgrp_ref, sc_ref, wo_ref,
                    npl_ref, wg_ref, wp_ref, out_ref, *, tm, S):
    i = pl.program_id(1)
    nt = pl.num_programs(1)
    h = h_ref[0]
    hp = hp_ref[0] * (i > 0).astype(F32)
    hn = hn_ref[0] * (i < nt - 1).astype(F32)
    n = tm + 2 * POOL_HALO
    h_ext = jnp.concatenate([hp, h, hn], axis=0)
    xn = _rms_rows(h_ext, nm_ref[...]).astype(BF16)
    u_ext = jnp.dot(xn, wi_ref[:, 0:1024], preferred_element_type=F32)
    g = jnp.dot(xn[POOL_HALO:POOL_HALO + tm], wi_ref[:, 1024:2048], preferred_element_type=F32)

    t = i * tm + lax.broadcasted_iota(jnp.int32, (tm, C_GRP), 0)
    mixed = []
    for gi, w in enumerate(POOL_WINDOWS):
        u = u_ext[:, C_GRP * gi:C_GRP * (gi + 1)]
        win = u + pltpu.roll(u, 1, 0)
        half = 1
        while 2 * half < w:
            win = pltpu.roll(win, half, 0) + pltpu.roll(win, n - half, 0)
            half *= 2
        cnt = (jnp.minimum(t + w // 2, S) - jnp.maximum(t - w // 2, 0)).astype(F32)
        pooled = win[POOL_HALO:POOL_HALO + tm] / cnt - u[POOL_HALO:POOL_HALO + tm]
        mixed.append(jnp.dot(pooled.astype(BF16), wgrp_ref[gi], preferred_element_type=F32))
    mixed = jnp.concatenate(mixed, axis=1)
    y = ((mixed * sc_ref[...]) * (g * jax.nn.sigmoid(g))).astype(BF16)
    h1 = h + jnp.dot(y, wo_ref[...], preferred_element_type=F32)
    out_ref[0] = _ple_tail(h1, p_ref[0, 0], npl_ref[...], wg_ref, wp_ref)


def _layer_c(h, p, layer, nm, wi, wgrp, sc, wo, npl, wg, wp, tm):
    B, S, _ = h.shape
    tok = lambda b, i: (b, i, 0)
    const2 = lambda shp: pl.BlockSpec(shp, lambda b, i: (0, 0))
    per = tm // POOL_HALO
    last = S // POOL_HALO - 1
    kern = functools.partial(_layer_c_kernel, tm=tm, S=S)
    return pl.pallas_call(
        kern,
        grid=(B, S // tm),
        in_specs=[
            pl.BlockSpec((1, tm, D_MODEL), tok),
            pl.BlockSpec((1, POOL_HALO, D_MODEL), lambda b, i: (b, jnp.maximum(i * per - 1, 0), 0)),
            pl.BlockSpec((1, POOL_HALO, D_MODEL), lambda b, i: (b, jnp.minimum((i + 1) * per, last), 0)),
            pl.BlockSpec((1, 1, tm, PLE_DIM), lambda b, i: (layer, b, i, 0)),
            const2((1, D_MODEL)), const2((D_MODEL, 2048)),
            pl.BlockSpec((4, C_GRP, C_GRP), lambda b, i: (0, 0, 0)),
            const2((1, 1024)), const2((1024, D_MODEL)), const2((1, D_MODEL)),
            const2((D_MODEL, D_MODEL)), const2((PLE_DIM, D_MODEL)),
        ],
        out_specs=pl.BlockSpec((1, tm, D_MODEL), tok),
        out_shape=jax.ShapeDtypeStruct((B, S, D_MODEL), F32),
        compiler_params=pltpu.CompilerParams(
            dimension_semantics=("parallel", "parallel"), vmem_limit_bytes=VMEM_LIMIT_BYTES),
        name="layer_c",
    )(h, h, h, p, nm, wi, wgrp, sc, wo, npl, wg, wp)


def _tile_gain(g, reps, scale=1.0):
    return jnp.tile(g.astype(F32) * scale, reps).reshape(1, -1)


def _trunk(x, p, prm, tq_a, tq_b, tk, tm):
    row = lambda v: v.astype(F32).reshape(1, -1)
    lam_init = 0.8 - 0.6 * math.exp(-0.3 * 0)
    gq_scale = SM_SCALE * LOG2E
    gqa = _tile_gain(prm["qn_a"][0], 4, gq_scale)
    gkv = jnp.concatenate([_tile_gain(prm["kn_a"][0], 2), jnp.ones((1, 128), F32)], axis=1)
    gqb = _tile_gain(prm["qn_b"][0], 4, gq_scale)
    gkb = _tile_gain(prm["kn_b"][0], 4)
    qa, ka, va, qb, kb, vb, gates = _proj_ab(
        x, prm["w_in_ab"][0], row(prm["norm_mix"][0]), gqa, gkv, gqb, gkb, prm["bd"], tm)
    oa = _flash_a(qa, ka, va, tq_a, tk)
    ob = _flash_b(qb, kb, vb, prm["slopes"], row(prm["lam_q1"][0]), row(prm["lam_k1"][0]),
                  row(prm["lam_q2"][0]), row(prm["lam_k2"][0]), row(prm["subln_b"][0]),
                  lam_init, tq_b, tk)
    h = _out_ple(x, oa, ob, gates, p, 0, prm["w_out_ab"][0], row(prm["norm_ple"][0]),
                 prm["w_ple_gate"][0], prm["w_ple_proj"][0], tm)
    h = _layer_c(h, p, 1, row(prm["norm_mix"][1]), prm["w_in_c"][0], prm["w_grp_c"][0],
                 row(prm["scale_c"][0]), prm["w_out_c"][0], row(prm["norm_ple"][1]),
                 prm["w_ple_gate"][1], prm["w_ple_proj"][1], tm)
    return h


def kernel(x_prompt, x_sample, p_prompt, p_sample, norm_mix, w_in_ab, qn_a, kn_a, qn_b, kn_b,
           lam_q1, lam_k1, lam_q2, lam_k2, subln_b, w_out_ab, w_in_c, w_grp_c, scale_c, w_out_c,
           norm_ple, w_ple_gate, w_ple_proj):
    blk = np.arange(MXU_DIM) // HEAD_DIM
    slopes = np.asarray(ALIBI_SLOPES, np.float32) * LOG2E
    prm = dict(
        norm_mix=norm_mix, qn_a=qn_a, kn_a=kn_a, qn_b=qn_b, kn_b=kn_b,
        lam_q1=lam_q1, lam_k1=lam_k1, lam_q2=lam_q2, lam_k2=lam_k2, subln_b=subln_b,
        scale_c=scale_c, norm_ple=norm_ple,
        w_in_ab=w_in_ab.astype(BF16), w_out_ab=w_out_ab.astype(BF16), w_in_c=w_in_c.astype(BF16),
        w_grp_c=w_grp_c.astype(BF16), w_out_c=w_out_c.astype(BF16),
        w_ple_gate=w_ple_gate.astype(BF16), w_ple_proj=w_ple_proj.astype(BF16),
        bd=jnp.asarray(blk[:, None] == blk[None, :], BF16),
        slopes=jnp.asarray(np.broadcast_to(slopes[:, None, None], (B_HEADS, 8, LANES))),
    )
    y_prompt = _trunk(x_prompt, p_prompt, prm, tq_a=256, tq_b=512, tk=512, tm=512)
    y_sample = _trunk(x_sample, p_sample, prm, tq_a=256, tq_b=512, tk=512, tm=512)
    return (y_prompt, y_sample)
```

```python
import functools
import math

import jax
import jax.numpy as jnp
import numpy as np
from jax import lax
from jax.experimental import pallas as pl
from jax.experimental.pallas import tpu as pltpu

F32 = jnp.float32
BF16 = jnp.bfloat16

D_MODEL = 1024
PLE_DIM = 256
GRID_W = 64
HEAD_DIM = 64
EPS = 1e-6
ROPE_THETA = 10000.0
A_HEADS = 8
A_KV_HEADS = 2
B_HEADS = 4
B_V_DIM = 2 * HEAD_DIM
ALIBI_SLOPES = tuple(2.0 ** (-8.0 * (h + 1) / B_HEADS) for h in range(B_HEADS))
POOL_WINDOWS = (2, 4, 8, 16)
C_GRP = 256
POOL_HALO = 8
LOG2E = math.log2(math.e)
SM_SCALE = HEAD_DIM ** -0.5

LANES = 128
MXU_DIM = 256
VMEM_LIMIT_BYTES = 56 * 1024 * 1024

_QA, _KA, _VA, _GA, _QB, _KB, _VB, _GB, _AB_END = 0, 512, 640, 768, 1280, 1792, 2304, 2816, 3328

NEG_BIG = -1e30
BOUNDED_SCORE_LIMIT = 60.0


def _rms_rows(x, g):
    ms = jnp.mean(x * x, axis=-1, keepdims=True)
    return x * lax.rsqrt(ms + EPS) * g


def _nt_dot(a, b):
    return lax.dot_general(a, b, (((1,), (1,)), ((), ())), preferred_element_type=F32)


def _proj_ab_kernel(x_ref, cos_ref, sa_ref, sb_ref, nm_ref, w_ref, gqa_ref, gkv_ref, gqb_ref,
                    gkb_ref, bd_ref,
                    qa_ref, ka_ref, va_ref, qb_ref, kb_ref, vb_ref, g_ref):
    x = x_ref[0]
    xn = _rms_rows(x, nm_ref[...]).astype(BF16)
    tm = x.shape[0]

    def seg(a, b):
        return jnp.dot(xn, w_ref[:, a:b], preferred_element_type=F32)

    def head_norm(z, gain):
        ss = jnp.dot((z * z).astype(BF16), bd_ref[...], preferred_element_type=F32)
        return z * lax.rsqrt(ss * (1.0 / HEAD_DIM) + EPS) * gain

    cos = cos_ref[...]
    sin_a = sa_ref[...]
    sin_b = sb_ref[...]

    def rope(z):
        return (z * cos + pltpu.roll(z, LANES - 16, 1) * sin_a + pltpu.roll(z, 16, 1) * sin_b)

    lane = lax.broadcasted_iota(jnp.int32, (tm, LANES), 1)
    lo = lane < HEAD_DIM

    for j in range(2):
        z = head_norm(seg(_QA + 256 * j, _QA + 256 * (j + 1)), gqa_ref[...])
        for p in range(2):
            c0 = 256 * j + 128 * p
            qa_ref[0, :, c0:c0 + 128] = rope(z[:, 128 * p:128 * (p + 1)]).astype(BF16)

    z = seg(_KA, _GA)
    zn = head_norm(z, gkv_ref[...])
    k = rope(zn[:, :128])
    k_sw = pltpu.roll(k, HEAD_DIM, 1)
    zero = jnp.zeros_like(k)
    ka_ref[0, 0, 0] = jnp.where(lo, k, zero).astype(BF16)
    ka_ref[0, 0, 1] = jnp.where(lo, zero, k_sw).astype(BF16)
    ka_ref[0, 1, 0] = jnp.where(lo, k_sw, zero).astype(BF16)
    ka_ref[0, 1, 1] = jnp.where(lo, zero, k).astype(BF16)
    v = z[:, 128:256]
    v_sw = pltpu.roll(v, HEAD_DIM, 1)
    one_a = (lane == HEAD_DIM).astype(F32)
    va_ref[0, 0] = jnp.where(lo, v, one_a).astype(BF16)
    va_ref[0, 1] = jnp.where(lo, v_sw, one_a).astype(BF16)

    for j in range(2):
        g = seg(_GA + 256 * j, _GA + 256 * (j + 1))
        g_ref[0, :, 256 * j:256 * (j + 1)] = g * jax.nn.sigmoid(g)
        g = seg(_GB + 256 * j, _GB + 256 * (j + 1))
        g_ref[0, :, 512 + 256 * j:512 + 256 * (j + 1)] = g * jax.nn.sigmoid(g)

    for j in range(2):
        z = head_norm(seg(_QB + 256 * j, _QB + 256 * (j + 1)), gqb_ref[...])
        qb_ref[0, :, 256 * j:256 * (j + 1)] = z.astype(BF16)
        z = head_norm(seg(_KB + 256 * j, _KB + 256 * (j + 1)), gkb_ref[...])
        for p in range(2):
            kh = z[:, 128 * p:128 * (p + 1)]
            zero = jnp.zeros_like(kh)
            kb_ref[0, 2 * j + p, 0] = jnp.where(lo, kh, zero).astype(BF16)
            kb_ref[0, 2 * j + p, 1] = jnp.where(lo, zero, kh).astype(BF16)

    one_b = (lane == 0).astype(BF16)
    for j in range(2):
        z = seg(_VB + 256 * j, _VB + 256 * (j + 1))
        for p in range(2):
            vb_ref[0, 2 * j + p, :, 0:128] = z[:, 128 * p:128 * (p + 1)].astype(BF16)
            vb_ref[0, 2 * j + p, :, 128:256] = one_b


def _rope_tables(S):
    rows = S // GRID_W
    row = jnp.repeat(jnp.arange(rows), GRID_W).astype(F32)
    col = jnp.tile(jnp.arange(GRID_W), rows).astype(F32)
    half = HEAD_DIM // 2
    inv = ROPE_THETA ** (-jnp.arange(0, half, 2, dtype=F32) / half)
    ar = row[:, None] * inv
    ac = col[:, None] * inv
    ang = jnp.concatenate([ar, ar, ac, ac], axis=-1)
    cos, sin = jnp.cos(ang), jnp.sin(ang)
    first = (np.arange(HEAD_DIM) % 32) < 16
    sin_a = jnp.where(first, -sin, 0.0)
    sin_b = jnp.where(first, 0.0, sin)
    two = lambda t: jnp.concatenate([t, t], axis=-1)
    return two(cos), two(sin_a), two(sin_b)


def _proj_ab(x, w_bf, nm, gqa, gkv, gqb, gkb, bd, tm):
    B, S, _ = x.shape
    cos, sin_a, sin_b = _rope_tables(S)
    tok = lambda b, i: (b, i, 0)
    tab = pl.BlockSpec((tm, LANES), lambda b, i: (i, 0))
    const2 = lambda shp: pl.BlockSpec(shp, lambda b, i: (0, 0))
    out_shape = (
        jax.ShapeDtypeStruct((B, S, 512), BF16),
        jax.ShapeDtypeStruct((B, 2, 2, S, LANES), BF16),
        jax.ShapeDtypeStruct((B, 2, S, LANES), BF16),
        jax.ShapeDtypeStruct((B, S, 512), BF16),
        jax.ShapeDtypeStruct((B, 4, 2, S, LANES), BF16),
        jax.ShapeDtypeStruct((B, 4, S, 2 * LANES), BF16),
        jax.ShapeDtypeStruct((B, S, 1024), F32),
    )
    out_specs = (
        pl.BlockSpec((1, tm, 512), tok),
        pl.BlockSpec((1, 2, 2, tm, LANES), lambda b, i: (b, 0, 0, i, 0)),
        pl.BlockSpec((1, 2, tm, LANES), lambda b, i: (b, 0, i, 0)),
        pl.BlockSpec((1, tm, 512), tok),
        pl.BlockSpec((1, 4, 2, tm, LANES), lambda b, i: (b, 0, 0, i, 0)),
        pl.BlockSpec((1, 4, tm, 2 * LANES), lambda b, i: (b, 0, i, 0)),
        pl.BlockSpec((1, tm, 1024), tok),
    )
    return pl.pallas_call(
        _proj_ab_kernel,
        grid=(B, S // tm),
        in_specs=[
            pl.BlockSpec((1, tm, D_MODEL), tok), tab, tab, tab,
            const2((1, D_MODEL)), const2((D_MODEL, _AB_END)),
            const2((1, 256)), const2((1, 256)), const2((1, 256)), const2((1, 256)),
            const2((MXU_DIM, MXU_DIM)),
        ],
        out_specs=out_specs,
        out_shape=out_shape,
        compiler_params=pltpu.CompilerParams(
            dimension_semantics=("parallel", "parallel"), vmem_limit_bytes=VMEM_LIMIT_BYTES),
        name="proj_ab",
    )(x, cos, sin_a, sin_b, nm, w_bf, gqa, gkv, gqb, gkb, bd)


def _online_update(s, v, m_ref, acc_ref, r0, nrows):
    rows = pl.ds(r0, nrows)
    m_old = m_ref[rows, :]
    m_new = jnp.maximum(m_old, jnp.max(s, axis=-1, keepdims=True))
    alpha = jnp.exp2(m_old - m_new)
    p = jnp.exp2(s - m_new[:, :1]).astype(BF16)
    pv = jnp.dot(p, v, preferred_element_type=F32)
    nv = v.shape[1]
    if nv != LANES:
        alpha = jnp.concatenate([alpha] * (nv // LANES), axis=1)
    acc_ref[rows, :] = acc_ref[rows, :] * alpha + pv
    m_ref[rows, :] = m_new


def _flash_a_kernel(q_ref, k_ref, v_ref, o_ref, q_scr, m_ref, acc_ref, *, tq, tk, n_kv):
    q_scr[0:tq, :] = q_ref[0, :, 0:128]
    q_scr[tq:2 * tq, :] = q_ref[0, :, 128:256]
    m_ref[...] = jnp.full(m_ref.shape, NEG_BIG, F32)
    acc_ref[...] = jnp.zeros(acc_ref.shape, F32)

    def body(c, carry):
        ks = pl.ds(pl.multiple_of(c * tk, tk), tk)
        q = q_scr[...]
        v = v_ref[0, 0, ks, :]
        for var in range(2):
            s = _nt_dot(q, k_ref[0, 0, var, ks, :])
            _online_update(s, v, m_ref, acc_ref, var * 2 * tq, 2 * tq)
        return carry

    lax.fori_loop(0, n_kv, body, 0)
    _flash_a_finalize(o_ref, acc_ref, tq)


def _flash_a_finalize(o_ref, acc_ref, tq):
    lane = lax.broadcasted_iota(jnp.int32, (tq, LANES), 1)
    lo = lane < HEAD_DIM

    def head_out(blk):
        a = acc_ref[blk * tq:(blk + 1) * tq, :]
        return a * (1.0 / a[:, HEAD_DIM:HEAD_DIM + 1])

    o_ref[0, :, 0:128] = jnp.where(lo, head_out(0), pltpu.roll(head_out(2), HEAD_DIM, 1))
    o_ref[0, :, 128:256] = jnp.where(lo, head_out(1), pltpu.roll(head_out(3), HEAD_DIM, 1))


def _flash_a_bounded_kernel(q_ref, k_ref, v_ref, o_ref, q_scr, acc_ref, *, tq, tk, n_kv, unroll):
    q_scr[0:tq, :] = q_ref[0, :, 0:128]
    q_scr[tq:2 * tq, :] = q_ref[0, :, 128:256]
    acc_ref[...] = jnp.zeros(acc_ref.shape, F32)

    def body(c, carry):
        ks = pl.ds(pl.multiple_of(c * tk, tk), tk)
        q = q_scr[...]
        v = v_ref[0, 0, ks, :]
        for var in range(2):
            p = jnp.exp2(_nt_dot(q, k_ref[0, 0, var, ks, :])).astype(BF16)
            rows = pl.ds(var * 2 * tq, 2 * tq)
            acc_ref[rows, :] += jnp.dot(p, v, preferred_element_type=F32)
        return carry

    lax.fori_loop(0, n_kv, body, 0, unroll=unroll)
    _flash_a_finalize(o_ref, acc_ref, tq)


def _kv_spec(block, index_map, single_buffer):
    if single_buffer:
        return pl.BlockSpec(block, index_map, pipeline_mode=pl.Buffered(1))
    return pl.BlockSpec(block, index_map)


def _flash_a(qa, ka, va, tq, tk, bounded):
    B, S, _ = qa.shape
    single = S * LANES * 2 * 3 * 2 > 8 * 1024 * 1024
    scratch = [pltpu.VMEM((2 * tq, LANES), BF16), pltpu.VMEM((4 * tq, LANES), F32)]
    if bounded:
        kern = functools.partial(_flash_a_bounded_kernel, tq=tq, tk=tk, n_kv=S // tk, unroll=2)
    else:
        kern = functools.partial(_flash_a_kernel, tq=tq, tk=tk, n_kv=S // tk)
        scratch.insert(1, pltpu.VMEM((4 * tq, LANES), F32))
    return pl.pallas_call(
        kern,
        grid=(B, A_KV_HEADS, S // tq),
        in_specs=[
            pl.BlockSpec((1, tq, 256), lambda b, g, i: (b, i, g)),
            _kv_spec((1, 1, 2, S, LANES), lambda b, g, i: (b, g, 0, 0, 0), single),
            _kv_spec((1, 1, S, LANES), lambda b, g, i: (b, g, 0, 0), single),
        ],
        out_specs=pl.BlockSpec((1, tq, 256), lambda b, g, i: (b, i, g)),
        out_shape=jax.ShapeDtypeStruct((B, S, 512), F32),
        scratch_shapes=scratch,
        compiler_params=pltpu.CompilerParams(
            dimension_semantics=("parallel", "parallel", "arbitrary"),
            vmem_limit_bytes=VMEM_LIMIT_BYTES),
        name="flash_a_bounded" if bounded else "flash_a",
    )(qa, ka, va)


def _flash_b_kernel(q_ref, k_ref, v_ref, slope_ref, lq1_ref, lk1_ref, lq2_ref, lk2_ref, sub_ref,
                    o_ref, rel_ref, m_ref, acc_ref, *, tq, tk, n_kv, lam_init):
    m_ref[...] = jnp.full(m_ref.shape, NEG_BIG, F32)
    acc_ref[...] = jnp.zeros(acc_ref.shape, F32)
    rel_ref[...] = (lax.broadcasted_iota(jnp.int32, (tq, tk), 0)
                    - lax.broadcasted_iota(jnp.int32, (tq, tk), 1)).astype(F32)
    neg_slope = -slope_ref[0, 0:1, 0:1]
    q0 = pl.program_id(2) * tq

    def body(c, carry):
        k0 = pl.multiple_of(c * tk, tk)
        ks = pl.ds(k0, tk)
        q = q_ref[0]
        v = v_ref[0, 0, ks, :]
        bias = jnp.abs(rel_ref[...] + (q0 - k0).astype(F32)) * neg_slope
        for mp in range(2):
            s = _nt_dot(q, k_ref[0, 0, mp, ks, :]) + bias
            _online_update(s, v, m_ref, acc_ref, mp * tq, tq)
        return carry

    lax.fori_loop(0, n_kv, body, 0)

    lam = (jnp.exp(jnp.sum(lq1_ref[...] * lk1_ref[...], keepdims=True))
           - jnp.exp(jnp.sum(lq2_ref[...] * lk2_ref[...], keepdims=True)) + lam_init)
    a1 = acc_ref[0:tq, :]
    a2 = acc_ref[tq:2 * tq, :]
    o1 = a1[:, 0:B_V_DIM] * (1.0 / a1[:, B_V_DIM:B_V_DIM + 1])
    o2 = a2[:, 0:B_V_DIM] * (1.0 / a2[:, B_V_DIM:B_V_DIM + 1])
    o = o1 - lam * o2
    o_ref[0] = _rms_rows(o, sub_ref[...]) * (1.0 - lam_init)


def _flash_b(qb, kb, vb, slopes, lq1, lk1, lq2, lk2, subln, lam_init, tq, tk):
    B, S, _ = qb.shape
    single = S * LANES * 2 * 4 * 2 > 8 * 1024 * 1024
    kern = functools.partial(_flash_b_kernel, tq=tq, tk=tk, n_kv=S // tk, lam_init=lam_init)
    vec = lambda n: pl.BlockSpec((1, n), lambda b, h, i: (0, 0))
    return pl.pallas_call(
        kern,
        grid=(B, B_HEADS, S // tq),
        in_specs=[
            pl.BlockSpec((1, tq, LANES), lambda b, h, i: (b, i, h)),
            _kv_spec((1, 1, 2, S, LANES), lambda b, h, i: (b, h, 0, 0, 0), single),
            _kv_spec((1, 1, S, 2 * LANES), lambda b, h, i: (b, h, 0, 0), single),
            pl.BlockSpec((1, 8, LANES), lambda b, h, i: (h, 0, 0)),
            vec(HEAD_DIM), vec(HEAD_DIM), vec(HEAD_DIM), vec(HEAD_DIM), vec(B_V_DIM),
        ],
        out_specs=pl.BlockSpec((1, tq, LANES), lambda b, h, i: (b, i, h)),
        out_shape=jax.ShapeDtypeStruct((B, S, 512), F32),
        scratch_shapes=[
            pltpu.VMEM((tq, tk), F32),
            pltpu.VMEM((2 * tq, LANES), F32),
            pltpu.VMEM((2 * tq, 2 * LANES), F32),
        ],
        compiler_params=pltpu.CompilerParams(
            dimension_semantics=("parallel", "parallel", "arbitrary"),
            vmem_limit_bytes=VMEM_LIMIT_BYTES),
        name="flash_b",
    )(qb, kb, vb, slopes, lq1, lk1, lq2, lk2, subln)


def _ple_tail(h1, p, npl, wg_ref, wp_ref):
    rn = _rms_rows(h1, npl).astype(BF16)
    gate = jax.nn.sigmoid(jnp.dot(rn, wg_ref[...], preferred_element_type=F32))
    return h1 + gate * jnp.dot(p.astype(BF16), wp_ref[...], preferred_element_type=F32)


def _out_ple_kernel(h_ref, oa_ref, ob_ref, g_ref, p_ref, wo_ref, npl_ref, wg_ref, wp_ref, out_ref):
    g = g_ref[0]
    y = jnp.concatenate([oa_ref[0] * g[:, 0:512], ob_ref[0] * g[:, 512:1024]], axis=1).astype(BF16)
    h1 = h_ref[0] + jnp.dot(y, wo_ref[...], preferred_element_type=F32)
    out_ref[0] = _ple_tail(h1, p_ref[0, 0], npl_ref[...], wg_ref, wp_ref)


def _out_ple(h, oa, ob, gates, p, layer, wo, npl, wg, wp, tm):
    B, S, _ = h.shape
    tok = lambda b, i: (b, i, 0)
    const2 = lambda shp: pl.BlockSpec(shp, lambda b, i: (0, 0))
    return pl.pallas_call(
        _out_ple_kernel,
        grid=(B, S // tm),
        in_specs=[
            pl.BlockSpec((1, tm, D_MODEL), tok),
            pl.BlockSpec((1, tm, 512), tok),
            pl.BlockSpec((1, tm, 512), tok),
            pl.BlockSpec((1, tm, 1024), tok),
            pl.BlockSpec((1, 1, tm, PLE_DIM), lambda b, i: (layer, b, i, 0)),
            const2((1024, D_MODEL)), const2((1, D_MODEL)), const2((D_MODEL, D_MODEL)),
            const2((PLE_DIM, D_MODEL)),
        ],
        out_specs=pl.BlockSpec((1, tm, D_MODEL), tok),
        out_shape=jax.ShapeDtypeStruct((B, S, D_MODEL), F32),
        compiler_params=pltpu.CompilerParams(
            dimension_semantics=("parallel", "parallel"), vmem_limit_bytes=VMEM_LIMIT_BYTES),
        name="out_ple",
    )(h, oa, ob, gates, p, wo, npl, wg, wp)


def _layer_c_kernel(h_ref, hp_ref, hn_ref, p_ref, nm_ref, wi_ref, wgrp_ref, sc_ref, wo_ref,
                    npl_ref, wg_ref, wp_ref, out_ref, *, tm, S):
    i = pl.program_id(1)
    nt = pl.num_programs(1)
    h = h_ref[0]
    hp = hp_ref[0] * (i > 0).astype(F32)
    hn = hn_ref[0] * (i < nt - 1).astype(F32)
    n = tm + 2 * POOL_HALO
    h_ext = jnp.concatenate([hp, h, hn], axis=0)
    xn = _rms_rows(h_ext, nm_ref[...]).astype(BF16)
    u_ext = jnp.dot(xn, wi_ref[:, 0:1024], preferred_element_type=F32)
    g = jnp.dot(xn[POOL_HALO:POOL_HALO + tm], wi_ref[:, 1024:2048], preferred_element_type=F32)

    t = i * tm + lax.broadcasted_iota(jnp.int32, (tm, C_GRP), 0)
    mixed = []
    for gi, w in enumerate(POOL_WINDOWS):
        u = u_ext[:, C_GRP * gi:C_GRP * (gi + 1)]
        win = u + pltpu.roll(u, 1, 0)
        half = 1
        while 2 * half < w:
            win = pltpu.roll(win, half, 0) + pltpu.roll(win, n - half, 0)
            half *= 2
        cnt = (jnp.minimum(t + w // 2, S) - jnp.maximum(t - w // 2, 0)).astype(F32)
        pooled = win[POOL_HALO:POOL_HALO + tm] / cnt - u[POOL_HALO:POOL_HALO + tm]
        mixed.append(jnp.dot(pooled.astype(BF16), wgrp_ref[gi], preferred_element_type=F32))
    mixed = jnp.concatenate(mixed, axis=1)
    y = ((mixed * sc_ref[...]) * (g * jax.nn.sigmoid(g))).astype(BF16)
    h1 = h + jnp.dot(y, wo_ref[...], preferred_element_type=F32)
    out_ref[0] = _ple_tail(h1, p_ref[0, 0], npl_ref[...], wg_ref, wp_ref)


def _layer_c(h, p, layer, nm, wi, wgrp, sc, wo, npl, wg, wp, tm):
    B, S, _ = h.shape
    tok = lambda b, i: (b, i, 0)
    const2 = lambda shp: pl.BlockSpec(shp, lambda b, i: (0, 0))
    per = tm // POOL_HALO
    last = S // POOL_HALO - 1
    kern = functools.partial(_layer_c_kernel, tm=tm, S=S)
    return pl.pallas_call(
        kern,
        grid=(B, S // tm),
        in_specs=[
            pl.BlockSpec((1, tm, D_MODEL), tok),
            pl.BlockSpec((1, POOL_HALO, D_MODEL), lambda b, i: (b, jnp.maximum(i * per - 1, 0), 0)),
            pl.BlockSpec((1, POOL_HALO, D_MODEL), lambda b, i: (b, jnp.minimum((i + 1) * per, last), 0)),
            pl.BlockSpec((1, 1, tm, PLE_DIM), lambda b, i: (layer, b, i, 0)),
            const2((1, D_MODEL)), const2((D_MODEL, 2048)),
            pl.BlockSpec((4, C_GRP, C_GRP), lambda b, i: (0, 0, 0)),
            const2((1, 1024)), const2((1024, D_MODEL)), const2((1, D_MODEL)),
            const2((D_MODEL, D_MODEL)), const2((PLE_DIM, D_MODEL)),
        ],
        out_specs=pl.BlockSpec((1, tm, D_MODEL), tok),
        out_shape=jax.ShapeDtypeStruct((B, S, D_MODEL), F32),
        compiler_params=pltpu.CompilerParams(
            dimension_semantics=("parallel", "parallel"), vmem_limit_bytes=VMEM_LIMIT_BYTES),
        name="layer_c",
    )(h, h, h, p, nm, wi, wgrp, sc, wo, npl, wg, wp)


def _score_bound(gq, gk):
    slack = 1.02
    return (HEAD_DIM * SM_SCALE * LOG2E * slack) * jnp.max(jnp.abs(gq)) * jnp.max(jnp.abs(gk))


def _tile_gain(g, reps, scale=1.0):
    return jnp.tile(g.astype(F32) * scale, reps).reshape(1, -1)


def _trunk(x, p, prm, tq_a, tq_b, tk, tm):
    row = lambda v: v.astype(F32).reshape(1, -1)
    lam_init = 0.8 - 0.6 * math.exp(-0.3 * 0)
    gq_scale = SM_SCALE * LOG2E
    gqa = _tile_gain(prm["qn_a"][0], 4, gq_scale)
    gkv = jnp.concatenate([_tile_gain(prm["kn_a"][0], 2), jnp.ones((1, 128), F32)], axis=1)
    gqb = _tile_gain(prm["qn_b"][0], 4, gq_scale)
    gkb = _tile_gain(prm["kn_b"][0], 4)
    qa, ka, va, qb, kb, vb, gates = _proj_ab(
        x, prm["w_in_ab"][0], row(prm["norm_mix"][0]), gqa, gkv, gqb, gkb, prm["bd"], tm)
    oa = lax.cond(
        _score_bound(prm["qn_a"][0], prm["kn_a"][0]) <= BOUNDED_SCORE_LIMIT,
        lambda q, k, v: _flash_a(q, k, v, tq_a, tk, True),
        lambda q, k, v: _flash_a(q, k, v, tq_a, tk, False),
        qa, ka, va)
    ob = _flash_b(qb, kb, vb, prm["slopes"], row(prm["lam_q1"][0]), row(prm["lam_k1"][0]),
                  row(prm["lam_q2"][0]), row(prm["lam_k2"][0]), row(prm["subln_b"][0]),
                  lam_init, tq_b, tk)
    h = _out_ple(x, oa, ob, gates, p, 0, prm["w_out_ab"][0], row(prm["norm_ple"][0]),
                 prm["w_ple_gate"][0], prm["w_ple_proj"][0], tm)
    h = _layer_c(h, p, 1, row(prm["norm_mix"][1]), prm["w_in_c"][0], prm["w_grp_c"][0],
                 row(prm["scale_c"][0]), prm["w_out_c"][0], row(prm["norm_ple"][1]),
                 prm["w_ple_gate"][1], prm["w_ple_proj"][1], tm)
    return h


def kernel(x_prompt, x_sample, p_prompt, p_sample, norm_mix, w_in_ab, qn_a, kn_a, qn_b, kn_b,
           lam_q1, lam_k1, lam_q2, lam_k2, subln_b, w_out_ab, w_in_c, w_grp_c, scale_c, w_out_c,
           norm_ple, w_ple_gate, w_ple_proj):
    blk = np.arange(MXU_DIM) // HEAD_DIM
    slopes = np.asarray(ALIBI_SLOPES, np.float32) * LOG2E
    prm = dict(
        norm_mix=norm_mix, qn_a=qn_a, kn_a=kn_a, qn_b=qn_b, kn_b=kn_b,
        lam_q1=lam_q1, lam_k1=lam_k1, lam_q2=lam_q2, lam_k2=lam_k2, subln_b=subln_b,
        scale_c=scale_c, norm_ple=norm_ple,
        w_in_ab=w_in_ab.astype(BF16), w_out_ab=w_out_ab.astype(BF16), w_in_c=w_in_c.astype(BF16),
        w_grp_c=w_grp_c.astype(BF16), w_out_c=w_out_c.astype(BF16),
        w_ple_gate=w_ple_gate.astype(BF16), w_ple_proj=w_ple_proj.astype(BF16),
        bd=jnp.asarray(blk[:, None] == blk[None, :], BF16),
        slopes=jnp.asarray(np.broadcast_to(slopes[:, None, None], (B_HEADS, 8, LANES))),
    )
    y_prompt = _trunk(x_prompt, p_prompt, prm, tq_a=256, tq_b=512, tk=512, tm=512)
    y_sample = _trunk(x_sample, p_sample, prm, tq_a=256, tq_b=512, tk=512, tm=512)
    return (y_prompt, y_sample)
```

```python
import functools
import math

import jax
import jax.numpy as jnp
import numpy as np
from jax import lax
from jax.experimental import pallas as pl
from jax.experimental.pallas import tpu as pltpu

F32 = jnp.float32
BF16 = jnp.bfloat16

D_MODEL = 1024
PLE_DIM = 256
GRID_W = 64
HEAD_DIM = 64
EPS = 1e-6
ROPE_THETA = 10000.0
A_HEADS = 8
A_KV_HEADS = 2
B_HEADS = 4
B_V_DIM = 2 * HEAD_DIM
ALIBI_SLOPES = tuple(2.0 ** (-8.0 * (h + 1) / B_HEADS) for h in range(B_HEADS))
POOL_WINDOWS = (2, 4, 8, 16)
C_GRP = 256
POOL_HALO = 8
LOG2E = math.log2(math.e)
SM_SCALE = HEAD_DIM ** -0.5

LANES = 128
MXU_DIM = 256
VMEM_LIMIT_BYTES = 56 * 1024 * 1024

_QA, _KA, _VA, _GA, _QB, _KB, _VB, _GB, _AB_END = 0, 512, 640, 768, 1280, 1792, 2304, 2816, 3328

NEG_BIG = -1e30
BOUNDED_SCORE_LIMIT = 60.0


def _rms_rows(x, g):
    ms = jnp.mean(x * x, axis=-1, keepdims=True)
    return x * lax.rsqrt(ms + EPS) * g


def _nt_dot(a, b):
    return lax.dot_general(a, b, (((1,), (1,)), ((), ())), preferred_element_type=F32)


def _proj_ab_kernel(x_ref, cos_ref, sa_ref, sb_ref, nm_ref, w_ref, gqa_ref, gkv_ref, gqb_ref,
                    gkb_ref, bd_ref, kaug_ref,
                    qa_ref, ka_ref, va_ref, qb_ref, kb_ref, vb_ref, g_ref):
    x = x_ref[0]
    xn = _rms_rows(x, nm_ref[...]).astype(BF16)
    tm = x.shape[0]

    def seg(a, b):
        return jnp.dot(xn, w_ref[:, a:b], preferred_element_type=F32)

    def head_norm(z, gain):
        ss = jnp.dot((z * z).astype(BF16), bd_ref[...], preferred_element_type=F32)
        return z * lax.rsqrt(ss * (1.0 / HEAD_DIM) + EPS) * gain

    cos = cos_ref[...]
    sin_a = sa_ref[...]
    sin_b = sb_ref[...]

    def rope(z):
        return (z * cos + pltpu.roll(z, LANES - 16, 1) * sin_a + pltpu.roll(z, 16, 1) * sin_b)

    lane = lax.broadcasted_iota(jnp.int32, (tm, LANES), 1)
    lo = lane < HEAD_DIM
    t = pl.program_id(1) * tm + lax.broadcasted_iota(jnp.int32, (tm, LANES), 0)
    pos_hi = (t >> 7).astype(F32)
    pos_lo = (t & 127).astype(F32)

    for j in range(2):
        z = head_norm(seg(_QA + 256 * j, _QA + 256 * (j + 1)), gqa_ref[...])
        for p in range(2):
            c0 = 256 * j + 128 * p
            qa_ref[0, :, c0:c0 + 128] = rope(z[:, 128 * p:128 * (p + 1)]).astype(BF16)

    z = seg(_KA, _GA)
    zn = head_norm(z, gkv_ref[...])
    k = rope(zn[:, :128])
    k_sw = pltpu.roll(k, HEAD_DIM, 1)
    zero = jnp.zeros_like(k)
    ka_ref[0, 0, 0] = jnp.where(lo, k, zero).astype(BF16)
    ka_ref[0, 0, 1] = jnp.where(lo, zero, k_sw).astype(BF16)
    ka_ref[0, 1, 0] = jnp.where(lo, k_sw, zero).astype(BF16)
    ka_ref[0, 1, 1] = jnp.where(lo, zero, k).astype(BF16)
    v = z[:, 128:256]
    v_sw = pltpu.roll(v, HEAD_DIM, 1)
    one_a = (lane == HEAD_DIM).astype(F32)
    va_ref[0, 0] = jnp.where(lo, v, one_a).astype(BF16)
    va_ref[0, 1] = jnp.where(lo, v_sw, one_a).astype(BF16)

    for j in range(2):
        g = seg(_GA + 256 * j, _GA + 256 * (j + 1))
        g_ref[0, :, 256 * j:256 * (j + 1)] = g * jax.nn.sigmoid(g)
        g = seg(_GB + 256 * j, _GB + 256 * (j + 1))
        g_ref[0, :, 512 + 256 * j:512 + 256 * (j + 1)] = g * jax.nn.sigmoid(g)

    for j in range(2):
        z = head_norm(seg(_QB + 256 * j, _QB + 256 * (j + 1)), gqb_ref[...])
        qb_ref[0, :, 256 * j:256 * (j + 1)] = z.astype(BF16)
        z = head_norm(seg(_KB + 256 * j, _KB + 256 * (j + 1)), gkb_ref[...])
        for p in range(2):
            kh = z[:, 128 * p:128 * (p + 1)]
            hd = 2 * j + p
            for mp in range(2):
                aug = (kaug_ref[2 * hd + mp:2 * hd + mp + 1, :] + pos_hi * kaug_ref[8 + mp:9 + mp, :]
                       + pos_lo * kaug_ref[10 + mp:11 + mp, :])
                keep = lo if mp == 0 else jnp.logical_not(lo)
                kb_ref[0, hd, mp] = jnp.where(keep, kh, aug).astype(BF16)

    one_b = (lane == 0).astype(BF16)
    for j in range(2):
        z = seg(_VB + 256 * j, _VB + 256 * (j + 1))
        for p in range(2):
            vb_ref[0, 2 * j + p, :, 0:128] = z[:, 128 * p:128 * (p + 1)].astype(BF16)
            vb_ref[0, 2 * j + p, :, 128:256] = one_b


def _rope_tables(S):
    rows = S // GRID_W
    row = jnp.repeat(jnp.arange(rows), GRID_W).astype(F32)
    col = jnp.tile(jnp.arange(GRID_W), rows).astype(F32)
    half = HEAD_DIM // 2
    inv = ROPE_THETA ** (-jnp.arange(0, half, 2, dtype=F32) / half)
    ar = row[:, None] * inv
    ac = col[:, None] * inv
    ang = jnp.concatenate([ar, ar, ac, ac], axis=-1)
    cos, sin = jnp.cos(ang), jnp.sin(ang)
    first = (np.arange(HEAD_DIM) % 32) < 16
    sin_a = jnp.where(first, -sin, 0.0)
    sin_b = jnp.where(first, 0.0, sin)
    two = lambda t: jnp.concatenate([t, t], axis=-1)
    return two(cos), two(sin_a), two(sin_b)


def _proj_ab(x, w_bf, nm, gqa, gkv, gqb, gkb, bd, kaug, tm):
    B, S, _ = x.shape
    cos, sin_a, sin_b = _rope_tables(S)
    tok = lambda b, i: (b, i, 0)
    tab = pl.BlockSpec((tm, LANES), lambda b, i: (i, 0))
    const2 = lambda shp: pl.BlockSpec(shp, lambda b, i: (0, 0))
    out_shape = (
        jax.ShapeDtypeStruct((B, S, 512), BF16),
        jax.ShapeDtypeStruct((B, 2, 2, S, LANES), BF16),
        jax.ShapeDtypeStruct((B, 2, S, LANES), BF16),
        jax.ShapeDtypeStruct((B, S, 512), BF16),
        jax.ShapeDtypeStruct((B, 4, 2, S, LANES), BF16),
        jax.ShapeDtypeStruct((B, 4, S, 2 * LANES), BF16),
        jax.ShapeDtypeStruct((B, S, 1024), F32),
    )
    out_specs = (
        pl.BlockSpec((1, tm, 512), tok),
        pl.BlockSpec((1, 2, 2, tm, LANES), lambda b, i: (b, 0, 0, i, 0)),
        pl.BlockSpec((1, 2, tm, LANES), lambda b, i: (b, 0, i, 0)),
        pl.BlockSpec((1, tm, 512), tok),
        pl.BlockSpec((1, 4, 2, tm, LANES), lambda b, i: (b, 0, 0, i, 0)),
        pl.BlockSpec((1, 4, tm, 2 * LANES), lambda b, i: (b, 0, i, 0)),
        pl.BlockSpec((1, tm, 1024), tok),
    )
    return pl.pallas_call(
        _proj_ab_kernel,
        grid=(B, S // tm),
        in_specs=[
            pl.BlockSpec((1, tm, D_MODEL), tok), tab, tab, tab,
            const2((1, D_MODEL)), const2((D_MODEL, _AB_END)),
            const2((1, 256)), const2((1, 256)), const2((1, 256)), const2((1, 256)),
            const2((MXU_DIM, MXU_DIM)), const2((16, LANES)),
        ],
        out_specs=out_specs,
        out_shape=out_shape,
        compiler_params=pltpu.CompilerParams(
            dimension_semantics=("parallel", "parallel"), vmem_limit_bytes=VMEM_LIMIT_BYTES),
        name="proj_ab",
    )(x, cos, sin_a, sin_b, nm, w_bf, gqa, gkv, gqb, gkb, bd, kaug)


def _online_update(s, v, m_ref, acc_ref, r0, nrows):
    rows = pl.ds(r0, nrows)
    m_old = m_ref[rows, :]
    m_new = jnp.maximum(m_old, jnp.max(s, axis=-1, keepdims=True))
    alpha = jnp.exp2(m_old - m_new)
    p = jnp.exp2(s - m_new[:, :1]).astype(BF16)
    pv = jnp.dot(p, v, preferred_element_type=F32)
    nv = v.shape[1]
    if nv != LANES:
        alpha = jnp.concatenate([alpha] * (nv // LANES), axis=1)
    acc_ref[rows, :] = acc_ref[rows, :] * alpha + pv
    m_ref[rows, :] = m_new


def _flash_a_kernel(q_ref, k_ref, v_ref, o_ref, q_scr, m_ref, acc_ref, *, tq, tk, n_kv):
    q_scr[0:tq, :] = q_ref[0, :, 0:128]
    q_scr[tq:2 * tq, :] = q_ref[0, :, 128:256]
    m_ref[...] = jnp.full(m_ref.shape, NEG_BIG, F32)
    acc_ref[...] = jnp.zeros(acc_ref.shape, F32)

    def body(c, carry):
        ks = pl.ds(pl.multiple_of(c * tk, tk), tk)
        q = q_scr[...]
        v = v_ref[0, 0, ks, :]
        for var in range(2):
            s = _nt_dot(q, k_ref[0, 0, var, ks, :])
            _online_update(s, v, m_ref, acc_ref, var * 2 * tq, 2 * tq)
        return carry

    lax.fori_loop(0, n_kv, body, 0)
    _flash_a_finalize(o_ref, acc_ref, tq)


def _flash_a_finalize(o_ref, acc_ref, tq):
    lane = lax.broadcasted_iota(jnp.int32, (tq, LANES), 1)
    lo = lane < HEAD_DIM

    def head_out(blk):
        a = acc_ref[blk * tq:(blk + 1) * tq, :]
        return a * (1.0 / a[:, HEAD_DIM:HEAD_DIM + 1])

    o_ref[0, :, 0:128] = jnp.where(lo, head_out(0), pltpu.roll(head_out(2), HEAD_DIM, 1))
    o_ref[0, :, 128:256] = jnp.where(lo, head_out(1), pltpu.roll(head_out(3), HEAD_DIM, 1))


def _flash_a_bounded_kernel(q_ref, k_ref, v_ref, o_ref, q_scr, acc_ref, *, tq, tk, n_kv, unroll):
    q_scr[0:tq, :] = q_ref[0, :, 0:128]
    q_scr[tq:2 * tq, :] = q_ref[0, :, 128:256]
    acc_ref[...] = jnp.zeros(acc_ref.shape, F32)

    def body(c, carry):
        ks = pl.ds(pl.multiple_of(c * tk, tk), tk)
        q = q_scr[...]
        v = v_ref[0, 0, ks, :]
        for var in range(2):
            p = jnp.exp2(_nt_dot(q, k_ref[0, 0, var, ks, :])).astype(BF16)
            rows = pl.ds(var * 2 * tq, 2 * tq)
            acc_ref[rows, :] += jnp.dot(p, v, preferred_element_type=F32)
        return carry

    lax.fori_loop(0, n_kv, body, 0, unroll=unroll)
    _flash_a_finalize(o_ref, acc_ref, tq)


def _kv_spec(block, index_map, single_buffer):
    if single_buffer:
        return pl.BlockSpec(block, index_map, pipeline_mode=pl.Buffered(1))
    return pl.BlockSpec(block, index_map)


def _flash_a(qa, ka, va, tq, tk, bounded):
    B, S, _ = qa.shape
    single = S * LANES * 2 * 3 * 2 > 8 * 1024 * 1024
    scratch = [pltpu.VMEM((2 * tq, LANES), BF16), pltpu.VMEM((4 * tq, LANES), F32)]
    if bounded:
        kern = functools.partial(_flash_a_bounded_kernel, tq=tq, tk=tk, n_kv=S // tk, unroll=2)
    else:
        kern = functools.partial(_flash_a_kernel, tq=tq, tk=tk, n_kv=S // tk)
        scratch.insert(1, pltpu.VMEM((4 * tq, LANES), F32))
    return pl.pallas_call(
        kern,
        grid=(B, A_KV_HEADS, S // tq),
        in_specs=[
            pl.BlockSpec((1, tq, 256), lambda b, g, i: (b, i, g)),
            _kv_spec((1, 1, 2, S, LANES), lambda b, g, i: (b, g, 0, 0, 0), single),
            _kv_spec((1, 1, S, LANES), lambda b, g, i: (b, g, 0, 0), single),
        ],
        out_specs=pl.BlockSpec((1, tq, 256), lambda b, g, i: (b, i, g)),
        out_shape=jax.ShapeDtypeStruct((B, S, 512), F32),
        scratch_shapes=scratch,
        compiler_params=pltpu.CompilerParams(
            dimension_semantics=("parallel", "parallel", "arbitrary"),
            vmem_limit_bytes=VMEM_LIMIT_BYTES),
        name="flash_a_bounded" if bounded else "flash_a",
    )(qa, ka, va)


def _flash_b_q_slots(q_ref, tab_ref, q_scr, tq, with_alibi):
    lane = lax.broadcasted_iota(jnp.int32, (tq, LANES), 1)
    lo = lane < HEAD_DIM
    slab = q_ref[0]
    if with_alibi:
        t = pl.program_id(2) * tq + lax.broadcasted_iota(jnp.int32, (tq, LANES), 0)
        pos_hi = (t >> 7).astype(F32)
        pos_lo = (t & 127).astype(F32)
    for mp in range(2):
        keep = lo if mp == 0 else jnp.logical_not(lo)
        if with_alibi:
            aug = (tab_ref[0, mp:mp + 1, :] + pos_hi * tab_ref[0, 2 + mp:3 + mp, :]
                   + pos_lo * tab_ref[0, 4 + mp:5 + mp, :])
            q_scr[mp, 0] = jnp.where(keep, slab, aug.astype(BF16))
            q_scr[mp, 1] = jnp.where(keep, slab, (-aug).astype(BF16))
        else:
            q_scr[mp, 0] = jnp.where(keep, slab, jnp.zeros_like(slab))


def _flash_b_finalize(o_ref, acc_ref, lq1_ref, lk1_ref, lq2_ref, lk2_ref, sub_ref, tq, lam_init):
    lam = (jnp.exp(jnp.sum(lq1_ref[...] * lk1_ref[...], keepdims=True))
           - jnp.exp(jnp.sum(lq2_ref[...] * lk2_ref[...], keepdims=True)) + lam_init)
    a1 = acc_ref[0:tq, :]
    a2 = acc_ref[tq:2 * tq, :]
    o1 = a1[:, 0:B_V_DIM] * (1.0 / a1[:, B_V_DIM:B_V_DIM + 1])
    o2 = a2[:, 0:B_V_DIM] * (1.0 / a2[:, B_V_DIM:B_V_DIM + 1])
    o = o1 - lam * o2
    o_ref[0] = _rms_rows(o, sub_ref[...]) * (1.0 - lam_init)


def _flash_b_kernel(q_ref, k_ref, v_ref, tab_ref, lq1_ref, lk1_ref, lq2_ref, lk2_ref, sub_ref,
                    o_ref, q_scr, rel_ref, m_ref, acc_ref, *, tq, tk, n_kv, lam_init):
    _flash_b_q_slots(q_ref, tab_ref, q_scr, tq, with_alibi=False)
    m_ref[...] = jnp.full(m_ref.shape, NEG_BIG, F32)
    acc_ref[...] = jnp.zeros(acc_ref.shape, F32)
    rel_ref[...] = (lax.broadcasted_iota(jnp.int32, (tq, tk), 0)
                    - lax.broadcasted_iota(jnp.int32, (tq, tk), 1)).astype(F32)
    neg_slope = -tab_ref[0, 6:7, 0:1]
    q0 = pl.program_id(2) * tq

    def body(c, carry):
        k0 = pl.multiple_of(c * tk, tk)
        ks = pl.ds(k0, tk)
        v = v_ref[0, 0, ks, :]
        bias = jnp.abs(rel_ref[...] + (q0 - k0).astype(F32)) * neg_slope
        for mp in range(2):
            s = _nt_dot(q_scr[mp, 0], k_ref[0, 0, mp, ks, :]) + bias
            _online_update(s, v, m_ref, acc_ref, mp * tq, tq)
        return carry

    lax.fori_loop(0, n_kv, body, 0)
    _flash_b_finalize(o_ref, acc_ref, lq1_ref, lk1_ref, lq2_ref, lk2_ref, sub_ref, tq, lam_init)


def _flash_b_bounded_kernel(q_ref, k_ref, v_ref, tab_ref, lq1_ref, lk1_ref, lq2_ref, lk2_ref, sub_ref,
                            o_ref, q_scr, corr_ref, acc_ref, *, tq, tk, n_kv, lam_init, unroll):
    _flash_b_q_slots(q_ref, tab_ref, q_scr, tq, with_alibi=True)
    acc_ref[...] = jnp.zeros(acc_ref.shape, F32)
    ahead = jnp.maximum(lax.broadcasted_iota(jnp.int32, (tq, tk), 1)
                        - lax.broadcasted_iota(jnp.int32, (tq, tk), 0), 0).astype(F32)
    corr_ref[0] = jnp.zeros((tq, tk), F32)
    corr_ref[1] = ahead * (-2.0 * tab_ref[0, 6:7, 0:1])
    qi = pl.program_id(2)

    def body(c, carry):
        ks = pl.ds(pl.multiple_of(c * tk, tk), tk)
        v = v_ref[0, 0, ks, :]
        side = (c > qi).astype(jnp.int32)
        corr = corr_ref[(c == qi).astype(jnp.int32)]
        for mp in range(2):
            s = _nt_dot(q_scr[mp, side], k_ref[0, 0, mp, ks, :]) + corr
            rows = pl.ds(mp * tq, tq)
            acc_ref[rows, :] += jnp.dot(jnp.exp2(s).astype(BF16), v, preferred_element_type=F32)
        return carry

    lax.fori_loop(0, n_kv, body, 0, unroll=unroll)
    _flash_b_finalize(o_ref, acc_ref, lq1_ref, lk1_ref, lq2_ref, lk2_ref, sub_ref, tq, lam_init)


def _flash_b(qb, kb, vb, tab, lq1, lk1, lq2, lk2, subln, lam_init, tq, tk, bounded):
    B, S, _ = qb.shape
    assert tq == tk
    single = S * LANES * 2 * 4 * 2 > 8 * 1024 * 1024
    acc = pltpu.VMEM((2 * tq, 2 * LANES), F32)
    if bounded:
        kern = functools.partial(_flash_b_bounded_kernel, tq=tq, tk=tk, n_kv=S // tk,
                                 lam_init=lam_init, unroll=2)
        scratch = [pltpu.VMEM((2, 2, tq, LANES), BF16), pltpu.VMEM((2, tq, tk), F32), acc]
    else:
        kern = functools.partial(_flash_b_kernel, tq=tq, tk=tk, n_kv=S // tk, lam_init=lam_init)
        scratch = [pltpu.VMEM((2, 1, tq, LANES), BF16), pltpu.VMEM((tq, tk), F32),
                   pltpu.VMEM((2 * tq, LANES), F32), acc]
    vec = lambda n: pl.BlockSpec((1, n), lambda b, h, i: (0, 0))
    return pl.pallas_call(
        kern,
        grid=(B, B_HEADS, S // tq),
        in_specs=[
            pl.BlockSpec((1, tq, LANES), lambda b, h, i: (b, i, h)),
            _kv_spec((1, 1, 2, S, LANES), lambda b, h, i: (b, h, 0, 0, 0), single),
            _kv_spec((1, 1, S, 2 * LANES), lambda b, h, i: (b, h, 0, 0), single),
            pl.BlockSpec((1, 8, LANES), lambda b, h, i: (h, 0, 0)),
            vec(HEAD_DIM), vec(HEAD_DIM), vec(HEAD_DIM), vec(HEAD_DIM), vec(B_V_DIM),
        ],
        out_specs=pl.BlockSpec((1, tq, LANES), lambda b, h, i: (b, i, h)),
        out_shape=jax.ShapeDtypeStruct((B, S, 512), F32),
        scratch_shapes=scratch,
        compiler_params=pltpu.CompilerParams(
            dimension_semantics=("parallel", "parallel", "arbitrary"),
            vmem_limit_bytes=VMEM_LIMIT_BYTES),
        name="flash_b_bounded" if bounded else "flash_b",
    )(qb, kb, vb, tab, lq1, lk1, lq2, lk2, subln)


def _ple_tail(h1, p, npl, wg_ref, wp_ref):
    rn = _rms_rows(h1, npl).astype(BF16)
    gate = jax.nn.sigmoid(jnp.dot(rn, wg_ref[...], preferred_element_type=F32))
    return h1 + gate * jnp.dot(p.astype(BF16), wp_ref[...], preferred_element_type=F32)


def _out_ple_kernel(h_ref, oa_ref, ob_ref, g_ref, p_ref, wo_ref, npl_ref, wg_ref, wp_ref, out_ref):
    g = g_ref[0]
    y = jnp.concatenate([oa_ref[0] * g[:, 0:512], ob_ref[0] * g[:, 512:1024]], axis=1).astype(BF16)
    h1 = h_ref[0] + jnp.dot(y, wo_ref[...], preferred_element_type=F32)
    out_ref[0] = _ple_tail(h1, p_ref[0, 0], npl_ref[...], wg_ref, wp_ref)


def _out_ple(h, oa, ob, gates, p, layer, wo, npl, wg, wp, tm):
    B, S, _ = h.shape
    tok = lambda b, i: (b, i, 0)
    const2 = lambda shp: pl.BlockSpec(shp, lambda b, i: (0, 0))
    return pl.pallas_call(
        _out_ple_kernel,
        grid=(B, S // tm),
        in_specs=[
            pl.BlockSpec((1, tm, D_MODEL), tok),
            pl.BlockSpec((1, tm, 512), tok),
            pl.BlockSpec((1, tm, 512), tok),
            pl.BlockSpec((1, tm, 1024), tok),
            pl.BlockSpec((1, 1, tm, PLE_DIM), lambda b, i: (layer, b, i, 0)),
            const2((1024, D_MODEL)), const2((1, D_MODEL)), const2((D_MODEL, D_MODEL)),
            const2((PLE_DIM, D_MODEL)),
        ],
        out_specs=pl.BlockSpec((1, tm, D_MODEL), tok),
        out_shape=jax.ShapeDtypeStruct((B, S, D_MODEL), F32),
        compiler_params=pltpu.CompilerParams(
            dimension_semantics=("parallel", "parallel"), vmem_limit_bytes=VMEM_LIMIT_BYTES),
        name="out_ple",
    )(h, oa, ob, gates, p, wo, npl, wg, wp)


def _layer_c_kernel(h_ref, hp_ref, hn_ref, p_ref, nm_ref, wi_ref, wgrp_ref, sc_ref, wo_ref,
                    npl_ref, wg_ref, wp_ref, out_ref, *, tm, S):
    i = pl.program_id(1)
    nt = pl.num_programs(1)
    h = h_ref[0]
    hp = hp_ref[0] * (i > 0).astype(F32)
    hn = hn_ref[0] * (i < nt - 1).astype(F32)
    n = tm + 2 * POOL_HALO
    h_ext = jnp.concatenate([hp, h, hn], axis=0)
    xn = _rms_rows(h_ext, nm_ref[...]).astype(BF16)
    u_ext = jnp.dot(xn, wi_ref[:, 0:1024], preferred_element_type=F32)
    g = jnp.dot(xn[POOL_HALO:POOL_HALO + tm], wi_ref[:, 1024:2048], preferred_element_type=F32)

    t = i * tm + lax.broadcasted_iota(jnp.int32, (tm, C_GRP), 0)
    mixed = []
    for gi, w in enumerate(POOL_WINDOWS):
        u = u_ext[:, C_GRP * gi:C_GRP * (gi + 1)]
        win = u + pltpu.roll(u, 1, 0)
        half = 1
        while 2 * half < w:
            win = pltpu.roll(win, half, 0) + pltpu.roll(win, n - half, 0)
            half *= 2
        cnt = (jnp.minimum(t + w // 2, S) - jnp.maximum(t - w // 2, 0)).astype(F32)
        pooled = win[POOL_HALO:POOL_HALO + tm] / cnt - u[POOL_HALO:POOL_HALO + tm]
        mixed.append(jnp.dot(pooled.astype(BF16), wgrp_ref[gi], preferred_element_type=F32))
    mixed = jnp.concatenate(mixed, axis=1)
    y = ((mixed * sc_ref[...]) * (g * jax.nn.sigmoid(g))).astype(BF16)
    h1 = h + jnp.dot(y, wo_ref[...], preferred_element_type=F32)
    out_ref[0] = _ple_tail(h1, p_ref[0, 0], npl_ref[...], wg_ref, wp_ref)


def _layer_c(h, p, layer, nm, wi, wgrp, sc, wo, npl, wg, wp, tm):
    B, S, _ = h.shape
    tok = lambda b, i: (b, i, 0)
    const2 = lambda shp: pl.BlockSpec(shp, lambda b, i: (0, 0))
    per = tm // POOL_HALO
    last = S // POOL_HALO - 1
    kern = functools.partial(_layer_c_kernel, tm=tm, S=S)
    return pl.pallas_call(
        kern,
        grid=(B, S // tm),
        in_specs=[
            pl.BlockSpec((1, tm, D_MODEL), tok),
            pl.BlockSpec((1, POOL_HALO, D_MODEL), lambda b, i: (b, jnp.maximum(i * per - 1, 0), 0)),
            pl.BlockSpec((1, POOL_HALO, D_MODEL), lambda b, i: (b, jnp.minimum((i + 1) * per, last), 0)),
            pl.BlockSpec((1, 1, tm, PLE_DIM), lambda b, i: (layer, b, i, 0)),
            const2((1, D_MODEL)), const2((D_MODEL, 2048)),
            pl.BlockSpec((4, C_GRP, C_GRP), lambda b, i: (0, 0, 0)),
            const2((1, 1024)), const2((1024, D_MODEL)), const2((1, D_MODEL)),
            const2((D_MODEL, D_MODEL)), const2((PLE_DIM, D_MODEL)),
        ],
        out_specs=pl.BlockSpec((1, tm, D_MODEL), tok),
        out_shape=jax.ShapeDtypeStruct((B, S, D_MODEL), F32),
        compiler_params=pltpu.CompilerParams(
            dimension_semantics=("parallel", "parallel"), vmem_limit_bytes=VMEM_LIMIT_BYTES),
        name="layer_c",
    )(h, h, h, p, nm, wi, wgrp, sc, wo, npl, wg, wp)


def _score_bound(gq, gk):
    slack = 1.02
    return (HEAD_DIM * SM_SCALE * LOG2E * slack) * jnp.max(jnp.abs(gq)) * jnp.max(jnp.abs(gk))


def _alibi_tables():
    kaug = np.zeros((16, LANES), np.float32)
    qtab = np.zeros((B_HEADS, 8, LANES), np.float32)
    for mp in range(2):
        base = HEAD_DIM if mp == 0 else 0
        kaug[8 + mp, base + 6:base + 9] = 128.0
        kaug[10 + mp, base + 9:base + 12] = 1.0
        qtab[:, 2 + mp, base:base + 3] = -128.0
        qtab[:, 4 + mp, base + 3:base + 6] = -1.0
    for h, slope in enumerate(ALIBI_SLOPES):
        c = np.float64(slope) * LOG2E
        pieces, rest = [], c
        for _ in range(3):
            piece = np.float64(np.float32(rest).astype(BF16))
            pieces.append(piece)
            rest -= piece
        for mp in range(2):
            base = HEAD_DIM if mp == 0 else 0
            kaug[2 * h + mp, base:base + 6] = pieces + pieces
            qtab[h, mp, base + 6:base + 12] = pieces + pieces
        qtab[h, 6, :] = np.float32(c)
    return jnp.asarray(kaug), jnp.asarray(qtab)


def _tile_gain(g, reps, scale=1.0):
    return jnp.tile(g.astype(F32) * scale, reps).reshape(1, -1)


def _trunk(x, p, prm, tq_a, tq_b, tk, tm):
    row = lambda v: v.astype(F32).reshape(1, -1)
    lam_init = 0.8 - 0.6 * math.exp(-0.3 * 0)
    gq_scale = SM_SCALE * LOG2E
    gqa = _tile_gain(prm["qn_a"][0], 4, gq_scale)
    gkv = jnp.concatenate([_tile_gain(prm["kn_a"][0], 2), jnp.ones((1, 128), F32)], axis=1)
    gqb = _tile_gain(prm["qn_b"][0], 4, gq_scale)
    gkb = _tile_gain(prm["kn_b"][0], 4)
    qa, ka, va, qb, kb, vb, gates = _proj_ab(
        x, prm["w_in_ab"][0], row(prm["norm_mix"][0]), gqa, gkv, gqb, gkb, prm["bd"], prm["kaug"], tm)
    oa = lax.cond(
        _score_bound(prm["qn_a"][0], prm["kn_a"][0]) <= BOUNDED_SCORE_LIMIT,
        lambda q, k, v: _flash_a(q, k, v, tq_a, tk, True),
        lambda q, k, v: _flash_a(q, k, v, tq_a, tk, False),
        qa, ka, va)
    lam_vecs = (row(prm["lam_q1"][0]), row(prm["lam_k1"][0]), row(prm["lam_q2"][0]),
                row(prm["lam_k2"][0]), row(prm["subln_b"][0]))
    ob = lax.cond(
        _score_bound(prm["qn_b"][0], prm["kn_b"][0]) <= BOUNDED_SCORE_LIMIT,
        lambda q, k, v: _flash_b(q, k, v, prm["qtab"], *lam_vecs, lam_init, tq_b, tk, True),
        lambda q, k, v: _flash_b(q, k, v, prm["qtab"], *lam_vecs, lam_init, tq_b, tk, False),
        qb, kb, vb)
    h = _out_ple(x, oa, ob, gates, p, 0, prm["w_out_ab"][0], row(prm["norm_ple"][0]),
                 prm["w_ple_gate"][0], prm["w_ple_proj"][0], tm)
    h = _layer_c(h, p, 1, row(prm["norm_mix"][1]), prm["w_in_c"][0], prm["w_grp_c"][0],
                 row(prm["scale_c"][0]), prm["w_out_c"][0], row(prm["norm_ple"][1]),
                 prm["w_ple_gate"][1], prm["w_ple_proj"][1], tm)
    return h


def kernel(x_prompt, x_sample, p_prompt, p_sample, norm_mix, w_in_ab, qn_a, kn_a, qn_b, kn_b,
           lam_q1, lam_k1, lam_q2, lam_k2, subln_b, w_out_ab, w_in_c, w_grp_c, scale_c, w_out_c,
           norm_ple, w_ple_gate, w_ple_proj):
    blk = np.arange(MXU_DIM) // HEAD_DIM
    kaug, qtab = _alibi_tables()
    prm = dict(
        norm_mix=norm_mix, qn_a=qn_a, kn_a=kn_a, qn_b=qn_b, kn_b=kn_b,
        lam_q1=lam_q1, lam_k1=lam_k1, lam_q2=lam_q2, lam_k2=lam_k2, subln_b=subln_b,
        scale_c=scale_c, norm_ple=norm_ple,
        w_in_ab=w_in_ab.astype(BF16), w_out_ab=w_out_ab.astype(BF16), w_in_c=w_in_c.astype(BF16),
        w_grp_c=w_grp_c.astype(BF16), w_out_c=w_out_c.astype(BF16),
        w_ple_gate=w_ple_gate.astype(BF16), w_ple_proj=w_ple_proj.astype(BF16),
        bd=jnp.asarray(blk[:, None] == blk[None, :], BF16),
        kaug=kaug, qtab=qtab,
    )
    y_prompt = _trunk(x_prompt, p_prompt, prm, tq_a=256, tq_b=512, tk=512, tm=512)
    y_sample = _trunk(x_sample, p_sample, prm, tq_a=256, tq_b=512, tk=512, tm=512)
    return (y_prompt, y_sample)
```

```python
import functools
import math

import jax
import jax.numpy as jnp
import numpy as np
from jax import lax
from jax.experimental import pallas as pl
from jax.experimental.pallas import tpu as pltpu

F32 = jnp.float32
BF16 = jnp.bfloat16

D_MODEL = 1024
PLE_DIM = 256
GRID_W = 64
HEAD_DIM = 64
EPS = 1e-6
ROPE_THETA = 10000.0
A_HEADS = 8
A_KV_HEADS = 2
B_HEADS = 4
B_V_DIM = 2 * HEAD_DIM
ALIBI_SLOPES = tuple(2.0 ** (-8.0 * (h + 1) / B_HEADS) for h in range(B_HEADS))
POOL_WINDOWS = (2, 4, 8, 16)
C_GRP = 256
POOL_HALO = 8
LOG2E = math.log2(math.e)
SM_SCALE = HEAD_DIM ** -0.5

LANES = 128
MXU_DIM = 256
VMEM_LIMIT_BYTES = 56 * 1024 * 1024

_QA, _KA, _VA, _GA, _QB, _KB, _VB, _GB, _AB_END = 0, 512, 640, 768, 1280, 1792, 2304, 2816, 3328

NEG_BIG = -1e30
BOUNDED_SCORE_LIMIT = 60.0
KV_UNROLL = 8


def _rms_rows(x, g):
    ms = jnp.mean(x * x, axis=-1, keepdims=True)
    return x * lax.rsqrt(ms + EPS) * g


def _nt_dot(a, b):
    return lax.dot_general(a, b, (((1,), (1,)), ((), ())), preferred_element_type=F32)


def _proj_ab_kernel(x_ref, cos_ref, sa_ref, sb_ref, nm_ref, w_ref, gqa_ref, gkv_ref, gqb_ref,
                    gkb_ref, bd_ref, kaug_ref,
                    qa_ref, ka_ref, va_ref, qb_ref, kb_ref, vb_ref, g_ref):
    x = x_ref[0]
    xn = _rms_rows(x, nm_ref[...]).astype(BF16)
    tm = x.shape[0]

    def seg(a, b):
        return jnp.dot(xn, w_ref[:, a:b], preferred_element_type=F32)

    def head_norm(z, gain):
        ss = jnp.dot((z * z).astype(BF16), bd_ref[...], preferred_element_type=F32)
        return z * lax.rsqrt(ss * (1.0 / HEAD_DIM) + EPS) * gain

    cos = cos_ref[...]
    sin_a = sa_ref[...]
    sin_b = sb_ref[...]

    def rope(z):
        return (z * cos + pltpu.roll(z, LANES - 16, 1) * sin_a + pltpu.roll(z, 16, 1) * sin_b)

    lane = lax.broadcasted_iota(jnp.int32, (tm, LANES), 1)
    lo = lane < HEAD_DIM
    t = pl.program_id(1) * tm + lax.broadcasted_iota(jnp.int32, (tm, LANES), 0)
    pos_hi = (t >> 7).astype(F32)
    pos_lo = (t & 127).astype(F32)

    for j in range(2):
        z = head_norm(seg(_QA + 256 * j, _QA + 256 * (j + 1)), gqa_ref[...])
        for p in range(2):
            c0 = 256 * j + 128 * p
            qa_ref[0, :, c0:c0 + 128] = rope(z[:, 128 * p:128 * (p + 1)]).astype(BF16)

    z = seg(_KA, _GA)
    zn = head_norm(z, gkv_ref[...])
    k = rope(zn[:, :128])
    k_sw = pltpu.roll(k, HEAD_DIM, 1)
    zero = jnp.zeros_like(k)
    ka_ref[0, 0, 0] = jnp.where(lo, k, zero).astype(BF16)
    ka_ref[0, 0, 1] = jnp.where(lo, zero, k_sw).astype(BF16)
    ka_ref[0, 1, 0] = jnp.where(lo, k_sw, zero).astype(BF16)
    ka_ref[0, 1, 1] = jnp.where(lo, zero, k).astype(BF16)
    v = z[:, 128:256]
    v_sw = pltpu.roll(v, HEAD_DIM, 1)
    one_a = (lane == HEAD_DIM).astype(F32)
    va_ref[0, 0] = jnp.where(lo, v, one_a).astype(BF16)
    va_ref[0, 1] = jnp.where(lo, v_sw, one_a).astype(BF16)

    for j in range(2):
        g = seg(_GA + 256 * j, _GA + 256 * (j + 1))
        g_ref[0, :, 256 * j:256 * (j + 1)] = g * jax.nn.sigmoid(g)
        g = seg(_GB + 256 * j, _GB + 256 * (j + 1))
        g_ref[0, :, 512 + 256 * j:512 + 256 * (j + 1)] = g * jax.nn.sigmoid(g)

    for j in range(2):
        z = head_norm(seg(_QB + 256 * j, _QB + 256 * (j + 1)), gqb_ref[...])
        qb_ref[0, :, 256 * j:256 * (j + 1)] = z.astype(BF16)
        z = head_norm(seg(_KB + 256 * j, _KB + 256 * (j + 1)), gkb_ref[...])
        for p in range(2):
            kh = z[:, 128 * p:128 * (p + 1)]
            hd = 2 * j + p
            for mp in range(2):
                aug = (kaug_ref[2 * hd + mp:2 * hd + mp + 1, :] + pos_hi * kaug_ref[8 + mp:9 + mp, :]
                       + pos_lo * kaug_ref[10 + mp:11 + mp, :])
                keep = lo if mp == 0 else jnp.logical_not(lo)
                kb_ref[0, hd, mp] = jnp.where(keep, kh, aug).astype(BF16)

    one_b = (lane == 0).astype(BF16)
    for j in range(2):
        z = seg(_VB + 256 * j, _VB + 256 * (j + 1))
        for p in range(2):
            vb_ref[0, 2 * j + p, :, 0:128] = z[:, 128 * p:128 * (p + 1)].astype(BF16)
            vb_ref[0, 2 * j + p, :, 128:256] = one_b


def _rope_tables(S):
    rows = S // GRID_W
    row = jnp.repeat(jnp.arange(rows), GRID_W).astype(F32)
    col = jnp.tile(jnp.arange(GRID_W), rows).astype(F32)
    half = HEAD_DIM // 2
    inv = ROPE_THETA ** (-jnp.arange(0, half, 2, dtype=F32) / half)
    ar = row[:, None] * inv
    ac = col[:, None] * inv
    ang = jnp.concatenate([ar, ar, ac, ac], axis=-1)
    cos, sin = jnp.cos(ang), jnp.sin(ang)
    first = (np.arange(HEAD_DIM) % 32) < 16
    sin_a = jnp.where(first, -sin, 0.0)
    sin_b = jnp.where(first, 0.0, sin)
    two = lambda t: jnp.concatenate([t, t], axis=-1)
    return two(cos), two(sin_a), two(sin_b)


def _proj_ab(x, w_bf, nm, gqa, gkv, gqb, gkb, bd, kaug, tm):
    B, S, _ = x.shape
    cos, sin_a, sin_b = _rope_tables(S)
    tok = lambda b, i: (b, i, 0)
    tab = pl.BlockSpec((tm, LANES), lambda b, i: (i, 0))
    const2 = lambda shp: pl.BlockSpec(shp, lambda b, i: (0, 0))
    out_shape = (
        jax.ShapeDtypeStruct((B, S, 512), BF16),
        jax.ShapeDtypeStruct((B, 2, 2, S, LANES), BF16),
        jax.ShapeDtypeStruct((B, 2, S, LANES), BF16),
        jax.ShapeDtypeStruct((B, S, 512), BF16),
        jax.ShapeDtypeStruct((B, 4, 2, S, LANES), BF16),
        jax.ShapeDtypeStruct((B, 4, S, 2 * LANES), BF16),
        jax.ShapeDtypeStruct((B, S, 1024), F32),
    )
    out_specs = (
        pl.BlockSpec((1, tm, 512), tok),
        pl.BlockSpec((1, 2, 2, tm, LANES), lambda b, i: (b, 0, 0, i, 0)),
        pl.BlockSpec((1, 2, tm, LANES), lambda b, i: (b, 0, i, 0)),
        pl.BlockSpec((1, tm, 512), tok),
        pl.BlockSpec((1, 4, 2, tm, LANES), lambda b, i: (b, 0, 0, i, 0)),
        pl.BlockSpec((1, 4, tm, 2 * LANES), lambda b, i: (b, 0, i, 0)),
        pl.BlockSpec((1, tm, 1024), tok),
    )
    return pl.pallas_call(
        _proj_ab_kernel,
        grid=(B, S // tm),
        in_specs=[
            pl.BlockSpec((1, tm, D_MODEL), tok), tab, tab, tab,
            const2((1, D_MODEL)), const2((D_MODEL, _AB_END)),
            const2((1, 256)), const2((1, 256)), const2((1, 256)), const2((1, 256)),
            const2((MXU_DIM, MXU_DIM)), const2((16, LANES)),
        ],
        out_specs=out_specs,
        out_shape=out_shape,
        compiler_params=pltpu.CompilerParams(
            dimension_semantics=("parallel", "parallel"), vmem_limit_bytes=VMEM_LIMIT_BYTES),
        name="proj_ab",
    )(x, cos, sin_a, sin_b, nm, w_bf, gqa, gkv, gqb, gkb, bd, kaug)


def _online_update(s, v, m_ref, acc_ref, r0, nrows):
    rows = pl.ds(r0, nrows)
    m_old = m_ref[rows, :]
    m_new = jnp.maximum(m_old, jnp.max(s, axis=-1, keepdims=True))
    alpha = jnp.exp2(m_old - m_new)
    p = jnp.exp2(s - m_new[:, :1]).astype(BF16)
    pv = jnp.dot(p, v, preferred_element_type=F32)
    nv = v.shape[1]
    if nv != LANES:
        alpha = jnp.concatenate([alpha] * (nv // LANES), axis=1)
    acc_ref[rows, :] = acc_ref[rows, :] * alpha + pv
    m_ref[rows, :] = m_new


def _flash_a_kernel(q_ref, k_ref, v_ref, o_ref, q_scr, m_ref, acc_ref, *, tq, tk, n_kv):
    q_scr[0:tq, :] = q_ref[0, :, 0:128]
    q_scr[tq:2 * tq, :] = q_ref[0, :, 128:256]
    m_ref[...] = jnp.full(m_ref.shape, NEG_BIG, F32)
    acc_ref[...] = jnp.zeros(acc_ref.shape, F32)

    def body(c, carry):
        ks = pl.ds(pl.multiple_of(c * tk, tk), tk)
        q = q_scr[...]
        v = v_ref[0, 0, ks, :]
        for var in range(2):
            s = _nt_dot(q, k_ref[0, 0, var, ks, :])
            _online_update(s, v, m_ref, acc_ref, var * 2 * tq, 2 * tq)
        return carry

    lax.fori_loop(0, n_kv, body, 0)
    _flash_a_finalize(o_ref, acc_ref, tq)


def _flash_a_finalize(o_ref, acc_ref, tq):
    lane = lax.broadcasted_iota(jnp.int32, (tq, LANES), 1)
    lo = lane < HEAD_DIM

    def head_out(blk):
        a = acc_ref[blk * tq:(blk + 1) * tq, :]
        return a * (1.0 / a[:, HEAD_DIM:HEAD_DIM + 1])

    o_ref[0, :, 0:128] = jnp.where(lo, head_out(0), pltpu.roll(head_out(2), HEAD_DIM, 1))
    o_ref[0, :, 128:256] = jnp.where(lo, head_out(1), pltpu.roll(head_out(3), HEAD_DIM, 1))


def _flash_a_bounded_kernel(q_ref, k_ref, v_ref, o_ref, q_scr, acc_ref, *, tq, tk, n_kv, unroll):
    q_scr[0:tq, :] = q_ref[0, :, 0:128]
    q_scr[tq:2 * tq, :] = q_ref[0, :, 128:256]
    acc_ref[...] = jnp.zeros(acc_ref.shape, F32)

    def body(c, carry):
        ks = pl.ds(pl.multiple_of(c * tk, tk), tk)
        q = q_scr[...]
        v = v_ref[0, 0, ks, :]
        for var in range(2):
            p = jnp.exp2(_nt_dot(q, k_ref[0, 0, var, ks, :])).astype(BF16)
            rows = pl.ds(var * 2 * tq, 2 * tq)
            acc_ref[rows, :] += jnp.dot(p, v, preferred_element_type=F32)
        return carry

    lax.fori_loop(0, n_kv, body, 0, unroll=unroll)
    _flash_a_finalize(o_ref, acc_ref, tq)


def _kv_spec(block, index_map, single_buffer):
    if single_buffer:
        return pl.BlockSpec(block, index_map, pipeline_mode=pl.Buffered(1))
    return pl.BlockSpec(block, index_map)


def _flash_a(qa, ka, va, tq, tk, bounded):
    B, S, _ = qa.shape
    single = S * LANES * 2 * 3 * 2 > 8 * 1024 * 1024
    scratch = [pltpu.VMEM((2 * tq, LANES), BF16), pltpu.VMEM((4 * tq, LANES), F32)]
    if bounded:
        kern = functools.partial(_flash_a_bounded_kernel, tq=tq, tk=tk, n_kv=S // tk,
                                 unroll=min(KV_UNROLL, S // tk))
    else:
        kern = functools.partial(_flash_a_kernel, tq=tq, tk=tk, n_kv=S // tk)
        scratch.insert(1, pltpu.VMEM((4 * tq, LANES), F32))
    return pl.pallas_call(
        kern,
        grid=(B, A_KV_HEADS, S // tq),
        in_specs=[
            pl.BlockSpec((1, tq, 256), lambda b, g, i: (b, i, g)),
            _kv_spec((1, 1, 2, S, LANES), lambda b, g, i: (b, g, 0, 0, 0), single),
            _kv_spec((1, 1, S, LANES), lambda b, g, i: (b, g, 0, 0), single),
        ],
        out_specs=pl.BlockSpec((1, tq, 256), lambda b, g, i: (b, i, g)),
        out_shape=jax.ShapeDtypeStruct((B, S, 512), F32),
        scratch_shapes=scratch,
        compiler_params=pltpu.CompilerParams(
            dimension_semantics=("parallel", "parallel", "arbitrary"),
            vmem_limit_bytes=VMEM_LIMIT_BYTES),
        name="flash_a_bounded" if bounded else "flash_a",
    )(qa, ka, va)


def _flash_b_q_slots(q_ref, tab_ref, q_scr, tq, with_alibi):
    lane = lax.broadcasted_iota(jnp.int32, (tq, LANES), 1)
    lo = lane < HEAD_DIM
    slab = q_ref[0]
    if with_alibi:
        t = pl.program_id(2) * tq + lax.broadcasted_iota(jnp.int32, (tq, LANES), 0)
        pos_hi = (t >> 7).astype(F32)
        pos_lo = (t & 127).astype(F32)
    for mp in range(2):
        keep = lo if mp == 0 else jnp.logical_not(lo)
        if with_alibi:
            aug = (tab_ref[0, mp:mp + 1, :] + pos_hi * tab_ref[0, 2 + mp:3 + mp, :]
                   + pos_lo * tab_ref[0, 4 + mp:5 + mp, :])
            q_scr[mp, 0] = jnp.where(keep, slab, aug.astype(BF16))
            q_scr[mp, 1] = jnp.where(keep, slab, (-aug).astype(BF16))
        else:
            q_scr[mp, 0] = jnp.where(keep, slab, jnp.zeros_like(slab))


def _flash_b_finalize(o_ref, acc_ref, lq1_ref, lk1_ref, lq2_ref, lk2_ref, sub_ref, tq, lam_init):
    lam = (jnp.exp(jnp.sum(lq1_ref[...] * lk1_ref[...], keepdims=True))
           - jnp.exp(jnp.sum(lq2_ref[...] * lk2_ref[...], keepdims=True)) + lam_init)
    a1 = acc_ref[0:tq, :]
    a2 = acc_ref[tq:2 * tq, :]
    o1 = a1[:, 0:B_V_DIM] * (1.0 / a1[:, B_V_DIM:B_V_DIM + 1])
    o2 = a2[:, 0:B_V_DIM] * (1.0 / a2[:, B_V_DIM:B_V_DIM + 1])
    o = o1 - lam * o2
    o_ref[0] = _rms_rows(o, sub_ref[...]) * (1.0 - lam_init)


def _flash_b_kernel(q_ref, k_ref, v_ref, tab_ref, lq1_ref, lk1_ref, lq2_ref, lk2_ref, sub_ref,
                    o_ref, q_scr, rel_ref, m_ref, acc_ref, *, tq, tk, n_kv, lam_init):
    _flash_b_q_slots(q_ref, tab_ref, q_scr, tq, with_alibi=False)
    m_ref[...] = jnp.full(m_ref.shape, NEG_BIG, F32)
    acc_ref[...] = jnp.zeros(acc_ref.shape, F32)
    rel_ref[...] = (lax.broadcasted_iota(jnp.int32, (tq, tk), 0)
                    - lax.broadcasted_iota(jnp.int32, (tq, tk), 1)).astype(F32)
    neg_slope = -tab_ref[0, 6:7, 0:1]
    q0 = pl.program_id(2) * tq

    def body(c, carry):
        k0 = pl.multiple_of(c * tk, tk)
        ks = pl.ds(k0, tk)
        v = v_ref[0, 0, ks, :]
        bias = jnp.abs(rel_ref[...] + (q0 - k0).astype(F32)) * neg_slope
        for mp in range(2):
            s = _nt_dot(q_scr[mp, 0], k_ref[0, 0, mp, ks, :]) + bias
            _online_update(s, v, m_ref, acc_ref, mp * tq, tq)
        return carry

    lax.fori_loop(0, n_kv, body, 0)
    _flash_b_finalize(o_ref, acc_ref, lq1_ref, lk1_ref, lq2_ref, lk2_ref, sub_ref, tq, lam_init)


def _flash_b_bounded_kernel(q_ref, k_ref, v_ref, tab_ref, lq1_ref, lk1_ref, lq2_ref, lk2_ref, sub_ref,
                            o_ref, q_scr, corr_ref, acc_ref, *, tq, tk, n_kv, lam_init, unroll):
    _flash_b_q_slots(q_ref, tab_ref, q_scr, tq, with_alibi=True)
    acc_ref[...] = jnp.zeros(acc_ref.shape, F32)
    qi = pl.program_id(2)

    @pl.when(qi == 0)
    def _():
        ahead = jnp.maximum(lax.broadcasted_iota(jnp.int32, (tq, tk), 1)
                            - lax.broadcasted_iota(jnp.int32, (tq, tk), 0), 0).astype(F32)
        corr_ref[0] = jnp.zeros((tq, tk), F32)
        corr_ref[1] = ahead * (-2.0 * tab_ref[0, 6:7, 0:1])

    def body(c, carry):
        ks = pl.ds(pl.multiple_of(c * tk, tk), tk)
        v = v_ref[0, 0, ks, :]
        side = (c > qi).astype(jnp.int32)
        corr = corr_ref[(c == qi).astype(jnp.int32)]
        for mp in range(2):
            s = _nt_dot(q_scr[mp, side], k_ref[0, 0, mp, ks, :]) + corr
            rows = pl.ds(mp * tq, tq)
            acc_ref[rows, :] += jnp.dot(jnp.exp2(s).astype(BF16), v, preferred_element_type=F32)
        return carry

    lax.fori_loop(0, n_kv, body, 0, unroll=unroll)
    _flash_b_finalize(o_ref, acc_ref, lq1_ref, lk1_ref, lq2_ref, lk2_ref, sub_ref, tq, lam_init)


def _flash_b(qb, kb, vb, tab, lq1, lk1, lq2, lk2, subln, lam_init, tq, tk, bounded):
    B, S, _ = qb.shape
    assert tq == tk
    single = S * LANES * 2 * 4 * 2 > 8 * 1024 * 1024
    acc = pltpu.VMEM((2 * tq, 2 * LANES), F32)
    if bounded:
        kern = functools.partial(_flash_b_bounded_kernel, tq=tq, tk=tk, n_kv=S // tk,
                                 lam_init=lam_init, unroll=min(KV_UNROLL, S // tk))
        scratch = [pltpu.VMEM((2, 2, tq, LANES), BF16), pltpu.VMEM((2, tq, tk), F32), acc]
    else:
        kern = functools.partial(_flash_b_kernel, tq=tq, tk=tk, n_kv=S // tk, lam_init=lam_init)
        scratch = [pltpu.VMEM((2, 1, tq, LANES), BF16), pltpu.VMEM((tq, tk), F32),
                   pltpu.VMEM((2 * tq, LANES), F32), acc]
    vec = lambda n: pl.BlockSpec((1, n), lambda b, h, i: (0, 0))
    return pl.pallas_call(
        kern,
        grid=(B, B_HEADS, S // tq),
        in_specs=[
            pl.BlockSpec((1, tq, LANES), lambda b, h, i: (b, i, h)),
            _kv_spec((1, 1, 2, S, LANES), lambda b, h, i: (b, h, 0, 0, 0), single),
            _kv_spec((1, 1, S, 2 * LANES), lambda b, h, i: (b, h, 0, 0), single),
            pl.BlockSpec((1, 8, LANES), lambda b, h, i: (h, 0, 0)),
            vec(HEAD_DIM), vec(HEAD_DIM), vec(HEAD_DIM), vec(HEAD_DIM), vec(B_V_DIM),
        ],
        out_specs=pl.BlockSpec((1, tq, LANES), lambda b, h, i: (b, i, h)),
        out_shape=jax.ShapeDtypeStruct((B, S, 512), F32),
        scratch_shapes=scratch,
        compiler_params=pltpu.CompilerParams(
            dimension_semantics=("parallel", "parallel", "arbitrary"),
            vmem_limit_bytes=VMEM_LIMIT_BYTES),
        name="flash_b_bounded" if bounded else "flash_b",
    )(qb, kb, vb, tab, lq1, lk1, lq2, lk2, subln)


def _ple_tail(h1, p, npl, wg_ref, wp_ref):
    rn = _rms_rows(h1, npl).astype(BF16)
    gate = jax.nn.sigmoid(jnp.dot(rn, wg_ref[...], preferred_element_type=F32))
    return h1 + gate * jnp.dot(p.astype(BF16), wp_ref[...], preferred_element_type=F32)


def _out_ple_kernel(h_ref, oa_ref, ob_ref, g_ref, p_ref, wo_ref, npl_ref, wg_ref, wp_ref, out_ref):
    g = g_ref[0]
    y = jnp.concatenate([oa_ref[0] * g[:, 0:512], ob_ref[0] * g[:, 512:1024]], axis=1).astype(BF16)
    h1 = h_ref[0] + jnp.dot(y, wo_ref[...], preferred_element_type=F32)
    out_ref[0] = _ple_tail(h1, p_ref[0, 0], npl_ref[...], wg_ref, wp_ref)


def _out_ple(h, oa, ob, gates, p, layer, wo, npl, wg, wp, tm):
    B, S, _ = h.shape
    tok = lambda b, i: (b, i, 0)
    const2 = lambda shp: pl.BlockSpec(shp, lambda b, i: (0, 0))
    return pl.pallas_call(
        _out_ple_kernel,
        grid=(B, S // tm),
        in_specs=[
            pl.BlockSpec((1, tm, D_MODEL), tok),
            pl.BlockSpec((1, tm, 512), tok),
            pl.BlockSpec((1, tm, 512), tok),
            pl.BlockSpec((1, tm, 1024), tok),
            pl.BlockSpec((1, 1, tm, PLE_DIM), lambda b, i: (layer, b, i, 0)),
            const2((1024, D_MODEL)), const2((1, D_MODEL)), const2((D_MODEL, D_MODEL)),
            const2((PLE_DIM, D_MODEL)),
        ],
        out_specs=pl.BlockSpec((1, tm, D_MODEL), tok),
        out_shape=jax.ShapeDtypeStruct((B, S, D_MODEL), F32),
        compiler_params=pltpu.CompilerParams(
            dimension_semantics=("parallel", "parallel"), vmem_limit_bytes=VMEM_LIMIT_BYTES),
        name="out_ple",
    )(h, oa, ob, gates, p, wo, npl, wg, wp)


def _layer_c_kernel(h_ref, hp_ref, hn_ref, p_ref, nm_ref, wi_ref, wgrp_ref, sc_ref, wo_ref,
                    npl_ref, wg_ref, wp_ref, out_ref, *, tm, S):
    i = pl.program_id(1)
    nt = pl.num_programs(1)
    h = h_ref[0]
    hp = hp_ref[0] * (i > 0).astype(F32)
    hn = hn_ref[0] * (i < nt - 1).astype(F32)
    n = tm + 2 * POOL_HALO
    h_ext = jnp.concatenate([hp, h, hn], axis=0)
    xn = _rms_rows(h_ext, nm_ref[...]).astype(BF16)
    u_ext = jnp.dot(xn, wi_ref[:, 0:1024], preferred_element_type=F32)
    g = jnp.dot(xn[POOL_HALO:POOL_HALO + tm], wi_ref[:, 1024:2048], preferred_element_type=F32)

    t = i * tm + lax.broadcasted_iota(jnp.int32, (tm, C_GRP), 0)
    mixed = []
    for gi, w in enumerate(POOL_WINDOWS):
        u = u_ext[:, C_GRP * gi:C_GRP * (gi + 1)]
        win = u + pltpu.roll(u, 1, 0)
        half = 1
        while 2 * half < w:
            win = pltpu.roll(win, half, 0) + pltpu.roll(win, n - half, 0)
            half *= 2
        cnt = (jnp.minimum(t + w // 2, S) - jnp.maximum(t - w // 2, 0)).astype(F32)
        pooled = win[POOL_HALO:POOL_HALO + tm] / cnt - u[POOL_HALO:POOL_HALO + tm]
        mixed.append(jnp.dot(pooled.astype(BF16), wgrp_ref[gi], preferred_element_type=F32))
    mixed = jnp.concatenate(mixed, axis=1)
    y = ((mixed * sc_ref[...]) * (g * jax.nn.sigmoid(g))).astype(BF16)
    h1 = h + jnp.dot(y, wo_ref[...], preferred_element_type=F32)
    out_ref[0] = _ple_tail(h1, p_ref[0, 0], npl_ref[...], wg_ref, wp_ref)


def _layer_c(h, p, layer, nm, wi, wgrp, sc, wo, npl, wg, wp, tm):
    B, S, _ = h.shape
    tok = lambda b, i: (b, i, 0)
    const2 = lambda shp: pl.BlockSpec(shp, lambda b, i: (0, 0))
    per = tm // POOL_HALO
    last = S // POOL_HALO - 1
    kern = functools.partial(_layer_c_kernel, tm=tm, S=S)
    return pl.pallas_call(
        kern,
        grid=(B, S // tm),
        in_specs=[
            pl.BlockSpec((1, tm, D_MODEL), tok),
            pl.BlockSpec((1, POOL_HALO, D_MODEL), lambda b, i: (b, jnp.maximum(i * per - 1, 0), 0)),
            pl.BlockSpec((1, POOL_HALO, D_MODEL), lambda b, i: (b, jnp.minimum((i + 1) * per, last), 0)),
            pl.BlockSpec((1, 1, tm, PLE_DIM), lambda b, i: (layer, b, i, 0)),
            const2((1, D_MODEL)), const2((D_MODEL, 2048)),
            pl.BlockSpec((4, C_GRP, C_GRP), lambda b, i: (0, 0, 0)),
            const2((1, 1024)), const2((1024, D_MODEL)), const2((1, D_MODEL)),
            const2((D_MODEL, D_MODEL)), const2((PLE_DIM, D_MODEL)),
        ],
        out_specs=pl.BlockSpec((1, tm, D_MODEL), tok),
        out_shape=jax.ShapeDtypeStruct((B, S, D_MODEL), F32),
        compiler_params=pltpu.CompilerParams(
            dimension_semantics=("parallel", "parallel"), vmem_limit_bytes=VMEM_LIMIT_BYTES),
        name="layer_c",
    )(h, h, h, p, nm, wi, wgrp, sc, wo, npl, wg, wp)


def _score_bound(gq, gk):
    slack = 1.02
    return (HEAD_DIM * SM_SCALE * LOG2E * slack) * jnp.max(jnp.abs(gq)) * jnp.max(jnp.abs(gk))


def _alibi_tables():
    kaug = np.zeros((16, LANES), np.float32)
    qtab = np.zeros((B_HEADS, 8, LANES), np.float32)
    for mp in range(2):
        base = HEAD_DIM if mp == 0 else 0
        kaug[8 + mp, base + 6:base + 9] = 128.0
        kaug[10 + mp, base + 9:base + 12] = 1.0
        qtab[:, 2 + mp, base:base + 3] = -128.0
        qtab[:, 4 + mp, base + 3:base + 6] = -1.0
    for h, slope in enumerate(ALIBI_SLOPES):
        c = np.float64(slope) * LOG2E
        pieces, rest = [], c
        for _ in range(3):
            piece = np.float64(np.float32(rest).astype(BF16))
            pieces.append(piece)
            rest -= piece
        for mp in range(2):
            base = HEAD_DIM if mp == 0 else 0
            kaug[2 * h + mp, base:base + 6] = pieces + pieces
            qtab[h, mp, base + 6:base + 12] = pieces + pieces
        qtab[h, 6, :] = np.float32(c)
    return jnp.asarray(kaug), jnp.asarray(qtab)


def _tile_gain(g, reps, scale=1.0):
    return jnp.tile(g.astype(F32) * scale, reps).reshape(1, -1)


def _trunk(x, p, prm, tq_a, tq_b, tk, tm):
    row = lambda v: v.astype(F32).reshape(1, -1)
    lam_init = 0.8 - 0.6 * math.exp(-0.3 * 0)
    gq_scale = SM_SCALE * LOG2E
    gqa = _tile_gain(prm["qn_a"][0], 4, gq_scale)
    gkv = jnp.concatenate([_tile_gain(prm["kn_a"][0], 2), jnp.ones((1, 128), F32)], axis=1)
    gqb = _tile_gain(prm["qn_b"][0], 4, gq_scale)
    gkb = _tile_gain(prm["kn_b"][0], 4)
    qa, ka, va, qb, kb, vb, gates = _proj_ab(
        x, prm["w_in_ab"][0], row(prm["norm_mix"][0]), gqa, gkv, gqb, gkb, prm["bd"], prm["kaug"], tm)
    oa = lax.cond(
        _score_bound(prm["qn_a"][0], prm["kn_a"][0]) <= BOUNDED_SCORE_LIMIT,
        lambda q, k, v: _flash_a(q, k, v, tq_a, tk, True),
        lambda q, k, v: _flash_a(q, k, v, tq_a, tk, False),
        qa, ka, va)
    lam_vecs = (row(prm["lam_q1"][0]), row(prm["lam_k1"][0]), row(prm["lam_q2"][0]),
                row(prm["lam_k2"][0]), row(prm["subln_b"][0]))
    ob = lax.cond(
        _score_bound(prm["qn_b"][0], prm["kn_b"][0]) <= BOUNDED_SCORE_LIMIT,
        lambda q, k, v: _flash_b(q, k, v, prm["qtab"], *lam_vecs, lam_init, tq_b, tk, True),
        lambda q, k, v: _flash_b(q, k, v, prm["qtab"], *lam_vecs, lam_init, tq_b, tk, False),
        qb, kb, vb)
    h = _out_ple(x, oa, ob, gates, p, 0, prm["w_out_ab"][0], row(prm["norm_ple"][0]),
                 prm["w_ple_gate"][0], prm["w_ple_proj"][0], tm)
    h = _layer_c(h, p, 1, row(prm["norm_mix"][1]), prm["w_in_c"][0], prm["w_grp_c"][0],
                 row(prm["scale_c"][0]), prm["w_out_c"][0], row(prm["norm_ple"][1]),
                 prm["w_ple_gate"][1], prm["w_ple_proj"][1], tm)
    return h


def kernel(x_prompt, x_sample, p_prompt, p_sample, norm_mix, w_in_ab, qn_a, kn_a, qn_b, kn_b,
           lam_q1, lam_k1, lam_q2, lam_k2, subln_b, w_out_ab, w_in_c, w_grp_c, scale_c, w_out_c,
           norm_ple, w_ple_gate, w_ple_proj):
    blk = np.arange(MXU_DIM) // HEAD_DIM
    kaug, qtab = _alibi_tables()
    prm = dict(
        norm_mix=norm_mix, qn_a=qn_a, kn_a=kn_a, qn_b=qn_b, kn_b=kn_b,
        lam_q1=lam_q1, lam_k1=lam_k1, lam_q2=lam_q2, lam_k2=lam_k2, subln_b=subln_b,
        scale_c=scale_c, norm_ple=norm_ple,
        w_in_ab=w_in_ab.astype(BF16), w_out_ab=w_out_ab.astype(BF16), w_in_c=w_in_c.astype(BF16),
        w_grp_c=w_grp_c.astype(BF16), w_out_c=w_out_c.astype(BF16),
        w_ple_gate=w_ple_gate.astype(BF16), w_ple_proj=w_ple_proj.astype(BF16),
        bd=jnp.asarray(blk[:, None] == blk[None, :], BF16),
        kaug=kaug, qtab=qtab,
    )
    y_prompt = _trunk(x_prompt, p_prompt, prm, tq_a=256, tq_b=512, tk=512, tm=512)
    y_sample = _trunk(x_sample, p_sample, prm, tq_a=256, tq_b=512, tk=512, tm=512)
    return (y_prompt, y_sample)
```

```python
import functools
import math

import jax
import jax.numpy as jnp
import numpy as np
from jax import lax
from jax.experimental import pallas as pl
from jax.experimental.pallas import tpu as pltpu

F32 = jnp.float32
BF16 = jnp.bfloat16

D_MODEL = 1024
PLE_DIM = 256
GRID_W = 64
HEAD_DIM = 64
EPS = 1e-6
ROPE_THETA = 10000.0
A_HEADS = 8
A_KV_HEADS = 2
B_HEADS = 4
B_V_DIM = 2 * HEAD_DIM
ALIBI_SLOPES = tuple(2.0 ** (-8.0 * (h + 1) / B_HEADS) for h in range(B_HEADS))
POOL_WINDOWS = (2, 4, 8, 16)
C_GRP = 256
POOL_HALO = 8
LOG2E = math.log2(math.e)
SM_SCALE = HEAD_DIM ** -0.5

LANES = 128
MXU_DIM = 256
VMEM_LIMIT_BYTES = 56 * 1024 * 1024

_QA, _KA, _VA, _GA, _QB, _KB, _VB, _GB, _AB_END = 0, 512, 640, 768, 1280, 1792, 2304, 2816, 3328

NEG_BIG = -1e30
BOUNDED_SCORE_LIMIT = 60.0
KV_UNROLL = 16


def _rms_rows(x, g):
    ms = jnp.mean(x * x, axis=-1, keepdims=True)
    return x * lax.rsqrt(ms + EPS) * g


def _nt_dot(a, b):
    return lax.dot_general(a, b, (((1,), (1,)), ((), ())), preferred_element_type=F32)


def _proj_ab_kernel(x_ref, cos_ref, sa_ref, sb_ref, nm_ref, w_ref, gqa_ref, gkv_ref, gqb_ref,
                    gkb_ref, bd_ref, kaug_ref,
                    qa_ref, ka_ref, va_ref, qb_ref, kb_ref, vb_ref, g_ref):
    x = x_ref[0]
    xn = _rms_rows(x, nm_ref[...]).astype(BF16)
    tm = x.shape[0]

    def seg(a, b):
        return jnp.dot(xn, w_ref[:, a:b], preferred_element_type=F32)

    def head_norm(z, gain):
        n = z.shape[1]
        ss = jnp.dot((z * z).astype(BF16), bd_ref[0:n, 0:n], preferred_element_type=F32)
        return z * lax.rsqrt(ss * (1.0 / HEAD_DIM) + EPS) * gain

    cos = cos_ref[...]
    sin_a = sa_ref[...]
    sin_b = sb_ref[...]

    def rope(z):
        return (z * cos + pltpu.roll(z, LANES - 16, 1) * sin_a + pltpu.roll(z, 16, 1) * sin_b)

    lane = lax.broadcasted_iota(jnp.int32, (tm, LANES), 1)
    lo = lane < HEAD_DIM
    t = pl.program_id(1) * tm + lax.broadcasted_iota(jnp.int32, (tm, LANES), 0)
    pos_hi = (t >> 7).astype(F32)
    pos_lo = (t & 127).astype(F32)

    p_a = seg(_QA, _GA)

    z = head_norm(p_a[:, 0:512], gqa_ref[...])
    for j in range(4):
        qa_ref[0, :, 128 * j:128 * (j + 1)] = rope(z[:, 128 * j:128 * (j + 1)]).astype(BF16)

    z = p_a[:, 512:768]
    zn = head_norm(z, gkv_ref[...])
    k = rope(zn[:, :128])
    k_sw = pltpu.roll(k, HEAD_DIM, 1)
    zero = jnp.zeros_like(k)
    ka_ref[0, 0, 0] = jnp.where(lo, k, zero).astype(BF16)
    ka_ref[0, 0, 1] = jnp.where(lo, zero, k_sw).astype(BF16)
    ka_ref[0, 1, 0] = jnp.where(lo, k_sw, zero).astype(BF16)
    ka_ref[0, 1, 1] = jnp.where(lo, zero, k).astype(BF16)
    v = z[:, 128:256]
    v_sw = pltpu.roll(v, HEAD_DIM, 1)
    one_a = (lane == HEAD_DIM).astype(F32)
    va_ref[0, 0] = jnp.where(lo, v, one_a).astype(BF16)
    va_ref[0, 1] = jnp.where(lo, v_sw, one_a).astype(BF16)

    g = seg(_GA, _QB)
    g_ref[0, :, 0:512] = (g * jax.nn.sigmoid(g)).astype(BF16)

    p_b = seg(_QB, _VB)
    qb_ref[0] = head_norm(p_b[:, 0:512], gqb_ref[...]).astype(BF16)
    z = head_norm(p_b[:, 512:1024], gkb_ref[...])
    for hd in range(B_HEADS):
        kh = z[:, 128 * hd:128 * (hd + 1)]
        for mp in range(2):
            aug = (kaug_ref[2 * hd + mp:2 * hd + mp + 1, :] + pos_hi * kaug_ref[8 + mp:9 + mp, :]
                   + pos_lo * kaug_ref[10 + mp:11 + mp, :])
            keep = lo if mp == 0 else jnp.logical_not(lo)
            kb_ref[0, hd, mp] = jnp.where(keep, kh, aug).astype(BF16)

    p_c = seg(_VB, _AB_END)
    one_b = (lane == 0).astype(BF16)
    for hd in range(B_HEADS):
        vb_ref[0, hd, :, 0:128] = p_c[:, 128 * hd:128 * (hd + 1)].astype(BF16)
        vb_ref[0, hd, :, 128:256] = one_b
    g = p_c[:, 512:1024]
    g_ref[0, :, 512:1024] = (g * jax.nn.sigmoid(g)).astype(BF16)


def _rope_tables(S):
    rows = S // GRID_W
    row = jnp.repeat(jnp.arange(rows), GRID_W).astype(F32)
    col = jnp.tile(jnp.arange(GRID_W), rows).astype(F32)
    half = HEAD_DIM // 2
    inv = ROPE_THETA ** (-jnp.arange(0, half, 2, dtype=F32) / half)
    ar = row[:, None] * inv
    ac = col[:, None] * inv
    ang = jnp.concatenate([ar, ar, ac, ac], axis=-1)
    cos, sin = jnp.cos(ang), jnp.sin(ang)
    first = (np.arange(HEAD_DIM) % 32) < 16
    sin_a = jnp.where(first, -sin, 0.0)
    sin_b = jnp.where(first, 0.0, sin)
    two = lambda t: jnp.concatenate([t, t], axis=-1)
    return two(cos), two(sin_a), two(sin_b)


def _proj_ab(x, w_bf, nm, gqa, gkv, gqb, gkb, bd, kaug, tm):
    B, S, _ = x.shape
    cos, sin_a, sin_b = _rope_tables(S)
    tok = lambda b, i: (b, i, 0)
    tab = pl.BlockSpec((tm, LANES), lambda b, i: (i, 0))
    const2 = lambda shp: pl.BlockSpec(shp, lambda b, i: (0, 0))
    out_shape = (
        jax.ShapeDtypeStruct((B, S, 512), BF16),
        jax.ShapeDtypeStruct((B, 2, 2, S, LANES), BF16),
        jax.ShapeDtypeStruct((B, 2, S, LANES), BF16),
        jax.ShapeDtypeStruct((B, S, 512), BF16),
        jax.ShapeDtypeStruct((B, 4, 2, S, LANES), BF16),
        jax.ShapeDtypeStruct((B, 4, S, 2 * LANES), BF16),
        jax.ShapeDtypeStruct((B, S, 1024), BF16),
    )
    out_specs = (
        pl.BlockSpec((1, tm, 512), tok),
        pl.BlockSpec((1, 2, 2, tm, LANES), lambda b, i: (b, 0, 0, i, 0)),
        pl.BlockSpec((1, 2, tm, LANES), lambda b, i: (b, 0, i, 0)),
        pl.BlockSpec((1, tm, 512), tok),
        pl.BlockSpec((1, 4, 2, tm, LANES), lambda b, i: (b, 0, 0, i, 0)),
        pl.BlockSpec((1, 4, tm, 2 * LANES), lambda b, i: (b, 0, i, 0)),
        pl.BlockSpec((1, tm, 1024), tok),
    )
    return pl.pallas_call(
        _proj_ab_kernel,
        grid=(B, S // tm),
        in_specs=[
            pl.BlockSpec((1, tm, D_MODEL), tok), tab, tab, tab,
            const2((1, D_MODEL)), const2((D_MODEL, _AB_END)),
            const2((1, 512)), const2((1, 256)), const2((1, 512)), const2((1, 512)),
            const2((2 * MXU_DIM, 2 * MXU_DIM)), const2((16, LANES)),
        ],
        out_specs=out_specs,
        out_shape=out_shape,
        compiler_params=pltpu.CompilerParams(
            dimension_semantics=("parallel", "parallel"), vmem_limit_bytes=VMEM_LIMIT_BYTES),
        name="proj_ab",
    )(x, cos, sin_a, sin_b, nm, w_bf, gqa, gkv, gqb, gkb, bd, kaug)


def _online_update(s, v, m_ref, acc_ref, r0, nrows):
    rows = pl.ds(r0, nrows)
    m_old = m_ref[rows, :]
    m_new = jnp.maximum(m_old, jnp.max(s, axis=-1, keepdims=True))
    alpha = jnp.exp2(m_old - m_new)
    p = jnp.exp2(s - m_new[:, :1]).astype(BF16)
    pv = jnp.dot(p, v, preferred_element_type=F32)
    nv = v.shape[1]
    if nv != LANES:
        alpha = jnp.concatenate([alpha] * (nv // LANES), axis=1)
    acc_ref[rows, :] = acc_ref[rows, :] * alpha + pv
    m_ref[rows, :] = m_new


def _flash_a_kernel(q_ref, k_ref, v_ref, o_ref, q_scr, m_ref, acc_ref, *, tq, tk, n_kv):
    q_scr[0:tq, :] = q_ref[0, :, 0:128]
    q_scr[tq:2 * tq, :] = q_ref[0, :, 128:256]
    m_ref[...] = jnp.full(m_ref.shape, NEG_BIG, F32)
    acc_ref[...] = jnp.zeros(acc_ref.shape, F32)

    def body(c, carry):
        ks = pl.ds(pl.multiple_of(c * tk, tk), tk)
        q = q_scr[...]
        v = v_ref[0, 0, ks, :]
        for var in range(2):
            s = _nt_dot(q, k_ref[0, 0, var, ks, :])
            _online_update(s, v, m_ref, acc_ref, var * 2 * tq, 2 * tq)
        return carry

    lax.fori_loop(0, n_kv, body, 0)
    _flash_a_finalize(o_ref, acc_ref, tq)


def _flash_a_finalize(o_ref, acc_ref, tq):
    lane = lax.broadcasted_iota(jnp.int32, (tq, LANES), 1)
    lo = lane < HEAD_DIM

    def head_out(blk):
        a = acc_ref[blk * tq:(blk + 1) * tq, :]
        return a * (1.0 / a[:, HEAD_DIM:HEAD_DIM + 1])

    o01 = jnp.where(lo, head_out(0), pltpu.roll(head_out(2), HEAD_DIM, 1))
    o23 = jnp.where(lo, head_out(1), pltpu.roll(head_out(3), HEAD_DIM, 1))
    o_ref[0, :, 0:128] = o01.astype(o_ref.dtype)
    o_ref[0, :, 128:256] = o23.astype(o_ref.dtype)


def _flash_a_bounded_kernel(q_ref, k_ref, v_ref, o_ref, q_scr, acc_ref, *, tq, tk, n_kv, unroll):
    q_scr[0:tq, :] = q_ref[0, :, 0:128]
    q_scr[tq:2 * tq, :] = q_ref[0, :, 128:256]
    acc_ref[...] = jnp.zeros(acc_ref.shape, F32)

    def body(c, carry):
        ks = pl.ds(pl.multiple_of(c * tk, tk), tk)
        q = q_scr[...]
        v = v_ref[0, 0, ks, :]
        for var in range(2):
            p = jnp.exp2(_nt_dot(q, k_ref[0, 0, var, ks, :])).astype(BF16)
            rows = pl.ds(var * 2 * tq, 2 * tq)
            acc_ref[rows, :] += jnp.dot(p, v, preferred_element_type=F32)
        return carry

    lax.fori_loop(0, n_kv, body, 0, unroll=unroll)
    _flash_a_finalize(o_ref, acc_ref, tq)


def _kv_spec(block, index_map, single_buffer):
    if single_buffer:
        return pl.BlockSpec(block, index_map, pipeline_mode=pl.Buffered(1))
    return pl.BlockSpec(block, index_map)


def _flash_a(qa, ka, va, tq, tk, bounded):
    B, S, _ = qa.shape
    single = S * LANES * 2 * 3 * 2 > 8 * 1024 * 1024
    scratch = [pltpu.VMEM((2 * tq, LANES), BF16), pltpu.VMEM((4 * tq, LANES), F32)]
    if bounded:
        kern = functools.partial(_flash_a_bounded_kernel, tq=tq, tk=tk, n_kv=S // tk,
                                 unroll=min(KV_UNROLL, S // tk))
    else:
        kern = functools.partial(_flash_a_kernel, tq=tq, tk=tk, n_kv=S // tk)
        scratch.insert(1, pltpu.VMEM((4 * tq, LANES), F32))
    return pl.pallas_call(
        kern,
        grid=(B, A_KV_HEADS, S // tq),
        in_specs=[
            pl.BlockSpec((1, tq, 256), lambda b, g, i: (b, i, g)),
            _kv_spec((1, 1, 2, S, LANES), lambda b, g, i: (b, g, 0, 0, 0), single),
            _kv_spec((1, 1, S, LANES), lambda b, g, i: (b, g, 0, 0), single),
        ],
        out_specs=pl.BlockSpec((1, tq, 256), lambda b, g, i: (b, i, g)),
        out_shape=jax.ShapeDtypeStruct((B, S, 512), BF16),
        scratch_shapes=scratch,
        compiler_params=pltpu.CompilerParams(
            dimension_semantics=("parallel", "parallel", "arbitrary"),
            vmem_limit_bytes=VMEM_LIMIT_BYTES),
        name="flash_a_bounded" if bounded else "flash_a",
    )(qa, ka, va)


def _flash_b_q_slots(q_ref, tab_ref, q_scr, tq, with_alibi):
    lane = lax.broadcasted_iota(jnp.int32, (tq, LANES), 1)
    lo = lane < HEAD_DIM
    slab = q_ref[0]
    if with_alibi:
        t = pl.program_id(2) * tq + lax.broadcasted_iota(jnp.int32, (tq, LANES), 0)
        pos_hi = (t >> 7).astype(F32)
        pos_lo = (t & 127).astype(F32)
    for mp in range(2):
        keep = lo if mp == 0 else jnp.logical_not(lo)
        if with_alibi:
            aug = (tab_ref[0, mp:mp + 1, :] + pos_hi * tab_ref[0, 2 + mp:3 + mp, :]
                   + pos_lo * tab_ref[0, 4 + mp:5 + mp, :])
            q_scr[mp, 0] = jnp.where(keep, slab, aug.astype(BF16))
            q_scr[mp, 1] = jnp.where(keep, slab, (-aug).astype(BF16))
        else:
            q_scr[mp, 0] = jnp.where(keep, slab, jnp.zeros_like(slab))


def _flash_b_finalize(o_ref, acc_ref, lq1_ref, lk1_ref, lq2_ref, lk2_ref, sub_ref, tq, lam_init):
    lam = (jnp.exp(jnp.sum(lq1_ref[...] * lk1_ref[...], keepdims=True))
           - jnp.exp(jnp.sum(lq2_ref[...] * lk2_ref[...], keepdims=True)) + lam_init)
    a1 = acc_ref[0:tq, :]
    a2 = acc_ref[tq:2 * tq, :]
    o1 = a1[:, 0:B_V_DIM] * (1.0 / a1[:, B_V_DIM:B_V_DIM + 1])
    o2 = a2[:, 0:B_V_DIM] * (1.0 / a2[:, B_V_DIM:B_V_DIM + 1])
    o = o1 - lam * o2
    o_ref[0] = (_rms_rows(o, sub_ref[...]) * (1.0 - lam_init)).astype(o_ref.dtype)


def _flash_b_kernel(q_ref, k_ref, v_ref, tab_ref, lq1_ref, lk1_ref, lq2_ref, lk2_ref, sub_ref,
                    o_ref, q_scr, rel_ref, m_ref, acc_ref, *, tq, tk, n_kv, lam_init):
    _flash_b_q_slots(q_ref, tab_ref, q_scr, tq, with_alibi=False)
    m_ref[...] = jnp.full(m_ref.shape, NEG_BIG, F32)
    acc_ref[...] = jnp.zeros(acc_ref.shape, F32)
    rel_ref[...] = (lax.broadcasted_iota(jnp.int32, (tq, tk), 0)
                    - lax.broadcasted_iota(jnp.int32, (tq, tk), 1)).astype(F32)
    neg_slope = -tab_ref[0, 6:7, 0:1]
    q0 = pl.program_id(2) * tq

    def body(c, carry):
        k0 = pl.multiple_of(c * tk, tk)
        ks = pl.ds(k0, tk)
        v = v_ref[0, 0, ks, :]
        bias = jnp.abs(rel_ref[...] + (q0 - k0).astype(F32)) * neg_slope
        for mp in range(2):
            s = _nt_dot(q_scr[mp, 0], k_ref[0, 0, mp, ks, :]) + bias
            _online_update(s, v, m_ref, acc_ref, mp * tq, tq)
        return carry

    lax.fori_loop(0, n_kv, body, 0)
    _flash_b_finalize(o_ref, acc_ref, lq1_ref, lk1_ref, lq2_ref, lk2_ref, sub_ref, tq, lam_init)


def _flash_b_bounded_kernel(q_ref, k_ref, v_ref, tab_ref, lq1_ref, lk1_ref, lq2_ref, lk2_ref, sub_ref,
                            o_ref, q_scr, corr_ref, acc_ref, *, tq, tk, n_kv, lam_init, unroll):
    _flash_b_q_slots(q_ref, tab_ref, q_scr, tq, with_alibi=True)
    acc_ref[...] = jnp.zeros(acc_ref.shape, F32)
    qi = pl.program_id(2)

    @pl.when(qi == 0)
    def _():
        ahead = jnp.maximum(lax.broadcasted_iota(jnp.int32, (tq, tk), 1)
                            - lax.broadcasted_iota(jnp.int32, (tq, tk), 0), 0).astype(F32)
        corr_ref[0] = jnp.zeros((tq, tk), F32)
        corr_ref[1] = ahead * (-2.0 * tab_ref[0, 6:7, 0:1])

    def body(c, carry):
        ks = pl.ds(pl.multiple_of(c * tk, tk), tk)
        v = v_ref[0, 0, ks, :]
        side = (c > qi).astype(jnp.int32)
        corr = corr_ref[(c == qi).astype(jnp.int32)]
        for mp in range(2):
            s = _nt_dot(q_scr[mp, side], k_ref[0, 0, mp, ks, :]) + corr
            rows = pl.ds(mp * tq, tq)
            acc_ref[rows, :] += jnp.dot(jnp.exp2(s).astype(BF16), v, preferred_element_type=F32)
        return carry

    lax.fori_loop(0, n_kv, body, 0, unroll=unroll)
    _flash_b_finalize(o_ref, acc_ref, lq1_ref, lk1_ref, lq2_ref, lk2_ref, sub_ref, tq, lam_init)


def _flash_b(qb, kb, vb, tab, lq1, lk1, lq2, lk2, subln, lam_init, tq, tk, bounded):
    B, S, _ = qb.shape
    assert tq == tk
    single = S * LANES * 2 * 4 * 2 > 8 * 1024 * 1024
    acc = pltpu.VMEM((2 * tq, 2 * LANES), F32)
    if bounded:
        kern = functools.partial(_flash_b_bounded_kernel, tq=tq, tk=tk, n_kv=S // tk,
                                 lam_init=lam_init, unroll=min(KV_UNROLL, S // tk))
        scratch = [pltpu.VMEM((2, 2, tq, LANES), BF16), pltpu.VMEM((2, tq, tk), F32), acc]
    else:
        kern = functools.partial(_flash_b_kernel, tq=tq, tk=tk, n_kv=S // tk, lam_init=lam_init)
        scratch = [pltpu.VMEM((2, 1, tq, LANES), BF16), pltpu.VMEM((tq, tk), F32),
                   pltpu.VMEM((2 * tq, LANES), F32), acc]
    vec = lambda n: pl.BlockSpec((1, n), lambda b, h, i: (0, 0))
    return pl.pallas_call(
        kern,
        grid=(B, B_HEADS, S // tq),
        in_specs=[
            pl.BlockSpec((1, tq, LANES), lambda b, h, i: (b, i, h)),
            _kv_spec((1, 1, 2, S, LANES), lambda b, h, i: (b, h, 0, 0, 0), single),
            _kv_spec((1, 1, S, 2 * LANES), lambda b, h, i: (b, h, 0, 0), single),
            pl.BlockSpec((1, 8, LANES), lambda b, h, i: (h, 0, 0)),
            vec(HEAD_DIM), vec(HEAD_DIM), vec(HEAD_DIM), vec(HEAD_DIM), vec(B_V_DIM),
        ],
        out_specs=pl.BlockSpec((1, tq, LANES), lambda b, h, i: (b, i, h)),
        out_shape=jax.ShapeDtypeStruct((B, S, 512), BF16),
        scratch_shapes=scratch,
        compiler_params=pltpu.CompilerParams(
            dimension_semantics=("parallel", "parallel", "arbitrary"),
            vmem_limit_bytes=VMEM_LIMIT_BYTES),
        name="flash_b_bounded" if bounded else "flash_b",
    )(qb, kb, vb, tab, lq1, lk1, lq2, lk2, subln)


def _ple_tail(h1, p, npl, wg_ref, wp_ref):
    rn = _rms_rows(h1, npl).astype(BF16)
    gate = jax.nn.sigmoid(jnp.dot(rn, wg_ref[...], preferred_element_type=F32))
    return h1 + gate * jnp.dot(p.astype(BF16), wp_ref[...], preferred_element_type=F32)


def _out_ple_kernel(h_ref, oa_ref, ob_ref, g_ref, p_ref, wo_ref, npl_ref, wg_ref, wp_ref, out_ref):
    g = g_ref[0]
    y = jnp.concatenate([oa_ref[0] * g[:, 0:512], ob_ref[0] * g[:, 512:1024]], axis=1)
    h1 = h_ref[0] + jnp.dot(y, wo_ref[...], preferred_element_type=F32)
    out_ref[0] = _ple_tail(h1, p_ref[0, 0], npl_ref[...], wg_ref, wp_ref)


def _out_ple(h, oa, ob, gates, p, layer, wo, npl, wg, wp, tm):
    B, S, _ = h.shape
    tok = lambda b, i: (b, i, 0)
    const2 = lambda shp: pl.BlockSpec(shp, lambda b, i: (0, 0))
    return pl.pallas_call(
        _out_ple_kernel,
        grid=(B, S // tm),
        in_specs=[
            pl.BlockSpec((1, tm, D_MODEL), tok),
            pl.BlockSpec((1, tm, 512), tok),
            pl.BlockSpec((1, tm, 512), tok),
            pl.BlockSpec((1, tm, 1024), tok),
            pl.BlockSpec((1, 1, tm, PLE_DIM), lambda b, i: (layer, b, i, 0)),
            const2((1024, D_MODEL)), const2((1, D_MODEL)), const2((D_MODEL, D_MODEL)),
            const2((PLE_DIM, D_MODEL)),
        ],
        out_specs=pl.BlockSpec((1, tm, D_MODEL), tok),
        out_shape=jax.ShapeDtypeStruct((B, S, D_MODEL), F32),
        compiler_params=pltpu.CompilerParams(
            dimension_semantics=("parallel", "parallel"), vmem_limit_bytes=VMEM_LIMIT_BYTES),
        name="out_ple",
    )(h, oa, ob, gates, p, wo, npl, wg, wp)


def _layer_c_kernel(h_ref, hp_ref, hn_ref, p_ref, nm_ref, wi_ref, wgrp_ref, sc_ref, wo_ref,
                    npl_ref, wg_ref, wp_ref, out_ref, *, tm, S):
    i = pl.program_id(1)
    nt = pl.num_programs(1)
    h = h_ref[0]
    hp = hp_ref[0] * (i > 0).astype(F32)
    hn = hn_ref[0] * (i < nt - 1).astype(F32)
    n = tm + 2 * POOL_HALO
    h_ext = jnp.concatenate([hp, h, hn], axis=0)
    xn = _rms_rows(h_ext, nm_ref[...]).astype(BF16)
    u_ext = jnp.dot(xn, wi_ref[:, 0:1024], preferred_element_type=F32)
    g = jnp.dot(xn[POOL_HALO:POOL_HALO + tm], wi_ref[:, 1024:2048], preferred_element_type=F32)

    t = i * tm + lax.broadcasted_iota(jnp.int32, (tm, C_GRP), 0)
    mixed = []
    for gi, w in enumerate(POOL_WINDOWS):
        u = u_ext[:, C_GRP * gi:C_GRP * (gi + 1)]
        win = u + pltpu.roll(u, 1, 0)
        half = 1
        while 2 * half < w:
            win = pltpu.roll(win, half, 0) + pltpu.roll(win, n - half, 0)
            half *= 2
        cnt = (jnp.minimum(t + w // 2, S) - jnp.maximum(t - w // 2, 0)).astype(F32)
        pooled = win[POOL_HALO:POOL_HALO + tm] / cnt - u[POOL_HALO:POOL_HALO + tm]
        mixed.append(jnp.dot(pooled.astype(BF16), wgrp_ref[gi], preferred_element_type=F32))
    mixed = jnp.concatenate(mixed, axis=1)
    y = ((mixed * sc_ref[...]) * (g * jax.nn.sigmoid(g))).astype(BF16)
    h1 = h + jnp.dot(y, wo_ref[...], preferred_element_type=F32)
    out_ref[0] = _ple_tail(h1, p_ref[0, 0], npl_ref[...], wg_ref, wp_ref)


def _layer_c(h, p, layer, nm, wi, wgrp, sc, wo, npl, wg, wp, tm):
    B, S, _ = h.shape
    tok = lambda b, i: (b, i, 0)
    const2 = lambda shp: pl.BlockSpec(shp, lambda b, i: (0, 0))
    per = tm // POOL_HALO
    last = S // POOL_HALO - 1
    kern = functools.partial(_layer_c_kernel, tm=tm, S=S)
    return pl.pallas_call(
        kern,
        grid=(B, S // tm),
        in_specs=[
            pl.BlockSpec((1, tm, D_MODEL), tok),
            pl.BlockSpec((1, POOL_HALO, D_MODEL), lambda b, i: (b, jnp.maximum(i * per - 1, 0), 0)),
            pl.BlockSpec((1, POOL_HALO, D_MODEL), lambda b, i: (b, jnp.minimum((i + 1) * per, last), 0)),
            pl.BlockSpec((1, 1, tm, PLE_DIM), lambda b, i: (layer, b, i, 0)),
            const2((1, D_MODEL)), const2((D_MODEL, 2048)),
            pl.BlockSpec((4, C_GRP, C_GRP), lambda b, i: (0, 0, 0)),
            const2((1, 1024)), const2((1024, D_MODEL)), const2((1, D_MODEL)),
            const2((D_MODEL, D_MODEL)), const2((PLE_DIM, D_MODEL)),
        ],
        out_specs=pl.BlockSpec((1, tm, D_MODEL), tok),
        out_shape=jax.ShapeDtypeStruct((B, S, D_MODEL), F32),
        compiler_params=pltpu.CompilerParams(
            dimension_semantics=("parallel", "parallel"), vmem_limit_bytes=VMEM_LIMIT_BYTES),
        name="layer_c",
    )(h, h, h, p, nm, wi, wgrp, sc, wo, npl, wg, wp)


def _score_bound(gq, gk):
    slack = 1.02
    return (HEAD_DIM * SM_SCALE * LOG2E * slack) * jnp.max(jnp.abs(gq)) * jnp.max(jnp.abs(gk))


def _alibi_tables():
    kaug = np.zeros((16, LANES), np.float32)
    qtab = np.zeros((B_HEADS, 8, LANES), np.float32)
    for mp in range(2):
        base = HEAD_DIM if mp == 0 else 0
        kaug[8 + mp, base + 6:base + 9] = 128.0
        kaug[10 + mp, base + 9:base + 12] = 1.0
        qtab[:, 2 + mp, base:base + 3] = -128.0
        qtab[:, 4 + mp, base + 3:base + 6] = -1.0
    for h, slope in enumerate(ALIBI_SLOPES):
        c = np.float64(slope) * LOG2E
        pieces, rest = [], c
        for _ in range(3):
            piece = np.float64(np.float32(rest).astype(BF16))
            pieces.append(piece)
            rest -= piece
        for mp in range(2):
            base = HEAD_DIM if mp == 0 else 0
            kaug[2 * h + mp, base:base + 6] = pieces + pieces
            qtab[h, mp, base + 6:base + 12] = pieces + pieces
        qtab[h, 6, :] = np.float32(c)
    return jnp.asarray(kaug), jnp.asarray(qtab)


def _tile_gain(g, reps, scale=1.0):
    return jnp.tile(g.astype(F32) * scale, reps).reshape(1, -1)


def _trunk(x, p, prm, tq_a, tq_b, tk, tm):
    row = lambda v: v.astype(F32).reshape(1, -1)
    lam_init = 0.8 - 0.6 * math.exp(-0.3 * 0)
    gq_scale = SM_SCALE * LOG2E
    gqa = _tile_gain(prm["qn_a"][0], 8, gq_scale)
    gkv = jnp.concatenate([_tile_gain(prm["kn_a"][0], 2), jnp.ones((1, 128), F32)], axis=1)
    gqb = _tile_gain(prm["qn_b"][0], 8, gq_scale)
    gkb = _tile_gain(prm["kn_b"][0], 8)
    qa, ka, va, qb, kb, vb, gates = _proj_ab(
        x, prm["w_in_ab"][0], row(prm["norm_mix"][0]), gqa, gkv, gqb, gkb, prm["bd"], prm["kaug"], tm)
    oa = lax.cond(
        _score_bound(prm["qn_a"][0], prm["kn_a"][0]) <= BOUNDED_SCORE_LIMIT,
        lambda q, k, v: _flash_a(q, k, v, tq_a, tk, True),
        lambda q, k, v: _flash_a(q, k, v, tq_a, tk, False),
        qa, ka, va)
    lam_vecs = (row(prm["lam_q1"][0]), row(prm["lam_k1"][0]), row(prm["lam_q2"][0]),
                row(prm["lam_k2"][0]), row(prm["subln_b"][0]))
    ob = lax.cond(
        _score_bound(prm["qn_b"][0], prm["kn_b"][0]) <= BOUNDED_SCORE_LIMIT,
        lambda q, k, v: _flash_b(q, k, v, prm["qtab"], *lam_vecs, lam_init, tq_b, tk, True),
        lambda q, k, v: _flash_b(q, k, v, prm["qtab"], *lam_vecs, lam_init, tq_b, tk, False),
        qb, kb, vb)
    h = _out_ple(x, oa, ob, gates, p, 0, prm["w_out_ab"][0], row(prm["norm_ple"][0]),
                 prm["w_ple_gate"][0], prm["w_ple_proj"][0], tm)
    h = _layer_c(h, p, 1, row(prm["norm_mix"][1]), prm["w_in_c"][0], prm["w_grp_c"][0],
                 row(prm["scale_c"][0]), prm["w_out_c"][0], row(prm["norm_ple"][1]),
                 prm["w_ple_gate"][1], prm["w_ple_proj"][1], tm)
    return h


def kernel(x_prompt, x_sample, p_prompt, p_sample, norm_mix, w_in_ab, qn_a, kn_a, qn_b, kn_b,
           lam_q1, lam_k1, lam_q2, lam_k2, subln_b, w_out_ab, w_in_c, w_grp_c, scale_c, w_out_c,
           norm_ple, w_ple_gate, w_ple_proj):
    blk = np.arange(2 * MXU_DIM) // HEAD_DIM
    kaug, qtab = _alibi_tables()
    prm = dict(
        norm_mix=norm_mix, qn_a=qn_a, kn_a=kn_a, qn_b=qn_b, kn_b=kn_b,
        lam_q1=lam_q1, lam_k1=lam_k1, lam_q2=lam_q2, lam_k2=lam_k2, subln_b=subln_b,
        scale_c=scale_c, norm_ple=norm_ple,
        w_in_ab=w_in_ab.astype(BF16), w_out_ab=w_out_ab.astype(BF16), w_in_c=w_in_c.astype(BF16),
        w_grp_c=w_grp_c.astype(BF16), w_out_c=w_out_c.astype(BF16),
        w_ple_gate=w_ple_gate.astype(BF16), w_ple_proj=w_ple_proj.astype(BF16),
        bd=jnp.asarray(blk[:, None] == blk[None, :], BF16),
        kaug=kaug, qtab=qtab,
    )
    y_prompt = _trunk(x_prompt, p_prompt, prm, tq_a=256, tq_b=512, tk=512, tm=512)
    y_sample = _trunk(x_sample, p_sample, prm, tq_a=512, tq_b=512, tk=512, tm=512)
    return (y_prompt, y_sample)
```

```python
import functools
import math

import jax
import jax.numpy as jnp
import numpy as np
from jax import lax
from jax.experimental import pallas as pl
from jax.experimental.pallas import tpu as pltpu

F32 = jnp.float32
BF16 = jnp.bfloat16

D_MODEL = 1024
PLE_DIM = 256
GRID_W = 64
HEAD_DIM = 64
EPS = 1e-6
ROPE_THETA = 10000.0
A_HEADS = 8
A_KV_HEADS = 2
B_HEADS = 4
B_V_DIM = 2 * HEAD_DIM
ALIBI_SLOPES = tuple(2.0 ** (-8.0 * (h + 1) / B_HEADS) for h in range(B_HEADS))
POOL_WINDOWS = (2, 4, 8, 16)
C_GRP = 256
POOL_HALO = 8
LOG2E = math.log2(math.e)
SM_SCALE = HEAD_DIM ** -0.5

LANES = 128
MXU_DIM = 256
VMEM_LIMIT_BYTES = 56 * 1024 * 1024

_QA, _KA, _VA, _GA, _QB, _KB, _VB, _GB, _AB_END = 0, 512, 640, 768, 1280, 1792, 2304, 2816, 3328

NEG_BIG = -1e30
BOUNDED_SCORE_LIMIT = 60.0
KV_UNROLL = 16
STATIC_Q_TILES = 4


def _rms_rows(x, g):
    ms = jnp.mean(x * x, axis=-1, keepdims=True)
    return x * lax.rsqrt(ms + EPS) * g


def _nt_dot(a, b):
    return lax.dot_general(a, b, (((1,), (1,)), ((), ())), preferred_element_type=F32)


def _proj_ab_kernel(x_ref, cos_ref, sa_ref, sb_ref, nm_ref, w_ref, gqa_ref, gkv_ref, gqb_ref,
                    gkb_ref, bd_ref, kaug_ref,
                    qa_ref, ka_ref, va_ref, qb_ref, kb_ref, vb_ref, g_ref):
    x = x_ref[0]
    xn = _rms_rows(x, nm_ref[...]).astype(BF16)
    tm = x.shape[0]

    def seg(a, b):
        return jnp.dot(xn, w_ref[:, a:b], preferred_element_type=F32)

    def head_norm(z, gain):
        n = z.shape[1]
        ss = jnp.dot((z * z).astype(BF16), bd_ref[0:n, 0:n], preferred_element_type=F32)
        return z * lax.rsqrt(ss * (1.0 / HEAD_DIM) + EPS) * gain

    cos = cos_ref[...]
    sin_a = sa_ref[...]
    sin_b = sb_ref[...]

    def rope(z):
        return (z * cos + pltpu.roll(z, LANES - 16, 1) * sin_a + pltpu.roll(z, 16, 1) * sin_b)

    lane = lax.broadcasted_iota(jnp.int32, (tm, LANES), 1)
    lo = lane < HEAD_DIM
    t = pl.program_id(1) * tm + lax.broadcasted_iota(jnp.int32, (tm, LANES), 0)
    pos_hi = (t >> 7).astype(F32)
    pos_lo = (t & 127).astype(F32)

    p_a = seg(_QA, _GA)

    z = head_norm(p_a[:, 0:512], gqa_ref[...])
    for j in range(4):
        qa_ref[0, :, 128 * j:128 * (j + 1)] = rope(z[:, 128 * j:128 * (j + 1)]).astype(BF16)

    z = p_a[:, 512:768]
    zn = head_norm(z, gkv_ref[...])
    k = rope(zn[:, :128])
    k_sw = pltpu.roll(k, HEAD_DIM, 1)
    zero = jnp.zeros_like(k)
    ka_ref[0, 0, 0] = jnp.where(lo, k, zero).astype(BF16)
    ka_ref[0, 0, 1] = jnp.where(lo, zero, k_sw).astype(BF16)
    ka_ref[0, 1, 0] = jnp.where(lo, k_sw, zero).astype(BF16)
    ka_ref[0, 1, 1] = jnp.where(lo, zero, k).astype(BF16)
    v = z[:, 128:256]
    v_sw = pltpu.roll(v, HEAD_DIM, 1)
    one_a = (lane == HEAD_DIM).astype(F32)
    va_ref[0, 0] = jnp.where(lo, v, one_a).astype(BF16)
    va_ref[0, 1] = jnp.where(lo, v_sw, one_a).astype(BF16)

    g = seg(_GA, _QB)
    g_ref[0, :, 0:512] = (g * jax.nn.sigmoid(g)).astype(BF16)

    p_b = seg(_QB, _VB)
    qb_ref[0] = head_norm(p_b[:, 0:512], gqb_ref[...]).astype(BF16)
    z = head_norm(p_b[:, 512:1024], gkb_ref[...])
    for hd in range(B_HEADS):
        kh = z[:, 128 * hd:128 * (hd + 1)]
        for mp in range(2):
            aug = (kaug_ref[2 * hd + mp:2 * hd + mp + 1, :] + pos_hi * kaug_ref[8 + mp:9 + mp, :]
                   + pos_lo * kaug_ref[10 + mp:11 + mp, :])
            keep = lo if mp == 0 else jnp.logical_not(lo)
            kb_ref[0, hd, mp] = jnp.where(keep, kh, aug).astype(BF16)

    p_c = seg(_VB, _AB_END)
    one_b = (lane == 0).astype(BF16)
    for hd in range(B_HEADS):
        vb_ref[0, hd, :, 0:128] = p_c[:, 128 * hd:128 * (hd + 1)].astype(BF16)
        vb_ref[0, hd, :, 128:256] = one_b
    g = p_c[:, 512:1024]
    g_ref[0, :, 512:1024] = (g * jax.nn.sigmoid(g)).astype(BF16)


def _rope_tables(S):
    rows = S // GRID_W
    row = jnp.repeat(jnp.arange(rows), GRID_W).astype(F32)
    col = jnp.tile(jnp.arange(GRID_W), rows).astype(F32)
    half = HEAD_DIM // 2
    inv = ROPE_THETA ** (-jnp.arange(0, half, 2, dtype=F32) / half)
    ar = row[:, None] * inv
    ac = col[:, None] * inv
    ang = jnp.concatenate([ar, ar, ac, ac], axis=-1)
    cos, sin = jnp.cos(ang), jnp.sin(ang)
    first = (np.arange(HEAD_DIM) % 32) < 16
    sin_a = jnp.where(first, -sin, 0.0)
    sin_b = jnp.where(first, 0.0, sin)
    two = lambda t: jnp.concatenate([t, t], axis=-1)
    return two(cos), two(sin_a), two(sin_b)


def _proj_ab(x, w_bf, nm, gqa, gkv, gqb, gkb, bd, kaug, tm):
    B, S, _ = x.shape
    cos, sin_a, sin_b = _rope_tables(S)
    tok = lambda b, i: (b, i, 0)
    tab = pl.BlockSpec((tm, LANES), lambda b, i: (i, 0))
    const2 = lambda shp: pl.BlockSpec(shp, lambda b, i: (0, 0))
    out_shape = (
        jax.ShapeDtypeStruct((B, S, 512), BF16),
        jax.ShapeDtypeStruct((B, 2, 2, S, LANES), BF16),
        jax.ShapeDtypeStruct((B, 2, S, LANES), BF16),
        jax.ShapeDtypeStruct((B, S, 512), BF16),
        jax.ShapeDtypeStruct((B, 4, 2, S, LANES), BF16),
        jax.ShapeDtypeStruct((B, 4, S, 2 * LANES), BF16),
        jax.ShapeDtypeStruct((B, S, 1024), BF16),
    )
    out_specs = (
        pl.BlockSpec((1, tm, 512), tok),
        pl.BlockSpec((1, 2, 2, tm, LANES), lambda b, i: (b, 0, 0, i, 0)),
        pl.BlockSpec((1, 2, tm, LANES), lambda b, i: (b, 0, i, 0)),
        pl.BlockSpec((1, tm, 512), tok),
        pl.BlockSpec((1, 4, 2, tm, LANES), lambda b, i: (b, 0, 0, i, 0)),
        pl.BlockSpec((1, 4, tm, 2 * LANES), lambda b, i: (b, 0, i, 0)),
        pl.BlockSpec((1, tm, 1024), tok),
    )
    return pl.pallas_call(
        _proj_ab_kernel,
        grid=(B, S // tm),
        in_specs=[
            pl.BlockSpec((1, tm, D_MODEL), tok), tab, tab, tab,
            const2((1, D_MODEL)), const2((D_MODEL, _AB_END)),
            const2((1, 512)), const2((1, 256)), const2((1, 512)), const2((1, 512)),
            const2((2 * MXU_DIM, 2 * MXU_DIM)), const2((16, LANES)),
        ],
        out_specs=out_specs,
        out_shape=out_shape,
        compiler_params=pltpu.CompilerParams(
            dimension_semantics=("parallel", "parallel"), vmem_limit_bytes=VMEM_LIMIT_BYTES),
        name="proj_ab",
    )(x, cos, sin_a, sin_b, nm, w_bf, gqa, gkv, gqb, gkb, bd, kaug)


def _online_update(s, v, m_ref, acc_ref, r0, nrows):
    rows = pl.ds(r0, nrows)
    m_old = m_ref[rows, :]
    m_new = jnp.maximum(m_old, jnp.max(s, axis=-1, keepdims=True))
    alpha = jnp.exp2(m_old - m_new)
    p = jnp.exp2(s - m_new[:, :1]).astype(BF16)
    pv = jnp.dot(p, v, preferred_element_type=F32)
    nv = v.shape[1]
    if nv != LANES:
        alpha = jnp.concatenate([alpha] * (nv // LANES), axis=1)
    acc_ref[rows, :] = acc_ref[rows, :] * alpha + pv
    m_ref[rows, :] = m_new


def _flash_a_kernel(q_ref, k_ref, v_ref, o_ref, q_scr, m_ref, acc_ref, *, tq, tk, n_kv):
    q_scr[0:tq, :] = q_ref[0, :, 0:128]
    q_scr[tq:2 * tq, :] = q_ref[0, :, 128:256]
    m_ref[...] = jnp.full(m_ref.shape, NEG_BIG, F32)
    acc_ref[...] = jnp.zeros(acc_ref.shape, F32)

    def body(c, carry):
        ks = pl.ds(pl.multiple_of(c * tk, tk), tk)
        q = q_scr[...]
        v = v_ref[0, 0, ks, :]
        for var in range(2):
            s = _nt_dot(q, k_ref[0, 0, var, ks, :])
            _online_update(s, v, m_ref, acc_ref, var * 2 * tq, 2 * tq)
        return carry

    lax.fori_loop(0, n_kv, body, 0)
    _flash_a_finalize(o_ref, acc_ref, tq)


def _flash_a_finalize(o_ref, acc_ref, tq):
    lane = lax.broadcasted_iota(jnp.int32, (tq, LANES), 1)
    lo = lane < HEAD_DIM

    def head_out(blk):
        a = acc_ref[blk * tq:(blk + 1) * tq, :]
        return a * (1.0 / a[:, HEAD_DIM:HEAD_DIM + 1])

    o01 = jnp.where(lo, head_out(0), pltpu.roll(head_out(2), HEAD_DIM, 1))
    o23 = jnp.where(lo, head_out(1), pltpu.roll(head_out(3), HEAD_DIM, 1))
    o_ref[0, :, 0:128] = o01.astype(o_ref.dtype)
    o_ref[0, :, 128:256] = o23.astype(o_ref.dtype)


def _flash_a_bounded_kernel(q_ref, k_ref, v_ref, o_ref, q_scr, acc_ref, *, tq, tk, n_kv, unroll):
    q_scr[0:tq, :] = q_ref[0, :, 0:128]
    q_scr[tq:2 * tq, :] = q_ref[0, :, 128:256]
    acc_ref[...] = jnp.zeros(acc_ref.shape, F32)

    def body(c, carry):
        ks = pl.ds(pl.multiple_of(c * tk, tk), tk)
        q = q_scr[...]
        v = v_ref[0, 0, ks, :]
        for var in range(2):
            p = jnp.exp2(_nt_dot(q, k_ref[0, 0, var, ks, :])).astype(BF16)
            rows = pl.ds(var * 2 * tq, 2 * tq)
            acc_ref[rows, :] += jnp.dot(p, v, preferred_element_type=F32)
        return carry

    lax.fori_loop(0, n_kv, body, 0, unroll=unroll)
    _flash_a_finalize(o_ref, acc_ref, tq)


def _kv_spec(block, index_map, single_buffer):
    if single_buffer:
        return pl.BlockSpec(block, index_map, pipeline_mode=pl.Buffered(1))
    return pl.BlockSpec(block, index_map)


def _flash_a(qa, ka, va, tq, tk, bounded):
    B, S, _ = qa.shape
    single = S * LANES * 2 * 3 * 2 > 8 * 1024 * 1024
    scratch = [pltpu.VMEM((2 * tq, LANES), BF16), pltpu.VMEM((4 * tq, LANES), F32)]
    if bounded:
        kern = functools.partial(_flash_a_bounded_kernel, tq=tq, tk=tk, n_kv=S // tk,
                                 unroll=min(KV_UNROLL, S // tk))
    else:
        kern = functools.partial(_flash_a_kernel, tq=tq, tk=tk, n_kv=S // tk)
        scratch.insert(1, pltpu.VMEM((4 * tq, LANES), F32))
    return pl.pallas_call(
        kern,
        grid=(B, A_KV_HEADS, S // tq),
        in_specs=[
            pl.BlockSpec((1, tq, 256), lambda b, g, i: (b, i, g)),
            _kv_spec((1, 1, 2, S, LANES), lambda b, g, i: (b, g, 0, 0, 0), single),
            _kv_spec((1, 1, S, LANES), lambda b, g, i: (b, g, 0, 0), single),
        ],
        out_specs=pl.BlockSpec((1, tq, 256), lambda b, g, i: (b, i, g)),
        out_shape=jax.ShapeDtypeStruct((B, S, 512), BF16),
        scratch_shapes=scratch,
        compiler_params=pltpu.CompilerParams(
            dimension_semantics=("parallel", "parallel", "arbitrary"),
            vmem_limit_bytes=VMEM_LIMIT_BYTES),
        name="flash_a_bounded" if bounded else "flash_a",
    )(qa, ka, va)


def _flash_b_q_slots(slab, t0, tab_ref, q_scr, tq, with_alibi):
    lane = lax.broadcasted_iota(jnp.int32, (tq, LANES), 1)
    lo = lane < HEAD_DIM
    if with_alibi:
        t = t0 + lax.broadcasted_iota(jnp.int32, (tq, LANES), 0)
        pos_hi = (t >> 7).astype(F32)
        pos_lo = (t & 127).astype(F32)
    for mp in range(2):
        keep = lo if mp == 0 else jnp.logical_not(lo)
        if with_alibi:
            aug = (tab_ref[0, mp:mp + 1, :] + pos_hi * tab_ref[0, 2 + mp:3 + mp, :]
                   + pos_lo * tab_ref[0, 4 + mp:5 + mp, :])
            q_scr[mp, 0] = jnp.where(keep, slab, aug.astype(BF16))
            q_scr[mp, 1] = jnp.where(keep, slab, (-aug).astype(BF16))
        else:
            q_scr[mp, 0] = jnp.where(keep, slab, jnp.zeros_like(slab))


def _flash_b_finalize(acc_ref, lq1_ref, lk1_ref, lq2_ref, lk2_ref, sub_ref, tq, lam_init):
    lam = (jnp.exp(jnp.sum(lq1_ref[...] * lk1_ref[...], keepdims=True))
           - jnp.exp(jnp.sum(lq2_ref[...] * lk2_ref[...], keepdims=True)) + lam_init)
    a1 = acc_ref[0:tq, :]
    a2 = acc_ref[tq:2 * tq, :]
    o1 = a1[:, 0:B_V_DIM] * (1.0 / a1[:, B_V_DIM:B_V_DIM + 1])
    o2 = a2[:, 0:B_V_DIM] * (1.0 / a2[:, B_V_DIM:B_V_DIM + 1])
    o = o1 - lam * o2
    return (_rms_rows(o, sub_ref[...]) * (1.0 - lam_init)).astype(BF16)


def _flash_b_kernel(q_ref, k_ref, v_ref, tab_ref, lq1_ref, lk1_ref, lq2_ref, lk2_ref, sub_ref,
                    o_ref, q_scr, rel_ref, m_ref, acc_ref, *, tq, tk, n_kv, lam_init):
    _flash_b_q_slots(q_ref[0], 0, tab_ref, q_scr, tq, with_alibi=False)
    m_ref[...] = jnp.full(m_ref.shape, NEG_BIG, F32)
    acc_ref[...] = jnp.zeros(acc_ref.shape, F32)
    rel_ref[...] = (lax.broadcasted_iota(jnp.int32, (tq, tk), 0)
                    - lax.broadcasted_iota(jnp.int32, (tq, tk), 1)).astype(F32)
    neg_slope = -tab_ref[0, 6:7, 0:1]
    q0 = pl.program_id(2) * tq

    def body(c, carry):
        k0 = pl.multiple_of(c * tk, tk)
        ks = pl.ds(k0, tk)
        v = v_ref[0, 0, ks, :]
        bias = jnp.abs(rel_ref[...] + (q0 - k0).astype(F32)) * neg_slope
        for mp in range(2):
            s = _nt_dot(q_scr[mp, 0], k_ref[0, 0, mp, ks, :]) + bias
            _online_update(s, v, m_ref, acc_ref, mp * tq, tq)
        return carry

    lax.fori_loop(0, n_kv, body, 0)
    o_ref[0] = _flash_b_finalize(acc_ref, lq1_ref, lk1_ref, lq2_ref, lk2_ref, sub_ref, tq, lam_init)


def _flash_b_bounded_kernel(q_ref, k_ref, v_ref, tab_ref, lq1_ref, lk1_ref, lq2_ref, lk2_ref, sub_ref,
                            o_ref, q_scr, corr_ref, acc_ref, *, tq, tk, n_kv, lam_init, unroll):
    _flash_b_q_slots(q_ref[0], pl.program_id(2) * tq, tab_ref, q_scr, tq, with_alibi=True)
    acc_ref[...] = jnp.zeros(acc_ref.shape, F32)
    qi = pl.program_id(2)

    @pl.when(qi == 0)
    def _():
        ahead = jnp.maximum(lax.broadcasted_iota(jnp.int32, (tq, tk), 1)
                            - lax.broadcasted_iota(jnp.int32, (tq, tk), 0), 0).astype(F32)
        corr_ref[0] = jnp.zeros((tq, tk), F32)
        corr_ref[1] = ahead * (-2.0 * tab_ref[0, 6:7, 0:1])

    def body(c, carry):
        ks = pl.ds(pl.multiple_of(c * tk, tk), tk)
        v = v_ref[0, 0, ks, :]
        side = (c > qi).astype(jnp.int32)
        corr = corr_ref[(c == qi).astype(jnp.int32)]
        for mp in range(2):
            s = _nt_dot(q_scr[mp, side], k_ref[0, 0, mp, ks, :]) + corr
            rows = pl.ds(mp * tq, tq)
            acc_ref[rows, :] += jnp.dot(jnp.exp2(s).astype(BF16), v, preferred_element_type=F32)
        return carry

    lax.fori_loop(0, n_kv, body, 0, unroll=unroll)
    o_ref[0] = _flash_b_finalize(acc_ref, lq1_ref, lk1_ref, lq2_ref, lk2_ref, sub_ref, tq, lam_init)


def _flash_b_bounded_static_kernel(q_ref, k_ref, v_ref, tab_ref, ahead_ref, lq1_ref, lk1_ref, lq2_ref,
                                   lk2_ref, sub_ref, o_ref, q_scr, acc_ref, *, tq, tk, n_q, n_kv,
                                   lam_init):
    neg_2c = -2.0 * tab_ref[0, 6:7, 0:1]
    for qq in range(n_q):
        _flash_b_q_slots(q_ref[0, qq * tq:(qq + 1) * tq, :], qq * tq, tab_ref, q_scr.at[qq], tq,
                         with_alibi=True)
    for qq in range(n_q):
        for c in range(n_kv):
            v = v_ref[0, 0, c * tk:(c + 1) * tk, :]
            for mp in range(2):
                s = _nt_dot(q_scr[qq, mp, int(c > qq)], k_ref[0, 0, mp, c * tk:(c + 1) * tk, :])
                if c == qq:
                    s = s + ahead_ref[...] * neg_2c
                pv = jnp.dot(jnp.exp2(s).astype(BF16), v, preferred_element_type=F32)
                rows = slice(mp * tq, (mp + 1) * tq)
                if c == 0:
                    acc_ref[qq, rows, :] = pv
                else:
                    acc_ref[qq, rows, :] += pv
        o_ref[0, qq * tq:(qq + 1) * tq, :] = _flash_b_finalize(
            acc_ref.at[qq], lq1_ref, lk1_ref, lq2_ref, lk2_ref, sub_ref, tq, lam_init)


def _flash_b_static(qb, kb, vb, tab, lq1, lk1, lq2, lk2, subln, lam_init, tq, tk):
    B, S, _ = qb.shape
    assert tq == tk
    n_q = S // tq
    ahead = np.maximum(np.arange(tk)[None, :] - np.arange(tq)[:, None], 0).astype(np.float32)
    kern = functools.partial(_flash_b_bounded_static_kernel, tq=tq, tk=tk, n_q=n_q, n_kv=S // tk,
                             lam_init=lam_init)
    vec = lambda n: pl.BlockSpec((1, n), lambda b, h: (0, 0))
    return pl.pallas_call(
        kern,
        grid=(B, B_HEADS),
        in_specs=[
            pl.BlockSpec((1, S, LANES), lambda b, h: (b, 0, h)),
            pl.BlockSpec((1, 1, 2, S, LANES), lambda b, h: (b, h, 0, 0, 0)),
            pl.BlockSpec((1, 1, S, 2 * LANES), lambda b, h: (b, h, 0, 0)),
            pl.BlockSpec((1, 8, LANES), lambda b, h: (h, 0, 0)),
            pl.BlockSpec((tq, tk), lambda b, h: (0, 0)),
            vec(HEAD_DIM), vec(HEAD_DIM), vec(HEAD_DIM), vec(HEAD_DIM), vec(B_V_DIM),
        ],
        out_specs=pl.BlockSpec((1, S, LANES), lambda b, h: (b, 0, h)),
        out_shape=jax.ShapeDtypeStruct((B, S, 512), BF16),
        scratch_shapes=[pltpu.VMEM((n_q, 2, 2, tq, LANES), BF16),
                        pltpu.VMEM((n_q, 2 * tq, 2 * LANES), F32)],
        compiler_params=pltpu.CompilerParams(
            dimension_semantics=("parallel", "parallel"), vmem_limit_bytes=VMEM_LIMIT_BYTES),
        name="flash_b_bounded_static",
    )(qb, kb, vb, tab, jnp.asarray(ahead), lq1, lk1, lq2, lk2, subln)


def _flash_b(qb, kb, vb, tab, lq1, lk1, lq2, lk2, subln, lam_init, tq, tk, bounded):
    B, S, _ = qb.shape
    assert tq == tk
    if bounded and S // tq <= STATIC_Q_TILES:
        return _flash_b_static(qb, kb, vb, tab, lq1, lk1, lq2, lk2, subln, lam_init, tq, tk)
    single = S * LANES * 2 * 4 * 2 > 8 * 1024 * 1024
    acc = pltpu.VMEM((2 * tq, 2 * LANES), F32)
    if bounded:
        kern = functools.partial(_flash_b_bounded_kernel, tq=tq, tk=tk, n_kv=S // tk,
                                 lam_init=lam_init, unroll=min(KV_UNROLL, S // tk))
        scratch = [pltpu.VMEM((2, 2, tq, LANES), BF16), pltpu.VMEM((2, tq, tk), F32), acc]
    else:
        kern = functools.partial(_flash_b_kernel, tq=tq, tk=tk, n_kv=S // tk, lam_init=lam_init)
        scratch = [pltpu.VMEM((2, 1, tq, LANES), BF16), pltpu.VMEM((tq, tk), F32),
                   pltpu.VMEM((2 * tq, LANES), F32), acc]
    vec = lambda n: pl.BlockSpec((1, n), lambda b, h, i: (0, 0))
    return pl.pallas_call(
        kern,
        grid=(B, B_HEADS, S // tq),
        in_specs=[
            pl.BlockSpec((1, tq, LANES), lambda b, h, i: (b, i, h)),
            _kv_spec((1, 1, 2, S, LANES), lambda b, h, i: (b, h, 0, 0, 0), single),
            _kv_spec((1, 1, S, 2 * LANES), lambda b, h, i: (b, h, 0, 0), single),
            pl.BlockSpec((1, 8, LANES), lambda b, h, i: (h, 0, 0)),
            vec(HEAD_DIM), vec(HEAD_DIM), vec(HEAD_DIM), vec(HEAD_DIM), vec(B_V_DIM),
        ],
        out_specs=pl.BlockSpec((1, tq, LANES), lambda b, h, i: (b, i, h)),
        out_shape=jax.ShapeDtypeStruct((B, S, 512), BF16),
        scratch_shapes=scratch,
        compiler_params=pltpu.CompilerParams(
            dimension_semantics=("parallel", "parallel", "arbitrary"),
            vmem_limit_bytes=VMEM_LIMIT_BYTES),
        name="flash_b_bounded" if bounded else "flash_b",
    )(qb, kb, vb, tab, lq1, lk1, lq2, lk2, subln)


def _ple_tail(h1, p, npl, wg_ref, wp_ref):
    rn = _rms_rows(h1, npl).astype(BF16)
    gate = jax.nn.sigmoid(jnp.dot(rn, wg_ref[...], preferred_element_type=F32))
    return h1 + gate * jnp.dot(p.astype(BF16), wp_ref[...], preferred_element_type=F32)


def _out_ple_kernel(h_ref, oa_ref, ob_ref, g_ref, p_ref, wo_ref, npl_ref, wg_ref, wp_ref, out_ref):
    g = g_ref[0]
    y = jnp.concatenate([oa_ref[0] * g[:, 0:512], ob_ref[0] * g[:, 512:1024]], axis=1)
    h1 = h_ref[0] + jnp.dot(y, wo_ref[...], preferred_element_type=F32)
    out_ref[0] = _ple_tail(h1, p_ref[0, 0], npl_ref[...], wg_ref, wp_ref)


def _out_ple(h, oa, ob, gates, p, layer, wo, npl, wg, wp, tm):
    B, S, _ = h.shape
    tok = lambda b, i: (b, i, 0)
    const2 = lambda shp: pl.BlockSpec(shp, lambda b, i: (0, 0))
    return pl.pallas_call(
        _out_ple_kernel,
        grid=(B, S // tm),
        in_specs=[
            pl.BlockSpec((1, tm, D_MODEL), tok),
            pl.BlockSpec((1, tm, 512), tok),
            pl.BlockSpec((1, tm, 512), tok),
            pl.BlockSpec((1, tm, 1024), tok),
            pl.BlockSpec((1, 1, tm, PLE_DIM), lambda b, i: (layer, b, i, 0)),
            const2((1024, D_MODEL)), const2((1, D_MODEL)), const2((D_MODEL, D_MODEL)),
            const2((PLE_DIM, D_MODEL)),
        ],
        out_specs=pl.BlockSpec((1, tm, D_MODEL), tok),
        out_shape=jax.ShapeDtypeStruct((B, S, D_MODEL), F32),
        compiler_params=pltpu.CompilerParams(
            dimension_semantics=("parallel", "parallel"), vmem_limit_bytes=VMEM_LIMIT_BYTES),
        name="out_ple",
    )(h, oa, ob, gates, p, wo, npl, wg, wp)


def _layer_c_kernel(h_ref, hp_ref, hn_ref, p_ref, nm_ref, wi_ref, wgrp_ref, sc_ref, wo_ref,
                    npl_ref, wg_ref, wp_ref, out_ref, *, tm, S):
    i = pl.program_id(1)
    nt = pl.num_programs(1)
    h = h_ref[0]
    hp = hp_ref[0] * (i > 0).astype(F32)
    hn = hn_ref[0] * (i < nt - 1).astype(F32)
    n = tm + 2 * POOL_HALO
    h_ext = jnp.concatenate([hp, h, hn], axis=0)
    xn = _rms_rows(h_ext, nm_ref[...]).astype(BF16)
    u_ext = jnp.dot(xn, wi_ref[:, 0:1024], preferred_element_type=F32)
    g = jnp.dot(xn[POOL_HALO:POOL_HALO + tm], wi_ref[:, 1024:2048], preferred_element_type=F32)

    t = i * tm + lax.broadcasted_iota(jnp.int32, (tm, C_GRP), 0)
    mixed = []
    for gi, w in enumerate(POOL_WINDOWS):
        u = u_ext[:, C_GRP * gi:C_GRP * (gi + 1)]
        win = u + pltpu.roll(u, 1, 0)
        half = 1
        while 2 * half < w:
            win = pltpu.roll(win, half, 0) + pltpu.roll(win, n - half, 0)
            half *= 2
        cnt = (jnp.minimum(t + w // 2, S) - jnp.maximum(t - w // 2, 0)).astype(F32)
        pooled = win[POOL_HALO:POOL_HALO + tm] / cnt - u[POOL_HALO:POOL_HALO + tm]
        mixed.append(jnp.dot(pooled.astype(BF16), wgrp_ref[gi], preferred_element_type=F32))
    mixed = jnp.concatenate(mixed, axis=1)
    y = ((mixed * sc_ref[...]) * (g * jax.nn.sigmoid(g))).astype(BF16)
    h1 = h + jnp.dot(y, wo_ref[...], preferred_element_type=F32)
    out_ref[0] = _ple_tail(h1, p_ref[0, 0], npl_ref[...], wg_ref, wp_ref)


def _layer_c(h, p, layer, nm, wi, wgrp, sc, wo, npl, wg, wp, tm):
    B, S, _ = h.shape
    tok = lambda b, i: (b, i, 0)
    const2 = lambda shp: pl.BlockSpec(shp, lambda b, i: (0, 0))
    per = tm // POOL_HALO
    last = S // POOL_HALO - 1
    kern = functools.partial(_layer_c_kernel, tm=tm, S=S)
    return pl.pallas_call(
        kern,
        grid=(B, S // tm),
        in_specs=[
            pl.BlockSpec((1, tm, D_MODEL), tok),
            pl.BlockSpec((1, POOL_HALO, D_MODEL), lambda b, i: (b, jnp.maximum(i * per - 1, 0), 0)),
            pl.BlockSpec((1, POOL_HALO, D_MODEL), lambda b, i: (b, jnp.minimum((i + 1) * per, last), 0)),
            pl.BlockSpec((1, 1, tm, PLE_DIM), lambda b, i: (layer, b, i, 0)),
            const2((1, D_MODEL)), const2((D_MODEL, 2048)),
            pl.BlockSpec((4, C_GRP, C_GRP), lambda b, i: (0, 0, 0)),
            const2((1, 1024)), const2((1024, D_MODEL)), const2((1, D_MODEL)),
            const2((D_MODEL, D_MODEL)), const2((PLE_DIM, D_MODEL)),
        ],
        out_specs=pl.BlockSpec((1, tm, D_MODEL), tok),
        out_shape=jax.ShapeDtypeStruct((B, S, D_MODEL), F32),
        compiler_params=pltpu.CompilerParams(
            dimension_semantics=("parallel", "parallel"), vmem_limit_bytes=VMEM_LIMIT_BYTES),
        name="layer_c",
    )(h, h, h, p, nm, wi, wgrp, sc, wo, npl, wg, wp)


def _score_bound(gq, gk):
    slack = 1.02
    return (HEAD_DIM * SM_SCALE * LOG2E * slack) * jnp.max(jnp.abs(gq)) * jnp.max(jnp.abs(gk))


def _alibi_tables():
    kaug = np.zeros((16, LANES), np.float32)
    qtab = np.zeros((B_HEADS, 8, LANES), np.float32)
    for mp in range(2):
        base = HEAD_DIM if mp == 0 else 0
        kaug[8 + mp, base + 6:base + 9] = 128.0
        kaug[10 + mp, base + 9:base + 12] = 1.0
        qtab[:, 2 + mp, base:base + 3] = -128.0
        qtab[:, 4 + mp, base + 3:base + 6] = -1.0
    for h, slope in enumerate(ALIBI_SLOPES):
        c = np.float64(slope) * LOG2E
        pieces, rest = [], c
        for _ in range(3):
            piece = np.float64(np.float32(rest).astype(BF16))
            pieces.append(piece)
            rest -= piece
        for mp in range(2):
            base = HEAD_DIM if mp == 0 else 0
            kaug[2 * h + mp, base:base + 6] = pieces + pieces
            qtab[h, mp, base + 6:base + 12] = pieces + pieces
        qtab[h, 6, :] = np.float32(c)
    return jnp.asarray(kaug), jnp.asarray(qtab)


def _tile_gain(g, reps, scale=1.0):
    return jnp.tile(g.astype(F32) * scale, reps).reshape(1, -1)


def _trunk(x, p, prm, tq_a, tq_b, tk, tm):
    row = lambda v: v.astype(F32).reshape(1, -1)
    lam_init = 0.8 - 0.6 * math.exp(-0.3 * 0)
    gq_scale = SM_SCALE * LOG2E
    gqa = _tile_gain(prm["qn_a"][0], 8, gq_scale)
    gkv = jnp.concatenate([_tile_gain(prm["kn_a"][0], 2), jnp.ones((1, 128), F32)], axis=1)
    gqb = _tile_gain(prm["qn_b"][0], 8, gq_scale)
    gkb = _tile_gain(prm["kn_b"][0], 8)
    qa, ka, va, qb, kb, vb, gates = _proj_ab(
        x, prm["w_in_ab"][0], row(prm["norm_mix"][0]), gqa, gkv, gqb, gkb, prm["bd"], prm["kaug"], tm)
    oa = lax.cond(
        _score_bound(prm["qn_a"][0], prm["kn_a"][0]) <= BOUNDED_SCORE_LIMIT,
        lambda q, k, v: _flash_a(q, k, v, tq_a, tk, True),
        lambda q, k, v: _flash_a(q, k, v, tq_a, tk, False),
        qa, ka, va)
    lam_vecs = (row(prm["lam_q1"][0]), row(prm["lam_k1"][0]), row(prm["lam_q2"][0]),
                row(prm["lam_k2"][0]), row(prm["subln_b"][0]))
    ob = lax.cond(
        _score_bound(prm["qn_b"][0], prm["kn_b"][0]) <= BOUNDED_SCORE_LIMIT,
        lambda q, k, v: _flash_b(q, k, v, prm["qtab"], *lam_vecs, lam_init, tq_b, tk, True),
        lambda q, k, v: _flash_b(q, k, v, prm["qtab"], *lam_vecs, lam_init, tq_b, tk, False),
        qb, kb, vb)
    h = _out_ple(x, oa, ob, gates, p, 0, prm["w_out_ab"][0], row(prm["norm_ple"][0]),
                 prm["w_ple_gate"][0], prm["w_ple_proj"][0], tm)
    h = _layer_c(h, p, 1, row(prm["norm_mix"][1]), prm["w_in_c"][0], prm["w_grp_c"][0],
                 row(prm["scale_c"][0]), prm["w_out_c"][0], row(prm["norm_ple"][1]),
                 prm["w_ple_gate"][1], prm["w_ple_proj"][1], tm)
    return h


def kernel(x_prompt, x_sample, p_prompt, p_sample, norm_mix, w_in_ab, qn_a, kn_a, qn_b, kn_b,
           lam_q1, lam_k1, lam_q2, lam_k2, subln_b, w_out_ab, w_in_c, w_grp_c, scale_c, w_out_c,
           norm_ple, w_ple_gate, w_ple_proj):
    blk = np.arange(2 * MXU_DIM) // HEAD_DIM
    kaug, qtab = _alibi_tables()
    prm = dict(
        norm_mix=norm_mix, qn_a=qn_a, kn_a=kn_a, qn_b=qn_b, kn_b=kn_b,
        lam_q1=lam_q1, lam_k1=lam_k1, lam_q2=lam_q2, lam_k2=lam_k2, subln_b=subln_b,
        scale_c=scale_c, norm_ple=norm_ple,
        w_in_ab=w_in_ab.astype(BF16), w_out_ab=w_out_ab.astype(BF16), w_in_c=w_in_c.astype(BF16),
        w_grp_c=w_grp_c.astype(BF16), w_out_c=w_out_c.astype(BF16),
        w_ple_gate=w_ple_gate.astype(BF16), w_ple_proj=w_ple_proj.astype(BF16),
        bd=jnp.asarray(blk[:, None] == blk[None, :], BF16),
        kaug=kaug, qtab=qtab,
    )
    y_prompt = _trunk(x_prompt, p_prompt, prm, tq_a=256, tq_b=512, tk=512, tm=1024)
    y_sample = _trunk(x_sample, p_sample, prm, tq_a=512, tq_b=512, tk=512, tm=1024)
    return (y_prompt, y_sample)
```

```python
import functools
import math

import jax
import jax.numpy as jnp
import numpy as np
from jax import lax
from jax.experimental import pallas as pl
from jax.experimental.pallas import tpu as pltpu

F32 = jnp.float32
BF16 = jnp.bfloat16

D_MODEL = 1024
PLE_DIM = 256
GRID_W = 64
HEAD_DIM = 64
EPS = 1e-6
ROPE_THETA = 10000.0
A_HEADS = 8
A_KV_HEADS = 2
B_HEADS = 4
B_V_DIM = 2 * HEAD_DIM
ALIBI_SLOPES = tuple(2.0 ** (-8.0 * (h + 1) / B_HEADS) for h in range(B_HEADS))
POOL_WINDOWS = (2, 4, 8, 16)
C_GRP = 256
POOL_HALO = 8
LOG2E = math.log2(math.e)
SM_SCALE = HEAD_DIM ** -0.5

LANES = 128
BF16_ROWS = 16
MXU_DIM = 256
VMEM_LIMIT_BYTES = 56 * 1024 * 1024

VA_ROWS = HEAD_DIM + BF16_ROWS
VB_ROWS = B_V_DIM + BF16_ROWS

_QA, _KA, _VA, _GA, _QB, _KB, _VB, _GB, _AB_END = 0, 512, 640, 768, 1280, 1792, 2304, 2816, 3328
_S_KB, _S_G, _S_END = 128, 640, 1664
_T_QB, _T_VA, _T_VB, _T_END = 512, 1024, 1152, 1664

NEG_BIG = -1e30
BOUNDED_SCORE_LIMIT = 60.0
KV_GROUP = 4
KV_UNROLL = 4


def _rms_rows(x, g):
    ms = jnp.mean(x * x, axis=-1, keepdims=True)
    return x * lax.rsqrt(ms + EPS) * g


def _rms_cols(x, g):
    ms = jnp.mean(x * x, axis=0, keepdims=True)
    return x * lax.rsqrt(ms + EPS) * g


def _nt_dot(a, b):
    return lax.dot_general(a, b, (((1,), (1,)), ((), ())), preferred_element_type=F32)


def _to_token_major(xt):
    return xt.astype(F32).T.astype(BF16)


def _proj_ab_kernel(x_ref, cos_ref, sa_ref, sb_ref, cost_ref, sint_ref, nm_ref, ws_ref, wt_ref,
                    gqat_ref, gqbt_ref, gka_ref, gkb_ref, bd_ref, kaug_ref,
                    qa_ref, ka_ref, va_ref, qb_ref, kb_ref, vb_ref, g_ref):
    x = x_ref[0]
    xn = _rms_rows(x, nm_ref[...]).astype(BF16)
    tm = x.shape[0]

    def seg_t(a, b):
        return _nt_dot(wt_ref[a:b, :], xn)

    cos_t = cost_ref[...]
    sin_t = sint_ref[...]
    z = seg_t(0, _T_QB)
    for h in range(A_HEADS):
        zh = _rms_cols(z[64 * h:64 * (h + 1)], gqat_ref[...])
        sw = jnp.concatenate([zh[16:32], zh[0:16], zh[48:64], zh[32:48]], axis=0)
        r0 = 64 * (h % 2)
        qa_ref[0, h // 2, r0:r0 + 64, :] = (zh * cos_t + sw * sin_t).astype(BF16)
    z = seg_t(_T_QB, _T_VA)
    for h in range(2 * B_HEADS):
        r0 = 64 * (h % 2)
        qb_ref[0, h // 2, r0:r0 + 64, :] = _rms_cols(z[64 * h:64 * (h + 1)], gqbt_ref[...]).astype(BF16)
    ones_rows = (lax.broadcasted_iota(jnp.int32, (BF16_ROWS, tm), 0) == 0).astype(BF16)
    z = seg_t(_T_VA, _T_END)
    for kv in range(A_KV_HEADS):
        va_ref[0, kv, 0:HEAD_DIM, :] = z[64 * kv:64 * (kv + 1)].astype(BF16)
        va_ref[0, kv, HEAD_DIM:VA_ROWS, :] = ones_rows
    for hd in range(B_HEADS):
        r0 = _T_VB - _T_VA + B_V_DIM * hd
        vb_ref[0, hd, 0:B_V_DIM, :] = z[r0:r0 + B_V_DIM].astype(BF16)
        vb_ref[0, hd, B_V_DIM:VB_ROWS, :] = ones_rows

    def seg(a, b):
        return jnp.dot(xn, ws_ref[:, a:b], preferred_element_type=F32)

    def head_norm(z, gain):
        n = z.shape[1]
        ss = jnp.dot((z * z).astype(BF16), bd_ref[0:n, 0:n], preferred_element_type=F32)
        return z * lax.rsqrt(ss * (1.0 / HEAD_DIM) + EPS) * gain

    cos = cos_ref[...]
    sin_a = sa_ref[...]
    sin_b = sb_ref[...]

    def rope(z):
        return (z * cos + pltpu.roll(z, LANES - 16, 1) * sin_a + pltpu.roll(z, 16, 1) * sin_b)

    lane = lax.broadcasted_iota(jnp.int32, (tm, LANES), 1)
    lo = lane < HEAD_DIM
    t = pl.program_id(1) * tm + lax.broadcasted_iota(jnp.int32, (tm, LANES), 0)
    pos_hi = (t >> 7).astype(F32)
    pos_lo = (t & 127).astype(F32)

    p_k = seg(0, _S_G)
    k = rope(head_norm(p_k[:, 0:_S_KB], gka_ref[...]))
    k_sw = pltpu.roll(k, HEAD_DIM, 1)
    zero = jnp.zeros_like(k)
    ka_ref[0, 0, 0] = jnp.where(lo, k, zero).astype(BF16)
    ka_ref[0, 0, 1] = jnp.where(lo, zero, k_sw).astype(BF16)
    ka_ref[0, 1, 0] = jnp.where(lo, k_sw, zero).astype(BF16)
    ka_ref[0, 1, 1] = jnp.where(lo, zero, k).astype(BF16)

    z = head_norm(p_k[:, _S_KB:_S_G], gkb_ref[...])
    for hd in range(B_HEADS):
        kh = z[:, 128 * hd:128 * (hd + 1)]
        for mp in range(2):
            aug = (kaug_ref[2 * hd + mp:2 * hd + mp + 1, :] + pos_hi * kaug_ref[8 + mp:9 + mp, :]
                   + pos_lo * kaug_ref[10 + mp:11 + mp, :])
            keep = lo if mp == 0 else jnp.logical_not(lo)
            kb_ref[0, hd, mp] = jnp.where(keep, kh, aug).astype(BF16)

    g = seg(_S_G, _S_END)
    g_ref[0] = (g * jax.nn.sigmoid(g)).astype(BF16)


def _rope_tables(S):
    rows = S // GRID_W
    row = jnp.repeat(jnp.arange(rows), GRID_W).astype(F32)
    col = jnp.tile(jnp.arange(GRID_W), rows).astype(F32)
    half = HEAD_DIM // 2
    inv = ROPE_THETA ** (-jnp.arange(0, half, 2, dtype=F32) / half)
    ar = row[:, None] * inv
    ac = col[:, None] * inv
    ang = jnp.concatenate([ar, ar, ac, ac], axis=-1)
    cos, sin = jnp.cos(ang), jnp.sin(ang)
    first = (np.arange(HEAD_DIM) % 32) < 16
    sin_a = jnp.where(first, -sin, 0.0)
    sin_b = jnp.where(first, 0.0, sin)
    two = lambda t: jnp.concatenate([t, t], axis=-1)
    return two(cos), two(sin_a), two(sin_b), cos.T, jnp.where(first, -sin, sin).T


def _proj_ab(x, ws, wt, nm, gqat, gqbt, gka, gkb, bd, kaug, tm):
    B, S, _ = x.shape
    cos, sin_a, sin_b, cos_t, sin_t = _rope_tables(S)
    tok = lambda b, i: (b, i, 0)
    tab = pl.BlockSpec((tm, LANES), lambda b, i: (i, 0))
    tab_t = pl.BlockSpec((HEAD_DIM, tm), lambda b, i: (0, i))
    const2 = lambda shp: pl.BlockSpec(shp, lambda b, i: (0, 0))
    chan = lambda n, r: pl.BlockSpec((1, n, r, tm), lambda b, i: (b, 0, 0, i))
    out_shape = (
        jax.ShapeDtypeStruct((B, 4, LANES, S), BF16),
        jax.ShapeDtypeStruct((B, 2, 2, S, LANES), BF16),
        jax.ShapeDtypeStruct((B, 2, VA_ROWS, S), BF16),
        jax.ShapeDtypeStruct((B, 4, LANES, S), BF16),
        jax.ShapeDtypeStruct((B, 4, 2, S, LANES), BF16),
        jax.ShapeDtypeStruct((B, 4, VB_ROWS, S), BF16),
        jax.ShapeDtypeStruct((B, S, 1024), BF16),
    )
    out_specs = (
        chan(4, LANES),
        pl.BlockSpec((1, 2, 2, tm, LANES), lambda b, i: (b, 0, 0, i, 0)),
        chan(2, VA_ROWS),
        chan(4, LANES),
        pl.BlockSpec((1, 4, 2, tm, LANES), lambda b, i: (b, 0, 0, i, 0)),
        chan(4, VB_ROWS),
        pl.BlockSpec((1, tm, 1024), tok),
    )
    return pl.pallas_call(
        _proj_ab_kernel,
        grid=(B, S // tm),
        in_specs=[
            pl.BlockSpec((1, tm, D_MODEL), tok), tab, tab, tab, tab_t, tab_t,
            const2((1, D_MODEL)), const2((D_MODEL, _S_END)), const2((_T_END, D_MODEL)),
            const2((HEAD_DIM, tm)), const2((HEAD_DIM, tm)), const2((1, 128)), const2((1, 512)),
            const2((2 * MXU_DIM, 2 * MXU_DIM)), const2((16, LANES)),
        ],
        out_specs=out_specs,
        out_shape=out_shape,
        compiler_params=pltpu.CompilerParams(
            dimension_semantics=("parallel", "parallel"), vmem_limit_bytes=VMEM_LIMIT_BYTES),
        name="proj_ab",
    )(x, cos, sin_a, sin_b, cos_t, sin_t, nm, ws, wt, gqat, gqbt, gka, gkb, bd, kaug)


def _online_update(s, v, m_ref, acc_ref, r0, nrows):
    rows = pl.ds(r0, nrows)
    m_old = m_ref[rows, :]
    m_new = jnp.maximum(m_old, jnp.max(s, axis=-1, keepdims=True))
    alpha = jnp.exp2(m_old - m_new)
    p = jnp.exp2(s - m_new[:, :1]).astype(BF16)
    pv = jnp.dot(p, v, preferred_element_type=F32)
    nv = v.shape[1]
    if nv != LANES:
        alpha = jnp.concatenate([alpha] * (nv // LANES), axis=1)
    acc_ref[rows, :] = acc_ref[rows, :] * alpha + pv
    m_ref[rows, :] = m_new


def _flash_a_online_kernel(qt_ref, k_ref, vt_ref, o_ref, q_scr, v_scr, m_ref, acc_ref, *, tq, tk, S):
    q_scr[0:tq, :] = _to_token_major(qt_ref[0, 0])
    q_scr[tq:2 * tq, :] = _to_token_major(qt_ref[0, 1])

    @pl.when(pl.program_id(2) == 0)
    def _():
        for c in range(S // tk):
            cols = slice(c * tk, (c + 1) * tk)
            blk = jnp.concatenate([vt_ref[0, 0, :, cols], jnp.zeros((LANES - VA_ROWS, tk), BF16)], axis=0)
            v_scr[cols, :] = _to_token_major(blk)

    m_ref[...] = jnp.full(m_ref.shape, NEG_BIG, F32)
    acc_ref[...] = jnp.zeros(acc_ref.shape, F32)

    def body(c, carry):
        ks = pl.ds(pl.multiple_of(c * tk, tk), tk)
        q = q_scr[...]
        v = v_scr[ks, :]
        for var in range(2):
            s = _nt_dot(q, k_ref[0, 0, var, ks, :])
            _online_update(s, v, m_ref, acc_ref, var * 2 * tq, 2 * tq)
        return carry

    lax.fori_loop(0, S // tk, body, 0)

    lane = lax.broadcasted_iota(jnp.int32, (tq, LANES), 1)
    lo = lane < HEAD_DIM

    def head_out(blk):
        a = acc_ref[blk * tq:(blk + 1) * tq, :]
        return a * (1.0 / a[:, HEAD_DIM:HEAD_DIM + 1])

    o01 = jnp.where(lo, head_out(0), pltpu.roll(head_out(2), HEAD_DIM, 1))
    o23 = jnp.where(lo, head_out(1), pltpu.roll(head_out(3), HEAD_DIM, 1))
    o_ref[0, :, 0:128] = o01.astype(BF16)
    o_ref[0, :, 128:256] = o23.astype(BF16)


def _flash_a_bounded_kernel(qt_ref, k_ref, vt_ref, o_ref, qt_scr, acc_ref, *, tq, big, n_grp, unroll):
    qt_scr[:, 0:tq] = qt_ref[0, 0]
    qt_scr[:, tq:2 * tq] = qt_ref[0, 1]
    acc_ref[...] = jnp.zeros(acc_ref.shape, F32)

    def body(c, carry):
        ks = pl.ds(pl.multiple_of(c * big, big), big)
        for var in range(2):
            st = jnp.dot(k_ref[0, 0, var, ks, :], qt_scr[...], preferred_element_type=F32)
            acc_ref[var] += jnp.dot(vt_ref[0, 0, :, ks], jnp.exp2(st).astype(BF16),
                                    preferred_element_type=F32)
        return carry

    lax.fori_loop(0, n_grp, body, 0, unroll=unroll)

    outs = []
    for var in range(2):
        a = acc_ref[var]
        outs.append(a[0:HEAD_DIM] * (1.0 / a[HEAD_DIM:HEAD_DIM + 1]))
    for pr in range(2):
        cols = slice(pr * tq, (pr + 1) * tq)
        pair_t = jnp.concatenate([outs[0][:, cols], outs[1][:, cols]], axis=0)
        o_ref[0, :, 128 * pr:128 * (pr + 1)] = pair_t.T.astype(BF16)


def _kv_spec(block, index_map, single_buffer):
    if single_buffer:
        return pl.BlockSpec(block, index_map, pipeline_mode=pl.Buffered(1))
    return pl.BlockSpec(block, index_map)


def _flash_a(qa_t, ka, va_t, tq, tk, bounded):
    B, _, _, S = qa_t.shape
    single = S * LANES * 2 * 3 * 2 > 8 * 1024 * 1024
    if bounded:
        grp = min(KV_GROUP, S // tk)
        n_grp = S // (tk * grp)
        kern = functools.partial(_flash_a_bounded_kernel, tq=tq, big=tk * grp, n_grp=n_grp,
                                 unroll=min(KV_UNROLL, n_grp))
        scratch = [pltpu.VMEM((LANES, 2 * tq), BF16), pltpu.VMEM((2, VA_ROWS, 2 * tq), F32)]
    else:
        kern = functools.partial(_flash_a_online_kernel, tq=tq, tk=tk, S=S)
        scratch = [pltpu.VMEM((2 * tq, LANES), BF16), pltpu.VMEM((S, LANES), BF16),
                   pltpu.VMEM((4 * tq, LANES), F32), pltpu.VMEM((4 * tq, LANES), F32)]
    return pl.pallas_call(
        kern,
        grid=(B, A_KV_HEADS, S // tq),
        in_specs=[
            pl.BlockSpec((1, 2, LANES, tq), lambda b, g, i: (b, g, 0, i)),
            _kv_spec((1, 1, 2, S, LANES), lambda b, g, i: (b, g, 0, 0, 0), single),
            _kv_spec((1, 1, VA_ROWS, S), lambda b, g, i: (b, g, 0, 0), single),
        ],
        out_specs=pl.BlockSpec((1, tq, 256), lambda b, g, i: (b, i, g)),
        out_shape=jax.ShapeDtypeStruct((B, S, 512), BF16),
        scratch_shapes=scratch,
        compiler_params=pltpu.CompilerParams(
            dimension_semantics=("parallel", "parallel", "arbitrary"),
            vmem_limit_bytes=VMEM_LIMIT_BYTES),
        name="flash_a_bounded" if bounded else "flash_a_online",
    )(qa_t, ka, va_t)


def _flash_b_lambda(lq1_ref, lk1_ref, lq2_ref, lk2_ref, lam_init):
    return (jnp.exp(jnp.sum(lq1_ref[...] * lk1_ref[...], keepdims=True))
            - jnp.exp(jnp.sum(lq2_ref[...] * lk2_ref[...], keepdims=True)) + lam_init)


def _flash_b_online_kernel(qt_ref, k_ref, vt_ref, tab_ref, lq1_ref, lk1_ref, lq2_ref, lk2_ref, sub_ref,
                           o_ref, q_scr, v_scr, rel_ref, m_ref, acc_ref, *, tq, tk, S, lam_init):
    lane = lax.broadcasted_iota(jnp.int32, (tq, LANES), 1)
    lo = lane < HEAD_DIM
    slab = _to_token_major(qt_ref[0, 0])
    zero = jnp.zeros_like(slab)
    q_scr[0] = jnp.where(lo, slab, zero)
    q_scr[1] = jnp.where(lo, zero, slab)

    @pl.when(pl.program_id(2) == 0)
    def _():
        pad = jnp.zeros((LANES - BF16_ROWS, tk), BF16)
        for c in range(S // tk):
            cols = slice(c * tk, (c + 1) * tk)
            v_scr[cols, 0:B_V_DIM] = _to_token_major(vt_ref[0, 0, 0:B_V_DIM, cols])
            ones_blk = jnp.concatenate([vt_ref[0, 0, B_V_DIM:VB_ROWS, cols], pad], axis=0)
            v_scr[cols, B_V_DIM:2 * LANES] = _to_token_major(ones_blk)

    m_ref[...] = jnp.full(m_ref.shape, NEG_BIG, F32)
    acc_ref[...] = jnp.zeros(acc_ref.shape, F32)
    rel_ref[...] = (lax.broadcasted_iota(jnp.int32, (tq, tk), 0)
                    - lax.broadcasted_iota(jnp.int32, (tq, tk), 1)).astype(F32)
    neg_slope = -tab_ref[0, 6:7, 0:1]
    q0 = pl.program_id(2) * tq

    def body(c, carry):
        k0 = pl.multiple_of(c * tk, tk)
        ks = pl.ds(k0, tk)
        v = v_scr[ks, :]
        bias = jnp.abs(rel_ref[...] + (q0 - k0).astype(F32)) * neg_slope
        for mp in range(2):
            s = _nt_dot(q_scr[mp], k_ref[0, 0, mp, ks, :]) + bias
            _online_update(s, v, m_ref, acc_ref, mp * tq, tq)
        return carry

    lax.fori_loop(0, S // tk, body, 0)

    lam = _flash_b_lambda(lq1_ref, lk1_ref, lq2_ref, lk2_ref, lam_init)
    a1 = acc_ref[0:tq, :]
    a2 = acc_ref[tq:2 * tq, :]
    o1 = a1[:, 0:B_V_DIM] * (1.0 / a1[:, B_V_DIM:B_V_DIM + 1])
    o2 = a2[:, 0:B_V_DIM] * (1.0 / a2[:, B_V_DIM:B_V_DIM + 1])
    o = o1 - lam * o2
    o_ref[0] = (_rms_rows(o, sub_ref[...]) * (1.0 - lam_init)).astype(BF16)


def _flash_b_bounded_kernel(qt_ref, k_ref, vt_ref, tab_ref, aug_ref, ahead_ref, lq1_ref, lk1_ref,
                            lq2_ref, lk2_ref, subt_ref, o_ref, qt_scr, acc_ref, *, tq, tk, grp, n_grp,
                            n_q, unroll, lam_init):
    big = grp * tk
    neg_2c = -2.0 * tab_ref[0, 6:7, 0:1]
    lam = _flash_b_lambda(lq1_ref, lk1_ref, lq2_ref, lk2_ref, lam_init)
    static = n_grp == 1
    zpad = jnp.zeros((HEAD_DIM - BF16_ROWS, tq), BF16)

    for qq in range(n_q):
        qi = qq if static else pl.program_id(2) * n_q + qq
        cols = slice(qq * tq, (qq + 1) * tq)
        t = qi * tq + lax.broadcasted_iota(jnp.int32, (BF16_ROWS, tq), 1)
        pos_hi = (t >> 7).astype(F32)
        pos_lo = (t & 127).astype(F32)
        for mp in range(2):
            rep = lambda r: jnp.concatenate([aug_ref[0, mp, r]] * (tq // LANES), axis=1)
            aug = rep(0) + pos_hi * rep(1) + pos_lo * rep(2)
            for side in range(2):
                rows = jnp.concatenate([(aug if side == 0 else -aug).astype(BF16), zpad], axis=0)
                if mp == 0:
                    qt_scr[qq, mp, side] = jnp.concatenate([qt_ref[0, 0, 0:HEAD_DIM, cols], rows], axis=0)
                else:
                    qt_scr[qq, mp, side] = jnp.concatenate([rows, qt_ref[0, 0, HEAD_DIM:LANES, cols]], axis=0)
        acc_ref[qq] = jnp.zeros(acc_ref.shape[1:], F32)

    for qq in range(n_q):
        qi = qq if static else pl.program_id(2) * n_q + qq
        gd = qi // grp
        r = qi - gd * grp

        def attend(ks, side, corr, qq=qq):
            for mp in range(2):
                st = jnp.dot(k_ref[0, 0, mp, ks, :], qt_scr[qq, mp, side], preferred_element_type=F32)
                if corr is not None:
                    st = st + corr
                acc_ref[qq, mp] += jnp.dot(vt_ref[0, 0, :, ks], jnp.exp2(st).astype(BF16),
                                           preferred_element_type=F32)

        if static:
            k0 = qq * tk
            if k0 > 0:
                attend(slice(0, k0), 0, None)
            attend(slice(k0, k0 + tk), 0, ahead_ref[big:big + tk, :] * neg_2c)
            if k0 + tk < big:
                attend(slice(k0 + tk, big), 1, None)
        else:
            w0 = pl.multiple_of(big - r * tk, tk)
            attend(pl.ds(pl.multiple_of(gd * big, big), big), 0, ahead_ref[pl.ds(w0, big), :] * neg_2c)

            def body(j, carry, gd=gd):
                g = gd + j
                g = jnp.where(g >= n_grp, g - n_grp, g)
                attend(pl.ds(pl.multiple_of(g * big, big), big), (g > gd).astype(jnp.int32), None)
                return carry
            lax.fori_loop(1, n_grp, body, 0, unroll=unroll)

        a1 = acc_ref[qq, 0]
        a2 = acc_ref[qq, 1]
        o1 = a1[0:B_V_DIM] * (1.0 / a1[B_V_DIM:B_V_DIM + 1])
        o2 = a2[0:B_V_DIM] * (1.0 / a2[B_V_DIM:B_V_DIM + 1])
        o = _rms_cols(o1 - lam * o2, subt_ref[...]) * (1.0 - lam_init)
        o_ref[0, qq * tq:(qq + 1) * tq, :] = o.T.astype(BF16)


def _flash_b(qb_t, kb, vb_t, tab, aug_t, lq1, lk1, lq2, lk2, subln, lam_init, tq, tk, bounded):
    B, _, _, S = qb_t.shape
    assert tq == tk
    single = S * LANES * 2 * 4 * 2 > 8 * 1024 * 1024
    n_tiles = S // tq
    vec = lambda n: pl.BlockSpec((1, n), lambda b, h, i: (0, 0))
    if bounded:
        grp = min(KV_GROUP, S // tk)
        n_grp = S // (tk * grp)
        n_q = n_tiles if n_grp == 1 else 1
        big = grp * tk
        ahead = np.maximum(np.arange(-big, big)[:, None] - np.arange(tq)[None, :], 0).astype(np.float32)
        sub_t = jnp.broadcast_to(subln.reshape(B_V_DIM, 1), (B_V_DIM, tq))
        kern = functools.partial(_flash_b_bounded_kernel, tq=tq, tk=tk, grp=grp, n_grp=n_grp, n_q=n_q,
                                 unroll=max(n_grp - 1, 1), lam_init=lam_init)
        extra_in = [jnp.asarray(ahead)]
        extra_specs = [
            pl.BlockSpec((1, 2, 3, BF16_ROWS, LANES), lambda b, h, i: (h, 0, 0, 0, 0)),
            pl.BlockSpec((2 * big, tq), lambda b, h, i: (0, 0), pipeline_mode=pl.Buffered(1)),
        ]
        sub_in, sub_spec = sub_t, pl.BlockSpec((B_V_DIM, tq), lambda b, h, i: (0, 0))
        scratch = [pltpu.VMEM((n_q, 2, 2, LANES, tq), BF16), pltpu.VMEM((n_q, 2, VB_ROWS, tq), F32)]
        args = (qb_t, kb, vb_t, tab, aug_t, *extra_in, lq1, lk1, lq2, lk2, sub_in)
    else:
        n_q = 1
        kern = functools.partial(_flash_b_online_kernel, tq=tq, tk=tk, S=S, lam_init=lam_init)
        extra_specs = []
        sub_in, sub_spec = subln, vec(B_V_DIM)
        scratch = [pltpu.VMEM((2, tq, LANES), BF16), pltpu.VMEM((S, 2 * LANES), BF16),
                   pltpu.VMEM((tq, tk), F32), pltpu.VMEM((2 * tq, LANES), F32),
                   pltpu.VMEM((2 * tq, 2 * LANES), F32)]
        args = (qb_t, kb, vb_t, tab, lq1, lk1, lq2, lk2, sub_in)
    return pl.pallas_call(
        kern,
        grid=(B, B_HEADS, n_tiles // n_q),
        in_specs=[
            pl.BlockSpec((1, 1, LANES, n_q * tq), lambda b, h, i: (b, h, 0, i)),
            _kv_spec((1, 1, 2, S, LANES), lambda b, h, i: (b, h, 0, 0, 0), single),
            _kv_spec((1, 1, VB_ROWS, S), lambda b, h, i: (b, h, 0, 0), single),
            pl.BlockSpec((1, 8, LANES), lambda b, h, i: (h, 0, 0)),
            *extra_specs,
            vec(HEAD_DIM), vec(HEAD_DIM), vec(HEAD_DIM), vec(HEAD_DIM), sub_spec,
        ],
        out_specs=pl.BlockSpec((1, n_q * tq, LANES), lambda b, h, i: (b, i, h)),
        out_shape=jax.ShapeDtypeStruct((B, S, 512), BF16),
        scratch_shapes=scratch,
        compiler_params=pltpu.CompilerParams(
            dimension_semantics=("parallel", "parallel", "arbitrary"),
            vmem_limit_bytes=VMEM_LIMIT_BYTES),
        name="flash_b_bounded" if bounded else "flash_b_online",
    )(*args)


def _ple_tail(h1, p, npl, wg_ref, wp_ref):
    rn = _rms_rows(h1, npl).astype(BF16)
    gate = jax.nn.sigmoid(jnp.dot(rn, wg_ref[...], preferred_element_type=F32))
    return h1 + gate * jnp.dot(p.astype(BF16), wp_ref[...], preferred_element_type=F32)


def _out_ple_kernel(h_ref, oa_ref, ob_ref, g_ref, p_ref, wo_ref, npl_ref, wg_ref, wp_ref, out_ref):
    g = g_ref[0]
    y = jnp.concatenate([oa_ref[0] * g[:, 0:512], ob_ref[0] * g[:, 512:1024]], axis=1)
    h1 = h_ref[0] + jnp.dot(y, wo_ref[...], preferred_element_type=F32)
    out_ref[0] = _ple_tail(h1, p_ref[0, 0], npl_ref[...], wg_ref, wp_ref)


def _out_ple(h, oa, ob, gates, p, layer, wo, npl, wg, wp, tm):
    B, S, _ = h.shape
    tok = lambda b, i: (b, i, 0)
    const2 = lambda shp: pl.BlockSpec(shp, lambda b, i: (0, 0))
    return pl.pallas_call(
        _out_ple_kernel,
        grid=(B, S // tm),
        in_specs=[
            pl.BlockSpec((1, tm, D_MODEL), tok),
            pl.BlockSpec((1, tm, 512), tok),
            pl.BlockSpec((1, tm, 512), tok),
            pl.BlockSpec((1, tm, 1024), tok),
            pl.BlockSpec((1, 1, tm, PLE_DIM), lambda b, i: (layer, b, i, 0)),
            const2((1024, D_MODEL)), const2((1, D_MODEL)), const2((D_MODEL, D_MODEL)),
            const2((PLE_DIM, D_MODEL)),
        ],
        out_specs=pl.BlockSpec((1, tm, D_MODEL), tok),
        out_shape=jax.ShapeDtypeStruct((B, S, D_MODEL), F32),
        compiler_params=pltpu.CompilerParams(
            dimension_semantics=("parallel", "parallel"), vmem_limit_bytes=VMEM_LIMIT_BYTES),
        name="out_ple",
    )(h, oa, ob, gates, p, wo, npl, wg, wp)


def _layer_c_kernel(h_ref, hp_ref, hn_ref, p_ref, nm_ref, wi_ref, wgrp_ref, sc_ref, wo_ref,
                    npl_ref, wg_ref, wp_ref, out_ref, *, tm, S):
    i = pl.program_id(1)
    nt = pl.num_programs(1)
    h = h_ref[0]
    hp = hp_ref[0] * (i > 0).astype(F32)
    hn = hn_ref[0] * (i < nt - 1).astype(F32)
    n = tm + 2 * POOL_HALO
    h_ext = jnp.concatenate([hp, h, hn], axis=0)
    xn = _rms_rows(h_ext, nm_ref[...]).astype(BF16)
    u_ext = jnp.dot(xn, wi_ref[:, 0:1024], preferred_element_type=F32)
    g = jnp.dot(xn[POOL_HALO:POOL_HALO + tm], wi_ref[:, 1024:2048], preferred_element_type=F32)

    t = i * tm + lax.broadcasted_iota(jnp.int32, (tm, C_GRP), 0)
    mixed = []
    for gi, w in enumerate(POOL_WINDOWS):
        u = u_ext[:, C_GRP * gi:C_GRP * (gi + 1)]
        win = u + pltpu.roll(u, 1, 0)
        half = 1
        while 2 * half < w:
            win = pltpu.roll(win, half, 0) + pltpu.roll(win, n - half, 0)
            half *= 2
        cnt = (jnp.minimum(t + w // 2, S) - jnp.maximum(t - w // 2, 0)).astype(F32)
        pooled = win[POOL_HALO:POOL_HALO + tm] / cnt - u[POOL_HALO:POOL_HALO + tm]
        mixed.append(jnp.dot(pooled.astype(BF16), wgrp_ref[gi], preferred_element_type=F32))
    mixed = jnp.concatenate(mixed, axis=1)
    y = ((mixed * sc_ref[...]) * (g * jax.nn.sigmoid(g))).astype(BF16)
    h1 = h + jnp.dot(y, wo_ref[...], preferred_element_type=F32)
    out_ref[0] = _ple_tail(h1, p_ref[0, 0], npl_ref[...], wg_ref, wp_ref)


def _layer_c(h, p, layer, nm, wi, wgrp, sc, wo, npl, wg, wp, tm):
    B, S, _ = h.shape
    tok = lambda b, i: (b, i, 0)
    const2 = lambda shp: pl.BlockSpec(shp, lambda b, i: (0, 0))
    per = tm // POOL_HALO
    last = S // POOL_HALO - 1
    kern = functools.partial(_layer_c_kernel, tm=tm, S=S)
    return pl.pallas_call(
        kern,
        grid=(B, S // tm),
        in_specs=[
            pl.BlockSpec((1, tm, D_MODEL), tok),
            pl.BlockSpec((1, POOL_HALO, D_MODEL), lambda b, i: (b, jnp.maximum(i * per - 1, 0), 0)),
            pl.BlockSpec((1, POOL_HALO, D_MODEL), lambda b, i: (b, jnp.minimum((i + 1) * per, last), 0)),
            pl.BlockSpec((1, 1, tm, PLE_DIM), lambda b, i: (layer, b, i, 0)),
            const2((1, D_MODEL)), const2((D_MODEL, 2048)),
            pl.BlockSpec((4, C_GRP, C_GRP), lambda b, i: (0, 0, 0)),
            const2((1, 1024)), const2((1024, D_MODEL)), const2((1, D_MODEL)),
            const2((D_MODEL, D_MODEL)), const2((PLE_DIM, D_MODEL)),
        ],
        out_specs=pl.BlockSpec((1, tm, D_MODEL), tok),
        out_shape=jax.ShapeDtypeStruct((B, S, D_MODEL), F32),
        compiler_params=pltpu.CompilerParams(
            dimension_semantics=("parallel", "parallel"), vmem_limit_bytes=VMEM_LIMIT_BYTES),
        name="layer_c",
    )(h, h, h, p, nm, wi, wgrp, sc, wo, npl, wg, wp)


def _score_bound(gq, gk):
    slack = 1.02
    return (HEAD_DIM * SM_SCALE * LOG2E * slack) * jnp.max(jnp.abs(gq)) * jnp.max(jnp.abs(gk))


def _alibi_tables():
    kaug = np.zeros((16, LANES), np.float32)
    qtab = np.zeros((B_HEADS, 8, LANES), np.float32)
    aug_t = np.zeros((B_HEADS, 2, 3, BF16_ROWS, LANES), np.float32)
    aug_t[:, :, 1, 0:3, :] = -128.0
    aug_t[:, :, 2, 3:6, :] = -1.0
    for mp in range(2):
        base = HEAD_DIM if mp == 0 else 0
        kaug[8 + mp, base + 6:base + 9] = 128.0
        kaug[10 + mp, base + 9:base + 12] = 1.0
    for h, slope in enumerate(ALIBI_SLOPES):
        c = np.float64(slope) * LOG2E
        pieces, rest = [], c
        for _ in range(3):
            piece = np.float64(np.float32(rest).astype(BF16))
            pieces.append(piece)
            rest -= piece
        for mp in range(2):
            base = HEAD_DIM if mp == 0 else 0
            kaug[2 * h + mp, base:base + 6] = pieces + pieces
        aug_t[h, :, 0, 6:12, :] = np.asarray(pieces + pieces)[:, None]
        qtab[h, 6, :] = np.float32(c)
    return jnp.asarray(kaug), jnp.asarray(qtab), jnp.asarray(aug_t)


def _trunk(x, p, prm, tq_a, tq_b, tk, tm):
    row = lambda v: v.astype(F32).reshape(1, -1)
    col = lambda v, s: jnp.broadcast_to((v.astype(F32) * s).reshape(-1, 1), (v.shape[0], tm))
    lam_init = 0.8 - 0.6 * math.exp(-0.3 * 0)
    gq_scale = SM_SCALE * LOG2E
    gka = jnp.tile(prm["kn_a"][0].astype(F32), 2).reshape(1, -1)
    gkb = jnp.tile(prm["kn_b"][0].astype(F32), 8).reshape(1, -1)
    qa_t, ka, va_t, qb_t, kb, vb_t, gates = _proj_ab(
        x, prm["w_ab_s"], prm["w_ab_t"], row(prm["norm_mix"][0]), col(prm["qn_a"][0], gq_scale),
        col(prm["qn_b"][0], gq_scale), gka, gkb, prm["bd"], prm["kaug"], tm)
    oa = lax.cond(
        _score_bound(prm["qn_a"][0], prm["kn_a"][0]) <= BOUNDED_SCORE_LIMIT,
        lambda q, k, v: _flash_a(q, k, v, tq_a, tk, True),
        lambda q, k, v: _flash_a(q, k, v, tq_a, tk, False),
        qa_t, ka, va_t)
    lam_vecs = (row(prm["lam_q1"][0]), row(prm["lam_k1"][0]), row(prm["lam_q2"][0]),
                row(prm["lam_k2"][0]), row(prm["subln_b"][0]))
    ob = lax.cond(
        _score_bound(prm["qn_b"][0], prm["kn_b"][0]) <= BOUNDED_SCORE_LIMIT,
        lambda q, k, v: _flash_b(q, k, v, prm["qtab"], prm["aug_t"], *lam_vecs, lam_init, tq_b, tk, True),
        lambda q, k, v: _flash_b(q, k, v, prm["qtab"], prm["aug_t"], *lam_vecs, lam_init, tq_b, tk, False),
        qb_t, kb, vb_t)
    h = _out_ple(x, oa, ob, gates, p, 0, prm["w_out_ab"][0], row(prm["norm_ple"][0]),
                 prm["w_ple_gate"][0], prm["w_ple_proj"][0], tm)
    h = _layer_c(h, p, 1, row(prm["norm_mix"][1]), prm["w_in_c"][0], prm["w_grp_c"][0],
                 row(prm["scale_c"][0]), prm["w_out_c"][0], row(prm["norm_ple"][1]),
                 prm["w_ple_gate"][1], prm["w_ple_proj"][1], tm)
    return h


def _prepare(norm_mix, w_in_ab, qn_a, kn_a, qn_b, kn_b, lam_q1, lam_k1, lam_q2, lam_k2, subln_b,
             w_out_ab, w_in_c, w_grp_c, scale_c, w_out_c, norm_ple, w_ple_gate, w_ple_proj):
    blk = np.arange(2 * MXU_DIM) // HEAD_DIM
    kaug, qtab, aug_t = _alibi_tables()
    w_ab = w_in_ab[0].astype(BF16)
    cols = lambda a, b: w_ab[:, a:b]
    return dict(
        norm_mix=norm_mix, qn_a=qn_a, kn_a=kn_a, qn_b=qn_b, kn_b=kn_b,
        lam_q1=lam_q1, lam_k1=lam_k1, lam_q2=lam_q2, lam_k2=lam_k2, subln_b=subln_b,
        scale_c=scale_c, norm_ple=norm_ple,
        w_ab_s=jnp.concatenate([cols(_KA, _VA), cols(_KB, _VB), cols(_GA, _QB), cols(_GB, _AB_END)], axis=1),
        w_ab_t=jnp.concatenate([cols(_QA, _KA), cols(_QB, _KB), cols(_VA, _GA), cols(_VB, _GB)], axis=1).T,
        w_out_ab=w_out_ab.astype(BF16), w_in_c=w_in_c.astype(BF16),
        w_grp_c=w_grp_c.astype(BF16), w_out_c=w_out_c.astype(BF16),
        w_ple_gate=w_ple_gate.astype(BF16), w_ple_proj=w_ple_proj.astype(BF16),
        bd=jnp.asarray(blk[:, None] == blk[None, :], BF16),
        kaug=kaug, qtab=qtab, aug_t=aug_t,
    )


def kernel(x_prompt, x_sample, p_prompt, p_sample, norm_mix, w_in_ab, qn_a, kn_a, qn_b, kn_b,
           lam_q1, lam_k1, lam_q2, lam_k2, subln_b, w_out_ab, w_in_c, w_grp_c, scale_c, w_out_c,
           norm_ple, w_ple_gate, w_ple_proj):
    prm = _prepare(norm_mix, w_in_ab, qn_a, kn_a, qn_b, kn_b, lam_q1, lam_k1, lam_q2, lam_k2, subln_b,
                   w_out_ab, w_in_c, w_grp_c, scale_c, w_out_c, norm_ple, w_ple_gate, w_ple_proj)
    y_prompt = _trunk(x_prompt, p_prompt, prm, tq_a=256, tq_b=512, tk=512, tm=1024)
    y_sample = _trunk(x_sample, p_sample, prm, tq_a=256, tq_b=512, tk=512, tm=1024)
    return (y_prompt, y_sample)
```

```python
import functools
import math

import jax
import jax.numpy as jnp
import numpy as np
from jax import lax
from jax.experimental import pallas as pl
from jax.experimental.pallas import tpu as pltpu

F32 = jnp.float32
BF16 = jnp.bfloat16

D_MODEL = 1024
PLE_DIM = 256
GRID_W = 64
HEAD_DIM = 64
EPS = 1e-6
ROPE_THETA = 10000.0
A_HEADS = 8
A_KV_HEADS = 2
B_HEADS = 4
B_V_DIM = 2 * HEAD_DIM
ALIBI_SLOPES = tuple(2.0 ** (-8.0 * (h + 1) / B_HEADS) for h in range(B_HEADS))
POOL_WINDOWS = (2, 4, 8, 16)
C_GRP = 256
POOL_HALO = 8
LOG2E = math.log2(math.e)
SM_SCALE = HEAD_DIM ** -0.5

LANES = 128
BF16_ROWS = 16
MXU_DIM = 256
VMEM_LIMIT_BYTES = 56 * 1024 * 1024

VA_ROWS = HEAD_DIM + BF16_ROWS
VB_ROWS = B_V_DIM + BF16_ROWS

_QA, _KA, _VA, _GA, _QB, _KB, _VB, _GB, _AB_END = 0, 512, 640, 768, 1280, 1792, 2304, 2816, 3328
_S_KB, _S_G, _S_END = 128, 640, 1664
_T_QB, _T_VA, _T_VB, _T_END = 512, 1024, 1152, 1664

NEG_BIG = -1e30
BOUNDED_SCORE_LIMIT = 60.0
KV_GROUP = 2
KV_UNROLL = 8
SHORT_SEQ_CHUNKS = 4


def _rms_rows(x, g):
    ms = jnp.mean(x * x, axis=-1, keepdims=True)
    return x * lax.rsqrt(ms + EPS) * g


def _rms_cols(x, g):
    ms = jnp.mean(x * x, axis=0, keepdims=True)
    return x * lax.rsqrt(ms + EPS) * g


def _nt_dot(a, b):
    return lax.dot_general(a, b, (((1,), (1,)), ((), ())), preferred_element_type=F32)


def _to_token_major(xt):
    return xt.astype(F32).T.astype(BF16)


def _proj_ab_kernel(x_ref, cos_ref, sa_ref, sb_ref, cost_ref, sint_ref, nm_ref, ws_ref, wt_ref,
                    gqat_ref, gqbt_ref, gka_ref, gkb_ref, bd_ref, kaug_ref,
                    qa_ref, ka_ref, va_ref, qb_ref, kb_ref, vb_ref, g_ref):
    x = x_ref[0]
    xn = _rms_rows(x, nm_ref[...]).astype(BF16)
    tm = x.shape[0]

    def seg_t(a, b):
        return _nt_dot(wt_ref[a:b, :], xn)

    cos_t = cost_ref[...]
    sin_t = sint_ref[...]
    z = seg_t(0, _T_QB)
    for h in range(A_HEADS):
        zh = _rms_cols(z[64 * h:64 * (h + 1)], gqat_ref[...])
        sw = jnp.concatenate([zh[16:32], zh[0:16], zh[48:64], zh[32:48]], axis=0)
        r0 = 64 * (h % 2)
        qa_ref[0, h // 2, r0:r0 + 64, :] = (zh * cos_t + sw * sin_t).astype(BF16)
    z = seg_t(_T_QB, _T_VA)
    for h in range(2 * B_HEADS):
        r0 = 64 * (h % 2)
        qb_ref[0, h // 2, r0:r0 + 64, :] = _rms_cols(z[64 * h:64 * (h + 1)], gqbt_ref[...]).astype(BF16)
    ones_rows = (lax.broadcasted_iota(jnp.int32, (BF16_ROWS, tm), 0) == 0).astype(BF16)
    z = seg_t(_T_VA, _T_END)
    for kv in range(A_KV_HEADS):
        va_ref[0, kv, 0:HEAD_DIM, :] = z[64 * kv:64 * (kv + 1)].astype(BF16)
        va_ref[0, kv, HEAD_DIM:VA_ROWS, :] = ones_rows
    for hd in range(B_HEADS):
        r0 = _T_VB - _T_VA + B_V_DIM * hd
        vb_ref[0, hd, 0:B_V_DIM, :] = z[r0:r0 + B_V_DIM].astype(BF16)
        vb_ref[0, hd, B_V_DIM:VB_ROWS, :] = ones_rows

    def seg(a, b):
        return jnp.dot(xn, ws_ref[:, a:b], preferred_element_type=F32)

    def head_norm(z, gain):
        n = z.shape[1]
        ss = jnp.dot((z * z).astype(BF16), bd_ref[0:n, 0:n], preferred_element_type=F32)
        return z * lax.rsqrt(ss * (1.0 / HEAD_DIM) + EPS) * gain

    cos = cos_ref[...]
    sin_a = sa_ref[...]
    sin_b = sb_ref[...]

    def rope(z):
        return (z * cos + pltpu.roll(z, LANES - 16, 1) * sin_a + pltpu.roll(z, 16, 1) * sin_b)

    lane = lax.broadcasted_iota(jnp.int32, (tm, LANES), 1)
    lo = lane < HEAD_DIM
    t = pl.program_id(1) * tm + lax.broadcasted_iota(jnp.int32, (tm, LANES), 0)
    pos_hi = (t >> 7).astype(F32)
    pos_lo = (t & 127).astype(F32)

    p_k = seg(0, _S_G)
    k = rope(head_norm(p_k[:, 0:_S_KB], gka_ref[...]))
    k_sw = pltpu.roll(k, HEAD_DIM, 1)
    zero = jnp.zeros_like(k)
    ka_ref[0, 0, 0] = jnp.where(lo, k, zero).astype(BF16)
    ka_ref[0, 0, 1] = jnp.where(lo, zero, k_sw).astype(BF16)
    ka_ref[0, 1, 0] = jnp.where(lo, k_sw, zero).astype(BF16)
    ka_ref[0, 1, 1] = jnp.where(lo, zero, k).astype(BF16)

    z = head_norm(p_k[:, _S_KB:_S_G], gkb_ref[...])
    for hd in range(B_HEADS):
        kh = z[:, 128 * hd:128 * (hd + 1)]
        for mp in range(2):
            aug = (kaug_ref[2 * hd + mp:2 * hd + mp + 1, :] + pos_hi * kaug_ref[8 + mp:9 + mp, :]
                   + pos_lo * kaug_ref[10 + mp:11 + mp, :])
            keep = lo if mp == 0 else jnp.logical_not(lo)
            kb_ref[0, hd, mp] = jnp.where(keep, kh, aug).astype(BF16)

    g = seg(_S_G, _S_END)
    g_ref[0] = (g * jax.nn.sigmoid(g)).astype(BF16)


def _rope_tables(S):
    rows = S // GRID_W
    row = jnp.repeat(jnp.arange(rows), GRID_W).astype(F32)
    col = jnp.tile(jnp.arange(GRID_W), rows).astype(F32)
    half = HEAD_DIM // 2
    inv = ROPE_THETA ** (-jnp.arange(0, half, 2, dtype=F32) / half)
    ar = row[:, None] * inv
    ac = col[:, None] * inv
    ang = jnp.concatenate([ar, ar, ac, ac], axis=-1)
    cos, sin = jnp.cos(ang), jnp.sin(ang)
    first = (np.arange(HEAD_DIM) % 32) < 16
    sin_a = jnp.where(first, -sin, 0.0)
    sin_b = jnp.where(first, 0.0, sin)
    two = lambda t: jnp.concatenate([t, t], axis=-1)
    return two(cos), two(sin_a), two(sin_b), cos.T, jnp.where(first, -sin, sin).T


def _proj_ab(x, ws, wt, nm, gqat, gqbt, gka, gkb, bd, kaug, tm):
    B, S, _ = x.shape
    cos, sin_a, sin_b, cos_t, sin_t = _rope_tables(S)
    tok = lambda b, i: (b, i, 0)
    tab = pl.BlockSpec((tm, LANES), lambda b, i: (i, 0))
    tab_t = pl.BlockSpec((HEAD_DIM, tm), lambda b, i: (0, i))
    const2 = lambda shp: pl.BlockSpec(shp, lambda b, i: (0, 0))
    chan = lambda n, r: pl.BlockSpec((1, n, r, tm), lambda b, i: (b, 0, 0, i))
    out_shape = (
        jax.ShapeDtypeStruct((B, 4, LANES, S), BF16),
        jax.ShapeDtypeStruct((B, 2, 2, S, LANES), BF16),
        jax.ShapeDtypeStruct((B, 2, VA_ROWS, S), BF16),
        jax.ShapeDtypeStruct((B, 4, LANES, S), BF16),
        jax.ShapeDtypeStruct((B, 4, 2, S, LANES), BF16),
        jax.ShapeDtypeStruct((B, 4, VB_ROWS, S), BF16),
        jax.ShapeDtypeStruct((B, S, 1024), BF16),
    )
    out_specs = (
        chan(4, LANES),
        pl.BlockSpec((1, 2, 2, tm, LANES), lambda b, i: (b, 0, 0, i, 0)),
        chan(2, VA_ROWS),
        chan(4, LANES),
        pl.BlockSpec((1, 4, 2, tm, LANES), lambda b, i: (b, 0, 0, i, 0)),
        chan(4, VB_ROWS),
        pl.BlockSpec((1, tm, 1024), tok),
    )
    return pl.pallas_call(
        _proj_ab_kernel,
        grid=(B, S // tm),
        in_specs=[
            pl.BlockSpec((1, tm, D_MODEL), tok), tab, tab, tab, tab_t, tab_t,
            const2((1, D_MODEL)), const2((D_MODEL, _S_END)), const2((_T_END, D_MODEL)),
            const2((HEAD_DIM, tm)), const2((HEAD_DIM, tm)), const2((1, 128)), const2((1, 512)),
            const2((2 * MXU_DIM, 2 * MXU_DIM)), const2((16, LANES)),
        ],
        out_specs=out_specs,
        out_shape=out_shape,
        compiler_params=pltpu.CompilerParams(
            dimension_semantics=("parallel", "parallel"), vmem_limit_bytes=VMEM_LIMIT_BYTES),
        name="proj_ab",
    )(x, cos, sin_a, sin_b, cos_t, sin_t, nm, ws, wt, gqat, gqbt, gka, gkb, bd, kaug)


def _online_update(s, v, m_ref, acc_ref, r0, nrows):
    rows = pl.ds(r0, nrows)
    m_old = m_ref[rows, :]
    m_new = jnp.maximum(m_old, jnp.max(s, axis=-1, keepdims=True))
    alpha = jnp.exp2(m_old - m_new)
    p = jnp.exp2(s - m_new[:, :1]).astype(BF16)
    pv = jnp.dot(p, v, preferred_element_type=F32)
    nv = v.shape[1]
    if nv != LANES:
        alpha = jnp.concatenate([alpha] * (nv // LANES), axis=1)
    acc_ref[rows, :] = acc_ref[rows, :] * alpha + pv
    m_ref[rows, :] = m_new


def _flash_a_online_kernel(qt_ref, k_ref, vt_ref, o_ref, q_scr, v_scr, m_ref, acc_ref, *, tq, tk, S):
    q_scr[0:tq, :] = _to_token_major(qt_ref[0, 0])
    q_scr[tq:2 * tq, :] = _to_token_major(qt_ref[0, 1])

    @pl.when(pl.program_id(2) == 0)
    def _():
        for c in range(S // tk):
            cols = slice(c * tk, (c + 1) * tk)
            blk = jnp.concatenate([vt_ref[0, 0, :, cols], jnp.zeros((LANES - VA_ROWS, tk), BF16)], axis=0)
            v_scr[cols, :] = _to_token_major(blk)

    m_ref[...] = jnp.full(m_ref.shape, NEG_BIG, F32)
    acc_ref[...] = jnp.zeros(acc_ref.shape, F32)

    def body(c, carry):
        ks = pl.ds(pl.multiple_of(c * tk, tk), tk)
        q = q_scr[...]
        v = v_scr[ks, :]
        for var in range(2):
            s = _nt_dot(q, k_ref[0, 0, var, ks, :])
            _online_update(s, v, m_ref, acc_ref, var * 2 * tq, 2 * tq)
        return carry

    lax.fori_loop(0, S // tk, body, 0)

    lane = lax.broadcasted_iota(jnp.int32, (tq, LANES), 1)
    lo = lane < HEAD_DIM

    def head_out(blk):
        a = acc_ref[blk * tq:(blk + 1) * tq, :]
        return a * (1.0 / a[:, HEAD_DIM:HEAD_DIM + 1])

    o01 = jnp.where(lo, head_out(0), pltpu.roll(head_out(2), HEAD_DIM, 1))
    o23 = jnp.where(lo, head_out(1), pltpu.roll(head_out(3), HEAD_DIM, 1))
    o_ref[0, :, 0:128] = o01.astype(BF16)
    o_ref[0, :, 128:256] = o23.astype(BF16)


def _flash_a_bounded_kernel(qt_ref, k_ref, vt_ref, o_ref, qt_scr, acc_ref, *, tq, big, n_grp, unroll):
    qt_scr[:, 0:tq] = qt_ref[0, 0]
    qt_scr[:, tq:2 * tq] = qt_ref[0, 1]
    acc_ref[...] = jnp.zeros(acc_ref.shape, F32)

    def body(c, carry):
        ks = pl.ds(pl.multiple_of(c * big, big), big)
        for var in range(2):
            st = jnp.dot(k_ref[0, 0, var, ks, :], qt_scr[...], preferred_element_type=F32)
            acc_ref[var] += jnp.dot(vt_ref[0, 0, :, ks], jnp.exp2(st).astype(BF16),
                                    preferred_element_type=F32)
        return carry

    lax.fori_loop(0, n_grp, body, 0, unroll=unroll)

    outs = []
    for var in range(2):
        a = acc_ref[var]
        outs.append(a[0:HEAD_DIM] * (1.0 / a[HEAD_DIM:HEAD_DIM + 1]))
    for pr in range(2):
        cols = slice(pr * tq, (pr + 1) * tq)
        pair_t = jnp.concatenate([outs[0][:, cols], outs[1][:, cols]], axis=0)
        o_ref[0, :, 128 * pr:128 * (pr + 1)] = pair_t.T.astype(BF16)


def _kv_spec(block, index_map, single_buffer):
    if single_buffer:
        return pl.BlockSpec(block, index_map, pipeline_mode=pl.Buffered(1))
    return pl.BlockSpec(block, index_map)


def _flash_a(qa_t, ka, va_t, tq, tk, bounded):
    B, _, _, S = qa_t.shape
    single = S * LANES * 2 * 3 * 2 > 8 * 1024 * 1024
    if bounded:
        grp = min(KV_GROUP, S // tk)
        n_grp = S // (tk * grp)
        kern = functools.partial(_flash_a_bounded_kernel, tq=tq, big=tk * grp, n_grp=n_grp,
                                 unroll=min(KV_UNROLL, n_grp))
        scratch = [pltpu.VMEM((LANES, 2 * tq), BF16), pltpu.VMEM((2, VA_ROWS, 2 * tq), F32)]
    else:
        kern = functools.partial(_flash_a_online_kernel, tq=tq, tk=tk, S=S)
        scratch = [pltpu.VMEM((2 * tq, LANES), BF16), pltpu.VMEM((S, LANES), BF16),
                   pltpu.VMEM((4 * tq, LANES), F32), pltpu.VMEM((4 * tq, LANES), F32)]
    return pl.pallas_call(
        kern,
        grid=(B, A_KV_HEADS, S // tq),
        in_specs=[
            pl.BlockSpec((1, 2, LANES, tq), lambda b, g, i: (b, g, 0, i)),
            _kv_spec((1, 1, 2, S, LANES), lambda b, g, i: (b, g, 0, 0, 0), single),
            _kv_spec((1, 1, VA_ROWS, S), lambda b, g, i: (b, g, 0, 0), single),
        ],
        out_specs=pl.BlockSpec((1, tq, 256), lambda b, g, i: (b, i, g)),
        out_shape=jax.ShapeDtypeStruct((B, S, 512), BF16),
        scratch_shapes=scratch,
        compiler_params=pltpu.CompilerParams(
            dimension_semantics=("parallel", "parallel", "arbitrary"),
            vmem_limit_bytes=VMEM_LIMIT_BYTES),
        name="flash_a_bounded" if bounded else "flash_a_online",
    )(qa_t, ka, va_t)


def _flash_b_lambda(lq1_ref, lk1_ref, lq2_ref, lk2_ref, lam_init):
    return (jnp.exp(jnp.sum(lq1_ref[...] * lk1_ref[...], keepdims=True))
            - jnp.exp(jnp.sum(lq2_ref[...] * lk2_ref[...], keepdims=True)) + lam_init)


def _flash_b_online_kernel(qt_ref, k_ref, vt_ref, tab_ref, lq1_ref, lk1_ref, lq2_ref, lk2_ref, sub_ref,
                           o_ref, q_scr, v_scr, rel_ref, m_ref, acc_ref, *, tq, tk, S, lam_init):
    lane = lax.broadcasted_iota(jnp.int32, (tq, LANES), 1)
    lo = lane < HEAD_DIM
    slab = _to_token_major(qt_ref[0, 0])
    zero = jnp.zeros_like(slab)
    q_scr[0] = jnp.where(lo, slab, zero)
    q_scr[1] = jnp.where(lo, zero, slab)

    @pl.when(pl.program_id(2) == 0)
    def _():
        pad = jnp.zeros((LANES - BF16_ROWS, tk), BF16)
        for c in range(S // tk):
            cols = slice(c * tk, (c + 1) * tk)
            v_scr[cols, 0:B_V_DIM] = _to_token_major(vt_ref[0, 0, 0:B_V_DIM, cols])
            ones_blk = jnp.concatenate([vt_ref[0, 0, B_V_DIM:VB_ROWS, cols], pad], axis=0)
            v_scr[cols, B_V_DIM:2 * LANES] = _to_token_major(ones_blk)

    m_ref[...] = jnp.full(m_ref.shape, NEG_BIG, F32)
    acc_ref[...] = jnp.zeros(acc_ref.shape, F32)
    rel_ref[...] = (lax.broadcasted_iota(jnp.int32, (tq, tk), 0)
                    - lax.broadcasted_iota(jnp.int32, (tq, tk), 1)).astype(F32)
    neg_slope = -tab_ref[0, 6:7, 0:1]
    q0 = pl.program_id(2) * tq

    def body(c, carry):
        k0 = pl.multiple_of(c * tk, tk)
        ks = pl.ds(k0, tk)
        v = v_scr[ks, :]
        bias = jnp.abs(rel_ref[...] + (q0 - k0).astype(F32)) * neg_slope
        for mp in range(2):
            s = _nt_dot(q_scr[mp], k_ref[0, 0, mp, ks, :]) + bias
            _online_update(s, v, m_ref, acc_ref, mp * tq, tq)
        return carry

    lax.fori_loop(0, S // tk, body, 0)

    lam = _flash_b_lambda(lq1_ref, lk1_ref, lq2_ref, lk2_ref, lam_init)
    a1 = acc_ref[0:tq, :]
    a2 = acc_ref[tq:2 * tq, :]
    o1 = a1[:, 0:B_V_DIM] * (1.0 / a1[:, B_V_DIM:B_V_DIM + 1])
    o2 = a2[:, 0:B_V_DIM] * (1.0 / a2[:, B_V_DIM:B_V_DIM + 1])
    o = o1 - lam * o2
    o_ref[0] = (_rms_rows(o, sub_ref[...]) * (1.0 - lam_init)).astype(BF16)


def _flash_b_bounded_kernel(qt_ref, k_ref, vt_ref, tab_ref, aug_ref, ahead_ref, lq1_ref, lk1_ref,
                            lq2_ref, lk2_ref, subt_ref, o_ref, qt_scr, acc_ref, *, tq, tk, grp, n_grp,
                            n_q, unroll, lam_init):
    big = grp * tk
    neg_2c = -2.0 * tab_ref[0, 6:7, 0:1]
    lam = _flash_b_lambda(lq1_ref, lk1_ref, lq2_ref, lk2_ref, lam_init)
    static = n_grp == 1
    zpad = jnp.zeros((HEAD_DIM - BF16_ROWS, tq), BF16)

    for qq in range(n_q):
        qi = qq if static else pl.program_id(2) * n_q + qq
        cols = slice(qq * tq, (qq + 1) * tq)
        t = qi * tq + lax.broadcasted_iota(jnp.int32, (BF16_ROWS, tq), 1)
        pos_hi = (t >> 7).astype(F32)
        pos_lo = (t & 127).astype(F32)
        for mp in range(2):
            rep = lambda r: jnp.concatenate([aug_ref[0, mp, r]] * (tq // LANES), axis=1)
            aug = rep(0) + pos_hi * rep(1) + pos_lo * rep(2)
            for side in range(2):
                rows = jnp.concatenate([(aug if side == 0 else -aug).astype(BF16), zpad], axis=0)
                if mp == 0:
                    qt_scr[qq, mp, side] = jnp.concatenate([qt_ref[0, 0, 0:HEAD_DIM, cols], rows], axis=0)
                else:
                    qt_scr[qq, mp, side] = jnp.concatenate([rows, qt_ref[0, 0, HEAD_DIM:LANES, cols]], axis=0)
        acc_ref[qq] = jnp.zeros(acc_ref.shape[1:], F32)

    for qq in range(n_q):
        qi = qq if static else pl.program_id(2) * n_q + qq
        gd = qi // grp
        r = qi - gd * grp

        def attend(ks, side, corr, qq=qq):
            for mp in range(2):
                st = jnp.dot(k_ref[0, 0, mp, ks, :], qt_scr[qq, mp, side], preferred_element_type=F32)
                if corr is not None:
                    st = st + corr
                acc_ref[qq, mp] += jnp.dot(vt_ref[0, 0, :, ks], jnp.exp2(st).astype(BF16),
                                           preferred_element_type=F32)

        if static:
            k0 = qq * tk
            if k0 > 0:
                attend(slice(0, k0), 0, None)
            attend(slice(k0, k0 + tk), 0, ahead_ref[big:big + tk, :] * neg_2c)
            if k0 + tk < big:
                attend(slice(k0 + tk, big), 1, None)
        else:
            w0 = pl.multiple_of(big - r * tk, tk)
            attend(pl.ds(pl.multiple_of(gd * big, big), big), 0, ahead_ref[pl.ds(w0, big), :] * neg_2c)

            def body(j, carry, gd=gd):
                g = gd + j
                g = jnp.where(g >= n_grp, g - n_grp, g)
                attend(pl.ds(pl.multiple_of(g * big, big), big), (g > gd).astype(jnp.int32), None)
                return carry
            lax.fori_loop(1, n_grp, body, 0, unroll=unroll)

        a1 = acc_ref[qq, 0]
        a2 = acc_ref[qq, 1]
        o1 = a1[0:B_V_DIM] * (1.0 / a1[B_V_DIM:B_V_DIM + 1])
        o2 = a2[0:B_V_DIM] * (1.0 / a2[B_V_DIM:B_V_DIM + 1])
        o = _rms_cols(o1 - lam * o2, subt_ref[...]) * (1.0 - lam_init)
        o_ref[0, qq * tq:(qq + 1) * tq, :] = o.T.astype(BF16)


def _flash_b(qb_t, kb, vb_t, tab, aug_t, lq1, lk1, lq2, lk2, subln, lam_init, tq, tk, bounded):
    B, _, _, S = qb_t.shape
    assert tq == tk
    single = S * LANES * 2 * 4 * 2 > 8 * 1024 * 1024
    n_tiles = S // tq
    vec = lambda n: pl.BlockSpec((1, n), lambda b, h, i: (0, 0))
    if bounded:
        grp = S // tk if S // tk <= SHORT_SEQ_CHUNKS else KV_GROUP
        n_grp = S // (tk * grp)
        n_q = n_tiles if n_grp == 1 else 1
        big = grp * tk
        ahead = np.maximum(np.arange(-big, big)[:, None] - np.arange(tq)[None, :], 0).astype(np.float32)
        sub_t = jnp.broadcast_to(subln.reshape(B_V_DIM, 1), (B_V_DIM, tq))
        kern = functools.partial(_flash_b_bounded_kernel, tq=tq, tk=tk, grp=grp, n_grp=n_grp, n_q=n_q,
                                 unroll=max(n_grp - 1, 1), lam_init=lam_init)
        extra_in = [jnp.asarray(ahead)]
        extra_specs = [
            pl.BlockSpec((1, 2, 3, BF16_ROWS, LANES), lambda b, h, i: (h, 0, 0, 0, 0)),
            pl.BlockSpec((2 * big, tq), lambda b, h, i: (0, 0), pipeline_mode=pl.Buffered(1)),
        ]
        sub_in, sub_spec = sub_t, pl.BlockSpec((B_V_DIM, tq), lambda b, h, i: (0, 0))
        scratch = [pltpu.VMEM((n_q, 2, 2, LANES, tq), BF16), pltpu.VMEM((n_q, 2, VB_ROWS, tq), F32)]
        args = (qb_t, kb, vb_t, tab, aug_t, *extra_in, lq1, lk1, lq2, lk2, sub_in)
    else:
        n_q = 1
        kern = functools.partial(_flash_b_online_kernel, tq=tq, tk=tk, S=S, lam_init=lam_init)
        extra_specs = []
        sub_in, sub_spec = subln, vec(B_V_DIM)
        scratch = [pltpu.VMEM((2, tq, LANES), BF16), pltpu.VMEM((S, 2 * LANES), BF16),
                   pltpu.VMEM((tq, tk), F32), pltpu.VMEM((2 * tq, LANES), F32),
                   pltpu.VMEM((2 * tq, 2 * LANES), F32)]
        args = (qb_t, kb, vb_t, tab, lq1, lk1, lq2, lk2, sub_in)
    return pl.pallas_call(
        kern,
        grid=(B, B_HEADS, n_tiles // n_q),
        in_specs=[
            pl.BlockSpec((1, 1, LANES, n_q * tq), lambda b, h, i: (b, h, 0, i)),
            _kv_spec((1, 1, 2, S, LANES), lambda b, h, i: (b, h, 0, 0, 0), single),
            _kv_spec((1, 1, VB_ROWS, S), lambda b, h, i: (b, h, 0, 0), single),
            pl.BlockSpec((1, 8, LANES), lambda b, h, i: (h, 0, 0)),
            *extra_specs,
            vec(HEAD_DIM), vec(HEAD_DIM), vec(HEAD_DIM), vec(HEAD_DIM), sub_spec,
        ],
        out_specs=pl.BlockSpec((1, n_q * tq, LANES), lambda b, h, i: (b, i, h)),
        out_shape=jax.ShapeDtypeStruct((B, S, 512), BF16),
        scratch_shapes=scratch,
        compiler_params=pltpu.CompilerParams(
            dimension_semantics=("parallel", "parallel", "arbitrary"),
            vmem_limit_bytes=VMEM_LIMIT_BYTES),
        name="flash_b_bounded" if bounded else "flash_b_online",
    )(*args)


def _ple_tail(h1, p, npl, wg_ref, wp_ref):
    rn = _rms_rows(h1, npl).astype(BF16)
    gate = jax.nn.sigmoid(jnp.dot(rn, wg_ref[...], preferred_element_type=F32))
    return h1 + gate * jnp.dot(p.astype(BF16), wp_ref[...], preferred_element_type=F32)


def _out_ple_kernel(h_ref, oa_ref, ob_ref, g_ref, p_ref, wo_ref, npl_ref, wg_ref, wp_ref, out_ref):
    g = g_ref[0]
    y = jnp.concatenate([oa_ref[0] * g[:, 0:512], ob_ref[0] * g[:, 512:1024]], axis=1)
    h1 = h_ref[0] + jnp.dot(y, wo_ref[...], preferred_element_type=F32)
    out_ref[0] = _ple_tail(h1, p_ref[0, 0], npl_ref[...], wg_ref, wp_ref)


def _out_ple(h, oa, ob, gates, p, layer, wo, npl, wg, wp, tm):
    B, S, _ = h.shape
    tok = lambda b, i: (b, i, 0)
    const2 = lambda shp: pl.BlockSpec(shp, lambda b, i: (0, 0))
    return pl.pallas_call(
        _out_ple_kernel,
        grid=(B, S // tm),
        in_specs=[
            pl.BlockSpec((1, tm, D_MODEL), tok),
            pl.BlockSpec((1, tm, 512), tok),
            pl.BlockSpec((1, tm, 512), tok),
            pl.BlockSpec((1, tm, 1024), tok),
            pl.BlockSpec((1, 1, tm, PLE_DIM), lambda b, i: (layer, b, i, 0)),
            const2((1024, D_MODEL)), const2((1, D_MODEL)), const2((D_MODEL, D_MODEL)),
            const2((PLE_DIM, D_MODEL)),
        ],
        out_specs=pl.BlockSpec((1, tm, D_MODEL), tok),
        out_shape=jax.ShapeDtypeStruct((B, S, D_MODEL), F32),
        compiler_params=pltpu.CompilerParams(
            dimension_semantics=("parallel", "parallel"), vmem_limit_bytes=VMEM_LIMIT_BYTES),
        name="out_ple",
    )(h, oa, ob, gates, p, wo, npl, wg, wp)


def _layer_c_kernel(h_ref, hp_ref, hn_ref, p_ref, nm_ref, wi_ref, wgrp_ref, sc_ref, wo_ref,
                    npl_ref, wg_ref, wp_ref, out_ref, *, tm, S):
    i = pl.program_id(1)
    nt = pl.num_programs(1)
    h = h_ref[0]
    hp = hp_ref[0] * (i > 0).astype(F32)
    hn = hn_ref[0] * (i < nt - 1).astype(F32)
    n = tm + 2 * POOL_HALO
    h_ext = jnp.concatenate([hp, h, hn], axis=0)
    xn = _rms_rows(h_ext, nm_ref[...]).astype(BF16)
    u_ext = jnp.dot(xn, wi_ref[:, 0:1024], preferred_element_type=F32)
    g = jnp.dot(xn[POOL_HALO:POOL_HALO + tm], wi_ref[:, 1024:2048], preferred_element_type=F32)

    t = i * tm + lax.broadcasted_iota(jnp.int32, (tm, C_GRP), 0)
    mixed = []
    for gi, w in enumerate(POOL_WINDOWS):
        u = u_ext[:, C_GRP * gi:C_GRP * (gi + 1)]
        win = u + pltpu.roll(u, 1, 0)
        half = 1
        while 2 * half < w:
            win = pltpu.roll(win, half, 0) + pltpu.roll(win, n - half, 0)
            half *= 2
        cnt = (jnp.minimum(t + w // 2, S) - jnp.maximum(t - w // 2, 0)).astype(F32)
        pooled = win[POOL_HALO:POOL_HALO + tm] / cnt - u[POOL_HALO:POOL_HALO + tm]
        mixed.append(jnp.dot(pooled.astype(BF16), wgrp_ref[gi], preferred_element_type=F32))
    mixed = jnp.concatenate(mixed, axis=1)
    y = ((mixed * sc_ref[...]) * (g * jax.nn.sigmoid(g))).astype(BF16)
    h1 = h + jnp.dot(y, wo_ref[...], preferred_element_type=F32)
    out_ref[0] = _ple_tail(h1, p_ref[0, 0], npl_ref[...], wg_ref, wp_ref)


def _layer_c(h, p, layer, nm, wi, wgrp, sc, wo, npl, wg, wp, tm):
    B, S, _ = h.shape
    tok = lambda b, i: (b, i, 0)
    const2 = lambda shp: pl.BlockSpec(shp, lambda b, i: (0, 0))
    per = tm // POOL_HALO
    last = S // POOL_HALO - 1
    kern = functools.partial(_layer_c_kernel, tm=tm, S=S)
    return pl.pallas_call(
        kern,
        grid=(B, S // tm),
        in_specs=[
            pl.BlockSpec((1, tm, D_MODEL), tok),
            pl.BlockSpec((1, POOL_HALO, D_MODEL), lambda b, i: (b, jnp.maximum(i * per - 1, 0), 0)),
            pl.BlockSpec((1, POOL_HALO, D_MODEL), lambda b, i: (b, jnp.minimum((i + 1) * per, last), 0)),
            pl.BlockSpec((1, 1, tm, PLE_DIM), lambda b, i: (layer, b, i, 0)),
            const2((1, D_MODEL)), const2((D_MODEL, 2048)),
            pl.BlockSpec((4, C_GRP, C_GRP), lambda b, i: (0, 0, 0)),
            const2((1, 1024)), const2((1024, D_MODEL)), const2((1, D_MODEL)),
            const2((D_MODEL, D_MODEL)), const2((PLE_DIM, D_MODEL)),
        ],
        out_specs=pl.BlockSpec((1, tm, D_MODEL), tok),
        out_shape=jax.ShapeDtypeStruct((B, S, D_MODEL), F32),
        compiler_params=pltpu.CompilerParams(
            dimension_semantics=("parallel", "parallel"), vmem_limit_bytes=VMEM_LIMIT_BYTES),
        name="layer_c",
    )(h, h, h, p, nm, wi, wgrp, sc, wo, npl, wg, wp)


def _score_bound(gq, gk):
    slack = 1.02
    return (HEAD_DIM * SM_SCALE * LOG2E * slack) * jnp.max(jnp.abs(gq)) * jnp.max(jnp.abs(gk))


def _alibi_tables():
    kaug = np.zeros((16, LANES), np.float32)
    qtab = np.zeros((B_HEADS, 8, LANES), np.float32)
    aug_t = np.zeros((B_HEADS, 2, 3, BF16_ROWS, LANES), np.float32)
    aug_t[:, :, 1, 0:3, :] = -128.0
    aug_t[:, :, 2, 3:6, :] = -1.0
    for mp in range(2):
        base = HEAD_DIM if mp == 0 else 0
        kaug[8 + mp, base + 6:base + 9] = 128.0
        kaug[10 + mp, base + 9:base + 12] = 1.0
    for h, slope in enumerate(ALIBI_SLOPES):
        c = np.float64(slope) * LOG2E
        pieces, rest = [], c
        for _ in range(3):
            piece = np.float64(np.float32(rest).astype(BF16))
            pieces.append(piece)
            rest -= piece
        for mp in range(2):
            base = HEAD_DIM if mp == 0 else 0
            kaug[2 * h + mp, base:base + 6] = pieces + pieces
        aug_t[h, :, 0, 6:12, :] = np.asarray(pieces + pieces)[:, None]
        qtab[h, 6, :] = np.float32(c)
    return jnp.asarray(kaug), jnp.asarray(qtab), jnp.asarray(aug_t)


def _trunk(x, p, prm, tq_a, tq_b, tk, tm):
    row = lambda v: v.astype(F32).reshape(1, -1)
    col = lambda v, s: jnp.broadcast_to((v.astype(F32) * s).reshape(-1, 1), (v.shape[0], tm))
    lam_init = 0.8 - 0.6 * math.exp(-0.3 * 0)
    gq_scale = SM_SCALE * LOG2E
    gka = jnp.tile(prm["kn_a"][0].astype(F32), 2).reshape(1, -1)
    gkb = jnp.tile(prm["kn_b"][0].astype(F32), 8).reshape(1, -1)
    qa_t, ka, va_t, qb_t, kb, vb_t, gates = _proj_ab(
        x, prm["w_ab_s"], prm["w_ab_t"], row(prm["norm_mix"][0]), col(prm["qn_a"][0], gq_scale),
        col(prm["qn_b"][0], gq_scale), gka, gkb, prm["bd"], prm["kaug"], tm)
    oa = lax.cond(
        _score_bound(prm["qn_a"][0], prm["kn_a"][0]) <= BOUNDED_SCORE_LIMIT,
        lambda q, k, v: _flash_a(q, k, v, tq_a, tk, True),
        lambda q, k, v: _flash_a(q, k, v, tq_a, tk, False),
        qa_t, ka, va_t)
    lam_vecs = (row(prm["lam_q1"][0]), row(prm["lam_k1"][0]), row(prm["lam_q2"][0]),
                row(prm["lam_k2"][0]), row(prm["subln_b"][0]))
    ob = lax.cond(
        _score_bound(prm["qn_b"][0], prm["kn_b"][0]) <= BOUNDED_SCORE_LIMIT,
        lambda q, k, v: _flash_b(q, k, v, prm["qtab"], prm["aug_t"], *lam_vecs, lam_init, tq_b, tk, True),
        lambda q, k, v: _flash_b(q, k, v, prm["qtab"], prm["aug_t"], *lam_vecs, lam_init, tq_b, tk, False),
        qb_t, kb, vb_t)
    h = _out_ple(x, oa, ob, gates, p, 0, prm["w_out_ab"][0], row(prm["norm_ple"][0]),
                 prm["w_ple_gate"][0], prm["w_ple_proj"][0], tm)
    h = _layer_c(h, p, 1, row(prm["norm_mix"][1]), prm["w_in_c"][0], prm["w_grp_c"][0],
                 row(prm["scale_c"][0]), prm["w_out_c"][0], row(prm["norm_ple"][1]),
                 prm["w_ple_gate"][1], prm["w_ple_proj"][1], tm)
    return h


def _prepare(norm_mix, w_in_ab, qn_a, kn_a, qn_b, kn_b, lam_q1, lam_k1, lam_q2, lam_k2, subln_b,
             w_out_ab, w_in_c, w_grp_c, scale_c, w_out_c, norm_ple, w_ple_gate, w_ple_proj):
    blk = np.arange(2 * MXU_DIM) // HEAD_DIM
    kaug, qtab, aug_t = _alibi_tables()
    w_ab = w_in_ab[0].astype(BF16)
    cols = lambda a, b: w_ab[:, a:b]
    return dict(
        norm_mix=norm_mix, qn_a=qn_a, kn_a=kn_a, qn_b=qn_b, kn_b=kn_b,
        lam_q1=lam_q1, lam_k1=lam_k1, lam_q2=lam_q2, lam_k2=lam_k2, subln_b=subln_b,
        scale_c=scale_c, norm_ple=norm_ple,
        w_ab_s=jnp.concatenate([cols(_KA, _VA), cols(_KB, _VB), cols(_GA, _QB), cols(_GB, _AB_END)], axis=1),
        w_ab_t=jnp.concatenate([cols(_QA, _KA), cols(_QB, _KB), cols(_VA, _GA), cols(_VB, _GB)], axis=1).T,
        w_out_ab=w_out_ab.astype(BF16), w_in_c=w_in_c.astype(BF16),
        w_grp_c=w_grp_c.astype(BF16), w_out_c=w_out_c.astype(BF16),
        w_ple_gate=w_ple_gate.astype(BF16), w_ple_proj=w_ple_proj.astype(BF16),
        bd=jnp.asarray(blk[:, None] == blk[None, :], BF16),
        kaug=kaug, qtab=qtab, aug_t=aug_t,
    )


def kernel(x_prompt, x_sample, p_prompt, p_sample, norm_mix, w_in_ab, qn_a, kn_a, qn_b, kn_b,
           lam_q1, lam_k1, lam_q2, lam_k2, subln_b, w_out_ab, w_in_c, w_grp_c, scale_c, w_out_c,
           norm_ple, w_ple_gate, w_ple_proj):
    prm = _prepare(norm_mix, w_in_ab, qn_a, kn_a, qn_b, kn_b, lam_q1, lam_k1, lam_q2, lam_k2, subln_b,
                   w_out_ab, w_in_c, w_grp_c, scale_c, w_out_c, norm_ple, w_ple_gate, w_ple_proj)
    y_prompt = _trunk(x_prompt, p_prompt, prm, tq_a=512, tq_b=512, tk=512, tm=1024)
    y_sample = _trunk(x_sample, p_sample, prm, tq_a=512, tq_b=512, tk=512, tm=1024)
    return (y_prompt, y_sample)
```

```python
import functools
import math

import jax
import jax.numpy as jnp
import numpy as np
from jax import lax
from jax.experimental import pallas as pl
from jax.experimental.pallas import tpu as pltpu

F32 = jnp.float32
BF16 = jnp.bfloat16

D_MODEL = 1024
PLE_DIM = 256
GRID_W = 64
HEAD_DIM = 64
EPS = 1e-6
ROPE_THETA = 10000.0
A_HEADS = 8
A_KV_HEADS = 2
B_HEADS = 4
B_V_DIM = 2 * HEAD_DIM
ALIBI_SLOPES = tuple(2.0 ** (-8.0 * (h + 1) / B_HEADS) for h in range(B_HEADS))
POOL_WINDOWS = (2, 4, 8, 16)
C_GRP = 256
POOL_HALO = 8
LOG2E = math.log2(math.e)
SM_SCALE = HEAD_DIM ** -0.5

LANES = 128
BF16_ROWS = 16
MXU_DIM = 256
VMEM_LIMIT_BYTES = 56 * 1024 * 1024

VA_ROWS = HEAD_DIM + BF16_ROWS
VB_ROWS = B_V_DIM + BF16_ROWS

_QA, _KA, _VA, _GA, _QB, _KB, _VB, _GB, _AB_END = 0, 512, 640, 768, 1280, 1792, 2304, 2816, 3328
_S_KB, _S_G, _S_END = 128, 640, 1664
_T_QB, _T_VA, _T_VB, _T_END = 512, 1024, 1152, 1664

NEG_BIG = -1e30
BOUNDED_SCORE_LIMIT = 60.0
PV_TILES_PER_DOT = 8
KV_UNROLL = 8
SHORT_SEQ_CHUNKS = 4


def _rms_rows(x, g):
    ms = jnp.mean(x * x, axis=-1, keepdims=True)
    return x * lax.rsqrt(ms + EPS) * g


def _rms_cols(x, g):
    ms = jnp.mean(x * x, axis=0, keepdims=True)
    return x * lax.rsqrt(ms + EPS) * g


def _nt_dot(a, b):
    return lax.dot_general(a, b, (((1,), (1,)), ((), ())), preferred_element_type=F32)


def _to_token_major(xt):
    return xt.astype(F32).T.astype(BF16)


def _proj_ab_kernel(x_ref, cos_ref, sa_ref, sb_ref, cost_ref, sint_ref, nm_ref, ws_ref, wt_ref,
                    gqat_ref, gqbt_ref, gka_ref, gkb_ref, bd_ref, kaug_ref,
                    qa_ref, ka_ref, va_ref, qb_ref, kb_ref, vb_ref, g_ref):
    x = x_ref[0]
    xn = _rms_rows(x, nm_ref[...]).astype(BF16)
    tm = x.shape[0]

    def seg_t(a, b):
        return _nt_dot(wt_ref[a:b, :], xn)

    cos_t = cost_ref[...]
    sin_t = sint_ref[...]
    z = seg_t(0, _T_QB)
    for h in range(A_HEADS):
        zh = _rms_cols(z[64 * h:64 * (h + 1)], gqat_ref[...])
        sw = jnp.concatenate([zh[16:32], zh[0:16], zh[48:64], zh[32:48]], axis=0)
        r0 = 64 * (h % 2)
        qa_ref[0, h // 2, r0:r0 + 64, :] = (zh * cos_t + sw * sin_t).astype(BF16)
    z = seg_t(_T_QB, _T_VA)
    for h in range(2 * B_HEADS):
        r0 = 64 * (h % 2)
        qb_ref[0, h // 2, r0:r0 + 64, :] = _rms_cols(z[64 * h:64 * (h + 1)], gqbt_ref[...]).astype(BF16)
    ones_rows = (lax.broadcasted_iota(jnp.int32, (BF16_ROWS, tm), 0) == 0).astype(BF16)
    z = seg_t(_T_VA, _T_END)
    for kv in range(A_KV_HEADS):
        va_ref[0, kv, 0:HEAD_DIM, :] = z[64 * kv:64 * (kv + 1)].astype(BF16)
        va_ref[0, kv, HEAD_DIM:VA_ROWS, :] = ones_rows
    for hd in range(B_HEADS):
        r0 = _T_VB - _T_VA + B_V_DIM * hd
        vb_ref[0, hd, 0:B_V_DIM, :] = z[r0:r0 + B_V_DIM].astype(BF16)
        vb_ref[0, hd, B_V_DIM:VB_ROWS, :] = ones_rows

    def seg(a, b):
        return jnp.dot(xn, ws_ref[:, a:b], preferred_element_type=F32)

    def head_norm(z, gain):
        n = z.shape[1]
        ss = jnp.dot((z * z).astype(BF16), bd_ref[0:n, 0:n], preferred_element_type=F32)
        return z * lax.rsqrt(ss * (1.0 / HEAD_DIM) + EPS) * gain

    cos = cos_ref[...]
    sin_a = sa_ref[...]
    sin_b = sb_ref[...]

    def rope(z):
        return (z * cos + pltpu.roll(z, LANES - 16, 1) * sin_a + pltpu.roll(z, 16, 1) * sin_b)

    lane = lax.broadcasted_iota(jnp.int32, (tm, LANES), 1)
    lo = lane < HEAD_DIM
    t = pl.program_id(1) * tm + lax.broadcasted_iota(jnp.int32, (tm, LANES), 0)
    pos_hi = (t >> 7).astype(F32)
    pos_lo = (t & 127).astype(F32)

    p_k = seg(0, _S_G)
    k = rope(head_norm(p_k[:, 0:_S_KB], gka_ref[...]))
    k_sw = pltpu.roll(k, HEAD_DIM, 1)
    zero = jnp.zeros_like(k)
    ka_ref[0, 0, 0] = jnp.where(lo, k, zero).astype(BF16)
    ka_ref[0, 0, 1] = jnp.where(lo, zero, k_sw).astype(BF16)
    ka_ref[0, 1, 0] = jnp.where(lo, k_sw, zero).astype(BF16)
    ka_ref[0, 1, 1] = jnp.where(lo, zero, k).astype(BF16)

    z = head_norm(p_k[:, _S_KB:_S_G], gkb_ref[...])
    for hd in range(B_HEADS):
        kh = z[:, 128 * hd:128 * (hd + 1)]
        for mp in range(2):
            aug = (kaug_ref[2 * hd + mp:2 * hd + mp + 1, :] + pos_hi * kaug_ref[8 + mp:9 + mp, :]
                   + pos_lo * kaug_ref[10 + mp:11 + mp, :])
            keep = lo if mp == 0 else jnp.logical_not(lo)
            kb_ref[0, hd, mp] = jnp.where(keep, kh, aug).astype(BF16)

    g = seg(_S_G, _S_END)
    g_ref[0] = (g * jax.nn.sigmoid(g)).astype(BF16)


def _rope_tables(S):
    rows = S // GRID_W
    row = jnp.repeat(jnp.arange(rows), GRID_W).astype(F32)
    col = jnp.tile(jnp.arange(GRID_W), rows).astype(F32)
    half = HEAD_DIM // 2
    inv = ROPE_THETA ** (-jnp.arange(0, half, 2, dtype=F32) / half)
    ar = row[:, None] * inv
    ac = col[:, None] * inv
    ang = jnp.concatenate([ar, ar, ac, ac], axis=-1)
    cos, sin = jnp.cos(ang), jnp.sin(ang)
    first = (np.arange(HEAD_DIM) % 32) < 16
    sin_a = jnp.where(first, -sin, 0.0)
    sin_b = jnp.where(first, 0.0, sin)
    two = lambda t: jnp.concatenate([t, t], axis=-1)
    return two(cos), two(sin_a), two(sin_b), cos.T, jnp.where(first, -sin, sin).T


def _proj_ab(x, ws, wt, nm, gqat, gqbt, gka, gkb, bd, kaug, tm):
    B, S, _ = x.shape
    cos, sin_a, sin_b, cos_t, sin_t = _rope_tables(S)
    tok = lambda b, i: (b, i, 0)
    tab = pl.BlockSpec((tm, LANES), lambda b, i: (i, 0))
    tab_t = pl.BlockSpec((HEAD_DIM, tm), lambda b, i: (0, i))
    const2 = lambda shp: pl.BlockSpec(shp, lambda b, i: (0, 0))
    chan = lambda n, r: pl.BlockSpec((1, n, r, tm), lambda b, i: (b, 0, 0, i))
    out_shape = (
        jax.ShapeDtypeStruct((B, 4, LANES, S), BF16),
        jax.ShapeDtypeStruct((B, 2, 2, S, LANES), BF16),
        jax.ShapeDtypeStruct((B, 2, VA_ROWS, S), BF16),
        jax.ShapeDtypeStruct((B, 4, LANES, S), BF16),
        jax.ShapeDtypeStruct((B, 4, 2, S, LANES), BF16),
        jax.ShapeDtypeStruct((B, 4, VB_ROWS, S), BF16),
        jax.ShapeDtypeStruct((B, S, 1024), BF16),
    )
    out_specs = (
        chan(4, LANES),
        pl.BlockSpec((1, 2, 2, tm, LANES), lambda b, i: (b, 0, 0, i, 0)),
        chan(2, VA_ROWS),
        chan(4, LANES),
        pl.BlockSpec((1, 4, 2, tm, LANES), lambda b, i: (b, 0, 0, i, 0)),
        chan(4, VB_ROWS),
        pl.BlockSpec((1, tm, 1024), tok),
    )
    return pl.pallas_call(
        _proj_ab_kernel,
        grid=(B, S // tm),
        in_specs=[
            pl.BlockSpec((1, tm, D_MODEL), tok), tab, tab, tab, tab_t, tab_t,
            const2((1, D_MODEL)), const2((D_MODEL, _S_END)), const2((_T_END, D_MODEL)),
            const2((HEAD_DIM, tm)), const2((HEAD_DIM, tm)), const2((1, 128)), const2((1, 512)),
            const2((2 * MXU_DIM, 2 * MXU_DIM)), const2((16, LANES)),
        ],
        out_specs=out_specs,
        out_shape=out_shape,
        compiler_params=pltpu.CompilerParams(
            dimension_semantics=("parallel", "parallel"), vmem_limit_bytes=VMEM_LIMIT_BYTES),
        name="proj_ab",
    )(x, cos, sin_a, sin_b, cos_t, sin_t, nm, ws, wt, gqat, gqbt, gka, gkb, bd, kaug)


def _online_update(s, v, m_ref, acc_ref, r0, nrows):
    rows = pl.ds(r0, nrows)
    m_old = m_ref[rows, :]
    m_new = jnp.maximum(m_old, jnp.max(s, axis=-1, keepdims=True))
    alpha = jnp.exp2(m_old - m_new)
    p = jnp.exp2(s - m_new[:, :1]).astype(BF16)
    pv = jnp.dot(p, v, preferred_element_type=F32)
    nv = v.shape[1]
    if nv != LANES:
        alpha = jnp.concatenate([alpha] * (nv // LANES), axis=1)
    acc_ref[rows, :] = acc_ref[rows, :] * alpha + pv
    m_ref[rows, :] = m_new


def _flash_a_online_kernel(qt_ref, k_ref, vt_ref, o_ref, q_scr, v_scr, m_ref, acc_ref, *, tq, tk, S):
    q_scr[0:tq, :] = _to_token_major(qt_ref[0, 0])
    q_scr[tq:2 * tq, :] = _to_token_major(qt_ref[0, 1])

    @pl.when(pl.program_id(2) == 0)
    def _():
        for c in range(S // tk):
            cols = slice(c * tk, (c + 1) * tk)
            blk = jnp.concatenate([vt_ref[0, 0, :, cols], jnp.zeros((LANES - VA_ROWS, tk), BF16)], axis=0)
            v_scr[cols, :] = _to_token_major(blk)

    m_ref[...] = jnp.full(m_ref.shape, NEG_BIG, F32)
    acc_ref[...] = jnp.zeros(acc_ref.shape, F32)

    def body(c, carry):
        ks = pl.ds(pl.multiple_of(c * tk, tk), tk)
        q = q_scr[...]
        v = v_scr[ks, :]
        for var in range(2):
            s = _nt_dot(q, k_ref[0, 0, var, ks, :])
            _online_update(s, v, m_ref, acc_ref, var * 2 * tq, 2 * tq)
        return carry

    lax.fori_loop(0, S // tk, body, 0)

    lane = lax.broadcasted_iota(jnp.int32, (tq, LANES), 1)
    lo = lane < HEAD_DIM

    def head_out(blk):
        a = acc_ref[blk * tq:(blk + 1) * tq, :]
        return a * (1.0 / a[:, HEAD_DIM:HEAD_DIM + 1])

    o01 = jnp.where(lo, head_out(0), pltpu.roll(head_out(2), HEAD_DIM, 1))
    o23 = jnp.where(lo, head_out(1), pltpu.roll(head_out(3), HEAD_DIM, 1))
    o_ref[0, :, 0:128] = o01.astype(BF16)
    o_ref[0, :, 128:256] = o23.astype(BF16)


def _flash_a_bounded_kernel(qt_ref, k_ref, vt_ref, o_ref, qt_scr, acc_ref, *, tq, big, n_grp, unroll):
    qt_scr[:, 0:tq] = qt_ref[0, 0]
    qt_scr[:, tq:2 * tq] = qt_ref[0, 1]
    acc_ref[...] = jnp.zeros(acc_ref.shape, F32)

    def body(c, carry):
        ks = pl.ds(pl.multiple_of(c * big, big), big)
        for var in range(2):
            st = jnp.dot(k_ref[0, 0, var, ks, :], qt_scr[...], preferred_element_type=F32)
            acc_ref[var] += jnp.dot(vt_ref[0, 0, :, ks], jnp.exp2(st).astype(BF16),
                                    preferred_element_type=F32)
        return carry

    lax.fori_loop(0, n_grp, body, 0, unroll=unroll)

    outs = []
    for var in range(2):
        a = acc_ref[var]
        outs.append(a[0:HEAD_DIM] * (1.0 / a[HEAD_DIM:HEAD_DIM + 1]))
    for pr in range(2):
        cols = slice(pr * tq, (pr + 1) * tq)
        pair_t = jnp.concatenate([outs[0][:, cols], outs[1][:, cols]], axis=0)
        o_ref[0, :, 128 * pr:128 * (pr + 1)] = pair_t.T.astype(BF16)


def _chunks_per_dot(n_cols, tk):
    return max(1, PV_TILES_PER_DOT // ((n_cols // MXU_DIM) * (tk // MXU_DIM)))


def _kv_spec(block, index_map, single_buffer):
    if single_buffer:
        return pl.BlockSpec(block, index_map, pipeline_mode=pl.Buffered(1))
    return pl.BlockSpec(block, index_map)


def _flash_a(qa_t, ka, va_t, tq, tk, bounded):
    B, _, _, S = qa_t.shape
    single = S * LANES * 2 * 3 * 2 > 8 * 1024 * 1024
    if bounded:
        grp = min(_chunks_per_dot(2 * tq, tk), S // tk)
        n_grp = S // (tk * grp)
        kern = functools.partial(_flash_a_bounded_kernel, tq=tq, big=tk * grp, n_grp=n_grp,
                                 unroll=min(KV_UNROLL, n_grp))
        scratch = [pltpu.VMEM((LANES, 2 * tq), BF16), pltpu.VMEM((2, VA_ROWS, 2 * tq), F32)]
    else:
        kern = functools.partial(_flash_a_online_kernel, tq=tq, tk=tk, S=S)
        scratch = [pltpu.VMEM((2 * tq, LANES), BF16), pltpu.VMEM((S, LANES), BF16),
                   pltpu.VMEM((4 * tq, LANES), F32), pltpu.VMEM((4 * tq, LANES), F32)]
    return pl.pallas_call(
        kern,
        grid=(B, A_KV_HEADS, S // tq),
        in_specs=[
            pl.BlockSpec((1, 2, LANES, tq), lambda b, g, i: (b, g, 0, i)),
            _kv_spec((1, 1, 2, S, LANES), lambda b, g, i: (b, g, 0, 0, 0), single),
            _kv_spec((1, 1, VA_ROWS, S), lambda b, g, i: (b, g, 0, 0), single),
        ],
        out_specs=pl.BlockSpec((1, tq, 256), lambda b, g, i: (b, i, g)),
        out_shape=jax.ShapeDtypeStruct((B, S, 512), BF16),
        scratch_shapes=scratch,
        compiler_params=pltpu.CompilerParams(
            dimension_semantics=("parallel", "parallel", "arbitrary"),
            vmem_limit_bytes=VMEM_LIMIT_BYTES),
        name="flash_a_bounded" if bounded else "flash_a_online",
    )(qa_t, ka, va_t)


def _flash_b_lambda(lq1_ref, lk1_ref, lq2_ref, lk2_ref, lam_init):
    return (jnp.exp(jnp.sum(lq1_ref[...] * lk1_ref[...], keepdims=True))
            - jnp.exp(jnp.sum(lq2_ref[...] * lk2_ref[...], keepdims=True)) + lam_init)


def _flash_b_online_kernel(qt_ref, k_ref, vt_ref, tab_ref, lq1_ref, lk1_ref, lq2_ref, lk2_ref, sub_ref,
                           o_ref, q_scr, v_scr, rel_ref, m_ref, acc_ref, *, tq, tk, S, lam_init):
    lane = lax.broadcasted_iota(jnp.int32, (tq, LANES), 1)
    lo = lane < HEAD_DIM
    slab = _to_token_major(qt_ref[0, 0])
    zero = jnp.zeros_like(slab)
    q_scr[0] = jnp.where(lo, slab, zero)
    q_scr[1] = jnp.where(lo, zero, slab)

    @pl.when(pl.program_id(2) == 0)
    def _():
        pad = jnp.zeros((LANES - BF16_ROWS, tk), BF16)
        for c in range(S // tk):
            cols = slice(c * tk, (c + 1) * tk)
            v_scr[cols, 0:B_V_DIM] = _to_token_major(vt_ref[0, 0, 0:B_V_DIM, cols])
            ones_blk = jnp.concatenate([vt_ref[0, 0, B_V_DIM:VB_ROWS, cols], pad], axis=0)
            v_scr[cols, B_V_DIM:2 * LANES] = _to_token_major(ones_blk)

    m_ref[...] = jnp.full(m_ref.shape, NEG_BIG, F32)
    acc_ref[...] = jnp.zeros(acc_ref.shape, F32)
    rel_ref[...] = (lax.broadcasted_iota(jnp.int32, (tq, tk), 0)
                    - lax.broadcasted_iota(jnp.int32, (tq, tk), 1)).astype(F32)
    neg_slope = -tab_ref[0, 6:7, 0:1]
    q0 = pl.program_id(2) * tq

    def body(c, carry):
        k0 = pl.multiple_of(c * tk, tk)
        ks = pl.ds(k0, tk)
        v = v_scr[ks, :]
        bias = jnp.abs(rel_ref[...] + (q0 - k0).astype(F32)) * neg_slope
        for mp in range(2):
            s = _nt_dot(q_scr[mp], k_ref[0, 0, mp, ks, :]) + bias
            _online_update(s, v, m_ref, acc_ref, mp * tq, tq)
        return carry

    lax.fori_loop(0, S // tk, body, 0)

    lam = _flash_b_lambda(lq1_ref, lk1_ref, lq2_ref, lk2_ref, lam_init)
    a1 = acc_ref[0:tq, :]
    a2 = acc_ref[tq:2 * tq, :]
    o1 = a1[:, 0:B_V_DIM] * (1.0 / a1[:, B_V_DIM:B_V_DIM + 1])
    o2 = a2[:, 0:B_V_DIM] * (1.0 / a2[:, B_V_DIM:B_V_DIM + 1])
    o = o1 - lam * o2
    o_ref[0] = (_rms_rows(o, sub_ref[...]) * (1.0 - lam_init)).astype(BF16)


def _flash_b_bounded_kernel(qt_ref, k_ref, vt_ref, tab_ref, aug_ref, ahead_ref, lq1_ref, lk1_ref,
                            lq2_ref, lk2_ref, subt_ref, o_ref, qt_scr, acc_ref, den_ref, *, tq, tk, grp,
                            n_grp, n_q, unroll, lam_init):
    big = grp * tk
    neg_2c = -2.0 * tab_ref[0, 6:7, 0:1]
    lam = _flash_b_lambda(lq1_ref, lk1_ref, lq2_ref, lk2_ref, lam_init)
    static = n_grp == 1
    zpad = jnp.zeros((HEAD_DIM - BF16_ROWS, tq), BF16)

    for qq in range(n_q):
        qi = qq if static else pl.program_id(2) * n_q + qq
        cols = slice(qq * tq, (qq + 1) * tq)
        t = qi * tq + lax.broadcasted_iota(jnp.int32, (BF16_ROWS, tq), 1)
        pos_hi = (t >> 7).astype(F32)
        pos_lo = (t & 127).astype(F32)
        for mp in range(2):
            rep = lambda r: jnp.concatenate([aug_ref[0, mp, r]] * (tq // LANES), axis=1)
            aug = rep(0) + pos_hi * rep(1) + pos_lo * rep(2)
            for side in range(2):
                rows = jnp.concatenate([(aug if side == 0 else -aug).astype(BF16), zpad], axis=0)
                if mp == 0:
                    qt_scr[qq, mp, side] = jnp.concatenate([qt_ref[0, 0, 0:HEAD_DIM, cols], rows], axis=0)
                else:
                    qt_scr[qq, mp, side] = jnp.concatenate([rows, qt_ref[0, 0, HEAD_DIM:LANES, cols]], axis=0)
        acc_ref[qq] = jnp.zeros(acc_ref.shape[1:], F32)
        den_ref[qq] = jnp.zeros(den_ref.shape[1:], F32)

    for qq in range(n_q):
        qi = qq if static else pl.program_id(2) * n_q + qq
        gd = qi // grp
        r = qi - gd * grp

        def attend(ks, side, corr, qq=qq):
            for mp in range(2):
                st = jnp.dot(k_ref[0, 0, mp, ks, :], qt_scr[qq, mp, side], preferred_element_type=F32)
                if corr is not None:
                    st = st + corr
                p = jnp.exp2(st)
                den_ref[qq, mp] += jnp.sum(p.reshape(-1, 8, tq), axis=0)
                acc_ref[qq, mp] += jnp.dot(vt_ref[0, 0, 0:B_V_DIM, ks], p.astype(BF16),
                                           preferred_element_type=F32)

        if static:
            k0 = qq * tk
            if k0 > 0:
                attend(slice(0, k0), 0, None)
            attend(slice(k0, k0 + tk), 0, ahead_ref[big:big + tk, :] * neg_2c)
            if k0 + tk < big:
                attend(slice(k0 + tk, big), 1, None)
        else:
            w0 = pl.multiple_of(big - r * tk, tk)
            attend(pl.ds(pl.multiple_of(gd * big, big), big), 0, ahead_ref[pl.ds(w0, big), :] * neg_2c)

            def body(j, carry, gd=gd):
                g = gd + j
                g = jnp.where(g >= n_grp, g - n_grp, g)
                attend(pl.ds(pl.multiple_of(g * big, big), big), (g > gd).astype(jnp.int32), None)
                return carry
            lax.fori_loop(1, n_grp, body, 0, unroll=unroll)

        o1 = acc_ref[qq, 0] * (1.0 / jnp.sum(den_ref[qq, 0], axis=0, keepdims=True))
        o2 = acc_ref[qq, 1] * (1.0 / jnp.sum(den_ref[qq, 1], axis=0, keepdims=True))
        o = _rms_cols(o1 - lam * o2, subt_ref[...]) * (1.0 - lam_init)
        o_ref[0, qq * tq:(qq + 1) * tq, :] = o.T.astype(BF16)


def _flash_b(qb_t, kb, vb_t, tab, aug_t, lq1, lk1, lq2, lk2, subln, lam_init, tq, tk, bounded):
    B, _, _, S = qb_t.shape
    assert tq == tk
    single = S * LANES * 2 * 4 * 2 > 8 * 1024 * 1024
    n_tiles = S // tq
    vec = lambda n: pl.BlockSpec((1, n), lambda b, h, i: (0, 0))
    if bounded:
        grp = S // tk if S // tk <= SHORT_SEQ_CHUNKS else _chunks_per_dot(tq, tk)
        n_grp = S // (tk * grp)
        n_q = n_tiles if n_grp == 1 else 1
        big = grp * tk
        ahead = np.maximum(np.arange(-big, big)[:, None] - np.arange(tq)[None, :], 0).astype(np.float32)
        sub_t = jnp.broadcast_to(subln.reshape(B_V_DIM, 1), (B_V_DIM, tq))
        kern = functools.partial(_flash_b_bounded_kernel, tq=tq, tk=tk, grp=grp, n_grp=n_grp, n_q=n_q,
                                 unroll=max(n_grp - 1, 1), lam_init=lam_init)
        extra_in = [jnp.asarray(ahead)]
        extra_specs = [
            pl.BlockSpec((1, 2, 3, BF16_ROWS, LANES), lambda b, h, i: (h, 0, 0, 0, 0)),
            pl.BlockSpec((2 * big, tq), lambda b, h, i: (0, 0), pipeline_mode=pl.Buffered(1)),
        ]
        sub_in, sub_spec = sub_t, pl.BlockSpec((B_V_DIM, tq), lambda b, h, i: (0, 0))
        scratch = [pltpu.VMEM((n_q, 2, 2, LANES, tq), BF16), pltpu.VMEM((n_q, 2, B_V_DIM, tq), F32),
                   pltpu.VMEM((n_q, 2, 8, tq), F32)]
        args = (qb_t, kb, vb_t, tab, aug_t, *extra_in, lq1, lk1, lq2, lk2, sub_in)
    else:
        n_q = 1
        kern = functools.partial(_flash_b_online_kernel, tq=tq, tk=tk, S=S, lam_init=lam_init)
        extra_specs = []
        sub_in, sub_spec = subln, vec(B_V_DIM)
        scratch = [pltpu.VMEM((2, tq, LANES), BF16), pltpu.VMEM((S, 2 * LANES), BF16),
                   pltpu.VMEM((tq, tk), F32), pltpu.VMEM((2 * tq, LANES), F32),
                   pltpu.VMEM((2 * tq, 2 * LANES), F32)]
        args = (qb_t, kb, vb_t, tab, lq1, lk1, lq2, lk2, sub_in)
    return pl.pallas_call(
        kern,
        grid=(B, B_HEADS, n_tiles // n_q),
        in_specs=[
            pl.BlockSpec((1, 1, LANES, n_q * tq), lambda b, h, i: (b, h, 0, i)),
            _kv_spec((1, 1, 2, S, LANES), lambda b, h, i: (b, h, 0, 0, 0), single),
            _kv_spec((1, 1, VB_ROWS, S), lambda b, h, i: (b, h, 0, 0), single),
            pl.BlockSpec((1, 8, LANES), lambda b, h, i: (h, 0, 0)),
            *extra_specs,
            vec(HEAD_DIM), vec(HEAD_DIM), vec(HEAD_DIM), vec(HEAD_DIM), sub_spec,
        ],
        out_specs=pl.BlockSpec((1, n_q * tq, LANES), lambda b, h, i: (b, i, h)),
        out_shape=jax.ShapeDtypeStruct((B, S, 512), BF16),
        scratch_shapes=scratch,
        compiler_params=pltpu.CompilerParams(
            dimension_semantics=("parallel", "parallel", "arbitrary"),
            vmem_limit_bytes=VMEM_LIMIT_BYTES),
        name="flash_b_bounded" if bounded else "flash_b_online",
    )(*args)


def _ple_tail(h1, p, npl, wg_ref, wp_ref):
    rn = _rms_rows(h1, npl).astype(BF16)
    gate = jax.nn.sigmoid(jnp.dot(rn, wg_ref[...], preferred_element_type=F32))
    return h1 + gate * jnp.dot(p.astype(BF16), wp_ref[...], preferred_element_type=F32)


def _out_ple_kernel(h_ref, oa_ref, ob_ref, g_ref, p_ref, wo_ref, npl_ref, wg_ref, wp_ref, out_ref):
    g = g_ref[0]
    y = jnp.concatenate([oa_ref[0] * g[:, 0:512], ob_ref[0] * g[:, 512:1024]], axis=1)
    h1 = h_ref[0] + jnp.dot(y, wo_ref[...], preferred_element_type=F32)
    out_ref[0] = _ple_tail(h1, p_ref[0, 0], npl_ref[...], wg_ref, wp_ref)


def _out_ple(h, oa, ob, gates, p, layer, wo, npl, wg, wp, tm):
    B, S, _ = h.shape
    tok = lambda b, i: (b, i, 0)
    const2 = lambda shp: pl.BlockSpec(shp, lambda b, i: (0, 0))
    return pl.pallas_call(
        _out_ple_kernel,
        grid=(B, S // tm),
        in_specs=[
            pl.BlockSpec((1, tm, D_MODEL), tok),
            pl.BlockSpec((1, tm, 512), tok),
            pl.BlockSpec((1, tm, 512), tok),
            pl.BlockSpec((1, tm, 1024), tok),
            pl.BlockSpec((1, 1, tm, PLE_DIM), lambda b, i: (layer, b, i, 0)),
            const2((1024, D_MODEL)), const2((1, D_MODEL)), const2((D_MODEL, D_MODEL)),
            const2((PLE_DIM, D_MODEL)),
        ],
        out_specs=pl.BlockSpec((1, tm, D_MODEL), tok),
        out_shape=jax.ShapeDtypeStruct((B, S, D_MODEL), F32),
        compiler_params=pltpu.CompilerParams(
            dimension_semantics=("parallel", "parallel"), vmem_limit_bytes=VMEM_LIMIT_BYTES),
        name="out_ple",
    )(h, oa, ob, gates, p, wo, npl, wg, wp)


def _layer_c_kernel(h_ref, hp_ref, hn_ref, p_ref, nm_ref, wi_ref, wgrp_ref, sc_ref, wo_ref,
                    npl_ref, wg_ref, wp_ref, out_ref, *, tm, S):
    i = pl.program_id(1)
    nt = pl.num_programs(1)
    h = h_ref[0]
    hp = hp_ref[0] * (i > 0).astype(F32)
    hn = hn_ref[0] * (i < nt - 1).astype(F32)
    n = tm + 2 * POOL_HALO
    h_ext = jnp.concatenate([hp, h, hn], axis=0)
    xn = _rms_rows(h_ext, nm_ref[...]).astype(BF16)
    u_ext = jnp.dot(xn, wi_ref[:, 0:1024], preferred_element_type=F32)
    g = jnp.dot(xn[POOL_HALO:POOL_HALO + tm], wi_ref[:, 1024:2048], preferred_element_type=F32)

    t = i * tm + lax.broadcasted_iota(jnp.int32, (tm, C_GRP), 0)
    mixed = []
    for gi, w in enumerate(POOL_WINDOWS):
        u = u_ext[:, C_GRP * gi:C_GRP * (gi + 1)]
        win = u + pltpu.roll(u, 1, 0)
        half = 1
        while 2 * half < w:
            win = pltpu.roll(win, half, 0) + pltpu.roll(win, n - half, 0)
            half *= 2
        cnt = (jnp.minimum(t + w // 2, S) - jnp.maximum(t - w // 2, 0)).astype(F32)
        pooled = win[POOL_HALO:POOL_HALO + tm] / cnt - u[POOL_HALO:POOL_HALO + tm]
        mixed.append(jnp.dot(pooled.astype(BF16), wgrp_ref[gi], preferred_element_type=F32))
    mixed = jnp.concatenate(mixed, axis=1)
    y = ((mixed * sc_ref[...]) * (g * jax.nn.sigmoid(g))).astype(BF16)
    h1 = h + jnp.dot(y, wo_ref[...], preferred_element_type=F32)
    out_ref[0] = _ple_tail(h1, p_ref[0, 0], npl_ref[...], wg_ref, wp_ref)


def _layer_c(h, p, layer, nm, wi, wgrp, sc, wo, npl, wg, wp, tm):
    B, S, _ = h.shape
    tok = lambda b, i: (b, i, 0)
    const2 = lambda shp: pl.BlockSpec(shp, lambda b, i: (0, 0))
    per = tm // POOL_HALO
    last = S // POOL_HALO - 1
    kern = functools.partial(_layer_c_kernel, tm=tm, S=S)
    return pl.pallas_call(
        kern,
        grid=(B, S // tm),
        in_specs=[
            pl.BlockSpec((1, tm, D_MODEL), tok),
            pl.BlockSpec((1, POOL_HALO, D_MODEL), lambda b, i: (b, jnp.maximum(i * per - 1, 0), 0)),
            pl.BlockSpec((1, POOL_HALO, D_MODEL), lambda b, i: (b, jnp.minimum((i + 1) * per, last), 0)),
            pl.BlockSpec((1, 1, tm, PLE_DIM), lambda b, i: (layer, b, i, 0)),
            const2((1, D_MODEL)), const2((D_MODEL, 2048)),
            pl.BlockSpec((4, C_GRP, C_GRP), lambda b, i: (0, 0, 0)),
            const2((1, 1024)), const2((1024, D_MODEL)), const2((1, D_MODEL)),
            const2((D_MODEL, D_MODEL)), const2((PLE_DIM, D_MODEL)),
        ],
        out_specs=pl.BlockSpec((1, tm, D_MODEL), tok),
        out_shape=jax.ShapeDtypeStruct((B, S, D_MODEL), F32),
        compiler_params=pltpu.CompilerParams(
            dimension_semantics=("parallel", "parallel"), vmem_limit_bytes=VMEM_LIMIT_BYTES),
        name="layer_c",
    )(h, h, h, p, nm, wi, wgrp, sc, wo, npl, wg, wp)


def _score_bound(gq, gk):
    slack = 1.02
    return (HEAD_DIM * SM_SCALE * LOG2E * slack) * jnp.max(jnp.abs(gq)) * jnp.max(jnp.abs(gk))


def _alibi_tables():
    kaug = np.zeros((16, LANES), np.float32)
    qtab = np.zeros((B_HEADS, 8, LANES), np.float32)
    aug_t = np.zeros((B_HEADS, 2, 3, BF16_ROWS, LANES), np.float32)
    aug_t[:, :, 1, 0:3, :] = -128.0
    aug_t[:, :, 2, 3:6, :] = -1.0
    for mp in range(2):
        base = HEAD_DIM if mp == 0 else 0
        kaug[8 + mp, base + 6:base + 9] = 128.0
        kaug[10 + mp, base + 9:base + 12] = 1.0
    for h, slope in enumerate(ALIBI_SLOPES):
        c = np.float64(slope) * LOG2E
        pieces, rest = [], c
        for _ in range(3):
            piece = np.float64(np.float32(rest).astype(BF16))
            pieces.append(piece)
            rest -= piece
        for mp in range(2):
            base = HEAD_DIM if mp == 0 else 0
            kaug[2 * h + mp, base:base + 6] = pieces + pieces
        aug_t[h, :, 0, 6:12, :] = np.asarray(pieces + pieces)[:, None]
        qtab[h, 6, :] = np.float32(c)
    return jnp.asarray(kaug), jnp.asarray(qtab), jnp.asarray(aug_t)


def _trunk(x, p, prm, tq_a, tq_b, tk, tm):
    row = lambda v: v.astype(F32).reshape(1, -1)
    col = lambda v, s: jnp.broadcast_to((v.astype(F32) * s).reshape(-1, 1), (v.shape[0], tm))
    lam_init = 0.8 - 0.6 * math.exp(-0.3 * 0)
    gq_scale = SM_SCALE * LOG2E
    gka = jnp.tile(prm["kn_a"][0].astype(F32), 2).reshape(1, -1)
    gkb = jnp.tile(prm["kn_b"][0].astype(F32), 8).reshape(1, -1)
    qa_t, ka, va_t, qb_t, kb, vb_t, gates = _proj_ab(
        x, prm["w_ab_s"], prm["w_ab_t"], row(prm["norm_mix"][0]), col(prm["qn_a"][0], gq_scale),
        col(prm["qn_b"][0], gq_scale), gka, gkb, prm["bd"], prm["kaug"], tm)
    oa = lax.cond(
        _score_bound(prm["qn_a"][0], prm["kn_a"][0]) <= BOUNDED_SCORE_LIMIT,
        lambda q, k, v: _flash_a(q, k, v, tq_a, tk, True),
        lambda q, k, v: _flash_a(q, k, v, tq_a, tk, False),
        qa_t, ka, va_t)
    lam_vecs = (row(prm["lam_q1"][0]), row(prm["lam_k1"][0]), row(prm["lam_q2"][0]),
                row(prm["lam_k2"][0]), row(prm["subln_b"][0]))
    ob = lax.cond(
        _score_bound(prm["qn_b"][0], prm["kn_b"][0]) <= BOUNDED_SCORE_LIMIT,
        lambda q, k, v: _flash_b(q, k, v, prm["qtab"], prm["aug_t"], *lam_vecs, lam_init, tq_b, tk, True),
        lambda q, k, v: _flash_b(q, k, v, prm["qtab"], prm["aug_t"], *lam_vecs, lam_init, tq_b, tk, False),
        qb_t, kb, vb_t)
    h = _out_ple(x, oa, ob, gates, p, 0, prm["w_out_ab"][0], row(prm["norm_ple"][0]),
                 prm["w_ple_gate"][0], prm["w_ple_proj"][0], tm)
    h = _layer_c(h, p, 1, row(prm["norm_mix"][1]), prm["w_in_c"][0], prm["w_grp_c"][0],
                 row(prm["scale_c"][0]), prm["w_out_c"][0], row(prm["norm_ple"][1]),
                 prm["w_ple_gate"][1], prm["w_ple_proj"][1], tm)
    return h


def _prepare(norm_mix, w_in_ab, qn_a, kn_a, qn_b, kn_b, lam_q1, lam_k1, lam_q2, lam_k2, subln_b,
             w_out_ab, w_in_c, w_grp_c, scale_c, w_out_c, norm_ple, w_ple_gate, w_ple_proj):
    blk = np.arange(2 * MXU_DIM) // HEAD_DIM
    kaug, qtab, aug_t = _alibi_tables()
    w_ab = w_in_ab[0].astype(BF16)
    cols = lambda a, b: w_ab[:, a:b]
    return dict(
        norm_mix=norm_mix, qn_a=qn_a, kn_a=kn_a, qn_b=qn_b, kn_b=kn_b,
        lam_q1=lam_q1, lam_k1=lam_k1, lam_q2=lam_q2, lam_k2=lam_k2, subln_b=subln_b,
        scale_c=scale_c, norm_ple=norm_ple,
        w_ab_s=jnp.concatenate([cols(_KA, _VA), cols(_KB, _VB), cols(_GA, _QB), cols(_GB, _AB_END)], axis=1),
        w_ab_t=jnp.concatenate([cols(_QA, _KA), cols(_QB, _KB), cols(_VA, _GA), cols(_VB, _GB)], axis=1).T,
        w_out_ab=w_out_ab.astype(BF16), w_in_c=w_in_c.astype(BF16),
        w_grp_c=w_grp_c.astype(BF16), w_out_c=w_out_c.astype(BF16),
        w_ple_gate=w_ple_gate.astype(BF16), w_ple_proj=w_ple_proj.astype(BF16),
        bd=jnp.asarray(blk[:, None] == blk[None, :], BF16),
        kaug=kaug, qtab=qtab, aug_t=aug_t,
    )


def kernel(x_prompt, x_sample, p_prompt, p_sample, norm_mix, w_in_ab, qn_a, kn_a, qn_b, kn_b,
           lam_q1, lam_k1, lam_q2, lam_k2, subln_b, w_out_ab, w_in_c, w_grp_c, scale_c, w_out_c,
           norm_ple, w_ple_gate, w_ple_proj):
    prm = _prepare(norm_mix, w_in_ab, qn_a, kn_a, qn_b, kn_b, lam_q1, lam_k1, lam_q2, lam_k2, subln_b,
                   w_out_ab, w_in_c, w_grp_c, scale_c, w_out_c, norm_ple, w_ple_gate, w_ple_proj)
    y_prompt = _trunk(x_prompt, p_prompt, prm, tq_a=512, tq_b=512, tk=512, tm=1024)
    y_sample = _trunk(x_sample, p_sample, prm, tq_a=512, tq_b=512, tk=512, tm=1024)
    return (y_prompt, y_sample)
```

```python
import functools
import math

import jax
import jax.numpy as jnp
import numpy as np
from jax import lax
from jax.experimental import pallas as pl
from jax.experimental.pallas import tpu as pltpu

F32 = jnp.float32
BF16 = jnp.bfloat16

D_MODEL = 1024
PLE_DIM = 256
GRID_W = 64
HEAD_DIM = 64
EPS = 1e-6
ROPE_THETA = 10000.0
A_HEADS = 8
A_KV_HEADS = 2
B_HEADS = 4
B_V_DIM = 2 * HEAD_DIM
ALIBI_SLOPES = tuple(2.0 ** (-8.0 * (h + 1) / B_HEADS) for h in range(B_HEADS))
POOL_WINDOWS = (2, 4, 8, 16)
C_GRP = 256
POOL_HALO = 8
LOG2E = math.log2(math.e)
SM_SCALE = HEAD_DIM ** -0.5

LANES = 128
BF16_ROWS = 16
MXU_DIM = 256
VMEM_LIMIT_BYTES = 56 * 1024 * 1024

VA_ROWS = HEAD_DIM + BF16_ROWS
VB_ROWS = B_V_DIM + BF16_ROWS

_QA, _KA, _VA, _GA, _QB, _KB, _VB, _GB, _AB_END = 0, 512, 640, 768, 1280, 1792, 2304, 2816, 3328
_S_KB, _S_G, _S_END = 128, 640, 1664
_T_QB, _T_VA, _T_VB, _T_END = 512, 1024, 1152, 1664

NEG_BIG = -1e30
BOUNDED_SCORE_LIMIT = 60.0
KV_GROUP = 2
KV_UNROLL = 8
SHORT_SEQ_CHUNKS = 4


def _rms_rows(x, g):
    ms = jnp.mean(x * x, axis=-1, keepdims=True)
    return x * lax.rsqrt(ms + EPS) * g


def _rms_cols(x, g):
    ms = jnp.mean(x * x, axis=0, keepdims=True)
    return x * lax.rsqrt(ms + EPS) * g


def _nt_dot(a, b):
    return lax.dot_general(a, b, (((1,), (1,)), ((), ())), preferred_element_type=F32)


def _to_token_major(xt):
    return xt.astype(F32).T.astype(BF16)


def _proj_ab_kernel(x_ref, cos_ref, sa_ref, sb_ref, cost_ref, sint_ref, nm_ref, ws_ref, wt_ref,
                    gqat_ref, gqbt_ref, gka_ref, gkb_ref, bd_ref, kaug_ref,
                    qa_ref, ka_ref, va_ref, qb_ref, kb_ref, vb_ref, g_ref):
    x = x_ref[0]
    xn = _rms_rows(x, nm_ref[...]).astype(BF16)
    tm = x.shape[0]

    def seg_t(a, b):
        return _nt_dot(wt_ref[a:b, :], xn)

    cos_t = cost_ref[...]
    sin_t = sint_ref[...]
    z = seg_t(0, _T_QB)
    for h in range(A_HEADS):
        zh = _rms_cols(z[64 * h:64 * (h + 1)], gqat_ref[...])
        sw = jnp.concatenate([zh[16:32], zh[0:16], zh[48:64], zh[32:48]], axis=0)
        r0 = 64 * (h % 2)
        qa_ref[0, h // 2, r0:r0 + 64, :] = (zh * cos_t + sw * sin_t).astype(BF16)
    z = seg_t(_T_QB, _T_VA)
    for h in range(2 * B_HEADS):
        r0 = 64 * (h % 2)
        qb_ref[0, h // 2, r0:r0 + 64, :] = _rms_cols(z[64 * h:64 * (h + 1)], gqbt_ref[...]).astype(BF16)
    ones_rows = (lax.broadcasted_iota(jnp.int32, (BF16_ROWS, tm), 0) == 0).astype(BF16)
    z = seg_t(_T_VA, _T_END)
    for kv in range(A_KV_HEADS):
        va_ref[0, kv, 0:HEAD_DIM, :] = z[64 * kv:64 * (kv + 1)].astype(BF16)
        va_ref[0, kv, HEAD_DIM:VA_ROWS, :] = ones_rows
    for hd in range(B_HEADS):
        r0 = _T_VB - _T_VA + B_V_DIM * hd
        vb_ref[0, hd, 0:B_V_DIM, :] = z[r0:r0 + B_V_DIM].astype(BF16)
        vb_ref[0, hd, B_V_DIM:VB_ROWS, :] = ones_rows

    def seg(a, b):
        return jnp.dot(xn, ws_ref[:, a:b], preferred_element_type=F32)

    def head_norm(z, gain):
        n = z.shape[1]
        ss = jnp.dot((z * z).astype(BF16), bd_ref[0:n, 0:n], preferred_element_type=F32)
        return z * lax.rsqrt(ss * (1.0 / HEAD_DIM) + EPS) * gain

    cos = cos_ref[...]
    sin_a = sa_ref[...]
    sin_b = sb_ref[...]

    def rope(z):
        return (z * cos + pltpu.roll(z, LANES - 16, 1) * sin_a + pltpu.roll(z, 16, 1) * sin_b)

    lane = lax.broadcasted_iota(jnp.int32, (tm, LANES), 1)
    lo = lane < HEAD_DIM
    t = pl.program_id(1) * tm + lax.broadcasted_iota(jnp.int32, (tm, LANES), 0)
    pos_hi = (t >> 7).astype(F32)
    pos_lo = (t & 127).astype(F32)

    p_k = seg(0, _S_G)
    k = rope(head_norm(p_k[:, 0:_S_KB], gka_ref[...]))
    k_sw = pltpu.roll(k, HEAD_DIM, 1)
    zero = jnp.zeros_like(k)
    ka_ref[0, 0, 0] = jnp.where(lo, k, zero).astype(BF16)
    ka_ref[0, 0, 1] = jnp.where(lo, zero, k_sw).astype(BF16)
    ka_ref[0, 1, 0] = jnp.where(lo, k_sw, zero).astype(BF16)
    ka_ref[0, 1, 1] = jnp.where(lo, zero, k).astype(BF16)

    z = head_norm(p_k[:, _S_KB:_S_G], gkb_ref[...])
    for hd in range(B_HEADS):
        kh = z[:, 128 * hd:128 * (hd + 1)]
        for mp in range(2):
            aug = (kaug_ref[2 * hd + mp:2 * hd + mp + 1, :] + pos_hi * kaug_ref[8 + mp:9 + mp, :]
                   + pos_lo * kaug_ref[10 + mp:11 + mp, :])
            keep = lo if mp == 0 else jnp.logical_not(lo)
            kb_ref[0, hd, mp] = jnp.where(keep, kh, aug).astype(BF16)

    g = seg(_S_G, _S_END)
    g_ref[0] = (g * jax.nn.sigmoid(g)).astype(BF16)


def _rope_tables(S):
    rows = S // GRID_W
    row = jnp.repeat(jnp.arange(rows), GRID_W).astype(F32)
    col = jnp.tile(jnp.arange(GRID_W), rows).astype(F32)
    half = HEAD_DIM // 2
    inv = ROPE_THETA ** (-jnp.arange(0, half, 2, dtype=F32) / half)
    ar = row[:, None] * inv
    ac = col[:, None] * inv
    ang = jnp.concatenate([ar, ar, ac, ac], axis=-1)
    cos, sin = jnp.cos(ang), jnp.sin(ang)
    first = (np.arange(HEAD_DIM) % 32) < 16
    sin_a = jnp.where(first, -sin, 0.0)
    sin_b = jnp.where(first, 0.0, sin)
    two = lambda t: jnp.concatenate([t, t], axis=-1)
    return two(cos), two(sin_a), two(sin_b), cos.T, jnp.where(first, -sin, sin).T


def _proj_ab(x, ws, wt, nm, gqat, gqbt, gka, gkb, bd, kaug, tm):
    B, S, _ = x.shape
    cos, sin_a, sin_b, cos_t, sin_t = _rope_tables(S)
    tok = lambda b, i: (b, i, 0)
    tab = pl.BlockSpec((tm, LANES), lambda b, i: (i, 0))
    tab_t = pl.BlockSpec((HEAD_DIM, tm), lambda b, i: (0, i))
    const2 = lambda shp: pl.BlockSpec(shp, lambda b, i: (0, 0))
    chan = lambda n, r: pl.BlockSpec((1, n, r, tm), lambda b, i: (b, 0, 0, i))
    out_shape = (
        jax.ShapeDtypeStruct((B, 4, LANES, S), BF16),
        jax.ShapeDtypeStruct((B, 2, 2, S, LANES), BF16),
        jax.ShapeDtypeStruct((B, 2, VA_ROWS, S), BF16),
        jax.ShapeDtypeStruct((B, 4, LANES, S), BF16),
        jax.ShapeDtypeStruct((B, 4, 2, S, LANES), BF16),
        jax.ShapeDtypeStruct((B, 4, VB_ROWS, S), BF16),
        jax.ShapeDtypeStruct((B, S, 1024), BF16),
    )
    out_specs = (
        chan(4, LANES),
        pl.BlockSpec((1, 2, 2, tm, LANES), lambda b, i: (b, 0, 0, i, 0)),
        chan(2, VA_ROWS),
        chan(4, LANES),
        pl.BlockSpec((1, 4, 2, tm, LANES), lambda b, i: (b, 0, 0, i, 0)),
        chan(4, VB_ROWS),
        pl.BlockSpec((1, tm, 1024), tok),
    )
    return pl.pallas_call(
        _proj_ab_kernel,
        grid=(B, S // tm),
        in_specs=[
            pl.BlockSpec((1, tm, D_MODEL), tok), tab, tab, tab, tab_t, tab_t,
            const2((1, D_MODEL)), const2((D_MODEL, _S_END)), const2((_T_END, D_MODEL)),
            const2((HEAD_DIM, tm)), const2((HEAD_DIM, tm)), const2((1, 128)), const2((1, 512)),
            const2((2 * MXU_DIM, 2 * MXU_DIM)), const2((16, LANES)),
        ],
        out_specs=out_specs,
        out_shape=out_shape,
        compiler_params=pltpu.CompilerParams(
            dimension_semantics=("parallel", "parallel"), vmem_limit_bytes=VMEM_LIMIT_BYTES),
        name="proj_ab",
    )(x, cos, sin_a, sin_b, cos_t, sin_t, nm, ws, wt, gqat, gqbt, gka, gkb, bd, kaug)


def _online_update(s, v, m_ref, acc_ref, r0, nrows):
    rows = pl.ds(r0, nrows)
    m_old = m_ref[rows, :]
    m_new = jnp.maximum(m_old, jnp.max(s, axis=-1, keepdims=True))
    alpha = jnp.exp2(m_old - m_new)
    p = jnp.exp2(s - m_new[:, :1]).astype(BF16)
    pv = jnp.dot(p, v, preferred_element_type=F32)
    nv = v.shape[1]
    if nv != LANES:
        alpha = jnp.concatenate([alpha] * (nv // LANES), axis=1)
    acc_ref[rows, :] = acc_ref[rows, :] * alpha + pv
    m_ref[rows, :] = m_new


def _flash_a_online_kernel(qt_ref, k_ref, vt_ref, o_ref, q_scr, v_scr, m_ref, acc_ref, *, tq, tk, S):
    q_scr[0:tq, :] = _to_token_major(qt_ref[0, 0])
    q_scr[tq:2 * tq, :] = _to_token_major(qt_ref[0, 1])

    @pl.when(pl.program_id(2) == 0)
    def _():
        for c in range(S // tk):
            cols = slice(c * tk, (c + 1) * tk)
            blk = jnp.concatenate([vt_ref[0, 0, :, cols], jnp.zeros((LANES - VA_ROWS, tk), BF16)], axis=0)
            v_scr[cols, :] = _to_token_major(blk)

    m_ref[...] = jnp.full(m_ref.shape, NEG_BIG, F32)
    acc_ref[...] = jnp.zeros(acc_ref.shape, F32)

    def body(c, carry):
        ks = pl.ds(pl.multiple_of(c * tk, tk), tk)
        q = q_scr[...]
        v = v_scr[ks, :]
        for var in range(2):
            s = _nt_dot(q, k_ref[0, 0, var, ks, :])
            _online_update(s, v, m_ref, acc_ref, var * 2 * tq, 2 * tq)
        return carry

    lax.fori_loop(0, S // tk, body, 0)

    lane = lax.broadcasted_iota(jnp.int32, (tq, LANES), 1)
    lo = lane < HEAD_DIM

    def head_out(blk):
        a = acc_ref[blk * tq:(blk + 1) * tq, :]
        return a * (1.0 / a[:, HEAD_DIM:HEAD_DIM + 1])

    o01 = jnp.where(lo, head_out(0), pltpu.roll(head_out(2), HEAD_DIM, 1))
    o23 = jnp.where(lo, head_out(1), pltpu.roll(head_out(3), HEAD_DIM, 1))
    o_ref[0, :, 0:128] = o01.astype(BF16)
    o_ref[0, :, 128:256] = o23.astype(BF16)


def _flash_a_bounded_kernel(qt_ref, k_ref, vt_ref, o_ref, qt_scr, acc_ref, *, tq, big, n_grp, unroll):
    qt_scr[:, 0:tq] = qt_ref[0, 0]
    qt_scr[:, tq:2 * tq] = qt_ref[0, 1]
    acc_ref[...] = jnp.zeros(acc_ref.shape, F32)

    def body(c, carry):
        ks = pl.ds(pl.multiple_of(c * big, big), big)
        for var in range(2):
            st = jnp.dot(k_ref[0, 0, var, ks, :], qt_scr[...], preferred_element_type=F32)
            acc_ref[var] += jnp.dot(vt_ref[0, 0, :, ks], jnp.exp2(st).astype(BF16),
                                    preferred_element_type=F32)
        return carry

    lax.fori_loop(0, n_grp, body, 0, unroll=unroll)

    outs = []
    for var in range(2):
        a = acc_ref[var]
        outs.append(a[0:HEAD_DIM] * (1.0 / a[HEAD_DIM:HEAD_DIM + 1]))
    for pr in range(2):
        cols = slice(pr * tq, (pr + 1) * tq)
        pair_t = jnp.concatenate([outs[0][:, cols], outs[1][:, cols]], axis=0)
        o_ref[0, :, 128 * pr:128 * (pr + 1)] = pair_t.T.astype(BF16)


def _kv_spec(block, index_map, single_buffer):
    if single_buffer:
        return pl.BlockSpec(block, index_map, pipeline_mode=pl.Buffered(1))
    return pl.BlockSpec(block, index_map)


def _flash_a(qa_t, ka, va_t, tq, tk, bounded):
    B, _, _, S = qa_t.shape
    single = S * LANES * 2 * 3 * 2 > 8 * 1024 * 1024
    if bounded:
        grp = max(1, min(KV_GROUP * tk // tq, S // tk))
        n_grp = S // (tk * grp)
        kern = functools.partial(_flash_a_bounded_kernel, tq=tq, big=tk * grp, n_grp=n_grp,
                                 unroll=min(KV_UNROLL, n_grp))
        scratch = [pltpu.VMEM((LANES, 2 * tq), BF16), pltpu.VMEM((2, VA_ROWS, 2 * tq), F32)]
    else:
        kern = functools.partial(_flash_a_online_kernel, tq=tq, tk=tk, S=S)
        scratch = [pltpu.VMEM((2 * tq, LANES), BF16), pltpu.VMEM((S, LANES), BF16),
                   pltpu.VMEM((4 * tq, LANES), F32), pltpu.VMEM((4 * tq, LANES), F32)]
    return pl.pallas_call(
        kern,
        grid=(B, A_KV_HEADS, S // tq),
        in_specs=[
            pl.BlockSpec((1, 2, LANES, tq), lambda b, g, i: (b, g, 0, i)),
            _kv_spec((1, 1, 2, S, LANES), lambda b, g, i: (b, g, 0, 0, 0), single),
            _kv_spec((1, 1, VA_ROWS, S), lambda b, g, i: (b, g, 0, 0), single),
        ],
        out_specs=pl.BlockSpec((1, tq, 256), lambda b, g, i: (b, i, g)),
        out_shape=jax.ShapeDtypeStruct((B, S, 512), BF16),
        scratch_shapes=scratch,
        compiler_params=pltpu.CompilerParams(
            dimension_semantics=("parallel", "parallel", "arbitrary"),
            vmem_limit_bytes=VMEM_LIMIT_BYTES),
        name="flash_a_bounded" if bounded else "flash_a_online",
    )(qa_t, ka, va_t)


def _flash_b_lambda(lq1_ref, lk1_ref, lq2_ref, lk2_ref, lam_init):
    return (jnp.exp(jnp.sum(lq1_ref[...] * lk1_ref[...], keepdims=True))
            - jnp.exp(jnp.sum(lq2_ref[...] * lk2_ref[...], keepdims=True)) + lam_init)


def _flash_b_online_kernel(qt_ref, k_ref, vt_ref, tab_ref, lq1_ref, lk1_ref, lq2_ref, lk2_ref, sub_ref,
                           o_ref, q_scr, v_scr, rel_ref, m_ref, acc_ref, *, tq, tk, S, lam_init):
    lane = lax.broadcasted_iota(jnp.int32, (tq, LANES), 1)
    lo = lane < HEAD_DIM
    slab = _to_token_major(qt_ref[0, 0])
    zero = jnp.zeros_like(slab)
    q_scr[0] = jnp.where(lo, slab, zero)
    q_scr[1] = jnp.where(lo, zero, slab)

    @pl.when(pl.program_id(2) == 0)
    def _():
        pad = jnp.zeros((LANES - BF16_ROWS, tk), BF16)
        for c in range(S // tk):
            cols = slice(c * tk, (c + 1) * tk)
            v_scr[cols, 0:B_V_DIM] = _to_token_major(vt_ref[0, 0, 0:B_V_DIM, cols])
            ones_blk = jnp.concatenate([vt_ref[0, 0, B_V_DIM:VB_ROWS, cols], pad], axis=0)
            v_scr[cols, B_V_DIM:2 * LANES] = _to_token_major(ones_blk)

    m_ref[...] = jnp.full(m_ref.shape, NEG_BIG, F32)
    acc_ref[...] = jnp.zeros(acc_ref.shape, F32)
    rel_ref[...] = (lax.broadcasted_iota(jnp.int32, (tq, tk), 0)
                    - lax.broadcasted_iota(jnp.int32, (tq, tk), 1)).astype(F32)
    neg_slope = -tab_ref[0, 6:7, 0:1]
    q0 = pl.program_id(2) * tq

    def body(c, carry):
        k0 = pl.multiple_of(c * tk, tk)
        ks = pl.ds(k0, tk)
        v = v_scr[ks, :]
        bias = jnp.abs(rel_ref[...] + (q0 - k0).astype(F32)) * neg_slope
        for mp in range(2):
            s = _nt_dot(q_scr[mp], k_ref[0, 0, mp, ks, :]) + bias
            _online_update(s, v, m_ref, acc_ref, mp * tq, tq)
        return carry

    lax.fori_loop(0, S // tk, body, 0)

    lam = _flash_b_lambda(lq1_ref, lk1_ref, lq2_ref, lk2_ref, lam_init)
    a1 = acc_ref[0:tq, :]
    a2 = acc_ref[tq:2 * tq, :]
    o1 = a1[:, 0:B_V_DIM] * (1.0 / a1[:, B_V_DIM:B_V_DIM + 1])
    o2 = a2[:, 0:B_V_DIM] * (1.0 / a2[:, B_V_DIM:B_V_DIM + 1])
    o = o1 - lam * o2
    o_ref[0] = (_rms_rows(o, sub_ref[...]) * (1.0 - lam_init)).astype(BF16)


def _flash_b_bounded_kernel(qt_ref, k_ref, vt_ref, tab_ref, aug_ref, ahead_ref, lq1_ref, lk1_ref,
                            lq2_ref, lk2_ref, subt_ref, o_ref, qt_scr, acc_ref, *, tq, tk, grp, n_grp,
                            n_q, unroll, lam_init):
    big = grp * tk
    neg_2c = -2.0 * tab_ref[0, 6:7, 0:1]
    lam = _flash_b_lambda(lq1_ref, lk1_ref, lq2_ref, lk2_ref, lam_init)
    static = n_grp == 1
    zpad = jnp.zeros((HEAD_DIM - BF16_ROWS, tq), BF16)

    for qq in range(n_q):
        qi = qq if static else pl.program_id(2) * n_q + qq
        cols = slice(qq * tq, (qq + 1) * tq)
        t = qi * tq + lax.broadcasted_iota(jnp.int32, (BF16_ROWS, tq), 1)
        pos_hi = (t >> 7).astype(F32)
        pos_lo = (t & 127).astype(F32)
        for mp in range(2):
            rep = lambda r: jnp.concatenate([aug_ref[0, mp, r]] * (tq // LANES), axis=1)
            aug = rep(0) + pos_hi * rep(1) + pos_lo * rep(2)
            for side in range(2):
                rows = jnp.concatenate([(aug if side == 0 else -aug).astype(BF16), zpad], axis=0)
                if mp == 0:
                    qt_scr[qq, mp, side] = jnp.concatenate([qt_ref[0, 0, 0:HEAD_DIM, cols], rows], axis=0)
                else:
                    qt_scr[qq, mp, side] = jnp.concatenate([rows, qt_ref[0, 0, HEAD_DIM:LANES, cols]], axis=0)
    acc_ref[...] = jnp.zeros(acc_ref.shape, F32)

    if static:
        corr = ahead_ref[big:big + tk, :] * neg_2c
        for c in range(grp):
            ks = slice(c * tk, (c + 1) * tk)
            for mp in range(2):
                slots = jnp.concatenate([qt_scr[qq, mp, int(c > qq)] for qq in range(n_q)], axis=1)
                st = jnp.dot(k_ref[0, 0, mp, ks, :], slots, preferred_element_type=F32)
                parts = [st[:, qq * tq:(qq + 1) * tq] + corr if qq == c else st[:, qq * tq:(qq + 1) * tq]
                         for qq in range(n_q)]
                p = jnp.exp2(jnp.concatenate(parts, axis=1)).astype(BF16)
                acc_ref[mp] += jnp.dot(vt_ref[0, 0, :, ks], p, preferred_element_type=F32)
    else:
        qi = pl.program_id(2)
        gd = qi // grp
        r = qi - gd * grp

        def attend(ks, side, corr):
            for mp in range(2):
                st = jnp.dot(k_ref[0, 0, mp, ks, :], qt_scr[0, mp, side], preferred_element_type=F32)
                if corr is not None:
                    st = st + corr
                acc_ref[mp] += jnp.dot(vt_ref[0, 0, :, ks], jnp.exp2(st).astype(BF16),
                                       preferred_element_type=F32)

        w0 = pl.multiple_of(big - r * tk, tk)
        attend(pl.ds(pl.multiple_of(gd * big, big), big), 0, ahead_ref[pl.ds(w0, big), :] * neg_2c)

        def body(j, carry):
            g = gd + j
            g = jnp.where(g >= n_grp, g - n_grp, g)
            attend(pl.ds(pl.multiple_of(g * big, big), big), (g > gd).astype(jnp.int32), None)
            return carry
        lax.fori_loop(1, n_grp, body, 0, unroll=unroll)

    for qq in range(n_q):
        a1 = acc_ref[0, :, qq * tq:(qq + 1) * tq]
        a2 = acc_ref[1, :, qq * tq:(qq + 1) * tq]
        o1 = a1[0:B_V_DIM] * (1.0 / a1[B_V_DIM:B_V_DIM + 1])
        o2 = a2[0:B_V_DIM] * (1.0 / a2[B_V_DIM:B_V_DIM + 1])
        o = _rms_cols(o1 - lam * o2, subt_ref[...]) * (1.0 - lam_init)
        o_ref[0, qq * tq:(qq + 1) * tq, :] = o.T.astype(BF16)


def _flash_b(qb_t, kb, vb_t, tab, aug_t, lq1, lk1, lq2, lk2, subln, lam_init, tq, tk, bounded):
    B, _, _, S = qb_t.shape
    assert tq == tk
    single = S * LANES * 2 * 4 * 2 > 8 * 1024 * 1024
    n_tiles = S // tq
    vec = lambda n: pl.BlockSpec((1, n), lambda b, h, i: (0, 0))
    if bounded:
        grp = S // tk if S // tk <= SHORT_SEQ_CHUNKS else KV_GROUP
        n_grp = S // (tk * grp)
        n_q = n_tiles if n_grp == 1 else 1
        big = grp * tk
        ahead = np.maximum(np.arange(-big, big)[:, None] - np.arange(tq)[None, :], 0).astype(np.float32)
        sub_t = jnp.broadcast_to(subln.reshape(B_V_DIM, 1), (B_V_DIM, tq))
        kern = functools.partial(_flash_b_bounded_kernel, tq=tq, tk=tk, grp=grp, n_grp=n_grp, n_q=n_q,
                                 unroll=max(n_grp - 1, 1), lam_init=lam_init)
        extra_in = [jnp.asarray(ahead)]
        extra_specs = [
            pl.BlockSpec((1, 2, 3, BF16_ROWS, LANES), lambda b, h, i: (h, 0, 0, 0, 0)),
            pl.BlockSpec((2 * big, tq), lambda b, h, i: (0, 0), pipeline_mode=pl.Buffered(1)),
        ]
        sub_in, sub_spec = sub_t, pl.BlockSpec((B_V_DIM, tq), lambda b, h, i: (0, 0))
        scratch = [pltpu.VMEM((n_q, 2, 2, LANES, tq), BF16), pltpu.VMEM((2, VB_ROWS, n_q * tq), F32)]
        args = (qb_t, kb, vb_t, tab, aug_t, *extra_in, lq1, lk1, lq2, lk2, sub_in)
    else:
        n_q = 1
        kern = functools.partial(_flash_b_online_kernel, tq=tq, tk=tk, S=S, lam_init=lam_init)
        extra_specs = []
        sub_in, sub_spec = subln, vec(B_V_DIM)
        scratch = [pltpu.VMEM((2, tq, LANES), BF16), pltpu.VMEM((S, 2 * LANES), BF16),
                   pltpu.VMEM((tq, tk), F32), pltpu.VMEM((2 * tq, LANES), F32),
                   pltpu.VMEM((2 * tq, 2 * LANES), F32)]
        args = (qb_t, kb, vb_t, tab, lq1, lk1, lq2, lk2, sub_in)
    return pl.pallas_call(
        kern,
        grid=(B, B_HEADS, n_tiles // n_q),
        in_specs=[
            pl.BlockSpec((1, 1, LANES, n_q * tq), lambda b, h, i: (b, h, 0, i)),
            _kv_spec((1, 1, 2, S, LANES), lambda b, h, i: (b, h, 0, 0, 0), single),
            _kv_spec((1, 1, VB_ROWS, S), lambda b, h, i: (b, h, 0, 0), single),
            pl.BlockSpec((1, 8, LANES), lambda b, h, i: (h, 0, 0)),
            *extra_specs,
            vec(HEAD_DIM), vec(HEAD_DIM), vec(HEAD_DIM), vec(HEAD_DIM), sub_spec,
        ],
        out_specs=pl.BlockSpec((1, n_q * tq, LANES), lambda b, h, i: (b, i, h)),
        out_shape=jax.ShapeDtypeStruct((B, S, 512), BF16),
        scratch_shapes=scratch,
        compiler_params=pltpu.CompilerParams(
            dimension_semantics=("parallel", "parallel", "arbitrary"),
            vmem_limit_bytes=VMEM_LIMIT_BYTES),
        name="flash_b_bounded" if bounded else "flash_b_online",
    )(*args)


def _ple_tail(h1, p, npl, wg_ref, wp_ref):
    rn = _rms_rows(h1, npl).astype(BF16)
    gate = jax.nn.sigmoid(jnp.dot(rn, wg_ref[...], preferred_element_type=F32))
    return h1 + gate * jnp.dot(p.astype(BF16), wp_ref[...], preferred_element_type=F32)


def _out_ple_kernel(h_ref, oa_ref, ob_ref, g_ref, p_ref, wo_ref, npl_ref, wg_ref, wp_ref, out_ref):
    g = g_ref[0]
    y = jnp.concatenate([oa_ref[0] * g[:, 0:512], ob_ref[0] * g[:, 512:1024]], axis=1)
    h1 = h_ref[0] + jnp.dot(y, wo_ref[...], preferred_element_type=F32)
    out_ref[0] = _ple_tail(h1, p_ref[0, 0], npl_ref[...], wg_ref, wp_ref)


def _out_ple(h, oa, ob, gates, p, layer, wo, npl, wg, wp, tm):
    B, S, _ = h.shape
    tok = lambda b, i: (b, i, 0)
    const2 = lambda shp: pl.BlockSpec(shp, lambda b, i: (0, 0))
    return pl.pallas_call(
        _out_ple_kernel,
        grid=(B, S // tm),
        in_specs=[
            pl.BlockSpec((1, tm, D_MODEL), tok),
            pl.BlockSpec((1, tm, 512), tok),
            pl.BlockSpec((1, tm, 512), tok),
            pl.BlockSpec((1, tm, 1024), tok),
            pl.BlockSpec((1, 1, tm, PLE_DIM), lambda b, i: (layer, b, i, 0)),
            const2((1024, D_MODEL)), const2((1, D_MODEL)), const2((D_MODEL, D_MODEL)),
            const2((PLE_DIM, D_MODEL)),
        ],
        out_specs=pl.BlockSpec((1, tm, D_MODEL), tok),
        out_shape=jax.ShapeDtypeStruct((B, S, D_MODEL), F32),
        compiler_params=pltpu.CompilerParams(
            dimension_semantics=("parallel", "parallel"), vmem_limit_bytes=VMEM_LIMIT_BYTES),
        name="out_ple",
    )(h, oa, ob, gates, p, wo, npl, wg, wp)


def _layer_c_kernel(h_ref, hp_ref, hn_ref, p_ref, nm_ref, wi_ref, wgrp_ref, sc_ref, wo_ref,
                    npl_ref, wg_ref, wp_ref, out_ref, *, tm, S):
    i = pl.program_id(1)
    nt = pl.num_programs(1)
    h = h_ref[0]
    hp = hp_ref[0] * (i > 0).astype(F32)
    hn = hn_ref[0] * (i < nt - 1).astype(F32)
    n = tm + 2 * POOL_HALO
    h_ext = jnp.concatenate([hp, h, hn], axis=0)
    xn = _rms_rows(h_ext, nm_ref[...]).astype(BF16)
    u_ext = jnp.dot(xn, wi_ref[:, 0:1024], preferred_element_type=F32)
    g = jnp.dot(xn[POOL_HALO:POOL_HALO + tm], wi_ref[:, 1024:2048], preferred_element_type=F32)

    t = i * tm + lax.broadcasted_iota(jnp.int32, (tm, C_GRP), 0)
    mixed = []
    for gi, w in enumerate(POOL_WINDOWS):
        u = u_ext[:, C_GRP * gi:C_GRP * (gi + 1)]
        win = u + pltpu.roll(u, 1, 0)
        half = 1
        while 2 * half < w:
            win = pltpu.roll(win, half, 0) + pltpu.roll(win, n - half, 0)
            half *= 2
        cnt = (jnp.minimum(t + w // 2, S) - jnp.maximum(t - w // 2, 0)).astype(F32)
        pooled = win[POOL_HALO:POOL_HALO + tm] / cnt - u[POOL_HALO:POOL_HALO + tm]
        mixed.append(jnp.dot(pooled.astype(BF16), wgrp_ref[gi], preferred_element_type=F32))
    mixed = jnp.concatenate(mixed, axis=1)
    y = ((mixed * sc_ref[...]) * (g * jax.nn.sigmoid(g))).astype(BF16)
    h1 = h + jnp.dot(y, wo_ref[...], preferred_element_type=F32)
    out_ref[0] = _ple_tail(h1, p_ref[0, 0], npl_ref[...], wg_ref, wp_ref)


def _layer_c(h, p, layer, nm, wi, wgrp, sc, wo, npl, wg, wp, tm):
    B, S, _ = h.shape
    tok = lambda b, i: (b, i, 0)
    const2 = lambda shp: pl.BlockSpec(shp, lambda b, i: (0, 0))
    per = tm // POOL_HALO
    last = S // POOL_HALO - 1
    kern = functools.partial(_layer_c_kernel, tm=tm, S=S)
    return pl.pallas_call(
        kern,
        grid=(B, S // tm),
        in_specs=[
            pl.BlockSpec((1, tm, D_MODEL), tok),
            pl.BlockSpec((1, POOL_HALO, D_MODEL), lambda b, i: (b, jnp.maximum(i * per - 1, 0), 0)),
            pl.BlockSpec((1, POOL_HALO, D_MODEL), lambda b, i: (b, jnp.minimum((i + 1) * per, last), 0)),
            pl.BlockSpec((1, 1, tm, PLE_DIM), lambda b, i: (layer, b, i, 0)),
            const2((1, D_MODEL)), const2((D_MODEL, 2048)),
            pl.BlockSpec((4, C_GRP, C_GRP), lambda b, i: (0, 0, 0)),
            const2((1, 1024)), const2((1024, D_MODEL)), const2((1, D_MODEL)),
            const2((D_MODEL, D_MODEL)), const2((PLE_DIM, D_MODEL)),
        ],
        out_specs=pl.BlockSpec((1, tm, D_MODEL), tok),
        out_shape=jax.ShapeDtypeStruct((B, S, D_MODEL), F32),
        compiler_params=pltpu.CompilerParams(
            dimension_semantics=("parallel", "parallel"), vmem_limit_bytes=VMEM_LIMIT_BYTES),
        name="layer_c",
    )(h, h, h, p, nm, wi, wgrp, sc, wo, npl, wg, wp)


def _score_bound(gq, gk):
    slack = 1.02
    return (HEAD_DIM * SM_SCALE * LOG2E * slack) * jnp.max(jnp.abs(gq)) * jnp.max(jnp.abs(gk))


def _alibi_tables():
    kaug = np.zeros((16, LANES), np.float32)
    qtab = np.zeros((B_HEADS, 8, LANES), np.float32)
    aug_t = np.zeros((B_HEADS, 2, 3, BF16_ROWS, LANES), np.float32)
    aug_t[:, :, 1, 0:3, :] = -128.0
    aug_t[:, :, 2, 3:6, :] = -1.0
    for mp in range(2):
        base = HEAD_DIM if mp == 0 else 0
        kaug[8 + mp, base + 6:base + 9] = 128.0
        kaug[10 + mp, base + 9:base + 12] = 1.0
    for h, slope in enumerate(ALIBI_SLOPES):
        c = np.float64(slope) * LOG2E
        pieces, rest = [], c
        for _ in range(3):
            piece = np.float64(np.float32(rest).astype(BF16))
            pieces.append(piece)
            rest -= piece
        for mp in range(2):
            base = HEAD_DIM if mp == 0 else 0
            kaug[2 * h + mp, base:base + 6] = pieces + pieces
        aug_t[h, :, 0, 6:12, :] = np.asarray(pieces + pieces)[:, None]
        qtab[h, 6, :] = np.float32(c)
    return jnp.asarray(kaug), jnp.asarray(qtab), jnp.asarray(aug_t)


def _trunk(x, p, prm, tq_a, tq_b, tk, tm):
    row = lambda v: v.astype(F32).reshape(1, -1)
    col = lambda v, s: jnp.broadcast_to((v.astype(F32) * s).reshape(-1, 1), (v.shape[0], tm))
    lam_init = 0.8 - 0.6 * math.exp(-0.3 * 0)
    gq_scale = SM_SCALE * LOG2E
    gka = jnp.tile(prm["kn_a"][0].astype(F32), 2).reshape(1, -1)
    gkb = jnp.tile(prm["kn_b"][0].astype(F32), 8).reshape(1, -1)
    qa_t, ka, va_t, qb_t, kb, vb_t, gates = _proj_ab(
        x, prm["w_ab_s"], prm["w_ab_t"], row(prm["norm_mix"][0]), col(prm["qn_a"][0], gq_scale),
        col(prm["qn_b"][0], gq_scale), gka, gkb, prm["bd"], prm["kaug"], tm)
    oa = lax.cond(
        _score_bound(prm["qn_a"][0], prm["kn_a"][0]) <= BOUNDED_SCORE_LIMIT,
        lambda q, k, v: _flash_a(q, k, v, tq_a, tk, True),
        lambda q, k, v: _flash_a(q, k, v, tq_a, tk, False),
        qa_t, ka, va_t)
    lam_vecs = (row(prm["lam_q1"][0]), row(prm["lam_k1"][0]), row(prm["lam_q2"][0]),
                row(prm["lam_k2"][0]), row(prm["subln_b"][0]))
    ob = lax.cond(
        _score_bound(prm["qn_b"][0], prm["kn_b"][0]) <= BOUNDED_SCORE_LIMIT,
        lambda q, k, v: _flash_b(q, k, v, prm["qtab"], prm["aug_t"], *lam_vecs, lam_init, tq_b, tk, True),
        lambda q, k, v: _flash_b(q, k, v, prm["qtab"], prm["aug_t"], *lam_vecs, lam_init, tq_b, tk, False),
        qb_t, kb, vb_t)
    h = _out_ple(x, oa, ob, gates, p, 0, prm["w_out_ab"][0], row(prm["norm_ple"][0]),
                 prm["w_ple_gate"][0], prm["w_ple_proj"][0], tm)
    h = _layer_c(h, p, 1, row(prm["norm_mix"][1]), prm["w_in_c"][0], prm["w_grp_c"][0],
                 row(prm["scale_c"][0]), prm["w_out_c"][0], row(prm["norm_ple"][1]),
                 prm["w_ple_gate"][1], prm["w_ple_proj"][1], tm)
    return h


def _prepare(norm_mix, w_in_ab, qn_a, kn_a, qn_b, kn_b, lam_q1, lam_k1, lam_q2, lam_k2, subln_b,
             w_out_ab, w_in_c, w_grp_c, scale_c, w_out_c, norm_ple, w_ple_gate, w_ple_proj):
    blk = np.arange(2 * MXU_DIM) // HEAD_DIM
    kaug, qtab, aug_t = _alibi_tables()
    w_ab = w_in_ab[0].astype(BF16)
    cols = lambda a, b: w_ab[:, a:b]
    return dict(
        norm_mix=norm_mix, qn_a=qn_a, kn_a=kn_a, qn_b=qn_b, kn_b=kn_b,
        lam_q1=lam_q1, lam_k1=lam_k1, lam_q2=lam_q2, lam_k2=lam_k2, subln_b=subln_b,
        scale_c=scale_c, norm_ple=norm_ple,
        w_ab_s=jnp.concatenate([cols(_KA, _VA), cols(_KB, _VB), cols(_GA, _QB), cols(_GB, _AB_END)], axis=1),
        w_ab_t=jnp.concatenate([cols(_QA, _KA), cols(_QB, _KB), cols(_VA, _GA), cols(_VB, _GB)], axis=1).T,
        w_out_ab=w_out_ab.astype(BF16), w_in_c=w_in_c.astype(BF16),
        w_grp_c=w_grp_c.astype(BF16), w_out_c=w_out_c.astype(BF16),
        w_ple_gate=w_ple_gate.astype(BF16), w_ple_proj=w_ple_proj.astype(BF16),
        bd=jnp.asarray(blk[:, None] == blk[None, :], BF16),
        kaug=kaug, qtab=qtab, aug_t=aug_t,
    )


def kernel(x_prompt, x_sample, p_prompt, p_sample, norm_mix, w_in_ab, qn_a, kn_a, qn_b, kn_b,
           lam_q1, lam_k1, lam_q2, lam_k2, subln_b, w_out_ab, w_in_c, w_grp_c, scale_c, w_out_c,
           norm_ple, w_ple_gate, w_ple_proj):
    prm = _prepare(norm_mix, w_in_ab, qn_a, kn_a, qn_b, kn_b, lam_q1, lam_k1, lam_q2, lam_k2, subln_b,
                   w_out_ab, w_in_c, w_grp_c, scale_c, w_out_c, norm_ple, w_ple_gate, w_ple_proj)
    y_prompt = _trunk(x_prompt, p_prompt, prm, tq_a=512, tq_b=512, tk=512, tm=1024)
    y_sample = _trunk(x_sample, p_sample, prm, tq_a=1024, tq_b=512, tk=512, tm=1024)
    return (y_prompt, y_sample)
```

```python
import functools
import math

import jax
import jax.numpy as jnp
import numpy as np
from jax import lax
from jax.experimental import pallas as pl
from jax.experimental.pallas import tpu as pltpu

F32 = jnp.float32
BF16 = jnp.bfloat16

D_MODEL = 1024
PLE_DIM = 256
GRID_W = 64
HEAD_DIM = 64
EPS = 1e-6
ROPE_THETA = 10000.0
A_HEADS = 8
A_KV_HEADS = 2
B_HEADS = 4
B_V_DIM = 2 * HEAD_DIM
ALIBI_SLOPES = tuple(2.0 ** (-8.0 * (h + 1) / B_HEADS) for h in range(B_HEADS))
POOL_WINDOWS = (2, 4, 8, 16)
C_GRP = 256
POOL_HALO = 8
LOG2E = math.log2(math.e)
SM_SCALE = HEAD_DIM ** -0.5

LANES = 128
BF16_ROWS = 16
MXU_DIM = 256
VMEM_LIMIT_BYTES = 56 * 1024 * 1024

VA_ROWS = HEAD_DIM + BF16_ROWS
VB_ROWS = B_V_DIM + BF16_ROWS

_QA, _KA, _VA, _GA, _QB, _KB, _VB, _GB, _AB_END = 0, 512, 640, 768, 1280, 1792, 2304, 2816, 3328
_S_KB, _S_G, _S_END = 128, 640, 1664
_T_QB, _T_VA, _T_VB, _T_END = 512, 1024, 1152, 1664

NEG_BIG = -1e30
BOUNDED_SCORE_LIMIT = 60.0
KV_GROUP = 2
KV_UNROLL = 8
SHORT_SEQ_CHUNKS = 4


def _rms_rows(x, g):
    ms = jnp.mean(x * x, axis=-1, keepdims=True)
    return x * lax.rsqrt(ms + EPS) * g


def _rms_cols(x, g):
    ms = jnp.mean(x * x, axis=0, keepdims=True)
    return x * lax.rsqrt(ms + EPS) * g


def _nt_dot(a, b):
    return lax.dot_general(a, b, (((1,), (1,)), ((), ())), preferred_element_type=F32)


def _to_token_major(xt):
    return xt.astype(F32).T.astype(BF16)


def _proj_ab_kernel(x_ref, cos_ref, sa_ref, sb_ref, cost_ref, sint_ref, nm_ref, ws_ref, wt_ref,
                    gqat_ref, gqbt_ref, gka_ref, gkb_ref, bd_ref, kaug_ref,
                    qa_ref, ka_ref, va_ref, qb_ref, kb_ref, vb_ref, g_ref):
    x = x_ref[0]
    xn = _rms_rows(x, nm_ref[...]).astype(BF16)
    tm = x.shape[0]

    def seg_t(a, b):
        return _nt_dot(wt_ref[a:b, :], xn)

    cos_t = cost_ref[...]
    sin_t = sint_ref[...]
    z = seg_t(0, _T_QB)
    for h in range(A_HEADS):
        zh = _rms_cols(z[64 * h:64 * (h + 1)], gqat_ref[...])
        sw = jnp.concatenate([zh[16:32], zh[0:16], zh[48:64], zh[32:48]], axis=0)
        r0 = 64 * (h % 2)
        qa_ref[0, h // 2, r0:r0 + 64, :] = (zh * cos_t + sw * sin_t).astype(BF16)
    z = seg_t(_T_QB, _T_VA)
    for h in range(2 * B_HEADS):
        r0 = 64 * (h % 2)
        qb_ref[0, h // 2, r0:r0 + 64, :] = _rms_cols(z[64 * h:64 * (h + 1)], gqbt_ref[...]).astype(BF16)
    ones_rows = (lax.broadcasted_iota(jnp.int32, (BF16_ROWS, tm), 0) == 0).astype(BF16)
    z = seg_t(_T_VA, _T_END)
    for kv in range(A_KV_HEADS):
        va_ref[0, kv, 0:HEAD_DIM, :] = z[64 * kv:64 * (kv + 1)].astype(BF16)
        va_ref[0, kv, HEAD_DIM:VA_ROWS, :] = ones_rows
    for hd in range(B_HEADS):
        r0 = _T_VB - _T_VA + B_V_DIM * hd
        vb_ref[0, hd, 0:B_V_DIM, :] = z[r0:r0 + B_V_DIM].astype(BF16)
        vb_ref[0, hd, B_V_DIM:VB_ROWS, :] = ones_rows

    def seg(a, b):
        return jnp.dot(xn, ws_ref[:, a:b], preferred_element_type=F32)

    def head_norm(z, gain):
        n = z.shape[1]
        ss = jnp.dot((z * z).astype(BF16), bd_ref[0:n, 0:n], preferred_element_type=F32)
        return z * lax.rsqrt(ss * (1.0 / HEAD_DIM) + EPS) * gain

    cos = cos_ref[...]
    sin_a = sa_ref[...]
    sin_b = sb_ref[...]

    def rope(z):
        return (z * cos + pltpu.roll(z, LANES - 16, 1) * sin_a + pltpu.roll(z, 16, 1) * sin_b)

    lane = lax.broadcasted_iota(jnp.int32, (tm, LANES), 1)
    lo = lane < HEAD_DIM
    t = pl.program_id(1) * tm + lax.broadcasted_iota(jnp.int32, (tm, LANES), 0)
    pos_hi = (t >> 7).astype(F32)
    pos_lo = (t & 127).astype(F32)

    p_k = seg(0, _S_G)
    k = rope(head_norm(p_k[:, 0:_S_KB], gka_ref[...]))
    k_sw = pltpu.roll(k, HEAD_DIM, 1)
    zero = jnp.zeros_like(k)
    ka_ref[0, 0, 0] = jnp.where(lo, k, zero).astype(BF16)
    ka_ref[0, 0, 1] = jnp.where(lo, zero, k_sw).astype(BF16)
    ka_ref[0, 1, 0] = jnp.where(lo, k_sw, zero).astype(BF16)
    ka_ref[0, 1, 1] = jnp.where(lo, zero, k).astype(BF16)

    z = head_norm(p_k[:, _S_KB:_S_G], gkb_ref[...])
    for hd in range(B_HEADS):
        kh = z[:, 128 * hd:128 * (hd + 1)]
        for mp in range(2):
            aug = (kaug_ref[2 * hd + mp:2 * hd + mp + 1, :] + pos_hi * kaug_ref[8 + mp:9 + mp, :]
                   + pos_lo * kaug_ref[10 + mp:11 + mp, :])
            keep = lo if mp == 0 else jnp.logical_not(lo)
            kb_ref[0, hd, mp] = jnp.where(keep, kh, aug).astype(BF16)

    g = seg(_S_G, _S_END)
    g_ref[0] = (g * jax.nn.sigmoid(g)).astype(BF16)


def _rope_tables(S):
    rows = S // GRID_W
    row = jnp.repeat(jnp.arange(rows), GRID_W).astype(F32)
    col = jnp.tile(jnp.arange(GRID_W), rows).astype(F32)
    half = HEAD_DIM // 2
    inv = ROPE_THETA ** (-jnp.arange(0, half, 2, dtype=F32) / half)
    ar = row[:, None] * inv
    ac = col[:, None] * inv
    ang = jnp.concatenate([ar, ar, ac, ac], axis=-1)
    cos, sin = jnp.cos(ang), jnp.sin(ang)
    first = (np.arange(HEAD_DIM) % 32) < 16
    sin_a = jnp.where(first, -sin, 0.0)
    sin_b = jnp.where(first, 0.0, sin)
    two = lambda t: jnp.concatenate([t, t], axis=-1)
    return two(cos), two(sin_a), two(sin_b), cos.T, jnp.where(first, -sin, sin).T


def _proj_ab(x, ws, wt, nm, gqat, gqbt, gka, gkb, bd, kaug, tm):
    B, S, _ = x.shape
    cos, sin_a, sin_b, cos_t, sin_t = _rope_tables(S)
    tok = lambda b, i: (b, i, 0)
    tab = pl.BlockSpec((tm, LANES), lambda b, i: (i, 0))
    tab_t = pl.BlockSpec((HEAD_DIM, tm), lambda b, i: (0, i))
    const2 = lambda shp: pl.BlockSpec(shp, lambda b, i: (0, 0))
    chan = lambda n, r: pl.BlockSpec((1, n, r, tm), lambda b, i: (b, 0, 0, i))
    out_shape = (
        jax.ShapeDtypeStruct((B, 4, LANES, S), BF16),
        jax.ShapeDtypeStruct((B, 2, 2, S, LANES), BF16),
        jax.ShapeDtypeStruct((B, 2, VA_ROWS, S), BF16),
        jax.ShapeDtypeStruct((B, 4, LANES, S), BF16),
        jax.ShapeDtypeStruct((B, 4, 2, S, LANES), BF16),
        jax.ShapeDtypeStruct((B, 4, VB_ROWS, S), BF16),
        jax.ShapeDtypeStruct((B, S, 1024), BF16),
    )
    out_specs = (
        chan(4, LANES),
        pl.BlockSpec((1, 2, 2, tm, LANES), lambda b, i: (b, 0, 0, i, 0)),
        chan(2, VA_ROWS),
        chan(4, LANES),
        pl.BlockSpec((1, 4, 2, tm, LANES), lambda b, i: (b, 0, 0, i, 0)),
        chan(4, VB_ROWS),
        pl.BlockSpec((1, tm, 1024), tok),
    )
    return pl.pallas_call(
        _proj_ab_kernel,
        grid=(B, S // tm),
        in_specs=[
            pl.BlockSpec((1, tm, D_MODEL), tok), tab, tab, tab, tab_t, tab_t,
            const2((1, D_MODEL)), const2((D_MODEL, _S_END)), const2((_T_END, D_MODEL)),
            const2((HEAD_DIM, tm)), const2((HEAD_DIM, tm)), const2((1, 128)), const2((1, 512)),
            const2((2 * MXU_DIM, 2 * MXU_DIM)), const2((16, LANES)),
        ],
        out_specs=out_specs,
        out_shape=out_shape,
        compiler_params=pltpu.CompilerParams(
            dimension_semantics=("parallel", "parallel"), vmem_limit_bytes=VMEM_LIMIT_BYTES),
        name="proj_ab",
    )(x, cos, sin_a, sin_b, cos_t, sin_t, nm, ws, wt, gqat, gqbt, gka, gkb, bd, kaug)


def _online_update(s, v, m_ref, acc_ref, r0, nrows):
    rows = pl.ds(r0, nrows)
    m_old = m_ref[rows, :]
    m_new = jnp.maximum(m_old, jnp.max(s, axis=-1, keepdims=True))
    alpha = jnp.exp2(m_old - m_new)
    p = jnp.exp2(s - m_new[:, :1]).astype(BF16)
    pv = jnp.dot(p, v, preferred_element_type=F32)
    nv = v.shape[1]
    if nv != LANES:
        alpha = jnp.concatenate([alpha] * (nv // LANES), axis=1)
    acc_ref[rows, :] = acc_ref[rows, :] * alpha + pv
    m_ref[rows, :] = m_new


def _flash_a_online_kernel(qt_ref, k_ref, vt_ref, o_ref, q_scr, v_scr, m_ref, acc_ref, *, tq, tk, S):
    q_scr[0:tq, :] = _to_token_major(qt_ref[0, 0])
    q_scr[tq:2 * tq, :] = _to_token_major(qt_ref[0, 1])

    @pl.when(pl.program_id(2) == 0)
    def _():
        for c in range(S // tk):
            cols = slice(c * tk, (c + 1) * tk)
            blk = jnp.concatenate([vt_ref[0, 0, :, cols], jnp.zeros((LANES - VA_ROWS, tk), BF16)], axis=0)
            v_scr[cols, :] = _to_token_major(blk)

    m_ref[...] = jnp.full(m_ref.shape, NEG_BIG, F32)
    acc_ref[...] = jnp.zeros(acc_ref.shape, F32)

    def body(c, carry):
        ks = pl.ds(pl.multiple_of(c * tk, tk), tk)
        q = q_scr[...]
        v = v_scr[ks, :]
        for var in range(2):
            s = _nt_dot(q, k_ref[0, 0, var, ks, :])
            _online_update(s, v, m_ref, acc_ref, var * 2 * tq, 2 * tq)
        return carry

    lax.fori_loop(0, S // tk, body, 0)

    lane = lax.broadcasted_iota(jnp.int32, (tq, LANES), 1)
    lo = lane < HEAD_DIM

    def head_out(blk):
        a = acc_ref[blk * tq:(blk + 1) * tq, :]
        return a * (1.0 / a[:, HEAD_DIM:HEAD_DIM + 1])

    o01 = jnp.where(lo, head_out(0), pltpu.roll(head_out(2), HEAD_DIM, 1))
    o23 = jnp.where(lo, head_out(1), pltpu.roll(head_out(3), HEAD_DIM, 1))
    o_ref[0, :, 0:128] = o01.astype(BF16)
    o_ref[0, :, 128:256] = o23.astype(BF16)


def _flash_a_bounded_kernel(qt_ref, k_ref, vt_ref, o_ref, qt_scr, acc_ref, *, tq, big, n_grp, unroll):
    qt_scr[:, 0:tq] = qt_ref[0, 0]
    qt_scr[:, tq:2 * tq] = qt_ref[0, 1]
    acc_ref[...] = jnp.zeros(acc_ref.shape, F32)

    def body(c, carry):
        ks = pl.ds(pl.multiple_of(c * big, big), big)
        for var in range(2):
            st = jnp.dot(k_ref[0, 0, var, ks, :], qt_scr[...], preferred_element_type=F32)
            acc_ref[var] += jnp.dot(vt_ref[0, 0, :, ks], jnp.exp2(st).astype(BF16),
                                    preferred_element_type=F32)
        return carry

    lax.fori_loop(0, n_grp, body, 0, unroll=unroll)

    outs = []
    for var in range(2):
        a = acc_ref[var]
        outs.append(a[0:HEAD_DIM] * (1.0 / a[HEAD_DIM:HEAD_DIM + 1]))
    for pr in range(2):
        cols = slice(pr * tq, (pr + 1) * tq)
        pair_t = jnp.concatenate([outs[0][:, cols], outs[1][:, cols]], axis=0)
        o_ref[0, :, 128 * pr:128 * (pr + 1)] = pair_t.T.astype(BF16)


def _kv_spec(block, index_map, single_buffer):
    if single_buffer:
        return pl.BlockSpec(block, index_map, pipeline_mode=pl.Buffered(1))
    return pl.BlockSpec(block, index_map)


def _flash_a(qa_t, ka, va_t, tq, tk, bounded):
    B, _, _, S = qa_t.shape
    single = S * LANES * 2 * 3 * 2 > 8 * 1024 * 1024
    if bounded:
        grp = max(1, min(KV_GROUP * tk // tq, S // tk))
        n_grp = S // (tk * grp)
        kern = functools.partial(_flash_a_bounded_kernel, tq=tq, big=tk * grp, n_grp=n_grp,
                                 unroll=min(KV_UNROLL, n_grp))
        scratch = [pltpu.VMEM((LANES, 2 * tq), BF16), pltpu.VMEM((2, VA_ROWS, 2 * tq), F32)]
    else:
        kern = functools.partial(_flash_a_online_kernel, tq=tq, tk=tk, S=S)
        scratch = [pltpu.VMEM((2 * tq, LANES), BF16), pltpu.VMEM((S, LANES), BF16),
                   pltpu.VMEM((4 * tq, LANES), F32), pltpu.VMEM((4 * tq, LANES), F32)]
    return pl.pallas_call(
        kern,
        grid=(B, A_KV_HEADS, S // tq),
        in_specs=[
            pl.BlockSpec((1, 2, LANES, tq), lambda b, g, i: (b, g, 0, i)),
            _kv_spec((1, 1, 2, S, LANES), lambda b, g, i: (b, g, 0, 0, 0), single),
            _kv_spec((1, 1, VA_ROWS, S), lambda b, g, i: (b, g, 0, 0), single),
        ],
        out_specs=pl.BlockSpec((1, tq, 256), lambda b, g, i: (b, i, g)),
        out_shape=jax.ShapeDtypeStruct((B, S, 512), BF16),
        scratch_shapes=scratch,
        compiler_params=pltpu.CompilerParams(
            dimension_semantics=("parallel", "parallel", "arbitrary"),
            vmem_limit_bytes=VMEM_LIMIT_BYTES),
        name="flash_a_bounded" if bounded else "flash_a_online",
    )(qa_t, ka, va_t)


def _flash_b_lambda(lq1_ref, lk1_ref, lq2_ref, lk2_ref, lam_init):
    return (jnp.exp(jnp.sum(lq1_ref[...] * lk1_ref[...], keepdims=True))
            - jnp.exp(jnp.sum(lq2_ref[...] * lk2_ref[...], keepdims=True)) + lam_init)


def _flash_b_online_kernel(qt_ref, k_ref, vt_ref, tab_ref, lq1_ref, lk1_ref, lq2_ref, lk2_ref, sub_ref,
                           o_ref, q_scr, v_scr, rel_ref, m_ref, acc_ref, *, tq, tk, S, lam_init):
    lane = lax.broadcasted_iota(jnp.int32, (tq, LANES), 1)
    lo = lane < HEAD_DIM
    slab = _to_token_major(qt_ref[0, 0])
    zero = jnp.zeros_like(slab)
    q_scr[0] = jnp.where(lo, slab, zero)
    q_scr[1] = jnp.where(lo, zero, slab)

    @pl.when(pl.program_id(2) == 0)
    def _():
        pad = jnp.zeros((LANES - BF16_ROWS, tk), BF16)
        for c in range(S // tk):
            cols = slice(c * tk, (c + 1) * tk)
            v_scr[cols, 0:B_V_DIM] = _to_token_major(vt_ref[0, 0, 0:B_V_DIM, cols])
            ones_blk = jnp.concatenate([vt_ref[0, 0, B_V_DIM:VB_ROWS, cols], pad], axis=0)
            v_scr[cols, B_V_DIM:2 * LANES] = _to_token_major(ones_blk)

    m_ref[...] = jnp.full(m_ref.shape, NEG_BIG, F32)
    acc_ref[...] = jnp.zeros(acc_ref.shape, F32)
    rel_ref[...] = (lax.broadcasted_iota(jnp.int32, (tq, tk), 0)
                    - lax.broadcasted_iota(jnp.int32, (tq, tk), 1)).astype(F32)
    neg_slope = -tab_ref[0, 6:7, 0:1]
    q0 = pl.program_id(2) * tq

    def body(c, carry):
        k0 = pl.multiple_of(c * tk, tk)
        ks = pl.ds(k0, tk)
        v = v_scr[ks, :]
        bias = jnp.abs(rel_ref[...] + (q0 - k0).astype(F32)) * neg_slope
        for mp in range(2):
            s = _nt_dot(q_scr[mp], k_ref[0, 0, mp, ks, :]) + bias
            _online_update(s, v, m_ref, acc_ref, mp * tq, tq)
        return carry

    lax.fori_loop(0, S // tk, body, 0)

    lam = _flash_b_lambda(lq1_ref, lk1_ref, lq2_ref, lk2_ref, lam_init)
    a1 = acc_ref[0:tq, :]
    a2 = acc_ref[tq:2 * tq, :]
    o1 = a1[:, 0:B_V_DIM] * (1.0 / a1[:, B_V_DIM:B_V_DIM + 1])
    o2 = a2[:, 0:B_V_DIM] * (1.0 / a2[:, B_V_DIM:B_V_DIM + 1])
    o = o1 - lam * o2
    o_ref[0] = (_rms_rows(o, sub_ref[...]) * (1.0 - lam_init)).astype(BF16)


def _flash_b_bounded_kernel(qt_ref, k_ref, vt_ref, tab_ref, aug_ref, ahead_ref, lq1_ref, lk1_ref,
                            lq2_ref, lk2_ref, subt_ref, o_ref, qt_scr, acc_ref, *, tq, tk, grp, n_grp,
                            n_q, unroll, lam_init):
    big = grp * tk
    neg_2c = -2.0 * tab_ref[0, 6:7, 0:1]
    lam = _flash_b_lambda(lq1_ref, lk1_ref, lq2_ref, lk2_ref, lam_init)
    static = n_grp == 1
    zpad = jnp.zeros((HEAD_DIM - BF16_ROWS, tq), BF16)

    for qq in range(n_q):
        qi = qq if static else pl.program_id(2) * n_q + qq
        cols = slice(qq * tq, (qq + 1) * tq)
        t = qi * tq + lax.broadcasted_iota(jnp.int32, (BF16_ROWS, tq), 1)
        pos_hi = (t >> 7).astype(F32)
        pos_lo = (t & 127).astype(F32)
        for mp in range(2):
            rep = lambda r: jnp.concatenate([aug_ref[0, mp, r]] * (tq // LANES), axis=1)
            aug = rep(0) + pos_hi * rep(1) + pos_lo * rep(2)
            for side in range(2):
                rows = jnp.concatenate([(aug if side == 0 else -aug).astype(BF16), zpad], axis=0)
                if mp == 0:
                    qt_scr[qq, mp, side] = jnp.concatenate([qt_ref[0, 0, 0:HEAD_DIM, cols], rows], axis=0)
                else:
                    qt_scr[qq, mp, side] = jnp.concatenate([rows, qt_ref[0, 0, HEAD_DIM:LANES, cols]], axis=0)
    acc_ref[...] = jnp.zeros(acc_ref.shape, F32)

    if static:
        corr = ahead_ref[big:big + tk, :] * neg_2c
        for c in range(grp):
            ks = slice(c * tk, (c + 1) * tk)
            for mp in range(2):
                slots = jnp.concatenate([qt_scr[qq, mp, int(c > qq)] for qq in range(n_q)], axis=1)
                st = jnp.dot(k_ref[0, 0, mp, ks, :], slots, preferred_element_type=F32)
                parts = [st[:, qq * tq:(qq + 1) * tq] + corr if qq == c else st[:, qq * tq:(qq + 1) * tq]
                         for qq in range(n_q)]
                p = jnp.exp2(jnp.concatenate(parts, axis=1)).astype(BF16)
                acc_ref[mp] += jnp.dot(vt_ref[0, 0, :, ks], p, preferred_element_type=F32)
    else:
        qi = pl.program_id(2)
        gd = qi // grp
        r = qi - gd * grp

        def attend(ks, side, corr):
            for mp in range(2):
                st = jnp.dot(k_ref[0, 0, mp, ks, :], qt_scr[0, mp, side], preferred_element_type=F32)
                if corr is not None:
                    st = st + corr
                acc_ref[mp] += jnp.dot(vt_ref[0, 0, :, ks], jnp.exp2(st).astype(BF16),
                                       preferred_element_type=F32)

        w0 = pl.multiple_of(big - r * tk, tk)
        attend(pl.ds(pl.multiple_of(gd * big, big), big), 0, ahead_ref[pl.ds(w0, big), :] * neg_2c)

        def body(j, carry):
            g = gd + j
            g = jnp.where(g >= n_grp, g - n_grp, g)
            attend(pl.ds(pl.multiple_of(g * big, big), big), (g > gd).astype(jnp.int32), None)
            return carry
        lax.fori_loop(1, n_grp, body, 0, unroll=unroll)

    for qq in range(n_q):
        a1 = acc_ref[0, :, qq * tq:(qq + 1) * tq]
        a2 = acc_ref[1, :, qq * tq:(qq + 1) * tq]
        o1 = a1[0:B_V_DIM] * (1.0 / a1[B_V_DIM:B_V_DIM + 1])
        o2 = a2[0:B_V_DIM] * (1.0 / a2[B_V_DIM:B_V_DIM + 1])
        o = _rms_cols(o1 - lam * o2, subt_ref[...]) * (1.0 - lam_init)
        o_ref[0, qq * tq:(qq + 1) * tq, :] = o.T.astype(BF16)


def _flash_b(qb_t, kb, vb_t, tab, aug_t, lq1, lk1, lq2, lk2, subln, lam_init, tq, tk, bounded):
    B, _, _, S = qb_t.shape
    assert tq == tk
    single = S * LANES * 2 * 4 * 2 > 8 * 1024 * 1024
    n_tiles = S // tq
    vec = lambda n: pl.BlockSpec((1, n), lambda b, h, i: (0, 0))
    if bounded:
        grp = S // tk if S // tk <= SHORT_SEQ_CHUNKS else KV_GROUP
        n_grp = S // (tk * grp)
        n_q = n_tiles if n_grp == 1 else 1
        big = grp * tk
        ahead = np.maximum(np.arange(-big, big)[:, None] - np.arange(tq)[None, :], 0).astype(np.float32)
        sub_t = jnp.broadcast_to(subln.reshape(B_V_DIM, 1), (B_V_DIM, tq))
        kern = functools.partial(_flash_b_bounded_kernel, tq=tq, tk=tk, grp=grp, n_grp=n_grp, n_q=n_q,
                                 unroll=max(n_grp - 1, 1), lam_init=lam_init)
        extra_in = [jnp.asarray(ahead)]
        extra_specs = [
            pl.BlockSpec((1, 2, 3, BF16_ROWS, LANES), lambda b, h, i: (h, 0, 0, 0, 0)),
            pl.BlockSpec((2 * big, tq), lambda b, h, i: (0, 0), pipeline_mode=pl.Buffered(1)),
        ]
        sub_in, sub_spec = sub_t, pl.BlockSpec((B_V_DIM, tq), lambda b, h, i: (0, 0))
        scratch = [pltpu.VMEM((n_q, 2, 2, LANES, tq), BF16), pltpu.VMEM((2, VB_ROWS, n_q * tq), F32)]
        args = (qb_t, kb, vb_t, tab, aug_t, *extra_in, lq1, lk1, lq2, lk2, sub_in)
    else:
        n_q = 1
        kern = functools.partial(_flash_b_online_kernel, tq=tq, tk=tk, S=S, lam_init=lam_init)
        extra_specs = []
        sub_in, sub_spec = subln, vec(B_V_DIM)
        scratch = [pltpu.VMEM((2, tq, LANES), BF16), pltpu.VMEM((S, 2 * LANES), BF16),
                   pltpu.VMEM((tq, tk), F32), pltpu.VMEM((2 * tq, LANES), F32),
                   pltpu.VMEM((2 * tq, 2 * LANES), F32)]
        args = (qb_t, kb, vb_t, tab, lq1, lk1, lq2, lk2, sub_in)
    return pl.pallas_call(
        kern,
        grid=(B, B_HEADS, n_tiles // n_q),
        in_specs=[
            pl.BlockSpec((1, 1, LANES, n_q * tq), lambda b, h, i: (b, h, 0, i)),
            _kv_spec((1, 1, 2, S, LANES), lambda b, h, i: (b, h, 0, 0, 0), single),
            _kv_spec((1, 1, VB_ROWS, S), lambda b, h, i: (b, h, 0, 0), single),
            pl.BlockSpec((1, 8, LANES), lambda b, h, i: (h, 0, 0)),
            *extra_specs,
            vec(HEAD_DIM), vec(HEAD_DIM), vec(HEAD_DIM), vec(HEAD_DIM), sub_spec,
        ],
        out_specs=pl.BlockSpec((1, n_q * tq, LANES), lambda b, h, i: (b, i, h)),
        out_shape=jax.ShapeDtypeStruct((B, S, 512), BF16),
        scratch_shapes=scratch,
        compiler_params=pltpu.CompilerParams(
            dimension_semantics=("parallel", "parallel", "arbitrary"),
            vmem_limit_bytes=VMEM_LIMIT_BYTES),
        name="flash_b_bounded" if bounded else "flash_b_online",
    )(*args)


def _ple_tail(h1, p, npl, wg_ref, wp_ref):
    rn = _rms_rows(h1, npl).astype(BF16)
    gate = jax.nn.sigmoid(jnp.dot(rn, wg_ref[...], preferred_element_type=F32))
    return h1 + gate * jnp.dot(p.astype(BF16), wp_ref[...], preferred_element_type=F32)


def _out_ple_kernel(h_ref, oa_ref, ob_ref, g_ref, p_ref, wo_ref, npl_ref, wg_ref, wp_ref, out_ref):
    g = g_ref[0]
    y = jnp.concatenate([oa_ref[0] * g[:, 0:512], ob_ref[0] * g[:, 512:1024]], axis=1)
    h1 = h_ref[0] + jnp.dot(y, wo_ref[...], preferred_element_type=F32)
    out_ref[0] = _ple_tail(h1, p_ref[0, 0], npl_ref[...], wg_ref, wp_ref)


def _out_ple(h, oa, ob, gates, p, layer, wo, npl, wg, wp, tm):
    B, S, _ = h.shape
    tok = lambda b, i: (b, i, 0)
    const2 = lambda shp: pl.BlockSpec(shp, lambda b, i: (0, 0))
    return pl.pallas_call(
        _out_ple_kernel,
        grid=(B, S // tm),
        in_specs=[
            pl.BlockSpec((1, tm, D_MODEL), tok),
            pl.BlockSpec((1, tm, 512), tok),
            pl.BlockSpec((1, tm, 512), tok),
            pl.BlockSpec((1, tm, 1024), tok),
            pl.BlockSpec((1, 1, tm, PLE_DIM), lambda b, i: (layer, b, i, 0)),
            const2((1024, D_MODEL)), const2((1, D_MODEL)), const2((D_MODEL, D_MODEL)),
            const2((PLE_DIM, D_MODEL)),
        ],
        out_specs=pl.BlockSpec((1, tm, D_MODEL), tok),
        out_shape=jax.ShapeDtypeStruct((B, S, D_MODEL), F32),
        compiler_params=pltpu.CompilerParams(
            dimension_semantics=("parallel", "parallel"), vmem_limit_bytes=VMEM_LIMIT_BYTES),
        name="out_ple",
    )(h, oa, ob, gates, p, wo, npl, wg, wp)


def _layer_c_kernel(h_ref, hp_ref, hn_ref, p_ref, nm_ref, wi_ref, wgrp_ref, sc_ref, wo_ref,
                    npl_ref, wg_ref, wp_ref, out_ref, *, tm, S):
    i = pl.program_id(1)
    nt = pl.num_programs(1)
    h = h_ref[0]
    hp = hp_ref[0] * (i > 0).astype(F32)
    hn = hn_ref[0] * (i < nt - 1).astype(F32)
    n = tm + 2 * POOL_HALO
    h_ext = jnp.concatenate([hp, h, hn], axis=0)
    xn = _rms_rows(h_ext, nm_ref[...]).astype(BF16)
    ug = jnp.dot(xn, wi_ref[...], preferred_element_type=F32)
    u_ext = ug[:, 0:1024]
    g = ug[POOL_HALO:POOL_HALO + tm, 1024:2048]

    t = i * tm + lax.broadcasted_iota(jnp.int32, (tm, C_GRP), 0)
    mixed = []
    for gi, w in enumerate(POOL_WINDOWS):
        u = u_ext[:, C_GRP * gi:C_GRP * (gi + 1)]
        win = u + pltpu.roll(u, 1, 0)
        half = 1
        while 2 * half < w:
            win = pltpu.roll(win, half, 0) + pltpu.roll(win, n - half, 0)
            half *= 2
        cnt = (jnp.minimum(t + w // 2, S) - jnp.maximum(t - w // 2, 0)).astype(F32)
        pooled = win[POOL_HALO:POOL_HALO + tm] / cnt - u[POOL_HALO:POOL_HALO + tm]
        mixed.append(jnp.dot(pooled.astype(BF16), wgrp_ref[gi], preferred_element_type=F32))
    mixed = jnp.concatenate(mixed, axis=1)
    y = ((mixed * sc_ref[...]) * (g * jax.nn.sigmoid(g))).astype(BF16)
    h1 = h + jnp.dot(y, wo_ref[...], preferred_element_type=F32)
    out_ref[0] = _ple_tail(h1, p_ref[0, 0], npl_ref[...], wg_ref, wp_ref)


def _layer_c(h, p, layer, nm, wi, wgrp, sc, wo, npl, wg, wp, tm):
    B, S, _ = h.shape
    tok = lambda b, i: (b, i, 0)
    const2 = lambda shp: pl.BlockSpec(shp, lambda b, i: (0, 0))
    per = tm // POOL_HALO
    last = S // POOL_HALO - 1
    kern = functools.partial(_layer_c_kernel, tm=tm, S=S)
    return pl.pallas_call(
        kern,
        grid=(B, S // tm),
        in_specs=[
            pl.BlockSpec((1, tm, D_MODEL), tok),
            pl.BlockSpec((1, POOL_HALO, D_MODEL), lambda b, i: (b, jnp.maximum(i * per - 1, 0), 0)),
            pl.BlockSpec((1, POOL_HALO, D_MODEL), lambda b, i: (b, jnp.minimum((i + 1) * per, last), 0)),
            pl.BlockSpec((1, 1, tm, PLE_DIM), lambda b, i: (layer, b, i, 0)),
            const2((1, D_MODEL)), const2((D_MODEL, 2048)),
            pl.BlockSpec((4, C_GRP, C_GRP), lambda b, i: (0, 0, 0)),
            const2((1, 1024)), const2((1024, D_MODEL)), const2((1, D_MODEL)),
            const2((D_MODEL, D_MODEL)), const2((PLE_DIM, D_MODEL)),
        ],
        out_specs=pl.BlockSpec((1, tm, D_MODEL), tok),
        out_shape=jax.ShapeDtypeStruct((B, S, D_MODEL), F32),
        compiler_params=pltpu.CompilerParams(
            dimension_semantics=("parallel", "parallel"), vmem_limit_bytes=VMEM_LIMIT_BYTES),
        name="layer_c",
    )(h, h, h, p, nm, wi, wgrp, sc, wo, npl, wg, wp)


def _score_bound(gq, gk):
    slack = 1.02
    return (HEAD_DIM * SM_SCALE * LOG2E * slack) * jnp.max(jnp.abs(gq)) * jnp.max(jnp.abs(gk))


def _alibi_tables():
    kaug = np.zeros((16, LANES), np.float32)
    qtab = np.zeros((B_HEADS, 8, LANES), np.float32)
    aug_t = np.zeros((B_HEADS, 2, 3, BF16_ROWS, LANES), np.float32)
    aug_t[:, :, 1, 0:3, :] = -128.0
    aug_t[:, :, 2, 3:6, :] = -1.0
    for mp in range(2):
        base = HEAD_DIM if mp == 0 else 0
        kaug[8 + mp, base + 6:base + 9] = 128.0
        kaug[10 + mp, base + 9:base + 12] = 1.0
    for h, slope in enumerate(ALIBI_SLOPES):
        c = np.float64(slope) * LOG2E
        pieces, rest = [], c
        for _ in range(3):
            piece = np.float64(np.float32(rest).astype(BF16))
            pieces.append(piece)
            rest -= piece
        for mp in range(2):
            base = HEAD_DIM if mp == 0 else 0
            kaug[2 * h + mp, base:base + 6] = pieces + pieces
        aug_t[h, :, 0, 6:12, :] = np.asarray(pieces + pieces)[:, None]
        qtab[h, 6, :] = np.float32(c)
    return jnp.asarray(kaug), jnp.asarray(qtab), jnp.asarray(aug_t)


def _trunk(x, p, prm, tq_a, tq_b, tk, tm):
    row = lambda v: v.astype(F32).reshape(1, -1)
    col = lambda v, s: jnp.broadcast_to((v.astype(F32) * s).reshape(-1, 1), (v.shape[0], tm))
    lam_init = 0.8 - 0.6 * math.exp(-0.3 * 0)
    gq_scale = SM_SCALE * LOG2E
    gka = jnp.tile(prm["kn_a"][0].astype(F32), 2).reshape(1, -1)
    gkb = jnp.tile(prm["kn_b"][0].astype(F32), 8).reshape(1, -1)
    qa_t, ka, va_t, qb_t, kb, vb_t, gates = _proj_ab(
        x, prm["w_ab_s"], prm["w_ab_t"], row(prm["norm_mix"][0]), col(prm["qn_a"][0], gq_scale),
        col(prm["qn_b"][0], gq_scale), gka, gkb, prm["bd"], prm["kaug"], tm)
    oa = lax.cond(
        _score_bound(prm["qn_a"][0], prm["kn_a"][0]) <= BOUNDED_SCORE_LIMIT,
        lambda q, k, v: _flash_a(q, k, v, tq_a, tk, True),
        lambda q, k, v: _flash_a(q, k, v, tq_a, tk, False),
        qa_t, ka, va_t)
    lam_vecs = (row(prm["lam_q1"][0]), row(prm["lam_k1"][0]), row(prm["lam_q2"][0]),
                row(prm["lam_k2"][0]), row(prm["subln_b"][0]))
    ob = lax.cond(
        _score_bound(prm["qn_b"][0], prm["kn_b"][0]) <= BOUNDED_SCORE_LIMIT,
        lambda q, k, v: _flash_b(q, k, v, prm["qtab"], prm["aug_t"], *lam_vecs, lam_init, tq_b, tk, True),
        lambda q, k, v: _flash_b(q, k, v, prm["qtab"], prm["aug_t"], *lam_vecs, lam_init, tq_b, tk, False),
        qb_t, kb, vb_t)
    h = _out_ple(x, oa, ob, gates, p, 0, prm["w_out_ab"][0], row(prm["norm_ple"][0]),
                 prm["w_ple_gate"][0], prm["w_ple_proj"][0], tm)
    h = _layer_c(h, p, 1, row(prm["norm_mix"][1]), prm["w_in_c"][0], prm["w_grp_c"][0],
                 row(prm["scale_c"][0]), prm["w_out_c"][0], row(prm["norm_ple"][1]),
                 prm["w_ple_gate"][1], prm["w_ple_proj"][1], tm)
    return h


def _prepare(norm_mix, w_in_ab, qn_a, kn_a, qn_b, kn_b, lam_q1, lam_k1, lam_q2, lam_k2, subln_b,
             w_out_ab, w_in_c, w_grp_c, scale_c, w_out_c, norm_ple, w_ple_gate, w_ple_proj):
    blk = np.arange(2 * MXU_DIM) // HEAD_DIM
    kaug, qtab, aug_t = _alibi_tables()
    w_ab = w_in_ab[0].astype(BF16)
    cols = lambda a, b: w_ab[:, a:b]
    return dict(
        norm_mix=norm_mix, qn_a=qn_a, kn_a=kn_a, qn_b=qn_b, kn_b=kn_b,
        lam_q1=lam_q1, lam_k1=lam_k1, lam_q2=lam_q2, lam_k2=lam_k2, subln_b=subln_b,
        scale_c=scale_c, norm_ple=norm_ple,
        w_ab_s=jnp.concatenate([cols(_KA, _VA), cols(_KB, _VB), cols(_GA, _QB), cols(_GB, _AB_END)], axis=1),
        w_ab_t=jnp.concatenate([cols(_QA, _KA), cols(_QB, _KB), cols(_VA, _GA), cols(_VB, _GB)], axis=1).T,
        w_out_ab=w_out_ab.astype(BF16), w_in_c=w_in_c.astype(BF16),
        w_grp_c=w_grp_c.astype(BF16), w_out_c=w_out_c.astype(BF16),
        w_ple_gate=w_ple_gate.astype(BF16), w_ple_proj=w_ple_proj.astype(BF16),
        bd=jnp.asarray(blk[:, None] == blk[None, :], BF16),
        kaug=kaug, qtab=qtab, aug_t=aug_t,
    )


def kernel(x_prompt, x_sample, p_prompt, p_sample, norm_mix, w_in_ab, qn_a, kn_a, qn_b, kn_b,
           lam_q1, lam_k1, lam_q2, lam_k2, subln_b, w_out_ab, w_in_c, w_grp_c, scale_c, w_out_c,
           norm_ple, w_ple_gate, w_ple_proj):
    prm = _prepare(norm_mix, w_in_ab, qn_a, kn_a, qn_b, kn_b, lam_q1, lam_k1, lam_q2, lam_k2, subln_b,
                   w_out_ab, w_in_c, w_grp_c, scale_c, w_out_c, norm_ple, w_ple_gate, w_ple_proj)
    y_prompt = _trunk(x_prompt, p_prompt, prm, tq_a=512, tq_b=512, tk=512, tm=1024)
    y_sample = _trunk(x_sample, p_sample, prm, tq_a=1024, tq_b=512, tk=512, tm=1024)
    return (y_prompt, y_sample)
```

```python
import functools
import math

import jax
import jax.numpy as jnp
import numpy as np
from jax import lax
from jax.experimental import pallas as pl
from jax.experimental.pallas import tpu as pltpu

F32 = jnp.float32
BF16 = jnp.bfloat16

D_MODEL = 1024
PLE_DIM = 256
GRID_W = 64
HEAD_DIM = 64
EPS = 1e-6
ROPE_THETA = 10000.0
A_HEADS = 8
A_KV_HEADS = 2
B_HEADS = 4
B_V_DIM = 2 * HEAD_DIM
ALIBI_SLOPES = tuple(2.0 ** (-8.0 * (h + 1) / B_HEADS) for h in range(B_HEADS))
POOL_WINDOWS = (2, 4, 8, 16)
C_GRP = 256
POOL_HALO = 8
LOG2E = math.log2(math.e)
SM_SCALE = HEAD_DIM ** -0.5

LANES = 128
BF16_ROWS = 16
MXU_DIM = 256
VMEM_LIMIT_BYTES = 56 * 1024 * 1024

VA_ROWS = LANES
VB_ROWS = B_V_DIM + BF16_ROWS

_QA, _KA, _VA, _GA, _QB, _KB, _VB, _GB, _AB_END = 0, 512, 640, 768, 1280, 1792, 2304, 2816, 3328
_S_KB, _S_G, _S_END = 128, 640, 1664
_T_QB, _T_VA, _T_VB, _T_END = 512, 1024, 1152, 1664

NEG_BIG = -1e30
BOUNDED_SCORE_LIMIT = 60.0
KV_GROUP = 2
KV_UNROLL = 8
SHORT_SEQ_CHUNKS = 4


def _rms_rows(x, g):
    ms = jnp.mean(x * x, axis=-1, keepdims=True)
    return x * lax.rsqrt(ms + EPS) * g


def _rms_cols(x, g):
    ms = jnp.mean(x * x, axis=0, keepdims=True)
    return x * lax.rsqrt(ms + EPS) * g


def _nt_dot(a, b):
    return lax.dot_general(a, b, (((1,), (1,)), ((), ())), preferred_element_type=F32)


def _to_token_major(xt):
    return xt.astype(F32).T.astype(BF16)


def _proj_ab_kernel(x_ref, cos_ref, sa_ref, sb_ref, cost_ref, sint_ref, nm_ref, ws_ref, wt_ref,
                    gqat_ref, gqbt_ref, gka_ref, gkb_ref, bd_ref, kaug_ref,
                    qa_ref, ka_ref, va_ref, qb_ref, kb_ref, vb_ref, g_ref):
    x = x_ref[0]
    xn = _rms_rows(x, nm_ref[...]).astype(BF16)
    tm = x.shape[0]

    def seg_t(a, b):
        return _nt_dot(wt_ref[a:b, :], xn)

    cos_t = cost_ref[...]
    sin_t = sint_ref[...]
    z = seg_t(0, _T_QB)
    for h in range(A_HEADS):
        zh = _rms_cols(z[64 * h:64 * (h + 1)], gqat_ref[...])
        sw = jnp.concatenate([zh[16:32], zh[0:16], zh[48:64], zh[32:48]], axis=0)
        r0 = 64 * (h % 2)
        qa_ref[0, h // 2, r0:r0 + 64, :] = (zh * cos_t + sw * sin_t).astype(BF16)
    z = seg_t(_T_QB, _T_VA)
    for h in range(2 * B_HEADS):
        r0 = 64 * (h % 2)
        qb_ref[0, h // 2, r0:r0 + 64, :] = _rms_cols(z[64 * h:64 * (h + 1)], gqbt_ref[...]).astype(BF16)
    ones_rows = (lax.broadcasted_iota(jnp.int32, (BF16_ROWS, tm), 0) == 0).astype(BF16)
    z = seg_t(_T_VA, _T_END)
    for kv in range(A_KV_HEADS):
        va_ref[0, kv, 0:HEAD_DIM, :] = z[64 * kv:64 * (kv + 1)].astype(BF16)
        va_ref[0, kv, HEAD_DIM:HEAD_DIM + BF16_ROWS, :] = ones_rows
        va_ref[0, kv, HEAD_DIM + BF16_ROWS:VA_ROWS, :] = jnp.zeros((VA_ROWS - HEAD_DIM - BF16_ROWS, tm), BF16)
    for hd in range(B_HEADS):
        r0 = _T_VB - _T_VA + B_V_DIM * hd
        vb_ref[0, hd, 0:B_V_DIM, :] = z[r0:r0 + B_V_DIM].astype(BF16)
        vb_ref[0, hd, B_V_DIM:VB_ROWS, :] = ones_rows

    def seg(a, b):
        return jnp.dot(xn, ws_ref[:, a:b], preferred_element_type=F32)

    def head_norm(z, gain):
        n = z.shape[1]
        ss = jnp.dot((z * z).astype(BF16), bd_ref[0:n, 0:n], preferred_element_type=F32)
        return z * lax.rsqrt(ss * (1.0 / HEAD_DIM) + EPS) * gain

    cos = cos_ref[...]
    sin_a = sa_ref[...]
    sin_b = sb_ref[...]

    def rope(z):
        return (z * cos + pltpu.roll(z, LANES - 16, 1) * sin_a + pltpu.roll(z, 16, 1) * sin_b)

    lane = lax.broadcasted_iota(jnp.int32, (tm, LANES), 1)
    lo = lane < HEAD_DIM
    t = pl.program_id(1) * tm + lax.broadcasted_iota(jnp.int32, (tm, LANES), 0)
    pos_hi = (t >> 7).astype(F32)
    pos_lo = (t & 127).astype(F32)

    p_k = seg(0, _S_G)
    k = rope(head_norm(p_k[:, 0:_S_KB], gka_ref[...]))
    k_sw = pltpu.roll(k, HEAD_DIM, 1)
    zero = jnp.zeros_like(k)
    ka_ref[0, 0, 0] = jnp.where(lo, k, zero).astype(BF16)
    ka_ref[0, 0, 1] = jnp.where(lo, zero, k_sw).astype(BF16)
    ka_ref[0, 1, 0] = jnp.where(lo, k_sw, zero).astype(BF16)
    ka_ref[0, 1, 1] = jnp.where(lo, zero, k).astype(BF16)

    z = head_norm(p_k[:, _S_KB:_S_G], gkb_ref[...])
    for hd in range(B_HEADS):
        kh = z[:, 128 * hd:128 * (hd + 1)]
        for mp in range(2):
            aug = (kaug_ref[2 * hd + mp:2 * hd + mp + 1, :] + pos_hi * kaug_ref[8 + mp:9 + mp, :]
                   + pos_lo * kaug_ref[10 + mp:11 + mp, :])
            keep = lo if mp == 0 else jnp.logical_not(lo)
            kb_ref[0, hd, mp] = jnp.where(keep, kh, aug).astype(BF16)

    g = seg(_S_G, _S_END)
    g_ref[0] = (g * jax.nn.sigmoid(g)).astype(BF16)


def _rope_tables(S):
    rows = S // GRID_W
    row = jnp.repeat(jnp.arange(rows), GRID_W).astype(F32)
    col = jnp.tile(jnp.arange(GRID_W), rows).astype(F32)
    half = HEAD_DIM // 2
    inv = ROPE_THETA ** (-jnp.arange(0, half, 2, dtype=F32) / half)
    ar = row[:, None] * inv
    ac = col[:, None] * inv
    ang = jnp.concatenate([ar, ar, ac, ac], axis=-1)
    cos, sin = jnp.cos(ang), jnp.sin(ang)
    first = (np.arange(HEAD_DIM) % 32) < 16
    sin_a = jnp.where(first, -sin, 0.0)
    sin_b = jnp.where(first, 0.0, sin)
    two = lambda t: jnp.concatenate([t, t], axis=-1)
    return two(cos), two(sin_a), two(sin_b), cos.T, jnp.where(first, -sin, sin).T


def _proj_ab(x, ws, wt, nm, gqat, gqbt, gka, gkb, bd, kaug, tm):
    B, S, _ = x.shape
    cos, sin_a, sin_b, cos_t, sin_t = _rope_tables(S)
    tok = lambda b, i: (b, i, 0)
    tab = pl.BlockSpec((tm, LANES), lambda b, i: (i, 0))
    tab_t = pl.BlockSpec((HEAD_DIM, tm), lambda b, i: (0, i))
    const2 = lambda shp: pl.BlockSpec(shp, lambda b, i: (0, 0))
    chan = lambda n, r: pl.BlockSpec((1, n, r, tm), lambda b, i: (b, 0, 0, i))
    out_shape = (
        jax.ShapeDtypeStruct((B, 4, LANES, S), BF16),
        jax.ShapeDtypeStruct((B, 2, 2, S, LANES), BF16),
        jax.ShapeDtypeStruct((B, 2, VA_ROWS, S), BF16),
        jax.ShapeDtypeStruct((B, 4, LANES, S), BF16),
        jax.ShapeDtypeStruct((B, 4, 2, S, LANES), BF16),
        jax.ShapeDtypeStruct((B, 4, VB_ROWS, S), BF16),
        jax.ShapeDtypeStruct((B, S, 1024), BF16),
    )
    out_specs = (
        chan(4, LANES),
        pl.BlockSpec((1, 2, 2, tm, LANES), lambda b, i: (b, 0, 0, i, 0)),
        chan(2, VA_ROWS),
        chan(4, LANES),
        pl.BlockSpec((1, 4, 2, tm, LANES), lambda b, i: (b, 0, 0, i, 0)),
        chan(4, VB_ROWS),
        pl.BlockSpec((1, tm, 1024), tok),
    )
    return pl.pallas_call(
        _proj_ab_kernel,
        grid=(B, S // tm),
        in_specs=[
            pl.BlockSpec((1, tm, D_MODEL), tok), tab, tab, tab, tab_t, tab_t,
            const2((1, D_MODEL)), const2((D_MODEL, _S_END)), const2((_T_END, D_MODEL)),
            const2((HEAD_DIM, tm)), const2((HEAD_DIM, tm)), const2((1, 128)), const2((1, 512)),
            const2((2 * MXU_DIM, 2 * MXU_DIM)), const2((16, LANES)),
        ],
        out_specs=out_specs,
        out_shape=out_shape,
        compiler_params=pltpu.CompilerParams(
            dimension_semantics=("parallel", "parallel"), vmem_limit_bytes=VMEM_LIMIT_BYTES),
        name="proj_ab",
    )(x, cos, sin_a, sin_b, cos_t, sin_t, nm, ws, wt, gqat, gqbt, gka, gkb, bd, kaug)


def _online_update(s, v, m_ref, acc_ref, r0, nrows):
    rows = pl.ds(r0, nrows)
    m_old = m_ref[rows, :]
    m_new = jnp.maximum(m_old, jnp.max(s, axis=-1, keepdims=True))
    alpha = jnp.exp2(m_old - m_new)
    p = jnp.exp2(s - m_new[:, :1]).astype(BF16)
    pv = jnp.dot(p, v, preferred_element_type=F32)
    nv = v.shape[1]
    if nv != LANES:
        alpha = jnp.concatenate([alpha] * (nv // LANES), axis=1)
    acc_ref[rows, :] = acc_ref[rows, :] * alpha + pv
    m_ref[rows, :] = m_new


def _flash_a_online_kernel(qt_ref, k_ref, vt_ref, o_ref, q_scr, v_scr, m_ref, acc_ref, *, tq, tk, S):
    q_scr[0:tq, :] = _to_token_major(qt_ref[0, 0])
    q_scr[tq:2 * tq, :] = _to_token_major(qt_ref[0, 1])

    @pl.when(pl.program_id(2) == 0)
    def _():
        for c in range(S // tk):
            cols = slice(c * tk, (c + 1) * tk)
            v_scr[cols, :] = _to_token_major(vt_ref[0, 0, :, cols])

    m_ref[...] = jnp.full(m_ref.shape, NEG_BIG, F32)
    acc_ref[...] = jnp.zeros(acc_ref.shape, F32)

    def body(c, carry):
        ks = pl.ds(pl.multiple_of(c * tk, tk), tk)
        q = q_scr[...]
        v = v_scr[ks, :]
        for var in range(2):
            s = _nt_dot(q, k_ref[0, 0, var, ks, :])
            _online_update(s, v, m_ref, acc_ref, var * 2 * tq, 2 * tq)
        return carry

    lax.fori_loop(0, S // tk, body, 0)

    lane = lax.broadcasted_iota(jnp.int32, (tq, LANES), 1)
    lo = lane < HEAD_DIM

    def head_out(blk):
        a = acc_ref[blk * tq:(blk + 1) * tq, :]
        return a * (1.0 / a[:, HEAD_DIM:HEAD_DIM + 1])

    o01 = jnp.where(lo, head_out(0), pltpu.roll(head_out(2), HEAD_DIM, 1))
    o23 = jnp.where(lo, head_out(1), pltpu.roll(head_out(3), HEAD_DIM, 1))
    o_ref[0, :, 0:128] = o01.astype(BF16)
    o_ref[0, :, 128:256] = o23.astype(BF16)


def _flash_a_bounded_kernel(qt_ref, k_ref, vt_ref, o_ref, qt_scr, acc_ref, *, tq, big, n_grp, unroll):
    qt_scr[:, 0:tq] = qt_ref[0, 0]
    qt_scr[:, tq:2 * tq] = qt_ref[0, 1]
    acc_ref[...] = jnp.zeros(acc_ref.shape, F32)

    def body(c, carry):
        ks = pl.ds(pl.multiple_of(c * big, big), big)
        for var in range(2):
            st = jnp.dot(k_ref[0, 0, var, ks, :], qt_scr[...], preferred_element_type=F32)
            acc_ref[var] += jnp.dot(vt_ref[0, 0, :, ks], jnp.exp2(st).astype(BF16),
                                    preferred_element_type=F32)
        return carry

    lax.fori_loop(0, n_grp, body, 0, unroll=unroll)

    outs = []
    for var in range(2):
        a = acc_ref[var]
        outs.append(a[0:HEAD_DIM] * (1.0 / a[HEAD_DIM:HEAD_DIM + 1]))
    for pr in range(2):
        cols = slice(pr * tq, (pr + 1) * tq)
        pair_t = jnp.concatenate([outs[0][:, cols], outs[1][:, cols]], axis=0)
        o_ref[0, :, 128 * pr:128 * (pr + 1)] = pair_t.T.astype(BF16)


def _kv_spec(block, index_map, single_buffer):
    if single_buffer:
        return pl.BlockSpec(block, index_map, pipeline_mode=pl.Buffered(1))
    return pl.BlockSpec(block, index_map)


def _flash_a(qa_t, ka, va_t, tq, tk, bounded):
    B, _, _, S = qa_t.shape
    single = S * LANES * 2 * 3 * 2 > 8 * 1024 * 1024
    if bounded:
        grp = max(1, min(KV_GROUP * tk // tq, S // tk))
        n_grp = S // (tk * grp)
        kern = functools.partial(_flash_a_bounded_kernel, tq=tq, big=tk * grp, n_grp=n_grp,
                                 unroll=min(KV_UNROLL, n_grp))
        scratch = [pltpu.VMEM((LANES, 2 * tq), BF16), pltpu.VMEM((2, VA_ROWS, 2 * tq), F32)]
    else:
        kern = functools.partial(_flash_a_online_kernel, tq=tq, tk=tk, S=S)
        scratch = [pltpu.VMEM((2 * tq, LANES), BF16), pltpu.VMEM((S, LANES), BF16),
                   pltpu.VMEM((4 * tq, LANES), F32), pltpu.VMEM((4 * tq, LANES), F32)]
    return pl.pallas_call(
        kern,
        grid=(B, A_KV_HEADS, S // tq),
        in_specs=[
            pl.BlockSpec((1, 2, LANES, tq), lambda b, g, i: (b, g, 0, i)),
            _kv_spec((1, 1, 2, S, LANES), lambda b, g, i: (b, g, 0, 0, 0), single),
            _kv_spec((1, 1, VA_ROWS, S), lambda b, g, i: (b, g, 0, 0), single),
        ],
        out_specs=pl.BlockSpec((1, tq, 256), lambda b, g, i: (b, i, g)),
        out_shape=jax.ShapeDtypeStruct((B, S, 512), BF16),
        scratch_shapes=scratch,
        compiler_params=pltpu.CompilerParams(
            dimension_semantics=("parallel", "parallel", "arbitrary"),
            vmem_limit_bytes=VMEM_LIMIT_BYTES),
        name="flash_a_bounded" if bounded else "flash_a_online",
    )(qa_t, ka, va_t)


def _flash_b_lambda(lq1_ref, lk1_ref, lq2_ref, lk2_ref, lam_init):
    return (jnp.exp(jnp.sum(lq1_ref[...] * lk1_ref[...], keepdims=True))
            - jnp.exp(jnp.sum(lq2_ref[...] * lk2_ref[...], keepdims=True)) + lam_init)


def _flash_b_online_kernel(qt_ref, k_ref, vt_ref, tab_ref, lq1_ref, lk1_ref, lq2_ref, lk2_ref, sub_ref,
                           o_ref, q_scr, v_scr, rel_ref, m_ref, acc_ref, *, tq, tk, S, lam_init):
    lane = lax.broadcasted_iota(jnp.int32, (tq, LANES), 1)
    lo = lane < HEAD_DIM
    slab = _to_token_major(qt_ref[0, 0])
    zero = jnp.zeros_like(slab)
    q_scr[0] = jnp.where(lo, slab, zero)
    q_scr[1] = jnp.where(lo, zero, slab)

    @pl.when(pl.program_id(2) == 0)
    def _():
        pad = jnp.zeros((LANES - BF16_ROWS, tk), BF16)
        for c in range(S // tk):
            cols = slice(c * tk, (c + 1) * tk)
            v_scr[cols, 0:B_V_DIM] = _to_token_major(vt_ref[0, 0, 0:B_V_DIM, cols])
            ones_blk = jnp.concatenate([vt_ref[0, 0, B_V_DIM:VB_ROWS, cols], pad], axis=0)
            v_scr[cols, B_V_DIM:2 * LANES] = _to_token_major(ones_blk)

    m_ref[...] = jnp.full(m_ref.shape, NEG_BIG, F32)
    acc_ref[...] = jnp.zeros(acc_ref.shape, F32)
    rel_ref[...] = (lax.broadcasted_iota(jnp.int32, (tq, tk), 0)
                    - lax.broadcasted_iota(jnp.int32, (tq, tk), 1)).astype(F32)
    neg_slope = -tab_ref[0, 6:7, 0:1]
    q0 = pl.program_id(2) * tq

    def body(c, carry):
        k0 = pl.multiple_of(c * tk, tk)
        ks = pl.ds(k0, tk)
        v = v_scr[ks, :]
        bias = jnp.abs(rel_ref[...] + (q0 - k0).astype(F32)) * neg_slope
        for mp in range(2):
            s = _nt_dot(q_scr[mp], k_ref[0, 0, mp, ks, :]) + bias
            _online_update(s, v, m_ref, acc_ref, mp * tq, tq)
        return carry

    lax.fori_loop(0, S // tk, body, 0)

    lam = _flash_b_lambda(lq1_ref, lk1_ref, lq2_ref, lk2_ref, lam_init)
    a1 = acc_ref[0:tq, :]
    a2 = acc_ref[tq:2 * tq, :]
    o1 = a1[:, 0:B_V_DIM] * (1.0 / a1[:, B_V_DIM:B_V_DIM + 1])
    o2 = a2[:, 0:B_V_DIM] * (1.0 / a2[:, B_V_DIM:B_V_DIM + 1])
    o = o1 - lam * o2
    o_ref[0] = (_rms_rows(o, sub_ref[...]) * (1.0 - lam_init)).astype(BF16)


def _flash_b_bounded_kernel(qt_ref, k_ref, vt_ref, tab_ref, aug_ref, ahead_ref, lq1_ref, lk1_ref,
                            lq2_ref, lk2_ref, subt_ref, o_ref, qt_scr, acc_ref, *, tq, tk, grp, n_grp,
                            n_q, unroll, lam_init):
    big = grp * tk
    neg_2c = -2.0 * tab_ref[0, 6:7, 0:1]
    lam = _flash_b_lambda(lq1_ref, lk1_ref, lq2_ref, lk2_ref, lam_init)
    static = n_grp == 1
    zpad = jnp.zeros((HEAD_DIM - BF16_ROWS, tq), BF16)

    for qq in range(n_q):
        qi = qq if static else pl.program_id(2) * n_q + qq
        cols = slice(qq * tq, (qq + 1) * tq)
        t = qi * tq + lax.broadcasted_iota(jnp.int32, (BF16_ROWS, tq), 1)
        pos_hi = (t >> 7).astype(F32)
        pos_lo = (t & 127).astype(F32)
        for mp in range(2):
            rep = lambda r: jnp.concatenate([aug_ref[0, mp, r]] * (tq // LANES), axis=1)
            aug = rep(0) + pos_hi * rep(1) + pos_lo * rep(2)
            for side in range(2):
                rows = jnp.concatenate([(aug if side == 0 else -aug).astype(BF16), zpad], axis=0)
                if mp == 0:
                    qt_scr[qq, mp, side] = jnp.concatenate([qt_ref[0, 0, 0:HEAD_DIM, cols], rows], axis=0)
                else:
                    qt_scr[qq, mp, side] = jnp.concatenate([rows, qt_ref[0, 0, HEAD_DIM:LANES, cols]], axis=0)
    acc_ref[...] = jnp.zeros(acc_ref.shape, F32)

    if static:
        corr = ahead_ref[big:big + tk, :] * neg_2c
        for c in range(grp):
            ks = slice(c * tk, (c + 1) * tk)
            for mp in range(2):
                slots = jnp.concatenate([qt_scr[qq, mp, int(c > qq)] for qq in range(n_q)], axis=1)
                st = jnp.dot(k_ref[0, 0, mp, ks, :], slots, preferred_element_type=F32)
                parts = [st[:, qq * tq:(qq + 1) * tq] + corr if qq == c else st[:, qq * tq:(qq + 1) * tq]
                         for qq in range(n_q)]
                p = jnp.exp2(jnp.concatenate(parts, axis=1)).astype(BF16)
                acc_ref[mp] += jnp.dot(vt_ref[0, 0, :, ks], p, preferred_element_type=F32)
    else:
        qi = pl.program_id(2)
        gd = qi // grp
        r = qi - gd * grp

        def attend(ks, side, corr):
            for mp in range(2):
                st = jnp.dot(k_ref[0, 0, mp, ks, :], qt_scr[0, mp, side], preferred_element_type=F32)
                if corr is not None:
                    st = st + corr
                acc_ref[mp] += jnp.dot(vt_ref[0, 0, :, ks], jnp.exp2(st).astype(BF16),
                                       preferred_element_type=F32)

        w0 = pl.multiple_of(big - r * tk, tk)
        attend(pl.ds(pl.multiple_of(gd * big, big), big), 0, ahead_ref[pl.ds(w0, big), :] * neg_2c)

        def body(j, carry):
            g = gd + j
            g = jnp.where(g >= n_grp, g - n_grp, g)
            attend(pl.ds(pl.multiple_of(g * big, big), big), (g > gd).astype(jnp.int32), None)
            return carry
        lax.fori_loop(1, n_grp, body, 0, unroll=unroll)

    for qq in range(n_q):
        a1 = acc_ref[0, :, qq * tq:(qq + 1) * tq]
        a2 = acc_ref[1, :, qq * tq:(qq + 1) * tq]
        o1 = a1[0:B_V_DIM] * (1.0 / a1[B_V_DIM:B_V_DIM + 1])
        o2 = a2[0:B_V_DIM] * (1.0 / a2[B_V_DIM:B_V_DIM + 1])
        o = _rms_cols(o1 - lam * o2, subt_ref[...]) * (1.0 - lam_init)
        o_ref[0, qq * tq:(qq + 1) * tq, :] = o.T.astype(BF16)


def _flash_b(qb_t, kb, vb_t, tab, aug_t, lq1, lk1, lq2, lk2, subln, lam_init, tq, tk, bounded):
    B, _, _, S = qb_t.shape
    assert tq == tk
    single = S * LANES * 2 * 4 * 2 > 8 * 1024 * 1024
    n_tiles = S // tq
    vec = lambda n: pl.BlockSpec((1, n), lambda b, h, i: (0, 0))
    if bounded:
        grp = S // tk if S // tk <= SHORT_SEQ_CHUNKS else KV_GROUP
        n_grp = S // (tk * grp)
        n_q = n_tiles if n_grp == 1 else 1
        big = grp * tk
        ahead = np.maximum(np.arange(-big, big)[:, None] - np.arange(tq)[None, :], 0).astype(np.float32)
        sub_t = jnp.broadcast_to(subln.reshape(B_V_DIM, 1), (B_V_DIM, tq))
        kern = functools.partial(_flash_b_bounded_kernel, tq=tq, tk=tk, grp=grp, n_grp=n_grp, n_q=n_q,
                                 unroll=max(n_grp - 1, 1), lam_init=lam_init)
        extra_in = [jnp.asarray(ahead)]
        extra_specs = [
            pl.BlockSpec((1, 2, 3, BF16_ROWS, LANES), lambda b, h, i: (h, 0, 0, 0, 0)),
            pl.BlockSpec((2 * big, tq), lambda b, h, i: (0, 0), pipeline_mode=pl.Buffered(1)),
        ]
        sub_in, sub_spec = sub_t, pl.BlockSpec((B_V_DIM, tq), lambda b, h, i: (0, 0))
        scratch = [pltpu.VMEM((n_q, 2, 2, LANES, tq), BF16), pltpu.VMEM((2, VB_ROWS, n_q * tq), F32)]
        args = (qb_t, kb, vb_t, tab, aug_t, *extra_in, lq1, lk1, lq2, lk2, sub_in)
    else:
        n_q = 1
        kern = functools.partial(_flash_b_online_kernel, tq=tq, tk=tk, S=S, lam_init=lam_init)
        extra_specs = []
        sub_in, sub_spec = subln, vec(B_V_DIM)
        scratch = [pltpu.VMEM((2, tq, LANES), BF16), pltpu.VMEM((S, 2 * LANES), BF16),
                   pltpu.VMEM((tq, tk), F32), pltpu.VMEM((2 * tq, LANES), F32),
                   pltpu.VMEM((2 * tq, 2 * LANES), F32)]
        args = (qb_t, kb, vb_t, tab, lq1, lk1, lq2, lk2, sub_in)
    return pl.pallas_call(
        kern,
        grid=(B, B_HEADS, n_tiles // n_q),
        in_specs=[
            pl.BlockSpec((1, 1, LANES, n_q * tq), lambda b, h, i: (b, h, 0, i)),
            _kv_spec((1, 1, 2, S, LANES), lambda b, h, i: (b, h, 0, 0, 0), single),
            _kv_spec((1, 1, VB_ROWS, S), lambda b, h, i: (b, h, 0, 0), single),
            pl.BlockSpec((1, 8, LANES), lambda b, h, i: (h, 0, 0)),
            *extra_specs,
            vec(HEAD_DIM), vec(HEAD_DIM), vec(HEAD_DIM), vec(HEAD_DIM), sub_spec,
        ],
        out_specs=pl.BlockSpec((1, n_q * tq, LANES), lambda b, h, i: (b, i, h)),
        out_shape=jax.ShapeDtypeStruct((B, S, 512), BF16),
        scratch_shapes=scratch,
        compiler_params=pltpu.CompilerParams(
            dimension_semantics=("parallel", "parallel", "arbitrary"),
            vmem_limit_bytes=VMEM_LIMIT_BYTES),
        name="flash_b_bounded" if bounded else "flash_b_online",
    )(*args)


def _ple_tail(h1, p, npl, wg_ref, wp_ref):
    rn = _rms_rows(h1, npl).astype(BF16)
    gate = jax.nn.sigmoid(jnp.dot(rn, wg_ref[...], preferred_element_type=F32))
    return h1 + gate * jnp.dot(p.astype(BF16), wp_ref[...], preferred_element_type=F32)


def _out_ple_kernel(h_ref, oa_ref, ob_ref, g_ref, p_ref, wo_ref, npl_ref, wg_ref, wp_ref, out_ref):
    g = g_ref[0]
    y = jnp.concatenate([oa_ref[0] * g[:, 0:512], ob_ref[0] * g[:, 512:1024]], axis=1)
    h1 = h_ref[0] + jnp.dot(y, wo_ref[...], preferred_element_type=F32)
    out_ref[0] = _ple_tail(h1, p_ref[0, 0], npl_ref[...], wg_ref, wp_ref)


def _out_ple(h, oa, ob, gates, p, layer, wo, npl, wg, wp, tm):
    B, S, _ = h.shape
    tok = lambda b, i: (b, i, 0)
    const2 = lambda shp: pl.BlockSpec(shp, lambda b, i: (0, 0))
    return pl.pallas_call(
        _out_ple_kernel,
        grid=(B, S // tm),
        in_specs=[
            pl.BlockSpec((1, tm, D_MODEL), tok),
            pl.BlockSpec((1, tm, 512), tok),
            pl.BlockSpec((1, tm, 512), tok),
            pl.BlockSpec((1, tm, 1024), tok),
            pl.BlockSpec((1, 1, tm, PLE_DIM), lambda b, i: (layer, b, i, 0)),
            const2((1024, D_MODEL)), const2((1, D_MODEL)), const2((D_MODEL, D_MODEL)),
            const2((PLE_DIM, D_MODEL)),
        ],
        out_specs=pl.BlockSpec((1, tm, D_MODEL), tok),
        out_shape=jax.ShapeDtypeStruct((B, S, D_MODEL), F32),
        compiler_params=pltpu.CompilerParams(
            dimension_semantics=("parallel", "parallel"), vmem_limit_bytes=VMEM_LIMIT_BYTES),
        name="out_ple",
    )(h, oa, ob, gates, p, wo, npl, wg, wp)


def _layer_c_kernel(h_ref, hp_ref, hn_ref, p_ref, nm_ref, wi_ref, wgrp_ref, sc_ref, wo_ref,
                    npl_ref, wg_ref, wp_ref, out_ref, *, tm, S):
    i = pl.program_id(1)
    nt = pl.num_programs(1)
    h = h_ref[0]
    hp = hp_ref[0] * (i > 0).astype(F32)
    hn = hn_ref[0] * (i < nt - 1).astype(F32)
    n = tm + 2 * POOL_HALO
    h_ext = jnp.concatenate([hp, h, hn], axis=0)
    xn = _rms_rows(h_ext, nm_ref[...]).astype(BF16)
    ug = jnp.dot(xn, wi_ref[...], preferred_element_type=F32)
    u_ext = ug[:, 0:1024]
    g = ug[POOL_HALO:POOL_HALO + tm, 1024:2048]

    t = i * tm + lax.broadcasted_iota(jnp.int32, (tm, C_GRP), 0)
    mixed = []
    for gi, w in enumerate(POOL_WINDOWS):
        u = u_ext[:, C_GRP * gi:C_GRP * (gi + 1)]
        win = u + pltpu.roll(u, 1, 0)
        half = 1
        while 2 * half < w:
            win = pltpu.roll(win, half, 0) + pltpu.roll(win, n - half, 0)
            half *= 2
        cnt = (jnp.minimum(t + w // 2, S) - jnp.maximum(t - w // 2, 0)).astype(F32)
        pooled = win[POOL_HALO:POOL_HALO + tm] / cnt - u[POOL_HALO:POOL_HALO + tm]
        mixed.append(jnp.dot(pooled.astype(BF16), wgrp_ref[gi], preferred_element_type=F32))
    mixed = jnp.concatenate(mixed, axis=1)
    y = ((mixed * sc_ref[...]) * (g * jax.nn.sigmoid(g))).astype(BF16)
    h1 = h + jnp.dot(y, wo_ref[...], preferred_element_type=F32)
    out_ref[0] = _ple_tail(h1, p_ref[0, 0], npl_ref[...], wg_ref, wp_ref)


def _layer_c(h, p, layer, nm, wi, wgrp, sc, wo, npl, wg, wp, tm):
    B, S, _ = h.shape
    tok = lambda b, i: (b, i, 0)
    const2 = lambda shp: pl.BlockSpec(shp, lambda b, i: (0, 0))
    per = tm // POOL_HALO
    last = S // POOL_HALO - 1
    kern = functools.partial(_layer_c_kernel, tm=tm, S=S)
    return pl.pallas_call(
        kern,
        grid=(B, S // tm),
        in_specs=[
            pl.BlockSpec((1, tm, D_MODEL), tok),
            pl.BlockSpec((1, POOL_HALO, D_MODEL), lambda b, i: (b, jnp.maximum(i * per - 1, 0), 0)),
            pl.BlockSpec((1, POOL_HALO, D_MODEL), lambda b, i: (b, jnp.minimum((i + 1) * per, last), 0)),
            pl.BlockSpec((1, 1, tm, PLE_DIM), lambda b, i: (layer, b, i, 0)),
            const2((1, D_MODEL)), const2((D_MODEL, 2048)),
            pl.BlockSpec((4, C_GRP, C_GRP), lambda b, i: (0, 0, 0)),
            const2((1, 1024)), const2((1024, D_MODEL)), const2((1, D_MODEL)),
            const2((D_MODEL, D_MODEL)), const2((PLE_DIM, D_MODEL)),
        ],
        out_specs=pl.BlockSpec((1, tm, D_MODEL), tok),
        out_shape=jax.ShapeDtypeStruct((B, S, D_MODEL), F32),
        compiler_params=pltpu.CompilerParams(
            dimension_semantics=("parallel", "parallel"), vmem_limit_bytes=VMEM_LIMIT_BYTES),
        name="layer_c",
    )(h, h, h, p, nm, wi, wgrp, sc, wo, npl, wg, wp)


def _score_bound(gq, gk):
    slack = 1.02
    return (HEAD_DIM * SM_SCALE * LOG2E * slack) * jnp.max(jnp.abs(gq)) * jnp.max(jnp.abs(gk))


def _alibi_tables():
    kaug = np.zeros((16, LANES), np.float32)
    qtab = np.zeros((B_HEADS, 8, LANES), np.float32)
    aug_t = np.zeros((B_HEADS, 2, 3, BF16_ROWS, LANES), np.float32)
    aug_t[:, :, 1, 0:3, :] = -128.0
    aug_t[:, :, 2, 3:6, :] = -1.0
    for mp in range(2):
        base = HEAD_DIM if mp == 0 else 0
        kaug[8 + mp, base + 6:base + 9] = 128.0
        kaug[10 + mp, base + 9:base + 12] = 1.0
    for h, slope in enumerate(ALIBI_SLOPES):
        c = np.float64(slope) * LOG2E
        pieces, rest = [], c
        for _ in range(3):
            piece = np.float64(np.float32(rest).astype(BF16))
            pieces.append(piece)
            rest -= piece
        for mp in range(2):
            base = HEAD_DIM if mp == 0 else 0
            kaug[2 * h + mp, base:base + 6] = pieces + pieces
        aug_t[h, :, 0, 6:12, :] = np.asarray(pieces + pieces)[:, None]
        qtab[h, 6, :] = np.float32(c)
    return jnp.asarray(kaug), jnp.asarray(qtab), jnp.asarray(aug_t)


def _trunk(x, p, prm, tq_a, tq_b, tk, tm):
    row = lambda v: v.astype(F32).reshape(1, -1)
    col = lambda v, s: jnp.broadcast_to((v.astype(F32) * s).reshape(-1, 1), (v.shape[0], tm))
    lam_init = 0.8 - 0.6 * math.exp(-0.3 * 0)
    gq_scale = SM_SCALE * LOG2E
    gka = jnp.tile(prm["kn_a"][0].astype(F32), 2).reshape(1, -1)
    gkb = jnp.tile(prm["kn_b"][0].astype(F32), 8).reshape(1, -1)
    qa_t, ka, va_t, qb_t, kb, vb_t, gates = _proj_ab(
        x, prm["w_ab_s"], prm["w_ab_t"], row(prm["norm_mix"][0]), col(prm["qn_a"][0], gq_scale),
        col(prm["qn_b"][0], gq_scale), gka, gkb, prm["bd"], prm["kaug"], tm)
    oa = lax.cond(
        _score_bound(prm["qn_a"][0], prm["kn_a"][0]) <= BOUNDED_SCORE_LIMIT,
        lambda q, k, v: _flash_a(q, k, v, tq_a, tk, True),
        lambda q, k, v: _flash_a(q, k, v, tq_a, tk, False),
        qa_t, ka, va_t)
    lam_vecs = (row(prm["lam_q1"][0]), row(prm["lam_k1"][0]), row(prm["lam_q2"][0]),
                row(prm["lam_k2"][0]), row(prm["subln_b"][0]))
    ob = lax.cond(
        _score_bound(prm["qn_b"][0], prm["kn_b"][0]) <= BOUNDED_SCORE_LIMIT,
        lambda q, k, v: _flash_b(q, k, v, prm["qtab"], prm["aug_t"], *lam_vecs, lam_init, tq_b, tk, True),
        lambda q, k, v: _flash_b(q, k, v, prm["qtab"], prm["aug_t"], *lam_vecs, lam_init, tq_b, tk, False),
        qb_t, kb, vb_t)
    h = _out_ple(x, oa, ob, gates, p, 0, prm["w_out_ab"][0], row(prm["norm_ple"][0]),
                 prm["w_ple_gate"][0], prm["w_ple_proj"][0], tm)
    h = _layer_c(h, p, 1, row(prm["norm_mix"][1]), prm["w_in_c"][0], prm["w_grp_c"][0],
                 row(prm["scale_c"][0]), prm["w_out_c"][0], row(prm["norm_ple"][1]),
                 prm["w_ple_gate"][1], prm["w_ple_proj"][1], tm)
    return h


def _prepare(norm_mix, w_in_ab, qn_a, kn_a, qn_b, kn_b, lam_q1, lam_k1, lam_q2, lam_k2, subln_b,
             w_out_ab, w_in_c, w_grp_c, scale_c, w_out_c, norm_ple, w_ple_gate, w_ple_proj):
    blk = np.arange(2 * MXU_DIM) // HEAD_DIM
    kaug, qtab, aug_t = _alibi_tables()
    w_ab = w_in_ab[0].astype(BF16)
    cols = lambda a, b: w_ab[:, a:b]
    return dict(
        norm_mix=norm_mix, qn_a=qn_a, kn_a=kn_a, qn_b=qn_b, kn_b=kn_b,
        lam_q1=lam_q1, lam_k1=lam_k1, lam_q2=lam_q2, lam_k2=lam_k2, subln_b=subln_b,
        scale_c=scale_c, norm_ple=norm_ple,
        w_ab_s=jnp.concatenate([cols(_KA, _VA), cols(_KB, _VB), cols(_GA, _QB), cols(_GB, _AB_END)], axis=1),
        w_ab_t=jnp.concatenate([cols(_QA, _KA), cols(_QB, _KB), cols(_VA, _GA), cols(_VB, _GB)], axis=1).T,
        w_out_ab=w_out_ab.astype(BF16), w_in_c=w_in_c.astype(BF16),
        w_grp_c=w_grp_c.astype(BF16), w_out_c=w_out_c.astype(BF16),
        w_ple_gate=w_ple_gate.astype(BF16), w_ple_proj=w_ple_proj.astype(BF16),
        bd=jnp.asarray(blk[:, None] == blk[None, :], BF16),
        kaug=kaug, qtab=qtab, aug_t=aug_t,
    )


def kernel(x_prompt, x_sample, p_prompt, p_sample, norm_mix, w_in_ab, qn_a, kn_a, qn_b, kn_b,
           lam_q1, lam_k1, lam_q2, lam_k2, subln_b, w_out_ab, w_in_c, w_grp_c, scale_c, w_out_c,
           norm_ple, w_ple_gate, w_ple_proj):
    prm = _prepare(norm_mix, w_in_ab, qn_a, kn_a, qn_b, kn_b, lam_q1, lam_k1, lam_q2, lam_k2, subln_b,
                   w_out_ab, w_in_c, w_grp_c, scale_c, w_out_c, norm_ple, w_ple_gate, w_ple_proj)
    y_prompt = _trunk(x_prompt, p_prompt, prm, tq_a=512, tq_b=512, tk=512, tm=1024)
    y_sample = _trunk(x_sample, p_sample, prm, tq_a=1024, tq_b=512, tk=512, tm=1024)
    return (y_prompt, y_sample)
```

```python
import functools
import math

import jax
import jax.numpy as jnp
import numpy as np
from jax import lax
from jax.experimental import pallas as pl
from jax.experimental.pallas import tpu as pltpu

F32 = jnp.float32
BF16 = jnp.bfloat16

D_MODEL = 1024
PLE_DIM = 256
GRID_W = 64
HEAD_DIM = 64
EPS = 1e-6
ROPE_THETA = 10000.0
A_HEADS = 8
A_KV_HEADS = 2
B_HEADS = 4
B_V_DIM = 2 * HEAD_DIM
ALIBI_SLOPES = tuple(2.0 ** (-8.0 * (h + 1) / B_HEADS) for h in range(B_HEADS))
POOL_WINDOWS = (2, 4, 8, 16)
C_GRP = 256
POOL_HALO = 8
LOG2E = math.log2(math.e)
SM_SCALE = HEAD_DIM ** -0.5

LANES = 128
BF16_ROWS = 16
MXU_DIM = 256
VMEM_LIMIT_BYTES = 56 * 1024 * 1024

VA_ROWS = LANES
VB_ROWS = B_V_DIM + BF16_ROWS

_QA, _KA, _VA, _GA, _QB, _KB, _VB, _GB, _AB_END = 0, 512, 640, 768, 1280, 1792, 2304, 2816, 3328
_S_G, _S_END = 512, 1536
_T_QB, _T_VA, _T_VB, _T_KA, _T_END = 512, 1024, 1152, 1664, 1792

NEG_BIG = -1e30
BOUNDED_SCORE_LIMIT = 60.0
KV_GROUP = 2
KV_UNROLL = 8
SHORT_SEQ_CHUNKS = 4


def _rms_rows(x, g):
    ms = jnp.mean(x * x, axis=-1, keepdims=True)
    return x * lax.rsqrt(ms + EPS) * g


def _rms_cols(x, g):
    ms = jnp.mean(x * x, axis=0, keepdims=True)
    return x * lax.rsqrt(ms + EPS) * g


def _nt_dot(a, b):
    return lax.dot_general(a, b, (((1,), (1,)), ((), ())), preferred_element_type=F32)


def _to_token_major(xt):
    return xt.astype(F32).T.astype(BF16)


def _proj_ab_kernel(x_ref, cost_ref, sint_ref, nm_ref, ws_ref, wt_ref,
                    gqat_ref, gqbt_ref, gkat_ref, gkb_ref, bd_ref, kaug_ref,
                    qa_ref, ka_ref, va_ref, qb_ref, kb_ref, vb_ref, g_ref):
    x = x_ref[0]
    xn = _rms_rows(x, nm_ref[...]).astype(BF16)
    tm = x.shape[0]

    def seg_t(a, b):
        return _nt_dot(wt_ref[a:b, :], xn)

    cos_t = cost_ref[...]
    sin_t = sint_ref[...]

    def norm_rope_t(zh, gain):
        zh = _rms_cols(zh, gain)
        sw = jnp.concatenate([zh[16:32], zh[0:16], zh[48:64], zh[32:48]], axis=0)
        return zh * cos_t + sw * sin_t

    z = seg_t(0, _T_QB)
    for h in range(A_HEADS):
        r0 = 64 * (h % 2)
        qa_ref[0, h // 2, r0:r0 + 64, :] = norm_rope_t(z[64 * h:64 * (h + 1)], gqat_ref[...]).astype(BF16)
    z = seg_t(_T_QB, _T_VA)
    for h in range(2 * B_HEADS):
        r0 = 64 * (h % 2)
        qb_ref[0, h // 2, r0:r0 + 64, :] = _rms_cols(z[64 * h:64 * (h + 1)], gqbt_ref[...]).astype(BF16)
    ones_rows = (lax.broadcasted_iota(jnp.int32, (BF16_ROWS, tm), 0) == 0).astype(BF16)
    z = seg_t(_T_VA, _T_KA)
    for kv in range(A_KV_HEADS):
        va_ref[0, kv, 0:HEAD_DIM, :] = z[64 * kv:64 * (kv + 1)].astype(BF16)
        va_ref[0, kv, HEAD_DIM:HEAD_DIM + BF16_ROWS, :] = ones_rows
        va_ref[0, kv, HEAD_DIM + BF16_ROWS:VA_ROWS, :] = jnp.zeros((VA_ROWS - HEAD_DIM - BF16_ROWS, tm), BF16)
    for hd in range(B_HEADS):
        r0 = _T_VB - _T_VA + B_V_DIM * hd
        vb_ref[0, hd, 0:B_V_DIM, :] = z[r0:r0 + B_V_DIM].astype(BF16)
        vb_ref[0, hd, B_V_DIM:VB_ROWS, :] = ones_rows

    z = seg_t(_T_KA, _T_END)
    k = jnp.concatenate([norm_rope_t(z[64 * kv:64 * (kv + 1)], gkat_ref[...])
                         for kv in range(A_KV_HEADS)], axis=0).T

    def seg(a, b):
        return jnp.dot(xn, ws_ref[:, a:b], preferred_element_type=F32)

    def head_norm(z, gain):
        n = z.shape[1]
        ss = jnp.dot((z * z).astype(BF16), bd_ref[0:n, 0:n], preferred_element_type=F32)
        return z * lax.rsqrt(ss * (1.0 / HEAD_DIM) + EPS) * gain

    lane = lax.broadcasted_iota(jnp.int32, (tm, LANES), 1)
    lo = lane < HEAD_DIM
    t = pl.program_id(1) * tm + lax.broadcasted_iota(jnp.int32, (tm, LANES), 0)
    pos_hi = (t >> 7).astype(F32)
    pos_lo = (t & 127).astype(F32)

    k_sw = pltpu.roll(k, HEAD_DIM, 1)
    zero = jnp.zeros_like(k)
    ka_ref[0, 0, 0] = jnp.where(lo, k, zero).astype(BF16)
    ka_ref[0, 0, 1] = jnp.where(lo, zero, k_sw).astype(BF16)
    ka_ref[0, 1, 0] = jnp.where(lo, k_sw, zero).astype(BF16)
    ka_ref[0, 1, 1] = jnp.where(lo, zero, k).astype(BF16)

    z = head_norm(seg(0, _S_G), gkb_ref[...])
    for hd in range(B_HEADS):
        kh = z[:, 128 * hd:128 * (hd + 1)]
        for mp in range(2):
            aug = (kaug_ref[2 * hd + mp:2 * hd + mp + 1, :] + pos_hi * kaug_ref[8 + mp:9 + mp, :]
                   + pos_lo * kaug_ref[10 + mp:11 + mp, :])
            keep = lo if mp == 0 else jnp.logical_not(lo)
            kb_ref[0, hd, mp] = jnp.where(keep, kh, aug).astype(BF16)

    g = seg(_S_G, _S_END)
    g_ref[0] = (g * jax.nn.sigmoid(g)).astype(BF16)


def _rope_tables(S):
    rows = S // GRID_W
    row = jnp.repeat(jnp.arange(rows), GRID_W).astype(F32)
    col = jnp.tile(jnp.arange(GRID_W), rows).astype(F32)
    half = HEAD_DIM // 2
    inv = ROPE_THETA ** (-jnp.arange(0, half, 2, dtype=F32) / half)
    ar = row[:, None] * inv
    ac = col[:, None] * inv
    ang = jnp.concatenate([ar, ar, ac, ac], axis=-1)
    first = (np.arange(HEAD_DIM) % 32) < 16
    return jnp.cos(ang).T, jnp.where(first, -jnp.sin(ang), jnp.sin(ang)).T


def _proj_ab(x, ws, wt, nm, gqat, gqbt, gkat, gkb, bd, kaug, tm):
    B, S, _ = x.shape
    cos_t, sin_t = _rope_tables(S)
    tok = lambda b, i: (b, i, 0)
    tab_t = pl.BlockSpec((HEAD_DIM, tm), lambda b, i: (0, i))
    const2 = lambda shp: pl.BlockSpec(shp, lambda b, i: (0, 0))
    chan = lambda n, r: pl.BlockSpec((1, n, r, tm), lambda b, i: (b, 0, 0, i))
    out_shape = (
        jax.ShapeDtypeStruct((B, 4, LANES, S), BF16),
        jax.ShapeDtypeStruct((B, 2, 2, S, LANES), BF16),
        jax.ShapeDtypeStruct((B, 2, VA_ROWS, S), BF16),
        jax.ShapeDtypeStruct((B, 4, LANES, S), BF16),
        jax.ShapeDtypeStruct((B, 4, 2, S, LANES), BF16),
        jax.ShapeDtypeStruct((B, 4, VB_ROWS, S), BF16),
        jax.ShapeDtypeStruct((B, S, 1024), BF16),
    )
    out_specs = (
        chan(4, LANES),
        pl.BlockSpec((1, 2, 2, tm, LANES), lambda b, i: (b, 0, 0, i, 0)),
        chan(2, VA_ROWS),
        chan(4, LANES),
        pl.BlockSpec((1, 4, 2, tm, LANES), lambda b, i: (b, 0, 0, i, 0)),
        chan(4, VB_ROWS),
        pl.BlockSpec((1, tm, 1024), tok),
    )
    return pl.pallas_call(
        _proj_ab_kernel,
        grid=(B, S // tm),
        in_specs=[
            pl.BlockSpec((1, tm, D_MODEL), tok), tab_t, tab_t,
            const2((1, D_MODEL)), const2((D_MODEL, _S_END)), const2((_T_END, D_MODEL)),
            const2((HEAD_DIM, tm)), const2((HEAD_DIM, tm)), const2((HEAD_DIM, tm)), const2((1, 512)),
            const2((2 * MXU_DIM, 2 * MXU_DIM)), const2((16, LANES)),
        ],
        out_specs=out_specs,
        out_shape=out_shape,
        compiler_params=pltpu.CompilerParams(
            dimension_semantics=("parallel", "parallel"), vmem_limit_bytes=VMEM_LIMIT_BYTES),
        name="proj_ab",
    )(x, cos_t, sin_t, nm, ws, wt, gqat, gqbt, gkat, gkb, bd, kaug)


def _online_update(s, v, m_ref, acc_ref, r0, nrows):
    rows = pl.ds(r0, nrows)
    m_old = m_ref[rows, :]
    m_new = jnp.maximum(m_old, jnp.max(s, axis=-1, keepdims=True))
    alpha = jnp.exp2(m_old - m_new)
    p = jnp.exp2(s - m_new[:, :1]).astype(BF16)
    pv = jnp.dot(p, v, preferred_element_type=F32)
    nv = v.shape[1]
    if nv != LANES:
        alpha = jnp.concatenate([alpha] * (nv // LANES), axis=1)
    acc_ref[rows, :] = acc_ref[rows, :] * alpha + pv
    m_ref[rows, :] = m_new


def _flash_a_online_kernel(qt_ref, k_ref, vt_ref, o_ref, q_scr, v_scr, m_ref, acc_ref, *, tq, tk, S):
    q_scr[0:tq, :] = _to_token_major(qt_ref[0, 0])
    q_scr[tq:2 * tq, :] = _to_token_major(qt_ref[0, 1])

    @pl.when(pl.program_id(2) == 0)
    def _():
        for c in range(S // tk):
            cols = slice(c * tk, (c + 1) * tk)
            v_scr[cols, :] = _to_token_major(vt_ref[0, 0, :, cols])

    m_ref[...] = jnp.full(m_ref.shape, NEG_BIG, F32)
    acc_ref[...] = jnp.zeros(acc_ref.shape, F32)

    def body(c, carry):
        ks = pl.ds(pl.multiple_of(c * tk, tk), tk)
        q = q_scr[...]
        v = v_scr[ks, :]
        for var in range(2):
            s = _nt_dot(q, k_ref[0, 0, var, ks, :])
            _online_update(s, v, m_ref, acc_ref, var * 2 * tq, 2 * tq)
        return carry

    lax.fori_loop(0, S // tk, body, 0)

    lane = lax.broadcasted_iota(jnp.int32, (tq, LANES), 1)
    lo = lane < HEAD_DIM

    def head_out(blk):
        a = acc_ref[blk * tq:(blk + 1) * tq, :]
        return a * (1.0 / a[:, HEAD_DIM:HEAD_DIM + 1])

    o01 = jnp.where(lo, head_out(0), pltpu.roll(head_out(2), HEAD_DIM, 1))
    o23 = jnp.where(lo, head_out(1), pltpu.roll(head_out(3), HEAD_DIM, 1))
    o_ref[0, :, 0:128] = o01.astype(BF16)
    o_ref[0, :, 128:256] = o23.astype(BF16)


def _flash_a_bounded_kernel(qt_ref, k_ref, vt_ref, o_ref, qt_scr, acc_ref, *, tq, big, n_grp, unroll):
    qt_scr[:, 0:tq] = qt_ref[0, 0]
    qt_scr[:, tq:2 * tq] = qt_ref[0, 1]
    acc_ref[...] = jnp.zeros(acc_ref.shape, F32)

    def body(c, carry):
        ks = pl.ds(pl.multiple_of(c * big, big), big)
        for var in range(2):
            st = jnp.dot(k_ref[0, 0, var, ks, :], qt_scr[...], preferred_element_type=F32)
            acc_ref[var] += jnp.dot(vt_ref[0, 0, :, ks], jnp.exp2(st).astype(BF16),
                                    preferred_element_type=F32)
        return carry

    lax.fori_loop(0, n_grp, body, 0, unroll=unroll)

    outs = []
    for var in range(2):
        a = acc_ref[var]
        outs.append(a[0:HEAD_DIM] * (1.0 / a[HEAD_DIM:HEAD_DIM + 1]))
    for pr in range(2):
        cols = slice(pr * tq, (pr + 1) * tq)
        pair_t = jnp.concatenate([outs[0][:, cols], outs[1][:, cols]], axis=0)
        o_ref[0, :, 128 * pr:128 * (pr + 1)] = pair_t.T.astype(BF16)


def _kv_spec(block, index_map, single_buffer):
    if single_buffer:
        return pl.BlockSpec(block, index_map, pipeline_mode=pl.Buffered(1))
    return pl.BlockSpec(block, index_map)


def _flash_a(qa_t, ka, va_t, tq, tk, bounded):
    B, _, _, S = qa_t.shape
    single = S * LANES * 2 * 3 * 2 > 8 * 1024 * 1024
    if bounded:
        grp = max(1, min(KV_GROUP * tk // tq, S // tk))
        n_grp = S // (tk * grp)
        kern = functools.partial(_flash_a_bounded_kernel, tq=tq, big=tk * grp, n_grp=n_grp,
                                 unroll=min(KV_UNROLL, n_grp))
        scratch = [pltpu.VMEM((LANES, 2 * tq), BF16), pltpu.VMEM((2, VA_ROWS, 2 * tq), F32)]
    else:
        kern = functools.partial(_flash_a_online_kernel, tq=tq, tk=tk, S=S)
        scratch = [pltpu.VMEM((2 * tq, LANES), BF16), pltpu.VMEM((S, LANES), BF16),
                   pltpu.VMEM((4 * tq, LANES), F32), pltpu.VMEM((4 * tq, LANES), F32)]
    return pl.pallas_call(
        kern,
        grid=(B, A_KV_HEADS, S // tq),
        in_specs=[
            pl.BlockSpec((1, 2, LANES, tq), lambda b, g, i: (b, g, 0, i)),
            _kv_spec((1, 1, 2, S, LANES), lambda b, g, i: (b, g, 0, 0, 0), single),
            _kv_spec((1, 1, VA_ROWS, S), lambda b, g, i: (b, g, 0, 0), single),
        ],
        out_specs=pl.BlockSpec((1, tq, 256), lambda b, g, i: (b, i, g)),
        out_shape=jax.ShapeDtypeStruct((B, S, 512), BF16),
        scratch_shapes=scratch,
        compiler_params=pltpu.CompilerParams(
            dimension_semantics=("parallel", "parallel", "arbitrary"),
            vmem_limit_bytes=VMEM_LIMIT_BYTES),
        name="flash_a_bounded" if bounded else "flash_a_online",
    )(qa_t, ka, va_t)


def _flash_b_lambda(lq1_ref, lk1_ref, lq2_ref, lk2_ref, lam_init):
    return (jnp.exp(jnp.sum(lq1_ref[...] * lk1_ref[...], keepdims=True))
            - jnp.exp(jnp.sum(lq2_ref[...] * lk2_ref[...], keepdims=True)) + lam_init)


def _flash_b_online_kernel(qt_ref, k_ref, vt_ref, tab_ref, lq1_ref, lk1_ref, lq2_ref, lk2_ref, sub_ref,
                           o_ref, q_scr, v_scr, rel_ref, m_ref, acc_ref, *, tq, tk, S, lam_init):
    lane = lax.broadcasted_iota(jnp.int32, (tq, LANES), 1)
    lo = lane < HEAD_DIM
    slab = _to_token_major(qt_ref[0, 0])
    zero = jnp.zeros_like(slab)
    q_scr[0] = jnp.where(lo, slab, zero)
    q_scr[1] = jnp.where(lo, zero, slab)

    @pl.when(pl.program_id(2) == 0)
    def _():
        pad = jnp.zeros((LANES - BF16_ROWS, tk), BF16)
        for c in range(S // tk):
            cols = slice(c * tk, (c + 1) * tk)
            v_scr[cols, 0:B_V_DIM] = _to_token_major(vt_ref[0, 0, 0:B_V_DIM, cols])
            ones_blk = jnp.concatenate([vt_ref[0, 0, B_V_DIM:VB_ROWS, cols], pad], axis=0)
            v_scr[cols, B_V_DIM:2 * LANES] = _to_token_major(ones_blk)

    m_ref[...] = jnp.full(m_ref.shape, NEG_BIG, F32)
    acc_ref[...] = jnp.zeros(acc_ref.shape, F32)
    rel_ref[...] = (lax.broadcasted_iota(jnp.int32, (tq, tk), 0)
                    - lax.broadcasted_iota(jnp.int32, (tq, tk), 1)).astype(F32)
    neg_slope = -tab_ref[0, 6:7, 0:1]
    q0 = pl.program_id(2) * tq

    def body(c, carry):
        k0 = pl.multiple_of(c * tk, tk)
        ks = pl.ds(k0, tk)
        v = v_scr[ks, :]
        bias = jnp.abs(rel_ref[...] + (q0 - k0).astype(F32)) * neg_slope
        for mp in range(2):
            s = _nt_dot(q_scr[mp], k_ref[0, 0, mp, ks, :]) + bias
            _online_update(s, v, m_ref, acc_ref, mp * tq, tq)
        return carry

    lax.fori_loop(0, S // tk, body, 0)

    lam = _flash_b_lambda(lq1_ref, lk1_ref, lq2_ref, lk2_ref, lam_init)
    a1 = acc_ref[0:tq, :]
    a2 = acc_ref[tq:2 * tq, :]
    o1 = a1[:, 0:B_V_DIM] * (1.0 / a1[:, B_V_DIM:B_V_DIM + 1])
    o2 = a2[:, 0:B_V_DIM] * (1.0 / a2[:, B_V_DIM:B_V_DIM + 1])
    o = o1 - lam * o2
    o_ref[0] = (_rms_rows(o, sub_ref[...]) * (1.0 - lam_init)).astype(BF16)


def _flash_b_bounded_kernel(qt_ref, k_ref, vt_ref, tab_ref, aug_ref, ahead_ref, lq1_ref, lk1_ref,
                            lq2_ref, lk2_ref, subt_ref, o_ref, qt_scr, acc_ref, *, tq, tk, grp, n_grp,
                            n_q, unroll, lam_init):
    big = grp * tk
    neg_2c = -2.0 * tab_ref[0, 6:7, 0:1]
    lam = _flash_b_lambda(lq1_ref, lk1_ref, lq2_ref, lk2_ref, lam_init)
    static = n_grp == 1
    zpad = jnp.zeros((HEAD_DIM - BF16_ROWS, tq), BF16)

    for qq in range(n_q):
        qi = qq if static else pl.program_id(2) * n_q + qq
        cols = slice(qq * tq, (qq + 1) * tq)
        t = qi * tq + lax.broadcasted_iota(jnp.int32, (BF16_ROWS, tq), 1)
        pos_hi = (t >> 7).astype(F32)
        pos_lo = (t & 127).astype(F32)
        for mp in range(2):
            rep = lambda r: jnp.concatenate([aug_ref[0, mp, r]] * (tq // LANES), axis=1)
            aug = rep(0) + pos_hi * rep(1) + pos_lo * rep(2)
            for side in range(2):
                rows = jnp.concatenate([(aug if side == 0 else -aug).astype(BF16), zpad], axis=0)
                if mp == 0:
                    qt_scr[qq, mp, side] = jnp.concatenate([qt_ref[0, 0, 0:HEAD_DIM, cols], rows], axis=0)
                else:
                    qt_scr[qq, mp, side] = jnp.concatenate([rows, qt_ref[0, 0, HEAD_DIM:LANES, cols]], axis=0)
    acc_ref[...] = jnp.zeros(acc_ref.shape, F32)

    if static:
        corr = ahead_ref[big:big + tk, :] * neg_2c
        for c in range(grp):
            ks = slice(c * tk, (c + 1) * tk)
            for mp in range(2):
                slots = jnp.concatenate([qt_scr[qq, mp, int(c > qq)] for qq in range(n_q)], axis=1)
                st = jnp.dot(k_ref[0, 0, mp, ks, :], slots, preferred_element_type=F32)
                parts = [st[:, qq * tq:(qq + 1) * tq] + corr if qq == c else st[:, qq * tq:(qq + 1) * tq]
                         for qq in range(n_q)]
                p = jnp.exp2(jnp.concatenate(parts, axis=1)).astype(BF16)
                acc_ref[mp] += jnp.dot(vt_ref[0, 0, :, ks], p, preferred_element_type=F32)
    else:
        qi = pl.program_id(2)
        gd = qi // grp
        r = qi - gd * grp

        def attend(ks, side, corr):
            for mp in range(2):
                st = jnp.dot(k_ref[0, 0, mp, ks, :], qt_scr[0, mp, side], preferred_element_type=F32)
                if corr is not None:
                    st = st + corr
                acc_ref[mp] += jnp.dot(vt_ref[0, 0, :, ks], jnp.exp2(st).astype(BF16),
                                       preferred_element_type=F32)

        w0 = pl.multiple_of(big - r * tk, tk)
        attend(pl.ds(pl.multiple_of(gd * big, big), big), 0, ahead_ref[pl.ds(w0, big), :] * neg_2c)

        def body(j, carry):
            g = gd + j
            g = jnp.where(g >= n_grp, g - n_grp, g)
            attend(pl.ds(pl.multiple_of(g * big, big), big), (g > gd).astype(jnp.int32), None)
            return carry
        lax.fori_loop(1, n_grp, body, 0, unroll=unroll)

    for qq in range(n_q):
        a1 = acc_ref[0, :, qq * tq:(qq + 1) * tq]
        a2 = acc_ref[1, :, qq * tq:(qq + 1) * tq]
        o1 = a1[0:B_V_DIM] * (1.0 / a1[B_V_DIM:B_V_DIM + 1])
        o2 = a2[0:B_V_DIM] * (1.0 / a2[B_V_DIM:B_V_DIM + 1])
        o = _rms_cols(o1 - lam * o2, subt_ref[...]) * (1.0 - lam_init)
        o_ref[0, qq * tq:(qq + 1) * tq, :] = o.T.astype(BF16)


def _flash_b(qb_t, kb, vb_t, tab, aug_t, lq1, lk1, lq2, lk2, subln, lam_init, tq, tk, bounded):
    B, _, _, S = qb_t.shape
    assert tq == tk
    single = S * LANES * 2 * 4 * 2 > 8 * 1024 * 1024
    n_tiles = S // tq
    vec = lambda n: pl.BlockSpec((1, n), lambda b, h, i: (0, 0))
    if bounded:
        grp = S // tk if S // tk <= SHORT_SEQ_CHUNKS else KV_GROUP
        n_grp = S // (tk * grp)
        n_q = n_tiles if n_grp == 1 else 1
        big = grp * tk
        ahead = np.maximum(np.arange(-big, big)[:, None] - np.arange(tq)[None, :], 0).astype(np.float32)
        sub_t = jnp.broadcast_to(subln.reshape(B_V_DIM, 1), (B_V_DIM, tq))
        kern = functools.partial(_flash_b_bounded_kernel, tq=tq, tk=tk, grp=grp, n_grp=n_grp, n_q=n_q,
                                 unroll=max(n_grp - 1, 1), lam_init=lam_init)
        extra_in = [jnp.asarray(ahead)]
        extra_specs = [
            pl.BlockSpec((1, 2, 3, BF16_ROWS, LANES), lambda b, h, i: (h, 0, 0, 0, 0)),
            pl.BlockSpec((2 * big, tq), lambda b, h, i: (0, 0), pipeline_mode=pl.Buffered(1)),
        ]
        sub_in, sub_spec = sub_t, pl.BlockSpec((B_V_DIM, tq), lambda b, h, i: (0, 0))
        scratch = [pltpu.VMEM((n_q, 2, 2, LANES, tq), BF16), pltpu.VMEM((2, VB_ROWS, n_q * tq), F32)]
        args = (qb_t, kb, vb_t, tab, aug_t, *extra_in, lq1, lk1, lq2, lk2, sub_in)
    else:
        n_q = 1
        kern = functools.partial(_flash_b_online_kernel, tq=tq, tk=tk, S=S, lam_init=lam_init)
        extra_specs = []
        sub_in, sub_spec = subln, vec(B_V_DIM)
        scratch = [pltpu.VMEM((2, tq, LANES), BF16), pltpu.VMEM((S, 2 * LANES), BF16),
                   pltpu.VMEM((tq, tk), F32), pltpu.VMEM((2 * tq, LANES), F32),
                   pltpu.VMEM((2 * tq, 2 * LANES), F32)]
        args = (qb_t, kb, vb_t, tab, lq1, lk1, lq2, lk2, sub_in)
    return pl.pallas_call(
        kern,
        grid=(B, B_HEADS, n_tiles // n_q),
        in_specs=[
            pl.BlockSpec((1, 1, LANES, n_q * tq), lambda b, h, i: (b, h, 0, i)),
            _kv_spec((1, 1, 2, S, LANES), lambda b, h, i: (b, h, 0, 0, 0), single),
            _kv_spec((1, 1, VB_ROWS, S), lambda b, h, i: (b, h, 0, 0), single),
            pl.BlockSpec((1, 8, LANES), lambda b, h, i: (h, 0, 0)),
            *extra_specs,
            vec(HEAD_DIM), vec(HEAD_DIM), vec(HEAD_DIM), vec(HEAD_DIM), sub_spec,
        ],
        out_specs=pl.BlockSpec((1, n_q * tq, LANES), lambda b, h, i: (b, i, h)),
        out_shape=jax.ShapeDtypeStruct((B, S, 512), BF16),
        scratch_shapes=scratch,
        compiler_params=pltpu.CompilerParams(
            dimension_semantics=("parallel", "parallel", "arbitrary"),
            vmem_limit_bytes=VMEM_LIMIT_BYTES),
        name="flash_b_bounded" if bounded else "flash_b_online",
    )(*args)


def _ple_tail(h1, p, npl, wg_ref, wp_ref):
    rn = _rms_rows(h1, npl).astype(BF16)
    gate = jax.nn.sigmoid(jnp.dot(rn, wg_ref[...], preferred_element_type=F32))
    return h1 + gate * jnp.dot(p.astype(BF16), wp_ref[...], preferred_element_type=F32)


def _out_ple_kernel(h_ref, oa_ref, ob_ref, g_ref, p_ref, wo_ref, npl_ref, wg_ref, wp_ref, out_ref):
    g = g_ref[0]
    y = jnp.concatenate([oa_ref[0] * g[:, 0:512], ob_ref[0] * g[:, 512:1024]], axis=1)
    h1 = h_ref[0] + jnp.dot(y, wo_ref[...], preferred_element_type=F32)
    out_ref[0] = _ple_tail(h1, p_ref[0, 0], npl_ref[...], wg_ref, wp_ref)


def _out_ple(h, oa, ob, gates, p, layer, wo, npl, wg, wp, tm):
    B, S, _ = h.shape
    tok = lambda b, i: (b, i, 0)
    const2 = lambda shp: pl.BlockSpec(shp, lambda b, i: (0, 0))
    return pl.pallas_call(
        _out_ple_kernel,
        grid=(B, S // tm),
        in_specs=[
            pl.BlockSpec((1, tm, D_MODEL), tok),
            pl.BlockSpec((1, tm, 512), tok),
            pl.BlockSpec((1, tm, 512), tok),
            pl.BlockSpec((1, tm, 1024), tok),
            pl.BlockSpec((1, 1, tm, PLE_DIM), lambda b, i: (layer, b, i, 0)),
            const2((1024, D_MODEL)), const2((1, D_MODEL)), const2((D_MODEL, D_MODEL)),
            const2((PLE_DIM, D_MODEL)),
        ],
        out_specs=pl.BlockSpec((1, tm, D_MODEL), tok),
        out_shape=jax.ShapeDtypeStruct((B, S, D_MODEL), F32),
        compiler_params=pltpu.CompilerParams(
            dimension_semantics=("parallel", "parallel"), vmem_limit_bytes=VMEM_LIMIT_BYTES),
        name="out_ple",
    )(h, oa, ob, gates, p, wo, npl, wg, wp)


def _layer_c_kernel(h_ref, hp_ref, hn_ref, p_ref, nm_ref, wi_ref, wgrp_ref, sc_ref, wo_ref,
                    npl_ref, wg_ref, wp_ref, out_ref, *, tm, S):
    i = pl.program_id(1)
    nt = pl.num_programs(1)
    h = h_ref[0]
    hp = hp_ref[0] * (i > 0).astype(F32)
    hn = hn_ref[0] * (i < nt - 1).astype(F32)
    n = tm + 2 * POOL_HALO
    h_ext = jnp.concatenate([hp, h, hn], axis=0)
    xn = _rms_rows(h_ext, nm_ref[...]).astype(BF16)
    ug = jnp.dot(xn, wi_ref[...], preferred_element_type=F32)
    u_ext = ug[:, 0:1024]
    g = ug[POOL_HALO:POOL_HALO + tm, 1024:2048]

    t = i * tm + lax.broadcasted_iota(jnp.int32, (tm, C_GRP), 0)
    mixed = []
    for gi, w in enumerate(POOL_WINDOWS):
        u = u_ext[:, C_GRP * gi:C_GRP * (gi + 1)]
        win = u + pltpu.roll(u, 1, 0)
        half = 1
        while 2 * half < w:
            win = pltpu.roll(win, half, 0) + pltpu.roll(win, n - half, 0)
            half *= 2
        cnt = (jnp.minimum(t + w // 2, S) - jnp.maximum(t - w // 2, 0)).astype(F32)
        pooled = win[POOL_HALO:POOL_HALO + tm] / cnt - u[POOL_HALO:POOL_HALO + tm]
        mixed.append(jnp.dot(pooled.astype(BF16), wgrp_ref[gi], preferred_element_type=F32))
    mixed = jnp.concatenate(mixed, axis=1)
    y = ((mixed * sc_ref[...]) * (g * jax.nn.sigmoid(g))).astype(BF16)
    h1 = h + jnp.dot(y, wo_ref[...], preferred_element_type=F32)
    out_ref[0] = _ple_tail(h1, p_ref[0, 0], npl_ref[...], wg_ref, wp_ref)


def _layer_c(h, p, layer, nm, wi, wgrp, sc, wo, npl, wg, wp, tm):
    B, S, _ = h.shape
    tok = lambda b, i: (b, i, 0)
    const2 = lambda shp: pl.BlockSpec(shp, lambda b, i: (0, 0))
    per = tm // POOL_HALO
    last = S // POOL_HALO - 1
    kern = functools.partial(_layer_c_kernel, tm=tm, S=S)
    return pl.pallas_call(
        kern,
        grid=(B, S // tm),
        in_specs=[
            pl.BlockSpec((1, tm, D_MODEL), tok),
            pl.BlockSpec((1, POOL_HALO, D_MODEL), lambda b, i: (b, jnp.maximum(i * per - 1, 0), 0)),
            pl.BlockSpec((1, POOL_HALO, D_MODEL), lambda b, i: (b, jnp.minimum((i + 1) * per, last), 0)),
            pl.BlockSpec((1, 1, tm, PLE_DIM), lambda b, i: (layer, b, i, 0)),
            const2((1, D_MODEL)), const2((D_MODEL, 2048)),
            pl.BlockSpec((4, C_GRP, C_GRP), lambda b, i: (0, 0, 0)),
            const2((1, 1024)), const2((1024, D_MODEL)), const2((1, D_MODEL)),
            const2((D_MODEL, D_MODEL)), const2((PLE_DIM, D_MODEL)),
        ],
        out_specs=pl.BlockSpec((1, tm, D_MODEL), tok),
        out_shape=jax.ShapeDtypeStruct((B, S, D_MODEL), F32),
        compiler_params=pltpu.CompilerParams(
            dimension_semantics=("parallel", "parallel"), vmem_limit_bytes=VMEM_LIMIT_BYTES),
        name="layer_c",
    )(h, h, h, p, nm, wi, wgrp, sc, wo, npl, wg, wp)


def _score_bound(gq, gk):
    slack = 1.02
    return (HEAD_DIM * SM_SCALE * LOG2E * slack) * jnp.max(jnp.abs(gq)) * jnp.max(jnp.abs(gk))


def _alibi_tables():
    kaug = np.zeros((16, LANES), np.float32)
    qtab = np.zeros((B_HEADS, 8, LANES), np.float32)
    aug_t = np.zeros((B_HEADS, 2, 3, BF16_ROWS, LANES), np.float32)
    aug_t[:, :, 1, 0:3, :] = -128.0
    aug_t[:, :, 2, 3:6, :] = -1.0
    for mp in range(2):
        base = HEAD_DIM if mp == 0 else 0
        kaug[8 + mp, base + 6:base + 9] = 128.0
        kaug[10 + mp, base + 9:base + 12] = 1.0
    for h, slope in enumerate(ALIBI_SLOPES):
        c = np.float64(slope) * LOG2E
        pieces, rest = [], c
        for _ in range(3):
            piece = np.float64(np.float32(rest).astype(BF16))
            pieces.append(piece)
            rest -= piece
        for mp in range(2):
            base = HEAD_DIM if mp == 0 else 0
            kaug[2 * h + mp, base:base + 6] = pieces + pieces
        aug_t[h, :, 0, 6:12, :] = np.asarray(pieces + pieces)[:, None]
        qtab[h, 6, :] = np.float32(c)
    return jnp.asarray(kaug), jnp.asarray(qtab), jnp.asarray(aug_t)


def _trunk(x, p, prm, tq_a, tq_b, tk, tm):
    row = lambda v: v.astype(F32).reshape(1, -1)
    col = lambda v, s: jnp.broadcast_to((v.astype(F32) * s).reshape(-1, 1), (v.shape[0], tm))
    lam_init = 0.8 - 0.6 * math.exp(-0.3 * 0)
    gq_scale = SM_SCALE * LOG2E
    gkb = jnp.tile(prm["kn_b"][0].astype(F32), 8).reshape(1, -1)
    qa_t, ka, va_t, qb_t, kb, vb_t, gates = _proj_ab(
        x, prm["w_ab_s"], prm["w_ab_t"], row(prm["norm_mix"][0]), col(prm["qn_a"][0], gq_scale),
        col(prm["qn_b"][0], gq_scale), col(prm["kn_a"][0], 1.0), gkb, prm["bd"], prm["kaug"], tm)
    oa = lax.cond(
        _score_bound(prm["qn_a"][0], prm["kn_a"][0]) <= BOUNDED_SCORE_LIMIT,
        lambda q, k, v: _flash_a(q, k, v, tq_a, tk, True),
        lambda q, k, v: _flash_a(q, k, v, tq_a, tk, False),
        qa_t, ka, va_t)
    lam_vecs = (row(prm["lam_q1"][0]), row(prm["lam_k1"][0]), row(prm["lam_q2"][0]),
                row(prm["lam_k2"][0]), row(prm["subln_b"][0]))
    ob = lax.cond(
        _score_bound(prm["qn_b"][0], prm["kn_b"][0]) <= BOUNDED_SCORE_LIMIT,
        lambda q, k, v: _flash_b(q, k, v, prm["qtab"], prm["aug_t"], *lam_vecs, lam_init, tq_b, tk, True),
        lambda q, k, v: _flash_b(q, k, v, prm["qtab"], prm["aug_t"], *lam_vecs, lam_init, tq_b, tk, False),
        qb_t, kb, vb_t)
    h = _out_ple(x, oa, ob, gates, p, 0, prm["w_out_ab"][0], row(prm["norm_ple"][0]),
                 prm["w_ple_gate"][0], prm["w_ple_proj"][0], tm)
    h = _layer_c(h, p, 1, row(prm["norm_mix"][1]), prm["w_in_c"][0], prm["w_grp_c"][0],
                 row(prm["scale_c"][0]), prm["w_out_c"][0], row(prm["norm_ple"][1]),
                 prm["w_ple_gate"][1], prm["w_ple_proj"][1], tm)
    return h


def _prepare(norm_mix, w_in_ab, qn_a, kn_a, qn_b, kn_b, lam_q1, lam_k1, lam_q2, lam_k2, subln_b,
             w_out_ab, w_in_c, w_grp_c, scale_c, w_out_c, norm_ple, w_ple_gate, w_ple_proj):
    blk = np.arange(2 * MXU_DIM) // HEAD_DIM
    kaug, qtab, aug_t = _alibi_tables()
    w_ab = w_in_ab[0].astype(BF16)
    cols = lambda a, b: w_ab[:, a:b]
    return dict(
        norm_mix=norm_mix, qn_a=qn_a, kn_a=kn_a, qn_b=qn_b, kn_b=kn_b,
        lam_q1=lam_q1, lam_k1=lam_k1, lam_q2=lam_q2, lam_k2=lam_k2, subln_b=subln_b,
        scale_c=scale_c, norm_ple=norm_ple,
        w_ab_s=jnp.concatenate([cols(_KB, _VB), cols(_GA, _QB), cols(_GB, _AB_END)], axis=1),
        w_ab_t=jnp.concatenate([cols(_QA, _KA), cols(_QB, _KB), cols(_VA, _GA), cols(_VB, _GB),
                                cols(_KA, _VA)], axis=1).T,
        w_out_ab=w_out_ab.astype(BF16), w_in_c=w_in_c.astype(BF16),
        w_grp_c=w_grp_c.astype(BF16), w_out_c=w_out_c.astype(BF16),
        w_ple_gate=w_ple_gate.astype(BF16), w_ple_proj=w_ple_proj.astype(BF16),
        bd=jnp.asarray(blk[:, None] == blk[None, :], BF16),
        kaug=kaug, qtab=qtab, aug_t=aug_t,
    )


def kernel(x_prompt, x_sample, p_prompt, p_sample, norm_mix, w_in_ab, qn_a, kn_a, qn_b, kn_b,
           lam_q1, lam_k1, lam_q2, lam_k2, subln_b, w_out_ab, w_in_c, w_grp_c, scale_c, w_out_c,
           norm_ple, w_ple_gate, w_ple_proj):
    prm = _prepare(norm_mix, w_in_ab, qn_a, kn_a, qn_b, kn_b, lam_q1, lam_k1, lam_q2, lam_k2, subln_b,
                   w_out_ab, w_in_c, w_grp_c, scale_c, w_out_c, norm_ple, w_ple_gate, w_ple_proj)
    y_prompt = _trunk(x_prompt, p_prompt, prm, tq_a=512, tq_b=512, tk=512, tm=1024)
    y_sample = _trunk(x_sample, p_sample, prm, tq_a=1024, tq_b=512, tk=512, tm=1024)
    return (y_prompt, y_sample)
```

```python
import functools
import math

import jax
import jax.numpy as jnp
import numpy as np
from jax import lax
from jax.experimental import pallas as pl
from jax.experimental.pallas import tpu as pltpu

F32 = jnp.float32
BF16 = jnp.bfloat16

D_MODEL = 1024
PLE_DIM = 256
GRID_W = 64
HEAD_DIM = 64
EPS = 1e-6
ROPE_THETA = 10000.0
A_HEADS = 8
A_KV_HEADS = 2
B_HEADS = 4
B_V_DIM = 2 * HEAD_DIM
ALIBI_SLOPES = tuple(2.0 ** (-8.0 * (h + 1) / B_HEADS) for h in range(B_HEADS))
POOL_WINDOWS = (2, 4, 8, 16)
C_GRP = 256
POOL_HALO = 8
LOG2E = math.log2(math.e)
SM_SCALE = HEAD_DIM ** -0.5

LANES = 128
BF16_ROWS = 16
VMEM_LIMIT_BYTES = 56 * 1024 * 1024

VA_ROWS = LANES
VB_ROWS = B_V_DIM + 2 * BF16_ROWS

_QA, _KA, _VA, _GA, _QB, _KB, _VB, _GB, _AB_END = 0, 512, 640, 768, 1280, 1792, 2304, 2816, 3328
_S_END = 1024
_T_QB, _T_VA, _T_VB, _T_KA, _T_KB, _T_END = 512, 1024, 1152, 1664, 1792, 2304

NEG_BIG = -1e30
BOUNDED_SCORE_LIMIT = 60.0
KV_GROUP = 2
KV_UNROLL = 8
SHORT_SEQ_CHUNKS = 4


def _rms_rows(x, g):
    ms = jnp.mean(x * x, axis=-1, keepdims=True)
    return x * lax.rsqrt(ms + EPS) * g


def _rms_cols(x, g):
    ms = jnp.mean(x * x, axis=0, keepdims=True)
    return x * lax.rsqrt(ms + EPS) * g


def _nt_dot(a, b):
    return lax.dot_general(a, b, (((1,), (1,)), ((), ())), preferred_element_type=F32)


def _to_token_major(xt):
    return xt.astype(F32).T.astype(BF16)


def _proj_ab_kernel(x_ref, cost_ref, sint_ref, nm_ref, ws_ref, wt_ref,
                    gqat_ref, gqbt_ref, gkat_ref, gkbt_ref, kaug_ref,
                    qa_ref, ka_ref, va_ref, qb_ref, kb_ref, vb_ref, g_ref):
    x = x_ref[0]
    xn = _rms_rows(x, nm_ref[...]).astype(BF16)
    tm = x.shape[0]

    def seg_t(a, b):
        return _nt_dot(wt_ref[a:b, :], xn)

    cos_t = cost_ref[...]
    sin_t = sint_ref[...]

    def norm_rope_t(zh, gain):
        zh = _rms_cols(zh, gain)
        sw = jnp.concatenate([zh[16:32], zh[0:16], zh[48:64], zh[32:48]], axis=0)
        return zh * cos_t + sw * sin_t

    z = seg_t(0, _T_VA)
    for h in range(A_HEADS):
        r0 = 64 * (h % 2)
        qa_ref[0, h // 2, r0:r0 + 64, :] = norm_rope_t(z[64 * h:64 * (h + 1)], gqat_ref[...]).astype(BF16)
    for h in range(2 * B_HEADS):
        r0 = 64 * (h % 2)
        zh = z[_T_QB + 64 * h:_T_QB + 64 * (h + 1)]
        qb_ref[0, h // 2, r0:r0 + 64, :] = _rms_cols(zh, gqbt_ref[...]).astype(BF16)
    ones_rows = (lax.broadcasted_iota(jnp.int32, (BF16_ROWS, tm), 0) == 0).astype(BF16)
    z = seg_t(_T_VA, _T_END)
    for kv in range(A_KV_HEADS):
        va_ref[0, kv, 0:HEAD_DIM, :] = z[64 * kv:64 * (kv + 1)].astype(BF16)
        va_ref[0, kv, HEAD_DIM:HEAD_DIM + BF16_ROWS, :] = ones_rows
        va_ref[0, kv, HEAD_DIM + BF16_ROWS:VA_ROWS, :] = jnp.zeros((VA_ROWS - HEAD_DIM - BF16_ROWS, tm), BF16)
    for hd in range(B_HEADS):
        r0 = _T_VB - _T_VA + B_V_DIM * hd
        vb_ref[0, hd, 0:B_V_DIM, :] = z[r0:r0 + B_V_DIM].astype(BF16)
        vb_ref[0, hd, B_V_DIM:B_V_DIM + BF16_ROWS, :] = ones_rows
        vb_ref[0, hd, B_V_DIM + BF16_ROWS:VB_ROWS, :] = jnp.zeros((VB_ROWS - B_V_DIM - BF16_ROWS, tm), BF16)

    r0 = _T_KA - _T_VA
    k = jnp.concatenate([norm_rope_t(z[r0 + 64 * kv:r0 + 64 * (kv + 1)], gkat_ref[...])
                         for kv in range(A_KV_HEADS)], axis=0).T

    lane = lax.broadcasted_iota(jnp.int32, (tm, LANES), 1)
    lo = lane < HEAD_DIM
    t = pl.program_id(1) * tm + lax.broadcasted_iota(jnp.int32, (tm, LANES), 0)
    pos_hi = (t >> 7).astype(F32)
    pos_lo = (t & 127).astype(F32)

    k_sw = pltpu.roll(k, HEAD_DIM, 1)
    zero = jnp.zeros_like(k)
    ka_ref[0, 0, 0] = jnp.where(lo, k, zero).astype(BF16)
    ka_ref[0, 0, 1] = jnp.where(lo, zero, k_sw).astype(BF16)
    ka_ref[0, 1, 0] = jnp.where(lo, k_sw, zero).astype(BF16)
    ka_ref[0, 1, 1] = jnp.where(lo, zero, k).astype(BF16)

    for hd in range(B_HEADS):
        r0 = _T_KB - _T_VA + 128 * hd
        kh = jnp.concatenate([_rms_cols(z[r0 + 64 * mp:r0 + 64 * (mp + 1)], gkbt_ref[...])
                              for mp in range(2)], axis=0).T
        for mp in range(2):
            aug = (kaug_ref[2 * hd + mp:2 * hd + mp + 1, :] + pos_hi * kaug_ref[8 + mp:9 + mp, :]
                   + pos_lo * kaug_ref[10 + mp:11 + mp, :])
            keep = lo if mp == 0 else jnp.logical_not(lo)
            kb_ref[0, hd, mp] = jnp.where(keep, kh, aug).astype(BF16)

    g = jnp.dot(xn, ws_ref[...], preferred_element_type=F32)
    g_ref[0] = (g * jax.nn.sigmoid(g)).astype(BF16)


def _rope_tables(S):
    rows = S // GRID_W
    row = jnp.repeat(jnp.arange(rows), GRID_W).astype(F32)
    col = jnp.tile(jnp.arange(GRID_W), rows).astype(F32)
    half = HEAD_DIM // 2
    inv = ROPE_THETA ** (-jnp.arange(0, half, 2, dtype=F32) / half)
    ar = row[:, None] * inv
    ac = col[:, None] * inv
    ang = jnp.concatenate([ar, ar, ac, ac], axis=-1)
    first = (np.arange(HEAD_DIM) % 32) < 16
    return jnp.cos(ang).T, jnp.where(first, -jnp.sin(ang), jnp.sin(ang)).T


def _proj_ab(x, ws, wt, nm, gqat, gqbt, gkat, gkbt, kaug, tm):
    B, S, _ = x.shape
    cos_t, sin_t = _rope_tables(S)
    tok = lambda b, i: (b, i, 0)
    tab_t = pl.BlockSpec((HEAD_DIM, tm), lambda b, i: (0, i))
    const2 = lambda shp: pl.BlockSpec(shp, lambda b, i: (0, 0))
    chan = lambda n, r: pl.BlockSpec((1, n, r, tm), lambda b, i: (b, 0, 0, i))
    out_shape = (
        jax.ShapeDtypeStruct((B, 4, LANES, S), BF16),
        jax.ShapeDtypeStruct((B, 2, 2, S, LANES), BF16),
        jax.ShapeDtypeStruct((B, 2, VA_ROWS, S), BF16),
        jax.ShapeDtypeStruct((B, 4, LANES, S), BF16),
        jax.ShapeDtypeStruct((B, 4, 2, S, LANES), BF16),
        jax.ShapeDtypeStruct((B, 4, VB_ROWS, S), BF16),
        jax.ShapeDtypeStruct((B, S, 1024), BF16),
    )
    out_specs = (
        chan(4, LANES),
        pl.BlockSpec((1, 2, 2, tm, LANES), lambda b, i: (b, 0, 0, i, 0)),
        chan(2, VA_ROWS),
        chan(4, LANES),
        pl.BlockSpec((1, 4, 2, tm, LANES), lambda b, i: (b, 0, 0, i, 0)),
        chan(4, VB_ROWS),
        pl.BlockSpec((1, tm, 1024), tok),
    )
    return pl.pallas_call(
        _proj_ab_kernel,
        grid=(B, S // tm),
        in_specs=[
            pl.BlockSpec((1, tm, D_MODEL), tok), tab_t, tab_t,
            const2((1, D_MODEL)), const2((D_MODEL, _S_END)), const2((_T_END, D_MODEL)),
            const2((HEAD_DIM, tm)), const2((HEAD_DIM, tm)), const2((HEAD_DIM, tm)), const2((HEAD_DIM, tm)),
            const2((16, LANES)),
        ],
        out_specs=out_specs,
        out_shape=out_shape,
        compiler_params=pltpu.CompilerParams(
            dimension_semantics=("parallel", "parallel"), vmem_limit_bytes=VMEM_LIMIT_BYTES),
        name="proj_ab",
    )(x, cos_t, sin_t, nm, ws, wt, gqat, gqbt, gkat, gkbt, kaug)


def _online_update(s, v, m_ref, acc_ref, r0, nrows):
    rows = pl.ds(r0, nrows)
    m_old = m_ref[rows, :]
    m_new = jnp.maximum(m_old, jnp.max(s, axis=-1, keepdims=True))
    alpha = jnp.exp2(m_old - m_new)
    p = jnp.exp2(s - m_new[:, :1]).astype(BF16)
    pv = jnp.dot(p, v, preferred_element_type=F32)
    nv = v.shape[1]
    if nv != LANES:
        alpha = jnp.concatenate([alpha] * (nv // LANES), axis=1)
    acc_ref[rows, :] = acc_ref[rows, :] * alpha + pv
    m_ref[rows, :] = m_new


def _flash_a_online_kernel(qt_ref, k_ref, vt_ref, o_ref, q_scr, v_scr, m_ref, acc_ref, *, tq, tk, S):
    q_scr[0:tq, :] = _to_token_major(qt_ref[0, 0])
    q_scr[tq:2 * tq, :] = _to_token_major(qt_ref[0, 1])

    @pl.when(pl.program_id(2) == 0)
    def _():
        for c in range(S // tk):
            cols = slice(c * tk, (c + 1) * tk)
            v_scr[cols, :] = _to_token_major(vt_ref[0, 0, :, cols])

    m_ref[...] = jnp.full(m_ref.shape, NEG_BIG, F32)
    acc_ref[...] = jnp.zeros(acc_ref.shape, F32)

    def body(c, carry):
        ks = pl.ds(pl.multiple_of(c * tk, tk), tk)
        q = q_scr[...]
        v = v_scr[ks, :]
        for var in range(2):
            s = _nt_dot(q, k_ref[0, 0, var, ks, :])
            _online_update(s, v, m_ref, acc_ref, var * 2 * tq, 2 * tq)
        return carry

    lax.fori_loop(0, S // tk, body, 0)

    lane = lax.broadcasted_iota(jnp.int32, (tq, LANES), 1)
    lo = lane < HEAD_DIM

    def head_out(blk):
        a = acc_ref[blk * tq:(blk + 1) * tq, :]
        return a * (1.0 / a[:, HEAD_DIM:HEAD_DIM + 1])

    o01 = jnp.where(lo, head_out(0), pltpu.roll(head_out(2), HEAD_DIM, 1))
    o23 = jnp.where(lo, head_out(1), pltpu.roll(head_out(3), HEAD_DIM, 1))
    o_ref[0, :, 0:128] = o01.astype(BF16)
    o_ref[0, :, 128:256] = o23.astype(BF16)


def _flash_a_bounded_kernel(qt_ref, k_ref, vt_ref, o_ref, qt_scr, acc_ref, *, tq, big, n_grp, unroll):
    qt_scr[:, 0:tq] = qt_ref[0, 0]
    qt_scr[:, tq:2 * tq] = qt_ref[0, 1]
    acc_ref[...] = jnp.zeros(acc_ref.shape, F32)

    def body(c, carry):
        ks = pl.ds(pl.multiple_of(c * big, big), big)
        for var in range(2):
            st = jnp.dot(k_ref[0, 0, var, ks, :], qt_scr[...], preferred_element_type=F32)
            acc_ref[var] += jnp.dot(vt_ref[0, 0, :, ks], jnp.exp2(st).astype(BF16),
                                    preferred_element_type=F32)
        return carry

    lax.fori_loop(0, n_grp, body, 0, unroll=unroll)

    outs = []
    for var in range(2):
        a = acc_ref[var]
        outs.append(a[0:HEAD_DIM] * (1.0 / a[HEAD_DIM:HEAD_DIM + 1]))
    for pr in range(2):
        cols = slice(pr * tq, (pr + 1) * tq)
        pair_t = jnp.concatenate([outs[0][:, cols], outs[1][:, cols]], axis=0)
        o_ref[0, :, 128 * pr:128 * (pr + 1)] = pair_t.T.astype(BF16)


def _kv_spec(block, index_map, single_buffer):
    if single_buffer:
        return pl.BlockSpec(block, index_map, pipeline_mode=pl.Buffered(1))
    return pl.BlockSpec(block, index_map)


def _flash_a(qa_t, ka, va_t, tq, tk, bounded):
    B, _, _, S = qa_t.shape
    single = S * LANES * 2 * 3 * 2 > 8 * 1024 * 1024
    if bounded:
        grp = max(1, min(KV_GROUP * tk // tq, S // tk))
        n_grp = S // (tk * grp)
        kern = functools.partial(_flash_a_bounded_kernel, tq=tq, big=tk * grp, n_grp=n_grp,
                                 unroll=min(KV_UNROLL, n_grp))
        scratch = [pltpu.VMEM((LANES, 2 * tq), BF16), pltpu.VMEM((2, VA_ROWS, 2 * tq), F32)]
    else:
        kern = functools.partial(_flash_a_online_kernel, tq=tq, tk=tk, S=S)
        scratch = [pltpu.VMEM((2 * tq, LANES), BF16), pltpu.VMEM((S, LANES), BF16),
                   pltpu.VMEM((4 * tq, LANES), F32), pltpu.VMEM((4 * tq, LANES), F32)]
    return pl.pallas_call(
        kern,
        grid=(B, A_KV_HEADS, S // tq),
        in_specs=[
            pl.BlockSpec((1, 2, LANES, tq), lambda b, g, i: (b, g, 0, i)),
            _kv_spec((1, 1, 2, S, LANES), lambda b, g, i: (b, g, 0, 0, 0), single),
            _kv_spec((1, 1, VA_ROWS, S), lambda b, g, i: (b, g, 0, 0), single),
        ],
        out_specs=pl.BlockSpec((1, tq, 256), lambda b, g, i: (b, i, g)),
        out_shape=jax.ShapeDtypeStruct((B, S, 512), BF16),
        scratch_shapes=scratch,
        compiler_params=pltpu.CompilerParams(
            dimension_semantics=("parallel", "parallel", "arbitrary"),
            vmem_limit_bytes=VMEM_LIMIT_BYTES),
        name="flash_a_bounded" if bounded else "flash_a_online",
    )(qa_t, ka, va_t)


def _flash_b_lambda(lq1_ref, lk1_ref, lq2_ref, lk2_ref, lam_init):
    return (jnp.exp(jnp.sum(lq1_ref[...] * lk1_ref[...], keepdims=True))
            - jnp.exp(jnp.sum(lq2_ref[...] * lk2_ref[...], keepdims=True)) + lam_init)


def _flash_b_online_kernel(qt_ref, k_ref, vt_ref, tab_ref, lq1_ref, lk1_ref, lq2_ref, lk2_ref, sub_ref,
                           o_ref, q_scr, v_scr, rel_ref, m_ref, acc_ref, *, tq, tk, S, lam_init):
    lane = lax.broadcasted_iota(jnp.int32, (tq, LANES), 1)
    lo = lane < HEAD_DIM
    slab = _to_token_major(qt_ref[0, 0])
    zero = jnp.zeros_like(slab)
    q_scr[0] = jnp.where(lo, slab, zero)
    q_scr[1] = jnp.where(lo, zero, slab)

    @pl.when(pl.program_id(2) == 0)
    def _():
        pad = jnp.zeros((LANES - BF16_ROWS, tk), BF16)
        for c in range(S // tk):
            cols = slice(c * tk, (c + 1) * tk)
            v_scr[cols, 0:B_V_DIM] = _to_token_major(vt_ref[0, 0, 0:B_V_DIM, cols])
            ones_blk = jnp.concatenate([vt_ref[0, 0, B_V_DIM:B_V_DIM + BF16_ROWS, cols], pad], axis=0)
            v_scr[cols, B_V_DIM:2 * LANES] = _to_token_major(ones_blk)

    m_ref[...] = jnp.full(m_ref.shape, NEG_BIG, F32)
    acc_ref[...] = jnp.zeros(acc_ref.shape, F32)
    rel_ref[...] = (lax.broadcasted_iota(jnp.int32, (tq, tk), 0)
                    - lax.broadcasted_iota(jnp.int32, (tq, tk), 1)).astype(F32)
    neg_slope = -tab_ref[0, 6:7, 0:1]
    q0 = pl.program_id(2) * tq

    def body(c, carry):
        k0 = pl.multiple_of(c * tk, tk)
        ks = pl.ds(k0, tk)
        v = v_scr[ks, :]
        bias = jnp.abs(rel_ref[...] + (q0 - k0).astype(F32)) * neg_slope
        for mp in range(2):
            s = _nt_dot(q_scr[mp], k_ref[0, 0, mp, ks, :]) + bias
            _online_update(s, v, m_ref, acc_ref, mp * tq, tq)
        return carry

    lax.fori_loop(0, S // tk, body, 0)

    lam = _flash_b_lambda(lq1_ref, lk1_ref, lq2_ref, lk2_ref, lam_init)
    a1 = acc_ref[0:tq, :]
    a2 = acc_ref[tq:2 * tq, :]
    o1 = a1[:, 0:B_V_DIM] * (1.0 / a1[:, B_V_DIM:B_V_DIM + 1])
    o2 = a2[:, 0:B_V_DIM] * (1.0 / a2[:, B_V_DIM:B_V_DIM + 1])
    o = o1 - lam * o2
    o_ref[0] = (_rms_rows(o, sub_ref[...]) * (1.0 - lam_init)).astype(BF16)


def _flash_b_bounded_kernel(qt_ref, k_ref, vt_ref, tab_ref, aug_ref, ahead_ref, lq1_ref, lk1_ref,
                            lq2_ref, lk2_ref, subt_ref, o_ref, qt_scr, acc_ref, *, tq, tk, grp, n_grp,
                            n_q, unroll, lam_init):
    big = grp * tk
    neg_2c = -2.0 * tab_ref[0, 6:7, 0:1]
    lam = _flash_b_lambda(lq1_ref, lk1_ref, lq2_ref, lk2_ref, lam_init)
    static = n_grp == 1
    zpad = jnp.zeros((HEAD_DIM - BF16_ROWS, tq), BF16)

    for qq in range(n_q):
        qi = qq if static else pl.program_id(2) * n_q + qq
        cols = slice(qq * tq, (qq + 1) * tq)
        t = qi * tq + lax.broadcasted_iota(jnp.int32, (BF16_ROWS, tq), 1)
        pos_hi = (t >> 7).astype(F32)
        pos_lo = (t & 127).astype(F32)
        for mp in range(2):
            rep = lambda r: jnp.concatenate([aug_ref[0, mp, r]] * (tq // LANES), axis=1)
            aug = rep(0) + pos_hi * rep(1) + pos_lo * rep(2)
            for side in range(2):
                rows = jnp.concatenate([(aug if side == 0 else -aug).astype(BF16), zpad], axis=0)
                if mp == 0:
                    qt_scr[qq, mp, side] = jnp.concatenate([qt_ref[0, 0, 0:HEAD_DIM, cols], rows], axis=0)
                else:
                    qt_scr[qq, mp, side] = jnp.concatenate([rows, qt_ref[0, 0, HEAD_DIM:LANES, cols]], axis=0)
    acc_ref[...] = jnp.zeros(acc_ref.shape, F32)

    if static:
        corr = ahead_ref[big:big + tk, :] * neg_2c
        for c in range(grp):
            ks = slice(c * tk, (c + 1) * tk)
            for mp in range(2):
                slots = jnp.concatenate([qt_scr[qq, mp, int(c > qq)] for qq in range(n_q)], axis=1)
                st = jnp.dot(k_ref[0, 0, mp, ks, :], slots, preferred_element_type=F32)
                parts = [st[:, qq * tq:(qq + 1) * tq] + corr if qq == c else st[:, qq * tq:(qq + 1) * tq]
                         for qq in range(n_q)]
                p = jnp.exp2(jnp.concatenate(parts, axis=1)).astype(BF16)
                acc_ref[mp] += jnp.dot(vt_ref[0, 0, :, ks], p, preferred_element_type=F32)
    else:
        qi = pl.program_id(2)
        gd = qi // grp
        r = qi - gd * grp

        def attend(ks, side, corr):
            for mp in range(2):
                st = jnp.dot(k_ref[0, 0, mp, ks, :], qt_scr[0, mp, side], preferred_element_type=F32)
                if corr is not None:
                    st = st + corr
                acc_ref[mp] += jnp.dot(vt_ref[0, 0, :, ks], jnp.exp2(st).astype(BF16),
                                       preferred_element_type=F32)

        w0 = pl.multiple_of(big - r * tk, tk)
        attend(pl.ds(pl.multiple_of(gd * big, big), big), 0, ahead_ref[pl.ds(w0, big), :] * neg_2c)

        def body(j, carry):
            g = gd + j
            g = jnp.where(g >= n_grp, g - n_grp, g)
            attend(pl.ds(pl.multiple_of(g * big, big), big), (g > gd).astype(jnp.int32), None)
            return carry
        lax.fori_loop(1, n_grp, body, 0, unroll=unroll)

    for qq in range(n_q):
        a1 = acc_ref[0, :, qq * tq:(qq + 1) * tq]
        a2 = acc_ref[1, :, qq * tq:(qq + 1) * tq]
        o1 = a1[0:B_V_DIM] * (1.0 / a1[B_V_DIM:B_V_DIM + 1])
        o2 = a2[0:B_V_DIM] * (1.0 / a2[B_V_DIM:B_V_DIM + 1])
        o = _rms_cols(o1 - lam * o2, subt_ref[...]) * (1.0 - lam_init)
        o_ref[0, qq * tq:(qq + 1) * tq, :] = o.T.astype(BF16)


def _flash_b(qb_t, kb, vb_t, tab, aug_t, lq1, lk1, lq2, lk2, subln, lam_init, tq, tk, bounded):
    B, _, _, S = qb_t.shape
    assert tq == tk
    single = S * LANES * 2 * 4 * 2 > 8 * 1024 * 1024
    n_tiles = S // tq
    vec = lambda n: pl.BlockSpec((1, n), lambda b, h, i: (0, 0))
    if bounded:
        grp = S // tk if S // tk <= SHORT_SEQ_CHUNKS else KV_GROUP
        n_grp = S // (tk * grp)
        n_q = n_tiles if n_grp == 1 else 1
        big = grp * tk
        ahead = np.maximum(np.arange(-big, big)[:, None] - np.arange(tq)[None, :], 0).astype(np.float32)
        sub_t = jnp.broadcast_to(subln.reshape(B_V_DIM, 1), (B_V_DIM, tq))
        kern = functools.partial(_flash_b_bounded_kernel, tq=tq, tk=tk, grp=grp, n_grp=n_grp, n_q=n_q,
                                 unroll=max(n_grp - 1, 1), lam_init=lam_init)
        extra_in = [jnp.asarray(ahead)]
        extra_specs = [
            pl.BlockSpec((1, 2, 3, BF16_ROWS, LANES), lambda b, h, i: (h, 0, 0, 0, 0)),
            pl.BlockSpec((2 * big, tq), lambda b, h, i: (0, 0), pipeline_mode=pl.Buffered(1)),
        ]
        sub_in, sub_spec = sub_t, pl.BlockSpec((B_V_DIM, tq), lambda b, h, i: (0, 0))
        scratch = [pltpu.VMEM((n_q, 2, 2, LANES, tq), BF16), pltpu.VMEM((2, VB_ROWS, n_q * tq), F32)]
        args = (qb_t, kb, vb_t, tab, aug_t, *extra_in, lq1, lk1, lq2, lk2, sub_in)
    else:
        n_q = 1
        kern = functools.partial(_flash_b_online_kernel, tq=tq, tk=tk, S=S, lam_init=lam_init)
        extra_specs = []
        sub_in, sub_spec = subln, vec(B_V_DIM)
        scratch = [pltpu.VMEM((2, tq, LANES), BF16), pltpu.VMEM((S, 2 * LANES), BF16),
                   pltpu.VMEM((tq, tk), F32), pltpu.VMEM((2 * tq, LANES), F32),
                   pltpu.VMEM((2 * tq, 2 * LANES), F32)]
        args = (qb_t, kb, vb_t, tab, lq1, lk1, lq2, lk2, sub_in)
    return pl.pallas_call(
        kern,
        grid=(B, B_HEADS, n_tiles // n_q),
        in_specs=[
            pl.BlockSpec((1, 1, LANES, n_q * tq), lambda b, h, i: (b, h, 0, i)),
            _kv_spec((1, 1, 2, S, LANES), lambda b, h, i: (b, h, 0, 0, 0), single),
            _kv_spec((1, 1, VB_ROWS, S), lambda b, h, i: (b, h, 0, 0), single),
            pl.BlockSpec((1, 8, LANES), lambda b, h, i: (h, 0, 0)),
            *extra_specs,
            vec(HEAD_DIM), vec(HEAD_DIM), vec(HEAD_DIM), vec(HEAD_DIM), sub_spec,
        ],
        out_specs=pl.BlockSpec((1, n_q * tq, LANES), lambda b, h, i: (b, i, h)),
        out_shape=jax.ShapeDtypeStruct((B, S, 512), BF16),
        scratch_shapes=scratch,
        compiler_params=pltpu.CompilerParams(
            dimension_semantics=("parallel", "parallel", "arbitrary"),
            vmem_limit_bytes=VMEM_LIMIT_BYTES),
        name="flash_b_bounded" if bounded else "flash_b_online",
    )(*args)


def _ple_tail(h1, p, npl, wg_ref, wp_ref):
    rn = _rms_rows(h1, npl).astype(BF16)
    gate = jax.nn.sigmoid(jnp.dot(rn, wg_ref[...], preferred_element_type=F32))
    return h1 + gate * jnp.dot(p.astype(BF16), wp_ref[...], preferred_element_type=F32)


def _out_ple_kernel(h_ref, oa_ref, ob_ref, g_ref, p_ref, wo_ref, npl_ref, wg_ref, wp_ref, out_ref):
    g = g_ref[0]
    y = jnp.concatenate([oa_ref[0] * g[:, 0:512], ob_ref[0] * g[:, 512:1024]], axis=1)
    h1 = h_ref[0] + jnp.dot(y, wo_ref[...], preferred_element_type=F32)
    out_ref[0] = _ple_tail(h1, p_ref[0, 0], npl_ref[...], wg_ref, wp_ref)


def _out_ple(h, oa, ob, gates, p, layer, wo, npl, wg, wp, tm):
    B, S, _ = h.shape
    tok = lambda b, i: (b, i, 0)
    const2 = lambda shp: pl.BlockSpec(shp, lambda b, i: (0, 0))
    return pl.pallas_call(
        _out_ple_kernel,
        grid=(B, S // tm),
        in_specs=[
            pl.BlockSpec((1, tm, D_MODEL), tok),
            pl.BlockSpec((1, tm, 512), tok),
            pl.BlockSpec((1, tm, 512), tok),
            pl.BlockSpec((1, tm, 1024), tok),
            pl.BlockSpec((1, 1, tm, PLE_DIM), lambda b, i: (layer, b, i, 0)),
            const2((1024, D_MODEL)), const2((1, D_MODEL)), const2((D_MODEL, D_MODEL)),
            const2((PLE_DIM, D_MODEL)),
        ],
        out_specs=pl.BlockSpec((1, tm, D_MODEL), tok),
        out_shape=jax.ShapeDtypeStruct((B, S, D_MODEL), F32),
        compiler_params=pltpu.CompilerParams(
            dimension_semantics=("parallel", "parallel"), vmem_limit_bytes=VMEM_LIMIT_BYTES),
        name="out_ple",
    )(h, oa, ob, gates, p, wo, npl, wg, wp)


def _layer_c_kernel(h_ref, hp_ref, hn_ref, p_ref, nm_ref, wi_ref, wgrp_ref, sc_ref, wo_ref,
                    npl_ref, wg_ref, wp_ref, out_ref, *, tm, S):
    i = pl.program_id(1)
    nt = pl.num_programs(1)
    h = h_ref[0]
    hp = hp_ref[0] * (i > 0).astype(F32)
    hn = hn_ref[0] * (i < nt - 1).astype(F32)
    n = tm + 2 * POOL_HALO
    h_ext = jnp.concatenate([hp, h, hn], axis=0)
    xn = _rms_rows(h_ext, nm_ref[...]).astype(BF16)
    ug = jnp.dot(xn, wi_ref[...], preferred_element_type=F32)
    u_ext = ug[:, 0:1024]
    g = ug[POOL_HALO:POOL_HALO + tm, 1024:2048]

    t = i * tm + lax.broadcasted_iota(jnp.int32, (tm, C_GRP), 0)
    mixed = []
    for gi, w in enumerate(POOL_WINDOWS):
        u = u_ext[:, C_GRP * gi:C_GRP * (gi + 1)]
        win = u + pltpu.roll(u, 1, 0)
        half = 1
        while 2 * half < w:
            win = pltpu.roll(win, half, 0) + pltpu.roll(win, n - half, 0)
            half *= 2
        cnt = (jnp.minimum(t + w // 2, S) - jnp.maximum(t - w // 2, 0)).astype(F32)
        pooled = win[POOL_HALO:POOL_HALO + tm] / cnt - u[POOL_HALO:POOL_HALO + tm]
        mixed.append(jnp.dot(pooled.astype(BF16), wgrp_ref[gi], preferred_element_type=F32))
    mixed = jnp.concatenate(mixed, axis=1)
    y = ((mixed * sc_ref[...]) * (g * jax.nn.sigmoid(g))).astype(BF16)
    h1 = h + jnp.dot(y, wo_ref[...], preferred_element_type=F32)
    out_ref[0] = _ple_tail(h1, p_ref[0, 0], npl_ref[...], wg_ref, wp_ref)


def _layer_c(h, p, layer, nm, wi, wgrp, sc, wo, npl, wg, wp, tm):
    B, S, _ = h.shape
    tok = lambda b, i: (b, i, 0)
    const2 = lambda shp: pl.BlockSpec(shp, lambda b, i: (0, 0))
    per = tm // POOL_HALO
    last = S // POOL_HALO - 1
    kern = functools.partial(_layer_c_kernel, tm=tm, S=S)
    return pl.pallas_call(
        kern,
        grid=(B, S // tm),
        in_specs=[
            pl.BlockSpec((1, tm, D_MODEL), tok),
            pl.BlockSpec((1, POOL_HALO, D_MODEL), lambda b, i: (b, jnp.maximum(i * per - 1, 0), 0)),
            pl.BlockSpec((1, POOL_HALO, D_MODEL), lambda b, i: (b, jnp.minimum((i + 1) * per, last), 0)),
            pl.BlockSpec((1, 1, tm, PLE_DIM), lambda b, i: (layer, b, i, 0)),
            const2((1, D_MODEL)), const2((D_MODEL, 2048)),
            pl.BlockSpec((4, C_GRP, C_GRP), lambda b, i: (0, 0, 0)),
            const2((1, 1024)), const2((1024, D_MODEL)), const2((1, D_MODEL)),
            const2((D_MODEL, D_MODEL)), const2((PLE_DIM, D_MODEL)),
        ],
        out_specs=pl.BlockSpec((1, tm, D_MODEL), tok),
        out_shape=jax.ShapeDtypeStruct((B, S, D_MODEL), F32),
        compiler_params=pltpu.CompilerParams(
            dimension_semantics=("parallel", "parallel"), vmem_limit_bytes=VMEM_LIMIT_BYTES),
        name="layer_c",
    )(h, h, h, p, nm, wi, wgrp, sc, wo, npl, wg, wp)


def _score_bound(gq, gk):
    slack = 1.02
    return (HEAD_DIM * SM_SCALE * LOG2E * slack) * jnp.max(jnp.abs(gq)) * jnp.max(jnp.abs(gk))


def _alibi_tables():
    kaug = np.zeros((16, LANES), np.float32)
    qtab = np.zeros((B_HEADS, 8, LANES), np.float32)
    aug_t = np.zeros((B_HEADS, 2, 3, BF16_ROWS, LANES), np.float32)
    aug_t[:, :, 1, 0:3, :] = -128.0
    aug_t[:, :, 2, 3:6, :] = -1.0
    for mp in range(2):
        base = HEAD_DIM if mp == 0 else 0
        kaug[8 + mp, base + 6:base + 9] = 128.0
        kaug[10 + mp, base + 9:base + 12] = 1.0
    for h, slope in enumerate(ALIBI_SLOPES):
        c = np.float64(slope) * LOG2E
        pieces, rest = [], c
        for _ in range(3):
            piece = np.float64(np.float32(rest).astype(BF16))
            pieces.append(piece)
            rest -= piece
        for mp in range(2):
            base = HEAD_DIM if mp == 0 else 0
            kaug[2 * h + mp, base:base + 6] = pieces + pieces
        aug_t[h, :, 0, 6:12, :] = np.asarray(pieces + pieces)[:, None]
        qtab[h, 6, :] = np.float32(c)
    return jnp.asarray(kaug), jnp.asarray(qtab), jnp.asarray(aug_t)


def _trunk(x, p, prm, tq_a, tq_b, tk, tm):
    row = lambda v: v.astype(F32).reshape(1, -1)
    col = lambda v, s: jnp.broadcast_to((v.astype(F32) * s).reshape(-1, 1), (v.shape[0], tm))
    lam_init = 0.8 - 0.6 * math.exp(-0.3 * 0)
    gq_scale = SM_SCALE * LOG2E
    qa_t, ka, va_t, qb_t, kb, vb_t, gates = _proj_ab(
        x, prm["w_ab_s"], prm["w_ab_t"], row(prm["norm_mix"][0]), col(prm["qn_a"][0], gq_scale),
        col(prm["qn_b"][0], gq_scale), col(prm["kn_a"][0], 1.0), col(prm["kn_b"][0], 1.0),
        prm["kaug"], tm)
    oa = lax.cond(
        _score_bound(prm["qn_a"][0], prm["kn_a"][0]) <= BOUNDED_SCORE_LIMIT,
        lambda q, k, v: _flash_a(q, k, v, tq_a, tk, True),
        lambda q, k, v: _flash_a(q, k, v, tq_a, tk, False),
        qa_t, ka, va_t)
    lam_vecs = (row(prm["lam_q1"][0]), row(prm["lam_k1"][0]), row(prm["lam_q2"][0]),
                row(prm["lam_k2"][0]), row(prm["subln_b"][0]))
    ob = lax.cond(
        _score_bound(prm["qn_b"][0], prm["kn_b"][0]) <= BOUNDED_SCORE_LIMIT,
        lambda q, k, v: _flash_b(q, k, v, prm["qtab"], prm["aug_t"], *lam_vecs, lam_init, tq_b, tk, True),
        lambda q, k, v: _flash_b(q, k, v, prm["qtab"], prm["aug_t"], *lam_vecs, lam_init, tq_b, tk, False),
        qb_t, kb, vb_t)
    h = _out_ple(x, oa, ob, gates, p, 0, prm["w_out_ab"][0], row(prm["norm_ple"][0]),
                 prm["w_ple_gate"][0], prm["w_ple_proj"][0], tm)
    h = _layer_c(h, p, 1, row(prm["norm_mix"][1]), prm["w_in_c"][0], prm["w_grp_c"][0],
                 row(prm["scale_c"][0]), prm["w_out_c"][0], row(prm["norm_ple"][1]),
                 prm["w_ple_gate"][1], prm["w_ple_proj"][1], tm)
    return h


def _prepare(norm_mix, w_in_ab, qn_a, kn_a, qn_b, kn_b, lam_q1, lam_k1, lam_q2, lam_k2, subln_b,
             w_out_ab, w_in_c, w_grp_c, scale_c, w_out_c, norm_ple, w_ple_gate, w_ple_proj):
    kaug, qtab, aug_t = _alibi_tables()
    w_ab = w_in_ab[0].astype(BF16)
    cols = lambda a, b: w_ab[:, a:b]
    return dict(
        norm_mix=norm_mix, qn_a=qn_a, kn_a=kn_a, qn_b=qn_b, kn_b=kn_b,
        lam_q1=lam_q1, lam_k1=lam_k1, lam_q2=lam_q2, lam_k2=lam_k2, subln_b=subln_b,
        scale_c=scale_c, norm_ple=norm_ple,
        w_ab_s=jnp.concatenate([cols(_GA, _QB), cols(_GB, _AB_END)], axis=1),
        w_ab_t=jnp.concatenate([cols(_QA, _KA), cols(_QB, _KB), cols(_VA, _GA), cols(_VB, _GB),
                                cols(_KA, _VA), cols(_KB, _VB)], axis=1).T,
        w_out_ab=w_out_ab.astype(BF16), w_in_c=w_in_c.astype(BF16),
        w_grp_c=w_grp_c.astype(BF16), w_out_c=w_out_c.astype(BF16),
        w_ple_gate=w_ple_gate.astype(BF16), w_ple_proj=w_ple_proj.astype(BF16),
        kaug=kaug, qtab=qtab, aug_t=aug_t,
    )


def kernel(x_prompt, x_sample, p_prompt, p_sample, norm_mix, w_in_ab, qn_a, kn_a, qn_b, kn_b,
           lam_q1, lam_k1, lam_q2, lam_k2, subln_b, w_out_ab, w_in_c, w_grp_c, scale_c, w_out_c,
           norm_ple, w_ple_gate, w_ple_proj):
    prm = _prepare(norm_mix, w_in_ab, qn_a, kn_a, qn_b, kn_b, lam_q1, lam_k1, lam_q2, lam_k2, subln_b,
                   w_out_ab, w_in_c, w_grp_c, scale_c, w_out_c, norm_ple, w_ple_gate, w_ple_proj)
    y_prompt = _trunk(x_prompt, p_prompt, prm, tq_a=512, tq_b=512, tk=512, tm=1024)
    y_sample = _trunk(x_sample, p_sample, prm, tq_a=1024, tq_b=512, tk=512, tm=1024)
    return (y_prompt, y_sample)
```

```python
import functools
import math

import jax
import jax.numpy as jnp
import numpy as np
from jax import lax
from jax.experimental import pallas as pl
from jax.experimental.pallas import tpu as pltpu

F32 = jnp.float32
BF16 = jnp.bfloat16

D_MODEL = 1024
PLE_DIM = 256
GRID_W = 64
HEAD_DIM = 64
EPS = 1e-6
ROPE_THETA = 10000.0
A_HEADS = 8
A_KV_HEADS = 2
B_HEADS = 4
B_V_DIM = 2 * HEAD_DIM
ALIBI_SLOPES = tuple(2.0 ** (-8.0 * (h + 1) / B_HEADS) for h in range(B_HEADS))
POOL_WINDOWS = (2, 4, 8, 16)
C_GRP = 256
POOL_HALO = 8
LOG2E = math.log2(math.e)
SM_SCALE = HEAD_DIM ** -0.5

LANES = 128
BF16_ROWS = 16
VMEM_LIMIT_BYTES = 56 * 1024 * 1024

VA_ROWS = LANES
VB_ROWS = B_V_DIM + BF16_ROWS

_QA, _KA, _VA, _GA, _QB, _KB, _VB, _GB, _AB_END = 0, 512, 640, 768, 1280, 1792, 2304, 2816, 3328
_S_END = 1024
_T_QB, _T_VA, _T_VB, _T_KA, _T_KB, _T_END = 512, 1024, 1152, 1664, 1792, 2304

NEG_BIG = -1e30
BOUNDED_SCORE_LIMIT = 60.0
KV_GROUP = 2
KV_UNROLL = 8
SHORT_SEQ_CHUNKS = 4


def _rms_rows(x, g):
    ms = jnp.mean(x * x, axis=-1, keepdims=True)
    return x * lax.rsqrt(ms + EPS) * g


def _rms_cols(x, g):
    ms = jnp.mean(x * x, axis=0, keepdims=True)
    return x * lax.rsqrt(ms + EPS) * g


def _nt_dot(a, b):
    return lax.dot_general(a, b, (((1,), (1,)), ((), ())), preferred_element_type=F32)


def _to_token_major(xt):
    return xt.astype(F32).T.astype(BF16)


def _proj_ab_kernel(x_ref, cost_ref, sint_ref, nm_ref, ws_ref, wt_ref,
                    gqat_ref, gqbt_ref, gkat_ref, gkbt_ref, kaug_ref,
                    qa_ref, ka_ref, va_ref, qb_ref, kb_ref, vb_ref, g_ref):
    x = x_ref[0]
    xn = _rms_rows(x, nm_ref[...]).astype(BF16)
    tm = x.shape[0]

    def seg_t(a, b):
        return _nt_dot(wt_ref[a:b, :], xn)

    cos_t = cost_ref[...]
    sin_t = sint_ref[...]

    def norm_rope_t(zh, gain):
        zh = _rms_cols(zh, gain)
        sw = jnp.concatenate([zh[16:32], zh[0:16], zh[48:64], zh[32:48]], axis=0)
        return zh * cos_t + sw * sin_t

    z = seg_t(0, _T_VA)
    for h in range(A_HEADS):
        r0 = 64 * (h % 2)
        qa_ref[0, h // 2, r0:r0 + 64, :] = norm_rope_t(z[64 * h:64 * (h + 1)], gqat_ref[...]).astype(BF16)
    for h in range(2 * B_HEADS):
        r0 = 64 * (h % 2)
        zh = z[_T_QB + 64 * h:_T_QB + 64 * (h + 1)]
        qb_ref[0, h // 2, r0:r0 + 64, :] = _rms_cols(zh, gqbt_ref[...]).astype(BF16)
    ones_rows = (lax.broadcasted_iota(jnp.int32, (BF16_ROWS, tm), 0) == 0).astype(BF16)
    z = seg_t(_T_VA, _T_END)
    for kv in range(A_KV_HEADS):
        va_ref[0, kv, 0:HEAD_DIM, :] = z[64 * kv:64 * (kv + 1)].astype(BF16)
        va_ref[0, kv, HEAD_DIM:HEAD_DIM + BF16_ROWS, :] = ones_rows
        va_ref[0, kv, HEAD_DIM + BF16_ROWS:VA_ROWS, :] = jnp.zeros((VA_ROWS - HEAD_DIM - BF16_ROWS, tm), BF16)
    for hd in range(B_HEADS):
        r0 = _T_VB - _T_VA + B_V_DIM * hd
        vb_ref[0, hd, 0:B_V_DIM, :] = z[r0:r0 + B_V_DIM].astype(BF16)
        vb_ref[0, hd, B_V_DIM:VB_ROWS, :] = ones_rows

    r0 = _T_KA - _T_VA
    k = jnp.concatenate([norm_rope_t(z[r0 + 64 * kv:r0 + 64 * (kv + 1)], gkat_ref[...])
                         for kv in range(A_KV_HEADS)], axis=0).T

    lane = lax.broadcasted_iota(jnp.int32, (tm, LANES), 1)
    lo = lane < HEAD_DIM
    t = pl.program_id(1) * tm + lax.broadcasted_iota(jnp.int32, (tm, LANES), 0)
    pos_hi = (t >> 7).astype(F32)
    pos_lo = (t & 127).astype(F32)

    k_sw = pltpu.roll(k, HEAD_DIM, 1)
    zero = jnp.zeros_like(k)
    ka_ref[0, 0, 0] = jnp.where(lo, k, zero).astype(BF16)
    ka_ref[0, 0, 1] = jnp.where(lo, zero, k_sw).astype(BF16)
    ka_ref[0, 1, 0] = jnp.where(lo, k_sw, zero).astype(BF16)
    ka_ref[0, 1, 1] = jnp.where(lo, zero, k).astype(BF16)

    for hd in range(B_HEADS):
        r0 = _T_KB - _T_VA + 128 * hd
        kh = jnp.concatenate([_rms_cols(z[r0 + 64 * mp:r0 + 64 * (mp + 1)], gkbt_ref[...])
                              for mp in range(2)], axis=0).T
        for mp in range(2):
            aug = (kaug_ref[2 * hd + mp:2 * hd + mp + 1, :] + pos_hi * kaug_ref[8 + mp:9 + mp, :]
                   + pos_lo * kaug_ref[10 + mp:11 + mp, :])
            keep = lo if mp == 0 else jnp.logical_not(lo)
            kb_ref[0, hd, mp] = jnp.where(keep, kh, aug).astype(BF16)

    g = jnp.dot(xn, ws_ref[...], preferred_element_type=F32)
    g_ref[0] = (g * jax.nn.sigmoid(g)).astype(BF16)


def _rope_tables(S):
    rows = S // GRID_W
    row = jnp.repeat(jnp.arange(rows), GRID_W).astype(F32)
    col = jnp.tile(jnp.arange(GRID_W), rows).astype(F32)
    half = HEAD_DIM // 2
    inv = ROPE_THETA ** (-jnp.arange(0, half, 2, dtype=F32) / half)
    ar = row[:, None] * inv
    ac = col[:, None] * inv
    ang = jnp.concatenate([ar, ar, ac, ac], axis=-1)
    first = (np.arange(HEAD_DIM) % 32) < 16
    return jnp.cos(ang).T, jnp.where(first, -jnp.sin(ang), jnp.sin(ang)).T


def _proj_ab(x, ws, wt, nm, gqat, gqbt, gkat, gkbt, kaug, tm):
    B, S, _ = x.shape
    cos_t, sin_t = _rope_tables(S)
    tok = lambda b, i: (b, i, 0)
    tab_t = pl.BlockSpec((HEAD_DIM, tm), lambda b, i: (0, i))
    const2 = lambda shp: pl.BlockSpec(shp, lambda b, i: (0, 0))
    chan = lambda n, r: pl.BlockSpec((1, n, r, tm), lambda b, i: (b, 0, 0, i))
    out_shape = (
        jax.ShapeDtypeStruct((B, 4, LANES, S), BF16),
        jax.ShapeDtypeStruct((B, 2, 2, S, LANES), BF16),
        jax.ShapeDtypeStruct((B, 2, VA_ROWS, S), BF16),
        jax.ShapeDtypeStruct((B, 4, LANES, S), BF16),
        jax.ShapeDtypeStruct((B, 4, 2, S, LANES), BF16),
        jax.ShapeDtypeStruct((B, 4, VB_ROWS, S), BF16),
        jax.ShapeDtypeStruct((B, S, 1024), BF16),
    )
    out_specs = (
        chan(4, LANES),
        pl.BlockSpec((1, 2, 2, tm, LANES), lambda b, i: (b, 0, 0, i, 0)),
        chan(2, VA_ROWS),
        chan(4, LANES),
        pl.BlockSpec((1, 4, 2, tm, LANES), lambda b, i: (b, 0, 0, i, 0)),
        chan(4, VB_ROWS),
        pl.BlockSpec((1, tm, 1024), tok),
    )
    return pl.pallas_call(
        _proj_ab_kernel,
        grid=(B, S // tm),
        in_specs=[
            pl.BlockSpec((1, tm, D_MODEL), tok), tab_t, tab_t,
            const2((1, D_MODEL)), const2((D_MODEL, _S_END)), const2((_T_END, D_MODEL)),
            const2((HEAD_DIM, tm)), const2((HEAD_DIM, tm)), const2((HEAD_DIM, tm)), const2((HEAD_DIM, tm)),
            const2((16, LANES)),
        ],
        out_specs=out_specs,
        out_shape=out_shape,
        compiler_params=pltpu.CompilerParams(
            dimension_semantics=("parallel", "parallel"), vmem_limit_bytes=VMEM_LIMIT_BYTES),
        name="proj_ab",
    )(x, cos_t, sin_t, nm, ws, wt, gqat, gqbt, gkat, gkbt, kaug)


def _online_update(s, v, m_ref, acc_ref, r0, nrows):
    rows = pl.ds(r0, nrows)
    m_old = m_ref[rows, :]
    m_new = jnp.maximum(m_old, jnp.max(s, axis=-1, keepdims=True))
    alpha = jnp.exp2(m_old - m_new)
    p = jnp.exp2(s - m_new[:, :1]).astype(BF16)
    pv = jnp.dot(p, v, preferred_element_type=F32)
    nv = v.shape[1]
    if nv != LANES:
        alpha = jnp.concatenate([alpha] * (nv // LANES), axis=1)
    acc_ref[rows, :] = acc_ref[rows, :] * alpha + pv
    m_ref[rows, :] = m_new


def _flash_a_online_kernel(qt_ref, k_ref, vt_ref, o_ref, q_scr, v_scr, m_ref, acc_ref, *, tq, tk, S):
    q_scr[0:tq, :] = _to_token_major(qt_ref[0, 0])
    q_scr[tq:2 * tq, :] = _to_token_major(qt_ref[0, 1])

    @pl.when(pl.program_id(2) == 0)
    def _():
        for c in range(S // tk):
            cols = slice(c * tk, (c + 1) * tk)
            v_scr[cols, :] = _to_token_major(vt_ref[0, 0, :, cols])

    m_ref[...] = jnp.full(m_ref.shape, NEG_BIG, F32)
    acc_ref[...] = jnp.zeros(acc_ref.shape, F32)

    def body(c, carry):
        ks = pl.ds(pl.multiple_of(c * tk, tk), tk)
        q = q_scr[...]
        v = v_scr[ks, :]
        for var in range(2):
            s = _nt_dot(q, k_ref[0, 0, var, ks, :])
            _online_update(s, v, m_ref, acc_ref, var * 2 * tq, 2 * tq)
        return carry

    lax.fori_loop(0, S // tk, body, 0)

    lane = lax.broadcasted_iota(jnp.int32, (tq, LANES), 1)
    lo = lane < HEAD_DIM

    def head_out(blk):
        a = acc_ref[blk * tq:(blk + 1) * tq, :]
        return a * (1.0 / a[:, HEAD_DIM:HEAD_DIM + 1])

    o01 = jnp.where(lo, head_out(0), pltpu.roll(head_out(2), HEAD_DIM, 1))
    o23 = jnp.where(lo, head_out(1), pltpu.roll(head_out(3), HEAD_DIM, 1))
    o_ref[0, :, 0:128] = o01.astype(BF16)
    o_ref[0, :, 128:256] = o23.astype(BF16)


def _flash_a_bounded_kernel(qt_ref, k_ref, vt_ref, o_ref, qt_scr, acc_ref, *, tq, big, n_grp, unroll):
    qt_scr[:, 0:tq] = qt_ref[0, 0]
    qt_scr[:, tq:2 * tq] = qt_ref[0, 1]
    acc_ref[...] = jnp.zeros(acc_ref.shape, F32)

    def body(c, carry):
        ks = pl.ds(pl.multiple_of(c * big, big), big)
        for var in range(2):
            st = jnp.dot(k_ref[0, 0, var, ks, :], qt_scr[...], preferred_element_type=F32)
            acc_ref[var] += jnp.dot(vt_ref[0, 0, :, ks], jnp.exp2(st).astype(BF16),
                                    preferred_element_type=F32)
        return carry

    lax.fori_loop(0, n_grp, body, 0, unroll=unroll)

    outs = []
    for var in range(2):
        a = acc_ref[var]
        outs.append(a[0:HEAD_DIM] * (1.0 / a[HEAD_DIM:HEAD_DIM + 1]))
    for pr in range(2):
        cols = slice(pr * tq, (pr + 1) * tq)
        pair_t = jnp.concatenate([outs[0][:, cols], outs[1][:, cols]], axis=0)
        o_ref[0, :, 128 * pr:128 * (pr + 1)] = pair_t.T.astype(BF16)


def _kv_spec(block, index_map, single_buffer):
    if single_buffer:
        return pl.BlockSpec(block, index_map, pipeline_mode=pl.Buffered(1))
    return pl.BlockSpec(block, index_map)


def _flash_a(qa_t, ka, va_t, tq, tk, bounded):
    B, _, _, S = qa_t.shape
    single = S * LANES * 2 * 3 * 2 > 8 * 1024 * 1024
    if bounded:
        grp = max(1, min(KV_GROUP * tk // tq, S // tk))
        n_grp = S // (tk * grp)
        kern = functools.partial(_flash_a_bounded_kernel, tq=tq, big=tk * grp, n_grp=n_grp,
                                 unroll=min(KV_UNROLL, n_grp))
        scratch = [pltpu.VMEM((LANES, 2 * tq), BF16), pltpu.VMEM((2, VA_ROWS, 2 * tq), F32)]
    else:
        kern = functools.partial(_flash_a_online_kernel, tq=tq, tk=tk, S=S)
        scratch = [pltpu.VMEM((2 * tq, LANES), BF16), pltpu.VMEM((S, LANES), BF16),
                   pltpu.VMEM((4 * tq, LANES), F32), pltpu.VMEM((4 * tq, LANES), F32)]
    return pl.pallas_call(
        kern,
        grid=(B, A_KV_HEADS, S // tq),
        in_specs=[
            pl.BlockSpec((1, 2, LANES, tq), lambda b, g, i: (b, g, 0, i)),
            _kv_spec((1, 1, 2, S, LANES), lambda b, g, i: (b, g, 0, 0, 0), single),
            _kv_spec((1, 1, VA_ROWS, S), lambda b, g, i: (b, g, 0, 0), single),
        ],
        out_specs=pl.BlockSpec((1, tq, 256), lambda b, g, i: (b, i, g)),
        out_shape=jax.ShapeDtypeStruct((B, S, 512), BF16),
        scratch_shapes=scratch,
        compiler_params=pltpu.CompilerParams(
            dimension_semantics=("parallel", "parallel", "arbitrary"),
            vmem_limit_bytes=VMEM_LIMIT_BYTES),
        name="flash_a_bounded" if bounded else "flash_a_online",
    )(qa_t, ka, va_t)


def _flash_b_lambda(lq1_ref, lk1_ref, lq2_ref, lk2_ref, lam_init):
    return (jnp.exp(jnp.sum(lq1_ref[...] * lk1_ref[...], keepdims=True))
            - jnp.exp(jnp.sum(lq2_ref[...] * lk2_ref[...], keepdims=True)) + lam_init)


def _flash_b_online_kernel(qt_ref, k_ref, vt_ref, tab_ref, lq1_ref, lk1_ref, lq2_ref, lk2_ref, sub_ref,
                           o_ref, q_scr, v_scr, rel_ref, m_ref, acc_ref, *, tq, tk, S, lam_init):
    lane = lax.broadcasted_iota(jnp.int32, (tq, LANES), 1)
    lo = lane < HEAD_DIM
    slab = _to_token_major(qt_ref[0, 0])
    zero = jnp.zeros_like(slab)
    q_scr[0] = jnp.where(lo, slab, zero)
    q_scr[1] = jnp.where(lo, zero, slab)

    @pl.when(pl.program_id(2) == 0)
    def _():
        pad = jnp.zeros((LANES - BF16_ROWS, tk), BF16)
        for c in range(S // tk):
            cols = slice(c * tk, (c + 1) * tk)
            v_scr[cols, 0:B_V_DIM] = _to_token_major(vt_ref[0, 0, 0:B_V_DIM, cols])
            ones_blk = jnp.concatenate([vt_ref[0, 0, B_V_DIM:VB_ROWS, cols], pad], axis=0)
            v_scr[cols, B_V_DIM:2 * LANES] = _to_token_major(ones_blk)

    m_ref[...] = jnp.full(m_ref.shape, NEG_BIG, F32)
    acc_ref[...] = jnp.zeros(acc_ref.shape, F32)
    rel_ref[...] = (lax.broadcasted_iota(jnp.int32, (tq, tk), 0)
                    - lax.broadcasted_iota(jnp.int32, (tq, tk), 1)).astype(F32)
    neg_slope = -tab_ref[0, 6:7, 0:1]
    q0 = pl.program_id(2) * tq

    def body(c, carry):
        k0 = pl.multiple_of(c * tk, tk)
        ks = pl.ds(k0, tk)
        v = v_scr[ks, :]
        bias = jnp.abs(rel_ref[...] + (q0 - k0).astype(F32)) * neg_slope
        for mp in range(2):
            s = _nt_dot(q_scr[mp], k_ref[0, 0, mp, ks, :]) + bias
            _online_update(s, v, m_ref, acc_ref, mp * tq, tq)
        return carry

    lax.fori_loop(0, S // tk, body, 0)

    lam = _flash_b_lambda(lq1_ref, lk1_ref, lq2_ref, lk2_ref, lam_init)
    a1 = acc_ref[0:tq, :]
    a2 = acc_ref[tq:2 * tq, :]
    o1 = a1[:, 0:B_V_DIM] * (1.0 / a1[:, B_V_DIM:B_V_DIM + 1])
    o2 = a2[:, 0:B_V_DIM] * (1.0 / a2[:, B_V_DIM:B_V_DIM + 1])
    o = o1 - lam * o2
    o_ref[0] = (_rms_rows(o, sub_ref[...]) * (1.0 - lam_init)).astype(BF16)


def _flash_b_bounded_kernel(qt_ref, k_ref, vt_ref, tab_ref, aug_ref, ahead_ref, lq1_ref, lk1_ref,
                            lq2_ref, lk2_ref, subt_ref, o_ref, qt_scr, acc_ref, *, tq, tk, grp, n_grp,
                            n_q, unroll, lam_init):
    big = grp * tk
    neg_2c = -2.0 * tab_ref[0, 6:7, 0:1]
    lam = _flash_b_lambda(lq1_ref, lk1_ref, lq2_ref, lk2_ref, lam_init)
    static = n_grp == 1
    zpad = jnp.zeros((HEAD_DIM - BF16_ROWS, tq), BF16)

    for qq in range(n_q):
        qi = qq if static else pl.program_id(2) * n_q + qq
        cols = slice(qq * tq, (qq + 1) * tq)
        t = qi * tq + lax.broadcasted_iota(jnp.int32, (BF16_ROWS, tq), 1)
        pos_hi = (t >> 7).astype(F32)
        pos_lo = (t & 127).astype(F32)
        for mp in range(2):
            rep = lambda r: jnp.concatenate([aug_ref[0, mp, r]] * (tq // LANES), axis=1)
            aug = rep(0) + pos_hi * rep(1) + pos_lo * rep(2)
            for side in range(2):
                rows = jnp.concatenate([(aug if side == 0 else -aug).astype(BF16), zpad], axis=0)
                if mp == 0:
                    qt_scr[qq, mp, side] = jnp.concatenate([qt_ref[0, 0, 0:HEAD_DIM, cols], rows], axis=0)
                else:
                    qt_scr[qq, mp, side] = jnp.concatenate([rows, qt_ref[0, 0, HEAD_DIM:LANES, cols]], axis=0)
    acc_ref[...] = jnp.zeros(acc_ref.shape, F32)

    if static:
        corr = ahead_ref[big:big + tk, :] * neg_2c
        for c in range(grp):
            ks = slice(c * tk, (c + 1) * tk)
            for mp in range(2):
                slots = jnp.concatenate([qt_scr[qq, mp, int(c > qq)] for qq in range(n_q)], axis=1)
                st = jnp.dot(k_ref[0, 0, mp, ks, :], slots, preferred_element_type=F32)
                parts = [st[:, qq * tq:(qq + 1) * tq] + corr if qq == c else st[:, qq * tq:(qq + 1) * tq]
                         for qq in range(n_q)]
                p = jnp.exp2(jnp.concatenate(parts, axis=1)).astype(BF16)
                acc_ref[mp] += jnp.dot(vt_ref[0, 0, :, ks], p, preferred_element_type=F32)
    else:
        qi = pl.program_id(2)
        gd = qi // grp
        r = qi - gd * grp

        def attend(ks, side, corr):
            for mp in range(2):
                st = jnp.dot(k_ref[0, 0, mp, ks, :], qt_scr[0, mp, side], preferred_element_type=F32)
                if corr is not None:
                    st = st + corr
                acc_ref[mp] += jnp.dot(vt_ref[0, 0, :, ks], jnp.exp2(st).astype(BF16),
                                       preferred_element_type=F32)

        w0 = pl.multiple_of(big - r * tk, tk)
        attend(pl.ds(pl.multiple_of(gd * big, big), big), 0, ahead_ref[pl.ds(w0, big), :] * neg_2c)

        def body(j, carry):
            g = gd + j
            g = jnp.where(g >= n_grp, g - n_grp, g)
            attend(pl.ds(pl.multiple_of(g * big, big), big), (g > gd).astype(jnp.int32), None)
            return carry
        lax.fori_loop(1, n_grp, body, 0, unroll=unroll)

    for qq in range(n_q):
        a1 = acc_ref[0, :, qq * tq:(qq + 1) * tq]
        a2 = acc_ref[1, :, qq * tq:(qq + 1) * tq]
        o1 = a1[0:B_V_DIM] * (1.0 / a1[B_V_DIM:B_V_DIM + 1])
        o2 = a2[0:B_V_DIM] * (1.0 / a2[B_V_DIM:B_V_DIM + 1])
        o = _rms_cols(o1 - lam * o2, subt_ref[...]) * (1.0 - lam_init)
        o_ref[0, qq * tq:(qq + 1) * tq, :] = o.T.astype(BF16)


def _flash_b(qb_t, kb, vb_t, tab, aug_t, lq1, lk1, lq2, lk2, subln, lam_init, tq, tk, bounded):
    B, _, _, S = qb_t.shape
    assert tq == tk
    single = S * LANES * 2 * 4 * 2 > 8 * 1024 * 1024
    n_tiles = S // tq
    vec = lambda n: pl.BlockSpec((1, n), lambda b, h, i: (0, 0))
    if bounded:
        grp = S // tk if S // tk <= SHORT_SEQ_CHUNKS else KV_GROUP
        n_grp = S // (tk * grp)
        n_q = n_tiles if n_grp == 1 else 1
        big = grp * tk
        ahead = np.maximum(np.arange(-big, big)[:, None] - np.arange(tq)[None, :], 0).astype(np.float32)
        sub_t = jnp.broadcast_to(subln.reshape(B_V_DIM, 1), (B_V_DIM, tq))
        kern = functools.partial(_flash_b_bounded_kernel, tq=tq, tk=tk, grp=grp, n_grp=n_grp, n_q=n_q,
                                 unroll=max(n_grp - 1, 1), lam_init=lam_init)
        extra_in = [jnp.asarray(ahead)]
        extra_specs = [
            pl.BlockSpec((1, 2, 3, BF16_ROWS, LANES), lambda b, h, i: (h, 0, 0, 0, 0)),
            pl.BlockSpec((2 * big, tq), lambda b, h, i: (0, 0), pipeline_mode=pl.Buffered(1)),
        ]
        sub_in, sub_spec = sub_t, pl.BlockSpec((B_V_DIM, tq), lambda b, h, i: (0, 0))
        scratch = [pltpu.VMEM((n_q, 2, 2, LANES, tq), BF16), pltpu.VMEM((2, VB_ROWS, n_q * tq), F32)]
        args = (qb_t, kb, vb_t, tab, aug_t, *extra_in, lq1, lk1, lq2, lk2, sub_in)
    else:
        n_q = 1
        kern = functools.partial(_flash_b_online_kernel, tq=tq, tk=tk, S=S, lam_init=lam_init)
        extra_specs = []
        sub_in, sub_spec = subln, vec(B_V_DIM)
        scratch = [pltpu.VMEM((2, tq, LANES), BF16), pltpu.VMEM((S, 2 * LANES), BF16),
                   pltpu.VMEM((tq, tk), F32), pltpu.VMEM((2 * tq, LANES), F32),
                   pltpu.VMEM((2 * tq, 2 * LANES), F32)]
        args = (qb_t, kb, vb_t, tab, lq1, lk1, lq2, lk2, sub_in)
    return pl.pallas_call(
        kern,
        grid=(B, B_HEADS, n_tiles // n_q),
        in_specs=[
            pl.BlockSpec((1, 1, LANES, n_q * tq), lambda b, h, i: (b, h, 0, i)),
            _kv_spec((1, 1, 2, S, LANES), lambda b, h, i: (b, h, 0, 0, 0), single),
            _kv_spec((1, 1, VB_ROWS, S), lambda b, h, i: (b, h, 0, 0), single),
            pl.BlockSpec((1, 8, LANES), lambda b, h, i: (h, 0, 0)),
            *extra_specs,
            vec(HEAD_DIM), vec(HEAD_DIM), vec(HEAD_DIM), vec(HEAD_DIM), sub_spec,
        ],
        out_specs=pl.BlockSpec((1, n_q * tq, LANES), lambda b, h, i: (b, i, h)),
        out_shape=jax.ShapeDtypeStruct((B, S, 512), BF16),
        scratch_shapes=scratch,
        compiler_params=pltpu.CompilerParams(
            dimension_semantics=("parallel", "parallel", "arbitrary"),
            vmem_limit_bytes=VMEM_LIMIT_BYTES),
        name="flash_b_bounded" if bounded else "flash_b_online",
    )(*args)


def _ple_tail(h1, p, npl, wg_ref, wp_ref):
    rn = _rms_rows(h1, npl).astype(BF16)
    gate = jax.nn.sigmoid(jnp.dot(rn, wg_ref[...], preferred_element_type=F32))
    return h1 + gate * jnp.dot(p.astype(BF16), wp_ref[...], preferred_element_type=F32)


def _out_ple_kernel(h_ref, oa_ref, ob_ref, g_ref, p_ref, wo_ref, npl_ref, wg_ref, wp_ref, out_ref):
    g = g_ref[0]
    y = jnp.concatenate([oa_ref[0] * g[:, 0:512], ob_ref[0] * g[:, 512:1024]], axis=1)
    h1 = h_ref[0] + jnp.dot(y, wo_ref[...], preferred_element_type=F32)
    out_ref[0] = _ple_tail(h1, p_ref[0, 0], npl_ref[...], wg_ref, wp_ref)


def _out_ple(h, oa, ob, gates, p, layer, wo, npl, wg, wp, tm):
    B, S, _ = h.shape
    tok = lambda b, i: (b, i, 0)
    const2 = lambda shp: pl.BlockSpec(shp, lambda b, i: (0, 0))
    return pl.pallas_call(
        _out_ple_kernel,
        grid=(B, S // tm),
        in_specs=[
            pl.BlockSpec((1, tm, D_MODEL), tok),
            pl.BlockSpec((1, tm, 512), tok),
            pl.BlockSpec((1, tm, 512), tok),
            pl.BlockSpec((1, tm, 1024), tok),
            pl.BlockSpec((1, 1, tm, PLE_DIM), lambda b, i: (layer, b, i, 0)),
            const2((1024, D_MODEL)), const2((1, D_MODEL)), const2((D_MODEL, D_MODEL)),
            const2((PLE_DIM, D_MODEL)),
        ],
        out_specs=pl.BlockSpec((1, tm, D_MODEL), tok),
        out_shape=jax.ShapeDtypeStruct((B, S, D_MODEL), F32),
        compiler_params=pltpu.CompilerParams(
            dimension_semantics=("parallel", "parallel"), vmem_limit_bytes=VMEM_LIMIT_BYTES),
        name="out_ple",
    )(h, oa, ob, gates, p, wo, npl, wg, wp)


def _layer_c_kernel(h_ref, hp_ref, hn_ref, p_ref, nm_ref, wi_ref, wgrp_ref, sc_ref, wo_ref,
                    npl_ref, wg_ref, wp_ref, out_ref, *, tm, S):
    i = pl.program_id(1)
    nt = pl.num_programs(1)
    h = h_ref[0]
    hp = hp_ref[0] * (i > 0).astype(F32)
    hn = hn_ref[0] * (i < nt - 1).astype(F32)
    n = tm + 2 * POOL_HALO
    h_ext = jnp.concatenate([hp, h, hn], axis=0)
    xn = _rms_rows(h_ext, nm_ref[...]).astype(BF16)
    ug = jnp.dot(xn, wi_ref[...], preferred_element_type=F32)
    u_ext = ug[:, 0:1024]
    g = ug[POOL_HALO:POOL_HALO + tm, 1024:2048]

    t = i * tm + lax.broadcasted_iota(jnp.int32, (tm, C_GRP), 0)
    mixed = []
    for gi, w in enumerate(POOL_WINDOWS):
        u = u_ext[:, C_GRP * gi:C_GRP * (gi + 1)]
        win = u + pltpu.roll(u, 1, 0)
        half = 1
        while 2 * half < w:
            win = pltpu.roll(win, half, 0) + pltpu.roll(win, n - half, 0)
            half *= 2
        cnt = (jnp.minimum(t + w // 2, S) - jnp.maximum(t - w // 2, 0)).astype(F32)
        pooled = win[POOL_HALO:POOL_HALO + tm] / cnt - u[POOL_HALO:POOL_HALO + tm]
        mixed.append(jnp.dot(pooled.astype(BF16), wgrp_ref[gi], preferred_element_type=F32))
    mixed = jnp.concatenate(mixed, axis=1)
    y = ((mixed * sc_ref[...]) * (g * jax.nn.sigmoid(g))).astype(BF16)
    h1 = h + jnp.dot(y, wo_ref[...], preferred_element_type=F32)
    out_ref[0] = _ple_tail(h1, p_ref[0, 0], npl_ref[...], wg_ref, wp_ref)


def _layer_c(h, p, layer, nm, wi, wgrp, sc, wo, npl, wg, wp, tm):
    B, S, _ = h.shape
    tok = lambda b, i: (b, i, 0)
    const2 = lambda shp: pl.BlockSpec(shp, lambda b, i: (0, 0))
    per = tm // POOL_HALO
    last = S // POOL_HALO - 1
    kern = functools.partial(_layer_c_kernel, tm=tm, S=S)
    return pl.pallas_call(
        kern,
        grid=(B, S // tm),
        in_specs=[
            pl.BlockSpec((1, tm, D_MODEL), tok),
            pl.BlockSpec((1, POOL_HALO, D_MODEL), lambda b, i: (b, jnp.maximum(i * per - 1, 0), 0)),
            pl.BlockSpec((1, POOL_HALO, D_MODEL), lambda b, i: (b, jnp.minimum((i + 1) * per, last), 0)),
            pl.BlockSpec((1, 1, tm, PLE_DIM), lambda b, i: (layer, b, i, 0)),
            const2((1, D_MODEL)), const2((D_MODEL, 2048)),
            pl.BlockSpec((4, C_GRP, C_GRP), lambda b, i: (0, 0, 0)),
            const2((1, 1024)), const2((1024, D_MODEL)), const2((1, D_MODEL)),
            const2((D_MODEL, D_MODEL)), const2((PLE_DIM, D_MODEL)),
        ],
        out_specs=pl.BlockSpec((1, tm, D_MODEL), tok),
        out_shape=jax.ShapeDtypeStruct((B, S, D_MODEL), F32),
        compiler_params=pltpu.CompilerParams(
            dimension_semantics=("parallel", "parallel"), vmem_limit_bytes=VMEM_LIMIT_BYTES),
        name="layer_c",
    )(h, h, h, p, nm, wi, wgrp, sc, wo, npl, wg, wp)


def _score_bound(gq, gk):
    slack = 1.02
    return (HEAD_DIM * SM_SCALE * LOG2E * slack) * jnp.max(jnp.abs(gq)) * jnp.max(jnp.abs(gk))


def _alibi_tables():
    kaug = np.zeros((16, LANES), np.float32)
    qtab = np.zeros((B_HEADS, 8, LANES), np.float32)
    aug_t = np.zeros((B_HEADS, 2, 3, BF16_ROWS, LANES), np.float32)
    aug_t[:, :, 1, 0:3, :] = -128.0
    aug_t[:, :, 2, 3:6, :] = -1.0
    for mp in range(2):
        base = HEAD_DIM if mp == 0 else 0
        kaug[8 + mp, base + 6:base + 9] = 128.0
        kaug[10 + mp, base + 9:base + 12] = 1.0
    for h, slope in enumerate(ALIBI_SLOPES):
        c = np.float64(slope) * LOG2E
        pieces, rest = [], c
        for _ in range(3):
            piece = np.float64(np.float32(rest).astype(BF16))
            pieces.append(piece)
            rest -= piece
        for mp in range(2):
            base = HEAD_DIM if mp == 0 else 0
            kaug[2 * h + mp, base:base + 6] = pieces + pieces
        aug_t[h, :, 0, 6:12, :] = np.asarray(pieces + pieces)[:, None]
        qtab[h, 6, :] = np.float32(c)
    return jnp.asarray(kaug), jnp.asarray(qtab), jnp.asarray(aug_t)


def _trunk(x, p, prm, tq_a, tq_b, tk, tm):
    row = lambda v: v.astype(F32).reshape(1, -1)
    col = lambda v, s: jnp.broadcast_to((v.astype(F32) * s).reshape(-1, 1), (v.shape[0], tm))
    lam_init = 0.8 - 0.6 * math.exp(-0.3 * 0)
    gq_scale = SM_SCALE * LOG2E
    qa_t, ka, va_t, qb_t, kb, vb_t, gates = _proj_ab(
        x, prm["w_ab_s"], prm["w_ab_t"], row(prm["norm_mix"][0]), col(prm["qn_a"][0], gq_scale),
        col(prm["qn_b"][0], gq_scale), col(prm["kn_a"][0], 1.0), col(prm["kn_b"][0], 1.0),
        prm["kaug"], tm)
    oa = lax.cond(
        _score_bound(prm["qn_a"][0], prm["kn_a"][0]) <= BOUNDED_SCORE_LIMIT,
        lambda q, k, v: _flash_a(q, k, v, tq_a, tk, True),
        lambda q, k, v: _flash_a(q, k, v, tq_a, tk, False),
        qa_t, ka, va_t)
    lam_vecs = (row(prm["lam_q1"][0]), row(prm["lam_k1"][0]), row(prm["lam_q2"][0]),
                row(prm["lam_k2"][0]), row(prm["subln_b"][0]))
    ob = lax.cond(
        _score_bound(prm["qn_b"][0], prm["kn_b"][0]) <= BOUNDED_SCORE_LIMIT,
        lambda q, k, v: _flash_b(q, k, v, prm["qtab"], prm["aug_t"], *lam_vecs, lam_init, tq_b, tk, True),
        lambda q, k, v: _flash_b(q, k, v, prm["qtab"], prm["aug_t"], *lam_vecs, lam_init, tq_b, tk, False),
        qb_t, kb, vb_t)
    h = _out_ple(x, oa, ob, gates, p, 0, prm["w_out_ab"][0], row(prm["norm_ple"][0]),
                 prm["w_ple_gate"][0], prm["w_ple_proj"][0], tm)
    h = _layer_c(h, p, 1, row(prm["norm_mix"][1]), prm["w_in_c"][0], prm["w_grp_c"][0],
                 row(prm["scale_c"][0]), prm["w_out_c"][0], row(prm["norm_ple"][1]),
                 prm["w_ple_gate"][1], prm["w_ple_proj"][1], tm)
    return h


def _prepare(norm_mix, w_in_ab, qn_a, kn_a, qn_b, kn_b, lam_q1, lam_k1, lam_q2, lam_k2, subln_b,
             w_out_ab, w_in_c, w_grp_c, scale_c, w_out_c, norm_ple, w_ple_gate, w_ple_proj):
    kaug, qtab, aug_t = _alibi_tables()
    w_ab = w_in_ab[0].astype(BF16)
    cols = lambda a, b: w_ab[:, a:b]
    return dict(
        norm_mix=norm_mix, qn_a=qn_a, kn_a=kn_a, qn_b=qn_b, kn_b=kn_b,
        lam_q1=lam_q1, lam_k1=lam_k1, lam_q2=lam_q2, lam_k2=lam_k2, subln_b=subln_b,
        scale_c=scale_c, norm_ple=norm_ple,
        w_ab_s=jnp.concatenate([cols(_GA, _QB), cols(_GB, _AB_END)], axis=1),
        w_ab_t=jnp.concatenate([cols(_QA, _KA), cols(_QB, _KB), cols(_VA, _GA), cols(_VB, _GB),
                                cols(_KA, _VA), cols(_KB, _VB)], axis=1).T,
        w_out_ab=w_out_ab.astype(BF16), w_in_c=w_in_c.astype(BF16),
        w_grp_c=w_grp_c.astype(BF16), w_out_c=w_out_c.astype(BF16),
        w_ple_gate=w_ple_gate.astype(BF16), w_ple_proj=w_ple_proj.astype(BF16),
        kaug=kaug, qtab=qtab, aug_t=aug_t,
    )


def kernel(x_prompt, x_sample, p_prompt, p_sample, norm_mix, w_in_ab, qn_a, kn_a, qn_b, kn_b,
           lam_q1, lam_k1, lam_q2, lam_k2, subln_b, w_out_ab, w_in_c, w_grp_c, scale_c, w_out_c,
           norm_ple, w_ple_gate, w_ple_proj):
    prm = _prepare(norm_mix, w_in_ab, qn_a, kn_a, qn_b, kn_b, lam_q1, lam_k1, lam_q2, lam_k2, subln_b,
                   w_out_ab, w_in_c, w_grp_c, scale_c, w_out_c, norm_ple, w_ple_gate, w_ple_proj)
    y_prompt = _trunk(x_prompt, p_prompt, prm, tq_a=512, tq_b=512, tk=512, tm=1024)
    y_sample = _trunk(x_sample, p_sample, prm, tq_a=1024, tq_b=512, tk=512, tm=1024)
    return (y_prompt, y_sample)
```

```python
import functools
import math

import jax
import jax.numpy as jnp
import numpy as np
from jax import lax
from jax.experimental import pallas as pl
from jax.experimental.pallas import tpu as pltpu

F32 = jnp.float32
BF16 = jnp.bfloat16

D_MODEL = 1024
PLE_DIM = 256
GRID_W = 64
HEAD_DIM = 64
EPS = 1e-6
ROPE_THETA = 10000.0
A_HEADS = 8
A_KV_HEADS = 2
B_HEADS = 4
B_V_DIM = 2 * HEAD_DIM
ALIBI_SLOPES = tuple(2.0 ** (-8.0 * (h + 1) / B_HEADS) for h in range(B_HEADS))
POOL_WINDOWS = (2, 4, 8, 16)
C_GRP = 256
POOL_HALO = 8
LOG2E = math.log2(math.e)
SM_SCALE = HEAD_DIM ** -0.5

LANES = 128
BF16_ROWS = 16
VMEM_LIMIT_BYTES = 56 * 1024 * 1024

VA_ROWS = LANES
VB_ROWS = B_V_DIM + BF16_ROWS

_QA, _KA, _VA, _GA, _QB, _KB, _VB, _GB, _AB_END = 0, 512, 640, 768, 1280, 1792, 2304, 2816, 3328
_S_END = 1024
_T_QB, _T_VA, _T_VB, _T_KA, _T_KB, _T_END = 512, 1024, 1152, 1664, 1792, 2304

NEG_BIG = -1e30
BOUNDED_SCORE_LIMIT = 60.0
KV_GROUP = 2
KV_UNROLL = 8
KV_DOUBLE_BUFFER_BYTES = 40 * 1024 * 1024
SHORT_SEQ_CHUNKS = 4


def _rms_rows(x, g):
    ms = jnp.mean(x * x, axis=-1, keepdims=True)
    return x * lax.rsqrt(ms + EPS) * g


def _rms_cols(x, g):
    ms = jnp.mean(x * x, axis=0, keepdims=True)
    return x * lax.rsqrt(ms + EPS) * g


def _nt_dot(a, b):
    return lax.dot_general(a, b, (((1,), (1,)), ((), ())), preferred_element_type=F32)


def _to_token_major(xt):
    return xt.astype(F32).T.astype(BF16)


def _proj_ab_kernel(x_ref, cost_ref, sint_ref, nm_ref, ws_ref, wt_ref,
                    gqat_ref, gqbt_ref, gkat_ref, gkbt_ref, kaug_ref,
                    qa_ref, ka_ref, va_ref, qb_ref, kb_ref, vb_ref, g_ref):
    x = x_ref[0]
    xn = _rms_rows(x, nm_ref[...]).astype(BF16)
    tm = x.shape[0]

    def seg_t(a, b):
        return _nt_dot(wt_ref[a:b, :], xn)

    cos_t = cost_ref[...]
    sin_t = sint_ref[...]

    def norm_rope_t(zh, gain):
        zh = _rms_cols(zh, gain)
        sw = jnp.concatenate([zh[16:32], zh[0:16], zh[48:64], zh[32:48]], axis=0)
        return zh * cos_t + sw * sin_t

    z = seg_t(0, _T_VA)
    for h in range(A_HEADS):
        r0 = 64 * (h % 2)
        qa_ref[0, h // 2, r0:r0 + 64, :] = norm_rope_t(z[64 * h:64 * (h + 1)], gqat_ref[...]).astype(BF16)
    for h in range(2 * B_HEADS):
        r0 = 64 * (h % 2)
        zh = z[_T_QB + 64 * h:_T_QB + 64 * (h + 1)]
        qb_ref[0, h // 2, r0:r0 + 64, :] = _rms_cols(zh, gqbt_ref[...]).astype(BF16)
    ones_rows = (lax.broadcasted_iota(jnp.int32, (BF16_ROWS, tm), 0) == 0).astype(BF16)
    z = seg_t(_T_VA, _T_END)
    for kv in range(A_KV_HEADS):
        va_ref[0, kv, 0:HEAD_DIM, :] = z[64 * kv:64 * (kv + 1)].astype(BF16)
        va_ref[0, kv, HEAD_DIM:HEAD_DIM + BF16_ROWS, :] = ones_rows
        va_ref[0, kv, HEAD_DIM + BF16_ROWS:VA_ROWS, :] = jnp.zeros((VA_ROWS - HEAD_DIM - BF16_ROWS, tm), BF16)
    for hd in range(B_HEADS):
        r0 = _T_VB - _T_VA + B_V_DIM * hd
        vb_ref[0, hd, 0:B_V_DIM, :] = z[r0:r0 + B_V_DIM].astype(BF16)
        vb_ref[0, hd, B_V_DIM:VB_ROWS, :] = ones_rows

    r0 = _T_KA - _T_VA
    k = jnp.concatenate([norm_rope_t(z[r0 + 64 * kv:r0 + 64 * (kv + 1)], gkat_ref[...])
                         for kv in range(A_KV_HEADS)], axis=0).T

    lane = lax.broadcasted_iota(jnp.int32, (tm, LANES), 1)
    lo = lane < HEAD_DIM
    t = pl.program_id(1) * tm + lax.broadcasted_iota(jnp.int32, (tm, LANES), 0)
    pos_hi = (t >> 7).astype(F32)
    pos_lo = (t & 127).astype(F32)

    k_sw = pltpu.roll(k, HEAD_DIM, 1)
    zero = jnp.zeros_like(k)
    ka_ref[0, 0, 0] = jnp.where(lo, k, zero).astype(BF16)
    ka_ref[0, 0, 1] = jnp.where(lo, zero, k_sw).astype(BF16)
    ka_ref[0, 1, 0] = jnp.where(lo, k_sw, zero).astype(BF16)
    ka_ref[0, 1, 1] = jnp.where(lo, zero, k).astype(BF16)

    for hd in range(B_HEADS):
        r0 = _T_KB - _T_VA + 128 * hd
        kh = jnp.concatenate([_rms_cols(z[r0 + 64 * mp:r0 + 64 * (mp + 1)], gkbt_ref[...])
                              for mp in range(2)], axis=0).T
        for mp in range(2):
            aug = (kaug_ref[2 * hd + mp:2 * hd + mp + 1, :] + pos_hi * kaug_ref[8 + mp:9 + mp, :]
                   + pos_lo * kaug_ref[10 + mp:11 + mp, :])
            keep = lo if mp == 0 else jnp.logical_not(lo)
            kb_ref[0, hd, mp] = jnp.where(keep, kh, aug).astype(BF16)

    g = jnp.dot(xn, ws_ref[...], preferred_element_type=F32)
    g_ref[0] = (g * jax.nn.sigmoid(g)).astype(BF16)


def _rope_tables(S):
    rows = S // GRID_W
    row = jnp.repeat(jnp.arange(rows), GRID_W).astype(F32)
    col = jnp.tile(jnp.arange(GRID_W), rows).astype(F32)
    half = HEAD_DIM // 2
    inv = ROPE_THETA ** (-jnp.arange(0, half, 2, dtype=F32) / half)
    ar = row[:, None] * inv
    ac = col[:, None] * inv
    ang = jnp.concatenate([ar, ar, ac, ac], axis=-1)
    first = (np.arange(HEAD_DIM) % 32) < 16
    return jnp.cos(ang).T, jnp.where(first, -jnp.sin(ang), jnp.sin(ang)).T


def _proj_ab(x, ws, wt, nm, gqat, gqbt, gkat, gkbt, kaug, tm):
    B, S, _ = x.shape
    cos_t, sin_t = _rope_tables(S)
    tok = lambda b, i: (b, i, 0)
    tab_t = pl.BlockSpec((HEAD_DIM, tm), lambda b, i: (0, i))
    const2 = lambda shp: pl.BlockSpec(shp, lambda b, i: (0, 0))
    chan = lambda n, r: pl.BlockSpec((1, n, r, tm), lambda b, i: (b, 0, 0, i))
    out_shape = (
        jax.ShapeDtypeStruct((B, 4, LANES, S), BF16),
        jax.ShapeDtypeStruct((B, 2, 2, S, LANES), BF16),
        jax.ShapeDtypeStruct((B, 2, VA_ROWS, S), BF16),
        jax.ShapeDtypeStruct((B, 4, LANES, S), BF16),
        jax.ShapeDtypeStruct((B, 4, 2, S, LANES), BF16),
        jax.ShapeDtypeStruct((B, 4, VB_ROWS, S), BF16),
        jax.ShapeDtypeStruct((B, S, 1024), BF16),
    )
    out_specs = (
        chan(4, LANES),
        pl.BlockSpec((1, 2, 2, tm, LANES), lambda b, i: (b, 0, 0, i, 0)),
        chan(2, VA_ROWS),
        chan(4, LANES),
        pl.BlockSpec((1, 4, 2, tm, LANES), lambda b, i: (b, 0, 0, i, 0)),
        chan(4, VB_ROWS),
        pl.BlockSpec((1, tm, 1024), tok),
    )
    return pl.pallas_call(
        _proj_ab_kernel,
        grid=(B, S // tm),
        in_specs=[
            pl.BlockSpec((1, tm, D_MODEL), tok), tab_t, tab_t,
            const2((1, D_MODEL)), const2((D_MODEL, _S_END)), const2((_T_END, D_MODEL)),
            const2((HEAD_DIM, tm)), const2((HEAD_DIM, tm)), const2((HEAD_DIM, tm)), const2((HEAD_DIM, tm)),
            const2((16, LANES)),
        ],
        out_specs=out_specs,
        out_shape=out_shape,
        compiler_params=pltpu.CompilerParams(
            dimension_semantics=("parallel", "parallel"), vmem_limit_bytes=VMEM_LIMIT_BYTES),
        name="proj_ab",
    )(x, cos_t, sin_t, nm, ws, wt, gqat, gqbt, gkat, gkbt, kaug)


def _online_update(s, v, m_ref, acc_ref, r0, nrows):
    rows = pl.ds(r0, nrows)
    m_old = m_ref[rows, :]
    m_new = jnp.maximum(m_old, jnp.max(s, axis=-1, keepdims=True))
    alpha = jnp.exp2(m_old - m_new)
    p = jnp.exp2(s - m_new[:, :1]).astype(BF16)
    pv = jnp.dot(p, v, preferred_element_type=F32)
    nv = v.shape[1]
    if nv != LANES:
        alpha = jnp.concatenate([alpha] * (nv // LANES), axis=1)
    acc_ref[rows, :] = acc_ref[rows, :] * alpha + pv
    m_ref[rows, :] = m_new


def _flash_a_online_kernel(qt_ref, k_ref, vt_ref, o_ref, q_scr, v_scr, m_ref, acc_ref, *, tq, tk, S):
    q_scr[0:tq, :] = _to_token_major(qt_ref[0, 0])
    q_scr[tq:2 * tq, :] = _to_token_major(qt_ref[0, 1])

    @pl.when(pl.program_id(2) == 0)
    def _():
        for c in range(S // tk):
            cols = slice(c * tk, (c + 1) * tk)
            v_scr[cols, :] = _to_token_major(vt_ref[0, 0, :, cols])

    m_ref[...] = jnp.full(m_ref.shape, NEG_BIG, F32)
    acc_ref[...] = jnp.zeros(acc_ref.shape, F32)

    def body(c, carry):
        ks = pl.ds(pl.multiple_of(c * tk, tk), tk)
        q = q_scr[...]
        v = v_scr[ks, :]
        for var in range(2):
            s = _nt_dot(q, k_ref[0, 0, var, ks, :])
            _online_update(s, v, m_ref, acc_ref, var * 2 * tq, 2 * tq)
        return carry

    lax.fori_loop(0, S // tk, body, 0)

    lane = lax.broadcasted_iota(jnp.int32, (tq, LANES), 1)
    lo = lane < HEAD_DIM

    def head_out(blk):
        a = acc_ref[blk * tq:(blk + 1) * tq, :]
        return a * (1.0 / a[:, HEAD_DIM:HEAD_DIM + 1])

    o01 = jnp.where(lo, head_out(0), pltpu.roll(head_out(2), HEAD_DIM, 1))
    o23 = jnp.where(lo, head_out(1), pltpu.roll(head_out(3), HEAD_DIM, 1))
    o_ref[0, :, 0:128] = o01.astype(BF16)
    o_ref[0, :, 128:256] = o23.astype(BF16)


def _flash_a_bounded_kernel(qt_ref, k_ref, vt_ref, o_ref, qt_scr, acc_ref, *, tq, big, n_grp, unroll):
    qt_scr[:, 0:tq] = qt_ref[0, 0]
    qt_scr[:, tq:2 * tq] = qt_ref[0, 1]
    acc_ref[...] = jnp.zeros(acc_ref.shape, F32)

    def body(c, carry):
        ks = pl.ds(pl.multiple_of(c * big, big), big)
        for var in range(2):
            st = jnp.dot(k_ref[0, 0, var, ks, :], qt_scr[...], preferred_element_type=F32)
            acc_ref[var] += jnp.dot(vt_ref[0, 0, :, ks], jnp.exp2(st).astype(BF16),
                                    preferred_element_type=F32)
        return carry

    lax.fori_loop(0, n_grp, body, 0, unroll=unroll)

    outs = []
    for var in range(2):
        a = acc_ref[var]
        outs.append(a[0:HEAD_DIM] * (1.0 / a[HEAD_DIM:HEAD_DIM + 1]))
    for pr in range(2):
        cols = slice(pr * tq, (pr + 1) * tq)
        pair_t = jnp.concatenate([outs[0][:, cols], outs[1][:, cols]], axis=0)
        o_ref[0, :, 128 * pr:128 * (pr + 1)] = pair_t.T.astype(BF16)


def _kv_spec(block, index_map, single_buffer):
    if single_buffer:
        return pl.BlockSpec(block, index_map, pipeline_mode=pl.Buffered(1))
    return pl.BlockSpec(block, index_map)


def _flash_a(qa_t, ka, va_t, tq, tk, bounded):
    B, _, _, S = qa_t.shape
    single = S * LANES * 2 * 3 * 2 > KV_DOUBLE_BUFFER_BYTES
    if bounded:
        grp = max(1, min(KV_GROUP * tk // tq, S // tk))
        n_grp = S // (tk * grp)
        kern = functools.partial(_flash_a_bounded_kernel, tq=tq, big=tk * grp, n_grp=n_grp,
                                 unroll=min(KV_UNROLL, n_grp))
        scratch = [pltpu.VMEM((LANES, 2 * tq), BF16), pltpu.VMEM((2, VA_ROWS, 2 * tq), F32)]
    else:
        kern = functools.partial(_flash_a_online_kernel, tq=tq, tk=tk, S=S)
        scratch = [pltpu.VMEM((2 * tq, LANES), BF16), pltpu.VMEM((S, LANES), BF16),
                   pltpu.VMEM((4 * tq, LANES), F32), pltpu.VMEM((4 * tq, LANES), F32)]
    return pl.pallas_call(
        kern,
        grid=(B, A_KV_HEADS, S // tq),
        in_specs=[
            pl.BlockSpec((1, 2, LANES, tq), lambda b, g, i: (b, g, 0, i)),
            _kv_spec((1, 1, 2, S, LANES), lambda b, g, i: (b, g, 0, 0, 0), single),
            _kv_spec((1, 1, VA_ROWS, S), lambda b, g, i: (b, g, 0, 0), single),
        ],
        out_specs=pl.BlockSpec((1, tq, 256), lambda b, g, i: (b, i, g)),
        out_shape=jax.ShapeDtypeStruct((B, S, 512), BF16),
        scratch_shapes=scratch,
        compiler_params=pltpu.CompilerParams(
            dimension_semantics=("parallel", "parallel", "arbitrary"),
            vmem_limit_bytes=VMEM_LIMIT_BYTES),
        name="flash_a_bounded" if bounded else "flash_a_online",
    )(qa_t, ka, va_t)


def _flash_b_lambda(lq1_ref, lk1_ref, lq2_ref, lk2_ref, lam_init):
    return (jnp.exp(jnp.sum(lq1_ref[...] * lk1_ref[...], keepdims=True))
            - jnp.exp(jnp.sum(lq2_ref[...] * lk2_ref[...], keepdims=True)) + lam_init)


def _flash_b_online_kernel(qt_ref, k_ref, vt_ref, tab_ref, lq1_ref, lk1_ref, lq2_ref, lk2_ref,
                           o_ref, q_scr, v_scr, rel_ref, m_ref, acc_ref, *, tq, tk, S, lam_init):
    lane = lax.broadcasted_iota(jnp.int32, (tq, LANES), 1)
    lo = lane < HEAD_DIM
    slab = _to_token_major(qt_ref[0, 0])
    zero = jnp.zeros_like(slab)
    q_scr[0] = jnp.where(lo, slab, zero)
    q_scr[1] = jnp.where(lo, zero, slab)

    @pl.when(pl.program_id(2) == 0)
    def _():
        pad = jnp.zeros((LANES - BF16_ROWS, tk), BF16)
        for c in range(S // tk):
            cols = slice(c * tk, (c + 1) * tk)
            v_scr[cols, 0:B_V_DIM] = _to_token_major(vt_ref[0, 0, 0:B_V_DIM, cols])
            ones_blk = jnp.concatenate([vt_ref[0, 0, B_V_DIM:VB_ROWS, cols], pad], axis=0)
            v_scr[cols, B_V_DIM:2 * LANES] = _to_token_major(ones_blk)

    m_ref[...] = jnp.full(m_ref.shape, NEG_BIG, F32)
    acc_ref[...] = jnp.zeros(acc_ref.shape, F32)
    rel_ref[...] = (lax.broadcasted_iota(jnp.int32, (tq, tk), 0)
                    - lax.broadcasted_iota(jnp.int32, (tq, tk), 1)).astype(F32)
    neg_slope = -tab_ref[0, 6:7, 0:1]
    q0 = pl.program_id(2) * tq

    def body(c, carry):
        k0 = pl.multiple_of(c * tk, tk)
        ks = pl.ds(k0, tk)
        v = v_scr[ks, :]
        bias = jnp.abs(rel_ref[...] + (q0 - k0).astype(F32)) * neg_slope
        for mp in range(2):
            s = _nt_dot(q_scr[mp], k_ref[0, 0, mp, ks, :]) + bias
            _online_update(s, v, m_ref, acc_ref, mp * tq, tq)
        return carry

    lax.fori_loop(0, S // tk, body, 0)

    lam = _flash_b_lambda(lq1_ref, lk1_ref, lq2_ref, lk2_ref, lam_init)
    a1 = acc_ref[0:tq, :]
    a2 = acc_ref[tq:2 * tq, :]
    o1 = a1[:, 0:B_V_DIM] * (1.0 / a1[:, B_V_DIM:B_V_DIM + 1])
    o2 = a2[:, 0:B_V_DIM] * (1.0 / a2[:, B_V_DIM:B_V_DIM + 1])
    o = o1 - lam * o2
    o_ref[0] = o.astype(BF16)


def _flash_b_bounded_kernel(qt_ref, k_ref, vt_ref, tab_ref, aug_ref, ahead_ref, lq1_ref, lk1_ref,
                            lq2_ref, lk2_ref, o_ref, qt_scr, acc_ref, *, tq, tk, grp, n_grp,
                            n_q, unroll, lam_init):
    big = grp * tk
    neg_2c = -2.0 * tab_ref[0, 6:7, 0:1]
    lam = _flash_b_lambda(lq1_ref, lk1_ref, lq2_ref, lk2_ref, lam_init)
    static = n_grp == 1
    zpad = jnp.zeros((HEAD_DIM - BF16_ROWS, tq), BF16)

    for qq in range(n_q):
        qi = qq if static else pl.program_id(2) * n_q + qq
        cols = slice(qq * tq, (qq + 1) * tq)
        t = qi * tq + lax.broadcasted_iota(jnp.int32, (BF16_ROWS, tq), 1)
        pos_hi = (t >> 7).astype(F32)
        pos_lo = (t & 127).astype(F32)
        for mp in range(2):
            rep = lambda r: jnp.concatenate([aug_ref[0, mp, r]] * (tq // LANES), axis=1)
            aug = rep(0) + pos_hi * rep(1) + pos_lo * rep(2)
            for side in range(2):
                rows = jnp.concatenate([(aug if side == 0 else -aug).astype(BF16), zpad], axis=0)
                if mp == 0:
                    qt_scr[qq, mp, side] = jnp.concatenate([qt_ref[0, 0, 0:HEAD_DIM, cols], rows], axis=0)
                else:
                    qt_scr[qq, mp, side] = jnp.concatenate([rows, qt_ref[0, 0, HEAD_DIM:LANES, cols]], axis=0)
    acc_ref[...] = jnp.zeros(acc_ref.shape, F32)

    if static:
        corr = ahead_ref[big:big + tk, :] * neg_2c
        for c in range(grp):
            ks = slice(c * tk, (c + 1) * tk)
            for mp in range(2):
                slots = jnp.concatenate([qt_scr[qq, mp, int(c > qq)] for qq in range(n_q)], axis=1)
                st = jnp.dot(k_ref[0, 0, mp, ks, :], slots, preferred_element_type=F32)
                parts = [st[:, qq * tq:(qq + 1) * tq] + corr if qq == c else st[:, qq * tq:(qq + 1) * tq]
                         for qq in range(n_q)]
                p = jnp.exp2(jnp.concatenate(parts, axis=1)).astype(BF16)
                acc_ref[mp] += jnp.dot(vt_ref[0, 0, :, ks], p, preferred_element_type=F32)
    else:
        qi = pl.program_id(2)
        gd = qi // grp
        r = qi - gd * grp

        def attend(ks, side, corr):
            for mp in range(2):
                st = jnp.dot(k_ref[0, 0, mp, ks, :], qt_scr[0, mp, side], preferred_element_type=F32)
                if corr is not None:
                    st = st + corr
                acc_ref[mp] += jnp.dot(vt_ref[0, 0, :, ks], jnp.exp2(st).astype(BF16),
                                       preferred_element_type=F32)

        w0 = pl.multiple_of(big - r * tk, tk)
        attend(pl.ds(pl.multiple_of(gd * big, big), big), 0, ahead_ref[pl.ds(w0, big), :] * neg_2c)

        def body(j, carry):
            g = gd + j
            g = jnp.where(g >= n_grp, g - n_grp, g)
            attend(pl.ds(pl.multiple_of(g * big, big), big), (g > gd).astype(jnp.int32), None)
            return carry
        lax.fori_loop(1, n_grp, body, 0, unroll=unroll)

    for qq in range(n_q):
        a1 = acc_ref[0, :, qq * tq:(qq + 1) * tq]
        a2 = acc_ref[1, :, qq * tq:(qq + 1) * tq]
        o1 = a1[0:B_V_DIM] * (1.0 / a1[B_V_DIM:B_V_DIM + 1])
        o2 = a2[0:B_V_DIM] * (1.0 / a2[B_V_DIM:B_V_DIM + 1])
        o = o1 - lam * o2
        o_ref[0, qq * tq:(qq + 1) * tq, :] = o.T.astype(BF16)


def _flash_b(qb_t, kb, vb_t, tab, aug_t, lq1, lk1, lq2, lk2, lam_init, tq, tk, bounded):
    B, _, _, S = qb_t.shape
    assert tq == tk
    single = S * LANES * 2 * 4 * 2 > KV_DOUBLE_BUFFER_BYTES
    n_tiles = S // tq
    vec = lambda n: pl.BlockSpec((1, n), lambda b, h, i: (0, 0))
    if bounded:
        grp = S // tk if S // tk <= SHORT_SEQ_CHUNKS else KV_GROUP
        n_grp = S // (tk * grp)
        n_q = n_tiles if n_grp == 1 else 1
        big = grp * tk
        ahead = np.maximum(np.arange(-big, big)[:, None] - np.arange(tq)[None, :], 0).astype(np.float32)
        kern = functools.partial(_flash_b_bounded_kernel, tq=tq, tk=tk, grp=grp, n_grp=n_grp, n_q=n_q,
                                 unroll=max(n_grp - 1, 1), lam_init=lam_init)
        extra_in = [jnp.asarray(ahead)]
        extra_specs = [
            pl.BlockSpec((1, 2, 3, BF16_ROWS, LANES), lambda b, h, i: (h, 0, 0, 0, 0)),
            pl.BlockSpec((2 * big, tq), lambda b, h, i: (0, 0), pipeline_mode=pl.Buffered(1)),
        ]
        scratch = [pltpu.VMEM((n_q, 2, 2, LANES, tq), BF16), pltpu.VMEM((2, VB_ROWS, n_q * tq), F32)]
        args = (qb_t, kb, vb_t, tab, aug_t, *extra_in, lq1, lk1, lq2, lk2)
    else:
        n_q = 1
        kern = functools.partial(_flash_b_online_kernel, tq=tq, tk=tk, S=S, lam_init=lam_init)
        extra_specs = []
        scratch = [pltpu.VMEM((2, tq, LANES), BF16), pltpu.VMEM((S, 2 * LANES), BF16),
                   pltpu.VMEM((tq, tk), F32), pltpu.VMEM((2 * tq, LANES), F32),
                   pltpu.VMEM((2 * tq, 2 * LANES), F32)]
        args = (qb_t, kb, vb_t, tab, lq1, lk1, lq2, lk2)
    return pl.pallas_call(
        kern,
        grid=(B, B_HEADS, n_tiles // n_q),
        in_specs=[
            pl.BlockSpec((1, 1, LANES, n_q * tq), lambda b, h, i: (b, h, 0, i)),
            _kv_spec((1, 1, 2, S, LANES), lambda b, h, i: (b, h, 0, 0, 0), single),
            _kv_spec((1, 1, VB_ROWS, S), lambda b, h, i: (b, h, 0, 0), single),
            pl.BlockSpec((1, 8, LANES), lambda b, h, i: (h, 0, 0)),
            *extra_specs,
            vec(HEAD_DIM), vec(HEAD_DIM), vec(HEAD_DIM), vec(HEAD_DIM),
        ],
        out_specs=pl.BlockSpec((1, n_q * tq, LANES), lambda b, h, i: (b, i, h)),
        out_shape=jax.ShapeDtypeStruct((B, S, 512), BF16),
        scratch_shapes=scratch,
        compiler_params=pltpu.CompilerParams(
            dimension_semantics=("parallel", "parallel", "arbitrary"),
            vmem_limit_bytes=VMEM_LIMIT_BYTES),
        name="flash_b_bounded" if bounded else "flash_b_online",
    )(*args)


def _ple_tail(h1, p, npl, wg_ref, wp_ref):
    rn = _rms_rows(h1, npl).astype(BF16)
    gate = jax.nn.sigmoid(jnp.dot(rn, wg_ref[...], preferred_element_type=F32))
    return h1 + gate * jnp.dot(p.astype(BF16), wp_ref[...], preferred_element_type=F32)


def _out_ple_kernel(h_ref, oa_ref, ob_ref, g_ref, p_ref, sub_ref, wo_ref, npl_ref, wg_ref, wp_ref, out_ref,
                    *, sub_scale):
    g = g_ref[0]
    ob = jnp.concatenate(
        [(_rms_rows(ob_ref[0, :, B_V_DIM * hd:B_V_DIM * (hd + 1)].astype(F32), sub_ref[...])
          * sub_scale).astype(BF16) for hd in range(B_HEADS)], axis=1)
    y = jnp.concatenate([oa_ref[0] * g[:, 0:512], ob * g[:, 512:1024]], axis=1)
    h1 = h_ref[0] + jnp.dot(y, wo_ref[...], preferred_element_type=F32)
    out_ref[0] = _ple_tail(h1, p_ref[0, 0], npl_ref[...], wg_ref, wp_ref)


def _out_ple(h, oa, ob, gates, p, layer, subln, sub_scale, wo, npl, wg, wp, tm):
    B, S, _ = h.shape
    tok = lambda b, i: (b, i, 0)
    const2 = lambda shp: pl.BlockSpec(shp, lambda b, i: (0, 0))
    return pl.pallas_call(
        functools.partial(_out_ple_kernel, sub_scale=sub_scale),
        grid=(B, S // tm),
        in_specs=[
            pl.BlockSpec((1, tm, D_MODEL), tok),
            pl.BlockSpec((1, tm, 512), tok),
            pl.BlockSpec((1, tm, 512), tok),
            pl.BlockSpec((1, tm, 1024), tok),
            pl.BlockSpec((1, 1, tm, PLE_DIM), lambda b, i: (layer, b, i, 0)),
            const2((1, B_V_DIM)),
            const2((1024, D_MODEL)), const2((1, D_MODEL)), const2((D_MODEL, D_MODEL)),
            const2((PLE_DIM, D_MODEL)),
        ],
        out_specs=pl.BlockSpec((1, tm, D_MODEL), tok),
        out_shape=jax.ShapeDtypeStruct((B, S, D_MODEL), F32),
        compiler_params=pltpu.CompilerParams(
            dimension_semantics=("parallel", "parallel"), vmem_limit_bytes=VMEM_LIMIT_BYTES),
        name="out_ple",
    )(h, oa, ob, gates, p, subln, wo, npl, wg, wp)


def _layer_c_kernel(h_ref, hp_ref, hn_ref, p_ref, nm_ref, wi_ref, wgrp_ref, sc_ref, wo_ref,
                    npl_ref, wg_ref, wp_ref, out_ref, *, tm, S):
    i = pl.program_id(1)
    nt = pl.num_programs(1)
    h = h_ref[0]
    hp = hp_ref[0] * (i > 0).astype(F32)
    hn = hn_ref[0] * (i < nt - 1).astype(F32)
    n = tm + 2 * POOL_HALO
    h_ext = jnp.concatenate([hp, h, hn], axis=0)
    xn = _rms_rows(h_ext, nm_ref[...]).astype(BF16)
    ug = jnp.dot(xn, wi_ref[...], preferred_element_type=F32)
    u_ext = ug[:, 0:1024]
    g = ug[POOL_HALO:POOL_HALO + tm, 1024:2048]

    t = i * tm + lax.broadcasted_iota(jnp.int32, (tm, C_GRP), 0)
    mixed = []
    for gi, w in enumerate(POOL_WINDOWS):
        u = u_ext[:, C_GRP * gi:C_GRP * (gi + 1)]
        win = u + pltpu.roll(u, 1, 0)
        half = 1
        while 2 * half < w:
            win = pltpu.roll(win, half, 0) + pltpu.roll(win, n - half, 0)
            half *= 2
        cnt = (jnp.minimum(t + w // 2, S) - jnp.maximum(t - w // 2, 0)).astype(F32)
        pooled = win[POOL_HALO:POOL_HALO + tm] / cnt - u[POOL_HALO:POOL_HALO + tm]
        mixed.append(jnp.dot(pooled.astype(BF16), wgrp_ref[gi], preferred_element_type=F32))
    mixed = jnp.concatenate(mixed, axis=1)
    y = ((mixed * sc_ref[...]) * (g * jax.nn.sigmoid(g))).astype(BF16)
    h1 = h + jnp.dot(y, wo_ref[...], preferred_element_type=F32)
    out_ref[0] = _ple_tail(h1, p_ref[0, 0], npl_ref[...], wg_ref, wp_ref)


def _layer_c(h, p, layer, nm, wi, wgrp, sc, wo, npl, wg, wp, tm):
    B, S, _ = h.shape
    tok = lambda b, i: (b, i, 0)
    const2 = lambda shp: pl.BlockSpec(shp, lambda b, i: (0, 0))
    per = tm // POOL_HALO
    last = S // POOL_HALO - 1
    kern = functools.partial(_layer_c_kernel, tm=tm, S=S)
    return pl.pallas_call(
        kern,
        grid=(B, S // tm),
        in_specs=[
            pl.BlockSpec((1, tm, D_MODEL), tok),
            pl.BlockSpec((1, POOL_HALO, D_MODEL), lambda b, i: (b, jnp.maximum(i * per - 1, 0), 0)),
            pl.BlockSpec((1, POOL_HALO, D_MODEL), lambda b, i: (b, jnp.minimum((i + 1) * per, last), 0)),
            pl.BlockSpec((1, 1, tm, PLE_DIM), lambda b, i: (layer, b, i, 0)),
            const2((1, D_MODEL)), const2((D_MODEL, 2048)),
            pl.BlockSpec((4, C_GRP, C_GRP), lambda b, i: (0, 0, 0)),
            const2((1, 1024)), const2((1024, D_MODEL)), const2((1, D_MODEL)),
            const2((D_MODEL, D_MODEL)), const2((PLE_DIM, D_MODEL)),
        ],
        out_specs=pl.BlockSpec((1, tm, D_MODEL), tok),
        out_shape=jax.ShapeDtypeStruct((B, S, D_MODEL), F32),
        compiler_params=pltpu.CompilerParams(
            dimension_semantics=("parallel", "parallel"), vmem_limit_bytes=VMEM_LIMIT_BYTES),
        name="layer_c",
    )(h, h, h, p, nm, wi, wgrp, sc, wo, npl, wg, wp)


def _score_bound(gq, gk):
    slack = 1.02
    return (HEAD_DIM * SM_SCALE * LOG2E * slack) * jnp.max(jnp.abs(gq)) * jnp.max(jnp.abs(gk))


def _alibi_tables():
    kaug = np.zeros((16, LANES), np.float32)
    qtab = np.zeros((B_HEADS, 8, LANES), np.float32)
    aug_t = np.zeros((B_HEADS, 2, 3, BF16_ROWS, LANES), np.float32)
    aug_t[:, :, 1, 0:3, :] = -128.0
    aug_t[:, :, 2, 3:6, :] = -1.0
    for mp in range(2):
        base = HEAD_DIM if mp == 0 else 0
        kaug[8 + mp, base + 6:base + 9] = 128.0
        kaug[10 + mp, base + 9:base + 12] = 1.0
    for h, slope in enumerate(ALIBI_SLOPES):
        c = np.float64(slope) * LOG2E
        pieces, rest = [], c
        for _ in range(3):
            piece = np.float64(np.float32(rest).astype(BF16))
            pieces.append(piece)
            rest -= piece
        for mp in range(2):
            base = HEAD_DIM if mp == 0 else 0
            kaug[2 * h + mp, base:base + 6] = pieces + pieces
        aug_t[h, :, 0, 6:12, :] = np.asarray(pieces + pieces)[:, None]
        qtab[h, 6, :] = np.float32(c)
    return jnp.asarray(kaug), jnp.asarray(qtab), jnp.asarray(aug_t)


def _trunk(x, p, prm, tq_a, tq_b, tk, tm):
    row = lambda v: v.astype(F32).reshape(1, -1)
    col = lambda v, s: jnp.broadcast_to((v.astype(F32) * s).reshape(-1, 1), (v.shape[0], tm))
    lam_init = 0.8 - 0.6 * math.exp(-0.3 * 0)
    gq_scale = SM_SCALE * LOG2E
    qa_t, ka, va_t, qb_t, kb, vb_t, gates = _proj_ab(
        x, prm["w_ab_s"], prm["w_ab_t"], row(prm["norm_mix"][0]), col(prm["qn_a"][0], gq_scale),
        col(prm["qn_b"][0], gq_scale), col(prm["kn_a"][0], 1.0), col(prm["kn_b"][0], 1.0),
        prm["kaug"], tm)
    oa = lax.cond(
        _score_bound(prm["qn_a"][0], prm["kn_a"][0]) <= BOUNDED_SCORE_LIMIT,
        lambda q, k, v: _flash_a(q, k, v, tq_a, tk, True),
        lambda q, k, v: _flash_a(q, k, v, tq_a, tk, False),
        qa_t, ka, va_t)
    lam_vecs = (row(prm["lam_q1"][0]), row(prm["lam_k1"][0]), row(prm["lam_q2"][0]),
                row(prm["lam_k2"][0]))
    ob = lax.cond(
        _score_bound(prm["qn_b"][0], prm["kn_b"][0]) <= BOUNDED_SCORE_LIMIT,
        lambda q, k, v: _flash_b(q, k, v, prm["qtab"], prm["aug_t"], *lam_vecs, lam_init, tq_b, tk, True),
        lambda q, k, v: _flash_b(q, k, v, prm["qtab"], prm["aug_t"], *lam_vecs, lam_init, tq_b, tk, False),
        qb_t, kb, vb_t)
    h = _out_ple(x, oa, ob, gates, p, 0, row(prm["subln_b"][0]), 1.0 - lam_init,
                 prm["w_out_ab"][0], row(prm["norm_ple"][0]),
                 prm["w_ple_gate"][0], prm["w_ple_proj"][0], tm)
    h = _layer_c(h, p, 1, row(prm["norm_mix"][1]), prm["w_in_c"][0], prm["w_grp_c"][0],
                 row(prm["scale_c"][0]), prm["w_out_c"][0], row(prm["norm_ple"][1]),
                 prm["w_ple_gate"][1], prm["w_ple_proj"][1], tm)
    return h


def _prepare(norm_mix, w_in_ab, qn_a, kn_a, qn_b, kn_b, lam_q1, lam_k1, lam_q2, lam_k2, subln_b,
             w_out_ab, w_in_c, w_grp_c, scale_c, w_out_c, norm_ple, w_ple_gate, w_ple_proj):
    kaug, qtab, aug_t = _alibi_tables()
    w_ab = w_in_ab[0].astype(BF16)
    cols = lambda a, b: w_ab[:, a:b]
    return dict(
        norm_mix=norm_mix, qn_a=qn_a, kn_a=kn_a, qn_b=qn_b, kn_b=kn_b,
        lam_q1=lam_q1, lam_k1=lam_k1, lam_q2=lam_q2, lam_k2=lam_k2, subln_b=subln_b,
        scale_c=scale_c, norm_ple=norm_ple,
        w_ab_s=jnp.concatenate([cols(_GA, _QB), cols(_GB, _AB_END)], axis=1),
        w_ab_t=jnp.concatenate([cols(_QA, _KA), cols(_QB, _KB), cols(_VA, _GA), cols(_VB, _GB),
                                cols(_KA, _VA), cols(_KB, _VB)], axis=1).T,
        w_out_ab=w_out_ab.astype(BF16), w_in_c=w_in_c.astype(BF16),
        w_grp_c=w_grp_c.astype(BF16), w_out_c=w_out_c.astype(BF16),
        w_ple_gate=w_ple_gate.astype(BF16), w_ple_proj=w_ple_proj.astype(BF16),
        kaug=kaug, qtab=qtab, aug_t=aug_t,
    )


def kernel(x_prompt, x_sample, p_prompt, p_sample, norm_mix, w_in_ab, qn_a, kn_a, qn_b, kn_b,
           lam_q1, lam_k1, lam_q2, lam_k2, subln_b, w_out_ab, w_in_c, w_grp_c, scale_c, w_out_c,
           norm_ple, w_ple_gate, w_ple_proj):
    prm = _prepare(norm_mix, w_in_ab, qn_a, kn_a, qn_b, kn_b, lam_q1, lam_k1, lam_q2, lam_k2, subln_b,
                   w_out_ab, w_in_c, w_grp_c, scale_c, w_out_c, norm_ple, w_ple_gate, w_ple_proj)
    y_prompt = _trunk(x_prompt, p_prompt, prm, tq_a=512, tq_b=512, tk=512, tm=1024)
    y_sample = _trunk(x_sample, p_sample, prm, tq_a=1024, tq_b=512, tk=512, tm=1024)
    return (y_prompt, y_sample)
```
